```python
import jax, jax.numpy as jnp
from jax import lax
import numpy as np

D_MODEL = 2048
BATCH = 4
SEQ = 4096
DEPTH = 4

N_MIXERS = 4
D_FF = 4 * D_MODEL
N_MOD = 6
NORM_EPS = 1e-6
F32 = jnp.float32
NEG_INF = -1e30
POS_BIG = 1e30

RWKV_HEAD_DIM = 64
RWKV_HEADS = D_MODEL // RWKV_HEAD_DIM
RWKV_DECAY_LORA = max(32, int(round(1.8 * D_MODEL ** 0.5 / 32)) * 32)
RWKV_ICLR_LORA = max(32, int(round(1.8 * D_MODEL ** 0.5 / 32)) * 32)
RWKV_GATE_LORA = max(32, int(round(0.6 * D_MODEL ** 0.8 / 32)) * 32)
RWKV_GN_EPS = 64e-5

RET_HEADS = 8
RET_QK_DIM = D_MODEL // RET_HEADS
RET_V_DIM = 2 * RET_QK_DIM
RET_CHUNK = 128
RET_GN_EPS = 1e-5
ROPE_BASE = 10000.0

CONV_WIDTH = 31
CONV_LN_EPS = 1e-5

NSA_HEADS = 16
NSA_KV_GROUPS = 4
NSA_HEAD_DIM = D_MODEL // NSA_HEADS
NSA_CMP_BLOCK = 32
NSA_CMP_STRIDE = 16
NSA_CMP_HIDDEN = 256
NSA_SEL_BLOCK = 64
NSA_SEL_TOP = 16
NSA_WINDOW = 512
NSA_Q_BLOCK = 32
NSA_IN_DIM = NSA_HEADS * NSA_HEAD_DIM + 6 * NSA_KV_GROUPS * NSA_HEAD_DIM + 3 * NSA_HEADS

kernel_name = 'hybrid_rwkv7_retnet_conformer_nsa_trunk'


def rms_norm(x, g):
    xf = x.astype(F32)
    y = xf * lax.rsqrt(jnp.mean(xf * xf, axis=-1, keepdims=True) + NORM_EPS)
    return (y * g.astype(F32)).astype(x.dtype)


def layer_norm(x, g, b, eps):
    xf = x.astype(F32)
    mu = jnp.mean(xf, axis=-1, keepdims=True)
    var = jnp.mean(jnp.square(xf - mu), axis=-1, keepdims=True)
    return ((xf - mu) * lax.rsqrt(var + eps) * g.astype(F32) + b.astype(F32)).astype(x.dtype)


def head_group_norm(y, g, b, eps):
    B, T, H, d = y.shape
    yf = y.astype(F32)
    mu = jnp.mean(yf, axis=-1, keepdims=True)
    var = jnp.mean(jnp.square(yf - mu), axis=-1, keepdims=True)
    yn = ((yf - mu) * lax.rsqrt(var + eps)).reshape(B, T, H * d)
    return yn * g.astype(F32) + b.astype(F32)


def token_shift(x):
    return jnp.pad(x, ((0, 0), (1, 0), (0, 0)))[:, :-1]


def rwkv7_time_mix(x, mu, w_rkv, w0, w_la, w_lb, a0, a_la, a_lb, g_la, g_lb, k_k, k_a, r_k, ln_g, ln_b, w_out):
    B, T, D = x.shape
    H, N = RWKV_HEADS, RWKV_HEAD_DIM
    xx = token_shift(x) - x
    xs = x[None] + xx[None] * mu[:, None, None, :]
    r, k, v = jnp.einsum('sbtd,sde->sbte', xs[:3], w_rkv)
    w_log = -jax.nn.softplus(-(w0 + jnp.tanh(xs[3] @ w_la) @ w_lb)) - 0.5
    decay = jnp.exp(-jnp.exp(w_log.astype(F32)))
    a = jax.nn.sigmoid(a0 + (xs[4] @ a_la) @ a_lb)
    g = jax.nn.sigmoid(xs[5] @ g_la) @ g_lb
    heads = lambda t: t.reshape(B, T, H, N)
    kk = heads(k * k_k).astype(F32)
    kk = kk / jnp.maximum(jnp.sqrt(jnp.sum(kk * kk, axis=-1, keepdims=True)), 1e-12)
    k = k * (1.0 + (a - 1.0) * k_a)
    r_h = heads(r).astype(F32)
    k_h = heads(k).astype(F32)
    v_h = heads(v).astype(F32)
    a_h = heads(a).astype(F32)
    w_h = heads(decay)

    def step(S, inp):
        r_t, w_t, k_t, v_t, kk_t, a_t = inp
        s_kk = jnp.einsum('bhvk,bhk->bhv', S, -kk_t)
        S = (S * w_t[:, :, None, :] + s_kk[..., None] * (kk_t * a_t)[:, :, None, :]
             + v_t[..., None] * k_t[:, :, None, :])
        return S, jnp.einsum('bhvk,bhk->bhv', S, r_t)

    sf = lambda t: jnp.swapaxes(t, 0, 1)
    S0 = jnp.zeros((B, H, N, N), F32)
    _, y = lax.scan(step, S0, (sf(r_h), sf(w_h), sf(k_h), sf(v_h), sf(kk), sf(a_h)))
    y = head_group_norm(sf(y), ln_g, ln_b, RWKV_GN_EPS)
    bonus = (jnp.sum(r_h * k_h * r_k.astype(F32), axis=-1, keepdims=True) * v_h).reshape(B, T, D)
    return ((y + bonus).astype(x.dtype) * g) @ w_out


def rotary(x, pos):
    d = x.shape[-1]
    inv_freq = ROPE_BASE ** (-jnp.arange(0, d, 2, dtype=F32) / d)
    ang = pos.astype(F32)[:, None] * inv_freq[None, :]
    cos, sin = jnp.cos(ang)[None, :, None, :], jnp.sin(ang)[None, :, None, :]
    xf = x.astype(F32)
    x1, x2 = xf[..., : d // 2], xf[..., d // 2:]
    return jnp.concatenate([x1 * cos - x2 * sin, x2 * cos + x1 * sin], axis=-1)


def retention_mix(x, w_in, gn_g, gn_b, w_out):
    B, T, D = x.shape
    H, dk, dv, C = RET_HEADS, RET_QK_DIM, RET_V_DIM, RET_CHUNK
    nc = T // C
    proj = x @ w_in
    q, k, v, gate = jnp.split(proj, [H * dk, 2 * H * dk, 2 * H * dk + H * dv], axis=-1)
    pos = jnp.arange(T)
    q = rotary(q.reshape(B, T, H, dk), pos)
    k = rotary(k.reshape(B, T, H, dk), pos) * (dk ** -0.5)
    v = v.reshape(B, T, H, dv).astype(F32)
    log_gamma = jnp.log(1.0 - 2.0 ** (-5.0 - jnp.arange(H, dtype=F32)))
    idx = jnp.arange(C, dtype=F32)
    diff = idx[:, None] - idx[None, :]
    inner = jnp.where(diff >= 0, jnp.exp(jnp.maximum(diff, 0.0)[None] * log_gamma[:, None, None]), 0.0)
    q_dec = jnp.exp((idx + 1.0)[None] * log_gamma[:, None])
    k_dec = jnp.exp((C - 1.0 - idx)[None] * log_gamma[:, None])
    c_dec = jnp.exp(C * log_gamma)
    chunks = lambda t: t.reshape(B, nc, C, H, t.shape[-1]).transpose(1, 0, 3, 2, 4)

    def step(R, inp):
        qc, kc, vc = inp
        s = jnp.einsum('bhid,bhjd->bhij', qc, kc) * inner
        o = (jnp.einsum('bhij,bhjv->bhiv', s, vc)
             + jnp.einsum('bhid,bhdv->bhiv', qc, R) * q_dec[None, :, :, None])
        R = R * c_dec[None, :, None, None] + jnp.einsum('bhjd,bhjv->bhdv', kc * k_dec[None, :, :, None], vc)
        return R, o

    R0 = jnp.zeros((B, H, dk, dv), F32)
    _, o = lax.scan(step, R0, (chunks(q), chunks(k), chunks(v)))
    o = o.transpose(1, 0, 3, 2, 4).reshape(B, T, H, dv)
    o = head_group_norm(o, gn_g, gn_b, RET_GN_EPS)
    return (jax.nn.silu(gate.astype(F32)) * o).astype(x.dtype) @ w_out


def conformer_conv_mix(x, pw1_w, pw1_b, dw_w, dw_b, ln_g, ln_b, pw2_w, pw2_b):
    D = x.shape[-1]
    h = x @ pw1_w + pw1_b
    h = h[..., :D] * jax.nn.sigmoid(h[..., D:])
    h = lax.conv_general_dilated(h, dw_w[:, None, :].astype(h.dtype), window_strides=(1,),
                                 padding=((CONV_WIDTH - 1, 0),),
                                 dimension_numbers=('NWC', 'WIO', 'NWC'),
                                 feature_group_count=D) + dw_b
    h = jax.nn.silu(layer_norm(h, ln_g, ln_b, CONV_LN_EPS))
    return h @ pw2_w + pw2_b


def nsa_compress(kv, pe, w1, w2):
    B, T, G, dh = kv.shape
    ch = kv.reshape(B, T // NSA_CMP_STRIDE, NSA_CMP_STRIDE, G, dh)
    blocks = jnp.concatenate([ch[:, :-1], ch[:, 1:]], axis=2) + pe[None, None, :, None, :]
    nc = blocks.shape[1]
    flat = jnp.swapaxes(blocks, 2, 3).reshape(B, nc, G, NSA_CMP_BLOCK * dh)
    return jax.nn.gelu(flat @ w1) @ w2


def masked_softmax(s, mask):
    s = jnp.where(mask, s.astype(F32), NEG_INF)
    return jax.nn.softmax(s, axis=-1) * mask


def nsa_mix(x, w_in, pe_k, pe_v, ck_w1, ck_w2, cv_w1, cv_w2, w_out):
    B, T, D = x.shape
    H, G, dh, QB, W, SB = NSA_HEADS, NSA_KV_GROUPS, NSA_HEAD_DIM, NSA_Q_BLOCK, NSA_WINDOW, NSA_SEL_BLOCK
    R = H // G
    hd, kd = H * dh, G * dh
    proj = x @ w_in
    q, k_cmp, v_cmp, k_sel, v_sel, k_win, v_win, gates = jnp.split(
        proj, [hd, hd + kd, hd + 2 * kd, hd + 3 * kd, hd + 4 * kd, hd + 5 * kd, hd + 6 * kd], axis=-1)
    q = q.reshape(B, T, G, R, dh) * (dh ** -0.5)
    kvh = lambda t: t.reshape(B, T, G, dh)
    k_cmp = nsa_compress(kvh(k_cmp), pe_k, ck_w1, ck_w2)
    v_cmp = nsa_compress(kvh(v_cmp), pe_v, cv_w1, cv_w2)
    nc = k_cmp.shape[1]
    ns = T // SB
    n_sel = min(NSA_SEL_TOP, ns)
    k_sel = kvh(k_sel).reshape(B, ns, SB, G, dh).transpose(0, 3, 1, 2, 4)
    v_sel = kvh(v_sel).reshape(B, ns, SB, G, dh).transpose(0, 3, 1, 2, 4)
    k_win = jnp.pad(kvh(k_win), ((0, 0), (W, 0), (0, 0), (0, 0)))
    v_win = jnp.pad(kvh(v_win), ((0, 0), (W, 0), (0, 0), (0, 0)))
    gates = jax.nn.sigmoid(gates.astype(F32)).reshape(B, T, 3, G, R)
    cmp_end = jnp.arange(nc) * NSA_CMP_STRIDE + NSA_CMP_BLOCK - 1
    cs = jnp.arange(nc)[:, None] * NSA_CMP_STRIDE
    ss = jnp.arange(ns)[None, :] * SB
    sel_map = jnp.maximum(jnp.minimum(cs + NSA_CMP_BLOCK, ss + SB) - jnp.maximum(cs, ss), 0).astype(F32) / NSA_CMP_BLOCK
    blk_ids = jnp.arange(ns)
    b_ix = jnp.arange(B)[:, None, None, None]
    g_ix = jnp.arange(G)[None, :, None, None]

    def block(i):
        s0 = i * QB
        t = s0 + jnp.arange(QB)
        qb = lax.dynamic_slice_in_dim(q, s0, QB, axis=1)
        gb = lax.dynamic_slice_in_dim(gates, s0, QB, axis=1)
        vis = cmp_end[None, :] <= t[:, None]
        p_cmp = masked_softmax(jnp.einsum('bqgrd,bcgd->bgrqc', qb, k_cmp), vis)
        o_cmp = jnp.einsum('bgrqc,bcgd->bqgrd', p_cmp.astype(v_cmp.dtype), v_cmp)
        imp = jnp.einsum('bgrqc,cn->bgqn', p_cmp, sel_map)
        cur = t // SB
        valid = blk_ids[None, :] * SB <= t[:, None]
        forced = ((blk_ids[None, :] == 0) | (blk_ids[None, :] == cur[:, None])
                  | (blk_ids[None, :] == cur[:, None] - 1))
        score = jnp.where(valid, jnp.where(forced, POS_BIG, imp), NEG_INF)
        _, idx = lax.top_k(score, n_sel)
        idx_ok = jnp.take_along_axis(jnp.broadcast_to(valid, score.shape), idx, axis=-1)
        kg = k_sel[b_ix, g_ix, idx]
        vg = v_sel[b_ix, g_ix, idx]
        key_pos = idx[..., None] * SB + jnp.arange(SB)
        sel_mask = (idx_ok[..., None] & (key_pos <= t[None, None, :, None, None])).reshape(B, G, 1, QB, n_sel * SB)
        s_sel = jnp.einsum('bqgrd,bgqnjd->bgrqnj', qb, kg).reshape(B, G, R, QB, n_sel * SB)
        p_sel = masked_softmax(s_sel, sel_mask).reshape(B, G, R, QB, n_sel, SB)
        o_sel = jnp.einsum('bgrqnj,bgqnjd->bqgrd', p_sel.astype(vg.dtype), vg)
        kw = lax.dynamic_slice_in_dim(k_win, s0, W + QB, axis=1)
        vw = lax.dynamic_slice_in_dim(v_win, s0, W + QB, axis=1)
        kpos = s0 - W + jnp.arange(W + QB)
        win_mask = (kpos[None, :] <= t[:, None]) & (kpos[None, :] > t[:, None] - W) & (kpos[None, :] >= 0)
        p_win = masked_softmax(jnp.einsum('bqgrd,bkgd->bgrqk', qb, kw), win_mask)
        o_win = jnp.einsum('bgrqk,bkgd->bqgrd', p_win.astype(vw.dtype), vw)
        return (gb[:, :, 0, ..., None] * o_cmp + gb[:, :, 1, ..., None] * o_sel
                + gb[:, :, 2, ..., None] * o_win)

    o = lax.map(block, jnp.arange(T // QB))
    o = jnp.moveaxis(o, 0, 1).reshape(B, T, H * dh).astype(x.dtype)
    return o @ w_out


def sqrelu_mlp(x, w1, w2):
    return jnp.square(jax.nn.relu(x @ w1)) @ w2


def setup_inputs(seed: int = 0) -> dict:
    key = jax.random.key(seed)
    keys = iter(jax.random.split(key, 64))

    def nrm(shape, scale):
        return jax.random.normal(next(keys), shape, F32) * scale

    def uni(shape, lo, hi):
        return jax.random.uniform(next(keys), shape, F32, lo, hi)

    D = D_MODEL
    dh = NSA_HEAD_DIM
    return {
        'x': nrm((BATCH, SEQ, D), 1.0),
        'c': nrm((BATCH, D), 1.0),
        'ada_w': nrm((DEPTH, D, N_MOD * D), 0.1 * D ** -0.5),
        'ada_b': nrm((DEPTH, N_MOD * D), 0.01),
        'norm_g': 1.0 + nrm((DEPTH, 4, D), 0.02),
        'mlp_w1': nrm((DEPTH, D, D_FF), D ** -0.5),
        'mlp_w2': nrm((DEPTH, D_FF, D), D_FF ** -0.5),
        'rwkv_mu': uni((6, D), 0.0, 1.0),
        'rwkv_w_rkv': nrm((3, D, D), D ** -0.5),
        'rwkv_w0': uni((D,), -6.0, 1.0),
        'rwkv_w_la': nrm((D, RWKV_DECAY_LORA), D ** -0.5),
        'rwkv_w_lb': nrm((RWKV_DECAY_LORA, D), 0.1 * RWKV_DECAY_LORA ** -0.5),
        'rwkv_a0': nrm((D,), 0.1),
        'rwkv_a_la': nrm((D, RWKV_ICLR_LORA), D ** -0.5),
        'rwkv_a_lb': nrm((RWKV_ICLR_LORA, D), 0.1 * RWKV_ICLR_LORA ** -0.5),
        'rwkv_g_la': nrm((D, RWKV_GATE_LORA), D ** -0.5),
        'rwkv_g_lb': nrm((RWKV_GATE_LORA, D), RWKV_GATE_LORA ** -0.5),
        'rwkv_k_k': uni((D,), 0.7, 1.0),
        'rwkv_k_a': uni((D,), 0.8, 1.2),
        'rwkv_r_k': nrm((RWKV_HEADS, RWKV_HEAD_DIM), 0.1),
        'rwkv_ln_g': 1.0 + nrm((D,), 0.02),
        'rwkv_ln_b': nrm((D,), 0.01),
        'rwkv_w_out': nrm((D, D), D ** -0.5),
        'ret_w_in': nrm((D, 6 * D), D ** -0.5),
        'ret_gn_g': 1.0 + nrm((RET_HEADS * RET_V_DIM,), 0.02),
        'ret_gn_b': nrm((RET_HEADS * RET_V_DIM,), 0.01),
        'ret_w_out': nrm((RET_HEADS * RET_V_DIM, D), (RET_HEADS * RET_V_DIM) ** -0.5),
        'conv_pw1_w': nrm((D, 2 * D), D ** -0.5),
        'conv_pw1_b': nrm((2 * D,), 0.01),
        'conv_dw_w': nrm((CONV_WIDTH, D), CONV_WIDTH ** -0.5),
        'conv_dw_b': nrm((D,), 0.01),
        'conv_ln_g': 1.0 + nrm((D,), 0.02),
        'conv_ln_b': nrm((D,), 0.01),
        'conv_pw2_w': nrm((D, D), D ** -0.5),
        'conv_pw2_b': nrm((D,), 0.01),
        'nsa_w_in': nrm((D, NSA_IN_DIM), D ** -0.5),
        'nsa_pe_k': nrm((NSA_CMP_BLOCK, dh), 0.1),
        'nsa_pe_v': nrm((NSA_CMP_BLOCK, dh), 0.1),
        'nsa_ck_w1': nrm((NSA_CMP_BLOCK * dh, NSA_CMP_HIDDEN), (NSA_CMP_BLOCK * dh) ** -0.5),
        'nsa_ck_w2': nrm((NSA_CMP_HIDDEN, dh), NSA_CMP_HIDDEN ** -0.5),
        'nsa_cv_w1': nrm((NSA_CMP_BLOCK * dh, NSA_CMP_HIDDEN), (NSA_CMP_BLOCK * dh) ** -0.5),
        'nsa_cv_w2': nrm((NSA_CMP_HIDDEN, dh), NSA_CMP_HIDDEN ** -0.5),
        'nsa_w_out': nrm((NSA_HEADS * dh, D), (NSA_HEADS * dh) ** -0.5),
    }


def reference(x, c, ada_w, ada_b, norm_g, mlp_w1, mlp_w2,
              rwkv_mu, rwkv_w_rkv, rwkv_w0, rwkv_w_la, rwkv_w_lb, rwkv_a0, rwkv_a_la, rwkv_a_lb,
              rwkv_g_la, rwkv_g_lb, rwkv_k_k, rwkv_k_a, rwkv_r_k, rwkv_ln_g, rwkv_ln_b, rwkv_w_out,
              ret_w_in, ret_gn_g, ret_gn_b, ret_w_out,
              conv_pw1_w, conv_pw1_b, conv_dw_w, conv_dw_b, conv_ln_g, conv_ln_b, conv_pw2_w, conv_pw2_b,
              nsa_w_in, nsa_pe_k, nsa_pe_v, nsa_ck_w1, nsa_ck_w2, nsa_cv_w1, nsa_cv_w2, nsa_w_out):
    cond = jax.nn.silu(c)
    for i in range(DEPTH):
        mod = (cond @ ada_w[i] + ada_b[i])[:, None, :]
        sh_t, sc_t, gt_t, sh_c, sc_c, gt_c = jnp.split(mod, N_MOD, axis=-1)
        h = rms_norm(x, norm_g[i, 0]) * (1.0 + sc_t) + sh_t
        kind = i % N_MIXERS
        if kind == 0:
            y = rwkv7_time_mix(h, rwkv_mu, rwkv_w_rkv, rwkv_w0, rwkv_w_la, rwkv_w_lb, rwkv_a0, rwkv_a_la,
                               rwkv_a_lb, rwkv_g_la, rwkv_g_lb, rwkv_k_k, rwkv_k_a, rwkv_r_k,
                               rwkv_ln_g, rwkv_ln_b, rwkv_w_out)
        elif kind == 1:
            y = retention_mix(h, ret_w_in, ret_gn_g, ret_gn_b, ret_w_out)
        elif kind == 2:
            y = conformer_conv_mix(h, conv_pw1_w, conv_pw1_b, conv_dw_w, conv_dw_b, conv_ln_g, conv_ln_b,
                                   conv_pw2_w, conv_pw2_b)
        else:
            y = nsa_mix(h, nsa_w_in, nsa_pe_k, nsa_pe_v, nsa_ck_w1, nsa_ck_w2, nsa_cv_w1, nsa_cv_w2, nsa_w_out)
        x = x + (1.0 + gt_t) * rms_norm(y, norm_g[i, 1])
        h = rms_norm(x, norm_g[i, 2]) * (1.0 + sc_c) + sh_c
        x = x + (1.0 + gt_c) * rms_norm(sqrelu_mlp(h, mlp_w1[i], mlp_w2[i]), norm_g[i, 3])
    return x
```

```python
import functools
import math

import jax
import jax.numpy as jnp
from jax import lax
from jax.experimental import pallas as pl
from jax.experimental.pallas import tpu as pltpu

F32 = jnp.float32
BF16 = jnp.bfloat16
HIGHEST = lax.Precision.HIGHEST

NORM_EPS = 1e-6
NEG_INF = -1e30
POS_BIG = 1e30

RWKV_HEAD_DIM = 64
RWKV_GN_EPS = 64e-5
RWKV_CHUNK = 64
RWKV_HEADS_PER_STEP = 4

RET_HEADS = 8
RET_CHUNK = 128
RET_GN_EPS = 1e-5
ROPE_BASE = 10000.0

CONV_WIDTH = 31
CONV_HALO = 32
CONV_LN_EPS = 1e-5

NSA_HEADS = 16
NSA_KV_GROUPS = 4
NSA_HEAD_DIM = 128
NSA_CMP_BLOCK = 32
NSA_CMP_STRIDE = 16
NSA_SEL_BLOCK = 64
NSA_SEL_TOP = 16
NSA_WINDOW = 512

VMEM_LIMIT_BYTES = 56 * 1024 * 1024


def _cparams(*sem):
    return pltpu.CompilerParams(dimension_semantics=sem, vmem_limit_bytes=VMEM_LIMIT_BYTES)


def _dot(a, b, precision=None):
    return jnp.dot(a, b, preferred_element_type=F32, precision=precision)


def _dot_nt(a, b, precision=None):
    return lax.dot_general(a, b, (((1,), (1,)), ((), ())), preferred_element_type=F32, precision=precision)


def _pick(n, pref):
    if n <= pref:
        return n
    t = pref
    while n % t:
        t //= 2
    return t


def _mm_body(*refs, n_w, n_ex, nk, epi, precision, has_a_add, a_act):
    a_ref = refs[0]
    pos = 1
    a_add_ref = None
    if has_a_add:
        a_add_ref = refs[pos]
        pos += 1
    w_refs = refs[pos:pos + n_w]
    pos += n_w
    ex_refs = refs[pos:pos + n_ex]
    pos += n_ex
    o_ref = refs[pos]
    acc_refs = refs[pos + 1:]

    a = a_ref[...]
    if a_add_ref is not None:
        a = a.astype(F32) + a_add_ref[...]
    if a_act is not None:
        a = a_act(a)
    if precision is None:
        a = a.astype(BF16)
    parts = []
    for w_ref in w_refs:
        w = w_ref[...]
        if precision is None:
            w = w.astype(BF16)
        parts.append(_dot(a, w, precision))

    if nk == 1:
        o_ref[...] = epi(parts, [e[...] for e in ex_refs]).astype(o_ref.dtype)
    else:
        k = pl.program_id(2)

        @pl.when(k == 0)
        def _():
            for acc, p in zip(acc_refs, parts):
                acc[...] = p

        @pl.when(k > 0)
        def _():
            for acc, p in zip(acc_refs, parts):
                acc[...] += p

        @pl.when(k == nk - 1)
        def _():
            o_ref[...] = epi([acc[...] for acc in acc_refs], [e[...] for e in ex_refs]).astype(o_ref.dtype)


def _first(parts, extras):
    return parts[0]


def mm(a, ws, *, n_out=None, extras=(), epi=_first, out_dtype=F32, tm=1024, tn=1024, tk=2048, precision=None,
       a_add=None, a_act=None, name="mm"):
    m, kdim = a.shape
    if not isinstance(ws, (list, tuple)):
        ws = [(ws, 0)]
    if n_out is None:
        n_out = ws[0][0].shape[1]
    tm = _pick(m, tm)
    tn = _pick(n_out, tn)
    tk = _pick(kdim, tk)
    nk = kdim // tk
    assert m % tm == 0 and n_out % tn == 0 and kdim % tk == 0
    in_specs = [pl.BlockSpec((tm, tk), lambda i, j, k: (i, k))]
    args = [a]
    if a_add is not None:
        in_specs.append(pl.BlockSpec((1, tk), lambda i, j, k: (0, k)))
        args.append(a_add)
    for w, off in ws:
        in_specs.append(pl.BlockSpec((tk, tn), lambda i, j, k, off=off: (k, j + off)))
        args.append(w.astype(BF16) if precision is None else w)
    for arr, bshape, imap in extras:
        in_specs.append(pl.BlockSpec(bshape, imap))
        args.append(arr)
    scratch = [pltpu.VMEM((tm, tn), F32) for _ in ws] if nk > 1 else []
    body = functools.partial(_mm_body, n_w=len(ws), n_ex=len(extras), nk=nk, epi=epi, precision=precision,
                             has_a_add=a_add is not None, a_act=a_act)
    return pl.pallas_call(
        body,
        grid=(m // tm, n_out // tn, nk),
        in_specs=in_specs,
        out_specs=pl.BlockSpec((tm, tn), lambda i, j, k: (i, j)),
        out_shape=jax.ShapeDtypeStruct((m, n_out), out_dtype),
        scratch_shapes=scratch,
        compiler_params=_cparams("parallel", "parallel", "arbitrary"),
        name=name,
    )(*args)


def _row_extra(vec, tn):
    return (vec.reshape(1, -1), (1, tn), lambda i, j, k: (0, j))


def _rms(x, g):
    return x * lax.rsqrt(jnp.mean(x * x, axis=-1, keepdims=True) + NORM_EPS) * g


def _resid_norm_body(*refs, has_y, has_h):
    pos = 0
    x_ref = refs[pos]; pos += 1
    if has_y:
        y_ref, gpost_ref, gate_ref = refs[pos:pos + 3]; pos += 3
    if has_h:
        gpre_ref, scale_ref, shift_ref = refs[pos:pos + 3]; pos += 3
    outs = refs[pos:]
    x = x_ref[0]
    o = 0
    if has_y:
        y = y_ref[0].astype(F32)
        x = x + (1.0 + gate_ref[0]) * _rms(y, gpost_ref[...])
        outs[o][0] = x
        o += 1
    if has_h:
        h = _rms(x, gpre_ref[...]) * (1.0 + scale_ref[0]) + shift_ref[0]
        outs[o][0] = h.astype(outs[o].dtype)


def resid_norm(x, y=None, post=None, pre=None, tt=256):
    b, t, d = x.shape
    tt = _pick(t, tt)
    row = pl.BlockSpec((1, tt, d), lambda bi, ti: (bi, ti, 0))
    vec = pl.BlockSpec((1, d), lambda bi, ti: (0, 0))
    bvec = pl.BlockSpec((1, 1, d), lambda bi, ti: (bi, 0, 0))
    args, in_specs, out_shapes, out_specs = [x], [row], [], []
    if y is not None:
        args += [y, post[0].reshape(1, d), post[1].reshape(b, 1, d)]
        in_specs += [row, vec, bvec]
        out_shapes.append(jax.ShapeDtypeStruct((b, t, d), F32))
        out_specs.append(row)
    if pre is not None:
        args += [pre[0].reshape(1, d), pre[1].reshape(b, 1, d), pre[2].reshape(b, 1, d)]
        in_specs += [vec, bvec, bvec]
        out_shapes.append(jax.ShapeDtypeStruct((b, t, d), BF16))
        out_specs.append(row)
    res = pl.pallas_call(
        functools.partial(_resid_norm_body, has_y=y is not None, has_h=pre is not None),
        grid=(b, t // tt),
        in_specs=in_specs,
        out_specs=out_specs,
        out_shape=out_shapes,
        compiler_params=_cparams("parallel", "parallel"),
        name="resid_norm",
    )(*args)
    return tuple(res)


def _relu2(parts, extras):
    r = jnp.maximum(parts[0], 0.0)
    return r * r


def sqrelu_mlp(h2d, w1, w2):
    a = mm(h2d, w1, epi=_relu2, out_dtype=BF16, name="mlp_up")
    return mm(a, w2, name="mlp_down")


def _glu(parts, extras):
    return (parts[0] + extras[0]) * jax.nn.sigmoid(parts[1] + extras[1])


def _add_bias(parts, extras):
    return parts[0] + extras[0]


def _conv_ln_body(cur_ref, prev_ref, w_ref, b_ref, g_ref, beta_ref, o_ref, buf_ref, acc_ref, *, tt, d):
    ti = pl.program_id(1)
    halo = prev_ref[0]
    buf_ref[0:CONV_HALO, :] = jnp.where(ti == 0, jnp.zeros_like(halo), halo)
    buf_ref[CONV_HALO:, :] = cur_ref[0]
    lane_chunk = min(d, 256)
    row_chunk = min(tt, 64)
    base = CONV_HALO - (CONV_WIDTH - 1)
    for c0 in range(0, d, lane_chunk):
        for r0 in range(0, tt, row_chunk):
            acc = jnp.zeros((row_chunk, lane_chunk), F32)
            for j in range(CONV_WIDTH):
                tap = w_ref[j:j + 1, c0:c0 + lane_chunk]
                acc = acc + buf_ref[base + r0 + j:base + r0 + j + row_chunk, c0:c0 + lane_chunk] * tap
            acc_ref[r0:r0 + row_chunk, c0:c0 + lane_chunk] = acc
    y = acc_ref[...] + b_ref[...]
    mu = jnp.mean(y, axis=-1, keepdims=True)
    yc = y - mu
    var = jnp.mean(yc * yc, axis=-1, keepdims=True)
    z = yc * lax.rsqrt(var + CONV_LN_EPS) * g_ref[...] + beta_ref[...]
    o_ref[0] = (z * jax.nn.sigmoid(z)).astype(o_ref.dtype)


def conv_ln_silu(u, dw_w, dw_b, ln_g, ln_b, tt=128):
    b, t, d = u.shape
    tt = _pick(t, tt)
    hb = tt // CONV_HALO
    vec = pl.BlockSpec((1, d), lambda bi, ti: (0, 0))
    return pl.pallas_call(
        functools.partial(_conv_ln_body, tt=tt, d=d),
        grid=(b, t // tt),
        in_specs=[
            pl.BlockSpec((1, tt, d), lambda bi, ti: (bi, ti, 0)),
            pl.BlockSpec((1, CONV_HALO, d), lambda bi, ti: (bi, jnp.maximum(ti * hb - 1, 0), 0)),
            pl.BlockSpec((CONV_WIDTH, d), lambda bi, ti: (0, 0)),
            vec, vec, vec,
        ],
        out_specs=pl.BlockSpec((1, tt, d), lambda bi, ti: (bi, ti, 0)),
        out_shape=jax.ShapeDtypeStruct((b, t, d), BF16),
        scratch_shapes=[pltpu.VMEM((tt + CONV_HALO, d), F32), pltpu.VMEM((tt, d), F32)],
        compiler_params=_cparams("parallel", "parallel"),
        name="conv_ln_silu",
    )(u, u, dw_w, dw_b.reshape(1, d), ln_g.reshape(1, d), ln_b.reshape(1, d))


def conformer_conv_mix(h, pw1_w, pw1_b, dw_w, dw_b, ln_g, ln_b, pw2_w, pw2_b):
    b, t, d = h.shape
    tn = _pick(d, 1024)
    b1 = pw1_b.reshape(1, -1)
    u = mm(h.reshape(b * t, d), [(pw1_w, 0), (pw1_w, d // tn)], n_out=d, tn=tn, epi=_glu,
           extras=[(b1, (1, tn), lambda i, j, k: (0, j)),
                   (b1, (1, tn), lambda i, j, k, o=d // tn: (0, j + o))], name="conf_pw1_glu")
    z = conv_ln_silu(u.reshape(b, t, d), dw_w, dw_b, ln_g, ln_b)
    y = mm(z.reshape(b * t, d), pw2_w, epi=_add_bias, extras=[_row_extra(pw2_b, _pick(d, 1024))], name="conf_pw2")
    return y.reshape(b, t, d)


def _rope(x, cos, sin):
    half = x.shape[-1] // 2
    x1, x2 = x[:, :half], x[:, half:]
    return jnp.concatenate([x1 * cos - x2 * sin, x2 * cos + x1 * sin], axis=-1)


def _retention_body(q_ref, k_ref, v_ref, gate_ref, cos_ref, sin_ref, inner_ref, qdec_ref, kdec_ref, cdec_ref,
                    gng_ref, gnb_ref, o_ref, state_ref, *, dk):
    @pl.when(pl.program_id(2) == 0)
    def _():
        state_ref[...] = jnp.zeros_like(state_ref)

    cos, sin = cos_ref[...], sin_ref[...]
    q = _rope(q_ref[0].astype(F32), cos, sin)
    k = _rope(k_ref[0].astype(F32), cos, sin) * (dk ** -0.5)
    v = v_ref[0].astype(BF16)
    qb = q.astype(BF16)
    s = _dot_nt(qb, k.astype(BF16)) * inner_ref[0]
    state = state_ref[...]
    o = _dot(s.astype(BF16), v) + _dot(qb, state.astype(BF16)) * qdec_ref[0]
    kd_t = (k * kdec_ref[0]).T.astype(BF16)
    state_ref[...] = state * cdec_ref[0] + _dot(kd_t, v)
    mu = jnp.mean(o, axis=-1, keepdims=True)
    oc = o - mu
    var = jnp.mean(oc * oc, axis=-1, keepdims=True)
    on = oc * lax.rsqrt(var + RET_GN_EPS) * gng_ref[...] + gnb_ref[...]
    gate = gate_ref[0].astype(F32)
    o_ref[0] = (gate * jax.nn.sigmoid(gate) * on).astype(o_ref.dtype)


def retention_mix(h, w_in, gn_g, gn_b, w_out):
    b, t, d = h.shape
    nh, c = RET_HEADS, RET_CHUNK
    dk = d // nh
    dv = 2 * dk
    proj = mm(h.reshape(b * t, d), w_in, name="ret_in").reshape(b, t, 6 * d)
    pos = jnp.arange(t, dtype=F32)
    inv_freq = ROPE_BASE ** (-jnp.arange(0, dk, 2, dtype=F32) / dk)
    ang = pos[:, None] * inv_freq[None, :]
    cos, sin = jnp.cos(ang), jnp.sin(ang)
    log_gamma = jnp.log(1.0 - 2.0 ** (-5.0 - jnp.arange(nh, dtype=F32)))
    idx = jnp.arange(c, dtype=F32)
    diff = idx[:, None] - idx[None, :]
    inner = jnp.where(diff >= 0, jnp.exp(jnp.maximum(diff, 0.0)[None] * log_gamma[:, None, None]), 0.0)
    q_dec = jnp.exp((idx + 1.0)[None] * log_gamma[:, None])[:, :, None]
    k_dec = jnp.exp((c - 1.0 - idx)[None] * log_gamma[:, None])[:, :, None]
    c_dec = jnp.exp(c * log_gamma)[:, None, None]
    nq = (nh * dk) // dk
    o = pl.pallas_call(
        functools.partial(_retention_body, dk=dk),
        grid=(b, nh, t // c),
        in_specs=[
            pl.BlockSpec((1, c, dk), lambda bi, hi, ci: (bi, ci, hi)),
            pl.BlockSpec((1, c, dk), lambda bi, hi, ci: (bi, ci, nq + hi)),
            pl.BlockSpec((1, c, dv), lambda bi, hi, ci: (bi, ci, nq + hi)),
            pl.BlockSpec((1, c, dv), lambda bi, hi, ci: (bi, ci, 2 * nq + hi)),
            pl.BlockSpec((c, dk // 2), lambda bi, hi, ci: (ci, 0)),
            pl.BlockSpec((c, dk // 2), lambda bi, hi, ci: (ci, 0)),
            pl.BlockSpec((1, c, c), lambda bi, hi, ci: (hi, 0, 0)),
            pl.BlockSpec((1, c, 1), lambda bi, hi, ci: (hi, 0, 0)),
            pl.BlockSpec((1, c, 1), lambda bi, hi, ci: (hi, 0, 0)),
            pl.BlockSpec((1, 1, 1), lambda bi, hi, ci: (hi, 0, 0)),
            pl.BlockSpec((1, dv), lambda bi, hi, ci: (0, hi)),
            pl.BlockSpec((1, dv), lambda bi, hi, ci: (0, hi)),
        ],
        out_specs=pl.BlockSpec((1, c, dv), lambda bi, hi, ci: (bi, ci, hi)),
        out_shape=jax.ShapeDtypeStruct((b, t, nh * dv), BF16),
        scratch_shapes=[pltpu.VMEM((dk, dv), F32)],
        compiler_params=_cparams("parallel", "parallel", "arbitrary"),
        name="retention",
    )(proj, proj, proj, proj, cos, sin, inner, q_dec, k_dec, c_dec, gn_g.reshape(1, -1), gn_b.reshape(1, -1))
    return mm(o.reshape(b * t, nh * dv), w_out, name="ret_out").reshape(b, t, d)


def _rwkv_pre_body(x_ref, prev_ref, g_ref, scale_ref, shift_ref, mu_ref, o_ref):
    ti = pl.program_id(1)
    g, scale, shift = g_ref[...], scale_ref[0], shift_ref[0]
    h = _rms(x_ref[0], g) * (1.0 + scale) + shift
    hp = _rms(prev_ref[0], g) * (1.0 + scale) + shift
    last = jnp.where(ti == 0, 0.0, hp[7:8, :])
    row = lax.broadcasted_iota(jnp.int32, h.shape, 0)
    shifted = jnp.where(row == 0, last, pltpu.roll(h, 1, axis=0))
    xx = shifted - h
    for s in range(6):
        o_ref[s, 0] = (h + xx * mu_ref[s:s + 1, :]).astype(o_ref.dtype)


def rwkv_pre(x, g_pre, scale, shift, mu, tt=256):
    b, t, d = x.shape
    tt = _pick(t, tt)
    bvec = pl.BlockSpec((1, 1, d), lambda bi, ti: (bi, 0, 0))
    return pl.pallas_call(
        _rwkv_pre_body,
        grid=(b, t // tt),
        in_specs=[
            pl.BlockSpec((1, tt, d), lambda bi, ti: (bi, ti, 0)),
            pl.BlockSpec((1, 8, d), lambda bi, ti: (bi, jnp.maximum(ti * (tt // 8) - 1, 0), 0)),
            pl.BlockSpec((1, d), lambda bi, ti: (0, 0)),
            bvec, bvec,
            pl.BlockSpec((6, d), lambda bi, ti: (0, 0)),
        ],
        out_specs=pl.BlockSpec((6, 1, tt, d), lambda bi, ti: (0, bi, ti, 0)),
        out_shape=jax.ShapeDtypeStruct((6, b, t, d), BF16),
        compiler_params=_cparams("parallel", "parallel"),
        name="rwkv_pre",
    )(x, x, g_pre.reshape(1, d), scale.reshape(b, 1, d), shift.reshape(b, 1, d), mu)


def _rwkv_scan_body(r_ref, k_ref, v_ref, a_ref, lw_ref, g_ref, kk_ref, ka_ref, rk_ref, lng_ref, lnb_ref, o_ref,
                    state_ref, *, chunk, heads, n):
    @pl.when(pl.program_id(2) == 0)
    def _():
        state_ref[...] = jnp.zeros_like(state_ref)

    ln = chunk
    row = lax.broadcasted_iota(jnp.int32, (ln, ln), 0)
    col = lax.broadcasted_iota(jnp.int32, (ln, ln), 1)
    incl = col <= row
    strict = col < row
    tri = incl.astype(BF16)
    eye = (row == col).astype(F32)
    outs = []
    for hd in range(heads):
        sl = slice(hd * n, (hd + 1) * n)
        r, k, v, a, lw = r_ref[0, :, sl], k_ref[0, :, sl], v_ref[0, :, sl], a_ref[0, :, sl], lw_ref[0, :, sl]
        kk = k * kk_ref[:, sl]
        kk = kk / jnp.maximum(jnp.sqrt(jnp.sum(kk * kk, axis=-1, keepdims=True)), 1e-12)
        k2 = k * (1.0 + (a - 1.0) * ka_ref[:, sl])
        beta = kk * a
        lw_hi = lw.astype(BF16)
        lw_lo = (lw - lw_hi.astype(F32)).astype(BF16)
        c = _dot(tri, lw_hi) + _dot(tri, lw_lo)
        c_last = c[ln - 1:ln, :]
        e_neg = jnp.exp(-c)
        e_rem = jnp.exp(c_last - c)
        a_t = -kk * jnp.exp(c - lw)
        r_t = r * jnp.exp(c)
        ar = jnp.concatenate([a_t, r_t], axis=0).astype(BF16)
        bk = jnp.concatenate([beta * e_neg, k2 * e_neg], axis=0).astype(BF16)
        p = _dot_nt(ar, bk)
        state = state_ref[hd]
        q0 = _dot_nt(ar, state.astype(BF16))
        m_ab = jnp.where(strict, p[:ln, :ln], 0.0)
        m_ak = jnp.where(strict, p[:ln, ln:], 0.0)
        m_rb = jnp.where(incl, p[ln:, :ln], 0.0)
        m_rk = jnp.where(incl, p[ln:, ln:], 0.0)
        inv = eye + m_ab
        pw = m_ab.astype(BF16)
        for _ in range(int(math.log2(ln)) - 1):
            pw = _dot(pw, pw).astype(BF16)
            inv = inv + _dot(inv.astype(BF16), pw)
        vb = v.astype(BF16)
        rhs = q0[:ln] + _dot(m_ak.astype(BF16), vb)
        u = _dot(inv.astype(BF16), rhs.astype(BF16))
        uv = jnp.concatenate([u, v], axis=0)
        y = q0[ln:] + _dot(jnp.concatenate([m_rb, m_rk], axis=1).astype(BF16), uv.astype(BF16))
        bk_rem = jnp.concatenate([beta * e_rem, k2 * e_rem], axis=0).astype(BF16)
        state_ref[hd] = state * jnp.exp(c_last) + _dot(uv.T.astype(BF16), bk_rem)
        mu = jnp.mean(y, axis=-1, keepdims=True)
        yc = y - mu
        var = jnp.mean(yc * yc, axis=-1, keepdims=True)
        yn = yc * lax.rsqrt(var + RWKV_GN_EPS) * lng_ref[:, sl] + lnb_ref[:, sl]
        bonus = jnp.sum(r * k2 * rk_ref[:, sl], axis=-1, keepdims=True) * v
        outs.append((yn + bonus) * g_ref[0, :, sl])
    o_ref[0] = jnp.concatenate(outs, axis=-1).astype(o_ref.dtype)


def rwkv_scan(r, k, v, a, lw, g, k_k, k_a, r_k, ln_g, ln_b):
    b, t, d = r.shape
    n = RWKV_HEAD_DIM
    hps = min(RWKV_HEADS_PER_STEP, d // n)
    w = hps * n
    ln = _pick(t, RWKV_CHUNK)
    tok = pl.BlockSpec((1, ln, w), lambda bi, hi, ci: (bi, ci, hi))
    vec = pl.BlockSpec((1, w), lambda bi, hi, ci: (0, hi))
    return pl.pallas_call(
        functools.partial(_rwkv_scan_body, chunk=ln, heads=hps, n=n),
        grid=(b, d // w, t // ln),
        in_specs=[tok] * 6 + [vec] * 5,
        out_specs=tok,
        out_shape=jax.ShapeDtypeStruct((b, t, d), BF16),
        scratch_shapes=[pltpu.VMEM((hps, n, n), F32)],
        compiler_params=_cparams("parallel", "parallel", "arbitrary"),
        name="rwkv_scan",
    )(r, k, v, a, lw, g, k_k.reshape(1, d), k_a.reshape(1, d), r_k.reshape(1, d), ln_g.reshape(1, d),
      ln_b.reshape(1, d))


def _tanh_epi(parts, extras):
    return jnp.tanh(parts[0])


def _sigmoid_epi(parts, extras):
    return jax.nn.sigmoid(parts[0])


def _sigmoid_bias_epi(parts, extras):
    return jax.nn.sigmoid(extras[0] + parts[0])


def _logdecay_epi(parts, extras):
    return -jnp.exp(-jax.nn.softplus(-(extras[0] + parts[0])) - 0.5)


def rwkv7_time_mix(x, g_pre, scale, shift, mu, w_rkv, w0, w_la, w_lb, a0, a_la, a_lb, g_la, g_lb, k_k, k_a, r_k,
                   ln_g, ln_b, w_out):
    b, t, d = x.shape
    xs = rwkv_pre(x, g_pre, scale, shift, mu).reshape(6, b * t, d)
    tn = _pick(d, 1024)
    r = mm(xs[0], w_rkv[0], name="rwkv_r")
    k = mm(xs[1], w_rkv[1], name="rwkv_k")
    v = mm(xs[2], w_rkv[2], name="rwkv_v")
    lw = mm(mm(xs[3], w_la, epi=_tanh_epi, out_dtype=BF16, name="rwkv_w_la"), w_lb, epi=_logdecay_epi,
            extras=[_row_extra(w0, tn)], name="rwkv_w_lb")
    a = mm(mm(xs[4], a_la, out_dtype=BF16, name="rwkv_a_la"), a_lb, epi=_sigmoid_bias_epi,
           extras=[_row_extra(a0, tn)], name="rwkv_a_lb")
    g = mm(mm(xs[5], g_la, epi=_sigmoid_epi, out_dtype=BF16, name="rwkv_g_la"), g_lb, name="rwkv_g_lb")
    sh = lambda z: z.reshape(b, t, d)
    z = rwkv_scan(sh(r), sh(k), sh(v), sh(a), sh(lw), sh(g), k_k, k_a, r_k, ln_g, ln_b)
    return mm(z.reshape(b * t, d), w_out, name="rwkv_out").reshape(b, t, d)


def _gelu_epi(parts, extras):
    return jax.nn.gelu(parts[0])


def nsa_compress(kv, pe, w1, w2):
    b, t, g, dh = kv.shape
    ch = kv.reshape(b, t // NSA_CMP_STRIDE, NSA_CMP_STRIDE, g, dh)
    blocks = jnp.concatenate([ch[:, :-1], ch[:, 1:]], axis=2)
    nc = blocks.shape[1]
    flat = jnp.swapaxes(blocks, 2, 3).reshape(b * nc * g, NSA_CMP_BLOCK * dh)
    rows = flat.shape[0]
    pad = (-rows) % 8
    flat = jnp.pad(flat, ((0, pad), (0, 0)))
    pe_row = pe.reshape(1, -1)
    hid = mm(flat, w1, a_add=pe_row, epi=_gelu_epi, out_dtype=BF16, tm=512, name="nsa_cmp_w1")
    out = mm(hid, w2, tm=512, name="nsa_cmp_w2")[:rows].reshape(b, nc, g, dh)
    out = jnp.transpose(out, (0, 2, 1, 3))
    nc_pad = t // NSA_CMP_STRIDE
    return jnp.pad(out, ((0, 0), (0, 0), (0, nc_pad - nc), (0, 0)))


def _stack_heads(q, r, dh):
    return jnp.concatenate([q[:, i * dh:(i + 1) * dh] for i in range(r)], axis=0)


def _unstack_heads(o, r, tq):
    return jnp.concatenate([o[i * tq:(i + 1) * tq] for i in range(r)], axis=-1)


def _gate_rows(gates, branch, r, tq):
    return jnp.concatenate([gates[:, branch * r + i:branch * r + i + 1] for i in range(r)], axis=0)


def _nsa_cmp_body(q_ref, kc_ref, vc_ref, gates_ref, selmap_ref, o_ref, sel_ref, *, tq, r, dh, n_sel):
    q0 = pl.program_id(2) * tq
    qs = (_stack_heads(q_ref[0], r, dh) * (dh ** -0.5)).astype(F32)
    kc, vc = kc_ref[0, 0], vc_ref[0, 0]
    ncp = kc.shape[0]
    s = _dot_nt(qs, kc, HIGHEST)
    rows = lax.broadcasted_iota(jnp.int32, (r * tq, ncp), 0)
    cmp_end = lax.broadcasted_iota(jnp.int32, (r * tq, ncp), 1) * NSA_CMP_STRIDE + (NSA_CMP_BLOCK - 1)
    t_pos = q0 + rows % tq
    vis = cmp_end <= t_pos
    s = jnp.where(vis, s, NEG_INF)
    e = jnp.exp(s - jnp.max(s, axis=-1, keepdims=True))
    p = e / jnp.sum(e, axis=-1, keepdims=True) * vis.astype(F32)
    o = _dot(p.astype(BF16), vc.astype(BF16))
    o = o * _gate_rows(gates_ref[0, 0], 0, r, tq)
    o_ref[0] = _unstack_heads(o, r, tq)
    p_sum = p[0:tq]
    for i in range(1, r):
        p_sum = p_sum + p[i * tq:(i + 1) * tq]
    imp = _dot(p_sum, selmap_ref[...], HIGHEST)
    ns = imp.shape[1]
    blk = lax.broadcasted_iota(jnp.int32, (tq, ns), 1)
    tq_pos = q0 + lax.broadcasted_iota(jnp.int32, (tq, ns), 0)
    cur = tq_pos // NSA_SEL_BLOCK
    valid = blk * NSA_SEL_BLOCK <= tq_pos
    forced = (blk == 0) | (blk == cur) | (blk == cur - 1)
    score = jnp.where(valid, jnp.where(forced, POS_BIG, imp), NEG_INF)
    rank = jnp.zeros((tq, ns), F32)
    for m in range(ns):
        cm = score[:, m:m + 1]
        beats = (cm > score) | ((cm == score) & (blk > m))
        rank = rank + beats.astype(F32)
    sel_ref[0, 0] = ((rank < n_sel) & valid).astype(F32)


def _flash_update(s, v, m_ref, l_ref, acc_ref):
    m_old = m_ref[...]
    m_new = jnp.maximum(m_old, jnp.max(s, axis=-1, keepdims=True))
    alpha = jnp.exp(m_old - m_new)
    p = jnp.exp(s - m_new)
    l_ref[...] = alpha * l_ref[...] + jnp.sum(p, axis=-1, keepdims=True)
    acc_ref[...] = alpha * acc_ref[...] + _dot(p.astype(BF16), v)
    m_ref[...] = m_new


def _nsa_attn_body(q_ref, k_ref, v_ref, gates_ref, oin_ref, *rest, tq, tk, r, dh, mode, nk):
    if mode == "sel":
        sel_ref, o_ref, m_ref, l_ref, acc_ref = rest
    else:
        o_ref, m_ref, l_ref, acc_ref = rest
    qi, ki = pl.program_id(2), pl.program_id(3)
    q0 = qi * tq

    @pl.when(ki == 0)
    def _():
        m_ref[...] = jnp.full_like(m_ref, NEG_INF)
        l_ref[...] = jnp.zeros_like(l_ref)
        acc_ref[...] = jnp.zeros_like(acc_ref)

    if mode == "sel":
        k0 = ki * tk
        active = k0 <= q0 + tq - 1
    else:
        kt = qi - (nk - 1) + ki
        k0 = kt * tk
        active = kt >= 0

    @pl.when(active)
    def _():
        qs = (_stack_heads(q_ref[0], r, dh) * (dh ** -0.5)).astype(BF16)
        s = _dot_nt(qs, k_ref[0].astype(BF16))
        rows = lax.broadcasted_iota(jnp.int32, (r * tq, tk), 0)
        t_pos = q0 + rows % tq
        k_pos = k0 + lax.broadcasted_iota(jnp.int32, (r * tq, tk), 1)
        mask = k_pos <= t_pos
        if mode == "sel":
            ns = sel_ref.shape[-1]
            blk_of_key = (k0 + lax.broadcasted_iota(jnp.int32, (ns, tk), 1)) // NSA_SEL_BLOCK
            expand = (blk_of_key == lax.broadcasted_iota(jnp.int32, (ns, tk), 0)).astype(BF16)
            chosen = _dot(sel_ref[0, 0].astype(BF16), expand)
            chosen = jnp.concatenate([chosen] * r, axis=0)
            mask = mask & (chosen > 0.5)
        else:
            mask = mask & (k_pos > t_pos - NSA_WINDOW)
        _flash_update(jnp.where(mask, s, NEG_INF), v_ref[0].astype(BF16), m_ref, l_ref, acc_ref)

    @pl.when(ki == nk - 1)
    def _():
        branch = 1 if mode == "sel" else 2
        o = acc_ref[...] / l_ref[...] * _gate_rows(gates_ref[0, 0], branch, r, tq)
        o_ref[0] = (oin_ref[0] + _unstack_heads(o, r, tq)).astype(o_ref.dtype)


def nsa_mix(h, w_in, pe_k, pe_v, ck_w1, ck_w2, cv_w1, cv_w2, w_out, tq=256):
    b, t, d = h.shape
    nh, g, dh = NSA_HEADS, NSA_KV_GROUPS, NSA_HEAD_DIM
    r = nh // g
    hd, kd = nh * dh, g * dh
    n_main = hd + 6 * kd
    h2 = h.reshape(b * t, d)
    proj = mm(h2, w_in[:, :n_main], name="nsa_in").reshape(b, t, n_main)
    n_gate = w_in.shape[1] - n_main
    w_gate = jnp.pad(w_in[:, n_main:], ((0, 0), (0, 128 - n_gate)))
    gates = mm(h2, w_gate, epi=_sigmoid_epi, name="nsa_gates")[:, :n_gate]
    gates = jnp.transpose(gates.reshape(b, t, 3, g, r), (0, 3, 1, 2, 4)).reshape(b, g, t, 3 * r)
    k_cmp = nsa_compress(proj[..., hd:hd + kd].reshape(b, t, g, dh), pe_k, ck_w1, ck_w2)
    v_cmp = nsa_compress(proj[..., hd + kd:hd + 2 * kd].reshape(b, t, g, dh), pe_v, cv_w1, cv_w2)
    ncp = k_cmp.shape[2]
    ns = t // NSA_SEL_BLOCK
    n_sel = min(NSA_SEL_TOP, ns)
    cs = jnp.arange(ncp)[:, None] * NSA_CMP_STRIDE
    ss = jnp.arange(ns)[None, :] * NSA_SEL_BLOCK
    sel_map = (jnp.maximum(jnp.minimum(cs + NSA_CMP_BLOCK, ss + NSA_SEL_BLOCK) - jnp.maximum(cs, ss), 0).astype(F32)
               / NSA_CMP_BLOCK)
    tq = _pick(t, tq)
    nq = t // tq
    wq = r * dh
    q_spec3 = pl.BlockSpec((1, tq, wq), lambda bi, gi, qi: (bi, qi, gi))
    o_cmp, sel = pl.pallas_call(
        functools.partial(_nsa_cmp_body, tq=tq, r=r, dh=dh, n_sel=n_sel),
        grid=(b, g, nq),
        in_specs=[
            q_spec3,
            pl.BlockSpec((1, 1, ncp, dh), lambda bi, gi, qi: (bi, gi, 0, 0)),
            pl.BlockSpec((1, 1, ncp, dh), lambda bi, gi, qi: (bi, gi, 0, 0)),
            pl.BlockSpec((1, 1, tq, 3 * r), lambda bi, gi, qi: (bi, gi, qi, 0)),
            pl.BlockSpec((ncp, ns), lambda bi, gi, qi: (0, 0)),
        ],
        out_specs=[q_spec3, pl.BlockSpec((1, 1, tq, ns), lambda bi, gi, qi: (bi, gi, qi, 0))],
        out_shape=[jax.ShapeDtypeStruct((b, t, hd), F32), jax.ShapeDtypeStruct((b, g, t, ns), F32)],
        compiler_params=_cparams("parallel", "parallel", "parallel"),
        name="nsa_cmp",
    )(proj, k_cmp, v_cmp, gates, sel_map)

    tk = tq
    q_spec = pl.BlockSpec((1, tq, wq), lambda bi, gi, qi, ki: (bi, qi, gi))
    gate_spec = pl.BlockSpec((1, 1, tq, 3 * r), lambda bi, gi, qi, ki: (bi, gi, qi, 0))
    scratch = [pltpu.VMEM((r * tq, 1), F32), pltpu.VMEM((r * tq, 1), F32), pltpu.VMEM((r * tq, dh), F32)]
    col = lambda base: (base // dh)

    def kv_spec(base, kt_of):
        return pl.BlockSpec((1, tk, dh), lambda bi, gi, qi, ki: (bi, kt_of(qi, ki), col(base) + gi))

    sel_kt = lambda qi, ki: jnp.minimum(ki, qi)
    o_sel = pl.pallas_call(
        functools.partial(_nsa_attn_body, tq=tq, tk=tk, r=r, dh=dh, mode="sel", nk=nq),
        grid=(b, g, nq, nq),
        in_specs=[q_spec, kv_spec(hd + 2 * kd, sel_kt), kv_spec(hd + 3 * kd, sel_kt), gate_spec, q_spec,
                  pl.BlockSpec((1, 1, tq, ns), lambda bi, gi, qi, ki: (bi, gi, qi, 0))],
        out_specs=q_spec,
        out_shape=jax.ShapeDtypeStruct((b, t, hd), F32),
        scratch_shapes=scratch,
        compiler_params=_cparams("parallel", "parallel", "parallel", "arbitrary"),
        name="nsa_sel",
    )(proj, proj, proj, gates, o_cmp, sel)

    nwin = (NSA_WINDOW - 1 + tq - 1) // tq + 1
    win_kt = lambda qi, ki: jnp.maximum(qi - (nwin - 1) + ki, 0)
    o_all = pl.pallas_call(
        functools.partial(_nsa_attn_body, tq=tq, tk=tk, r=r, dh=dh, mode="win", nk=nwin),
        grid=(b, g, nq, nwin),
        in_specs=[q_spec, kv_spec(hd + 4 * kd, win_kt), kv_spec(hd + 5 * kd, win_kt), gate_spec, q_spec],
        out_specs=q_spec,
        out_shape=jax.ShapeDtypeStruct((b, t, hd), BF16),
        scratch_shapes=scratch,
        compiler_params=_cparams("parallel", "parallel", "parallel", "arbitrary"),
        name="nsa_win",
    )(proj, proj, proj, gates, o_sel)
    return mm(o_all.reshape(b * t, hd), w_out, name="nsa_out").reshape(b, t, d)


def _silu(a):
    return a * jax.nn.sigmoid(a)


def _ada_epi(parts, extras):
    return parts[0] + extras[0]


def ada_modulation(c, ada_w, ada_b):
    depth, d, n6 = ada_w.shape
    b = c.shape[0]
    rows = ((b + 7) // 8) * 8
    cond = jnp.pad(c, ((0, rows - b), (0, 0)))
    mods = []
    for i in range(depth):
        mods.append(mm(cond, ada_w[i], a_act=_silu, epi=_ada_epi, extras=[_row_extra(ada_b[i], _pick(n6, 1024))],
                       precision=HIGHEST, name="ada_mod")[:b])
    return jnp.stack(mods)


def kernel(x, c, ada_w, ada_b, norm_g, mlp_w1, mlp_w2, rwkv_mu, rwkv_w_rkv, rwkv_w0, rwkv_w_la, rwkv_w_lb, rwkv_a0, rwkv_a_la, rwkv_a_lb, rwkv_g_la, rwkv_g_lb, rwkv_k_k, rwkv_k_a, rwkv_r_k, rwkv_ln_g, rwkv_ln_b, rwkv_w_out, ret_w_in, ret_gn_g, ret_gn_b, ret_w_out, conv_pw1_w, conv_pw1_b, conv_dw_w, conv_dw_b, conv_ln_g, conv_ln_b, conv_pw2_w, conv_pw2_b, nsa_w_in, nsa_pe_k, nsa_pe_v, nsa_ck_w1, nsa_ck_w2, nsa_cv_w1, nsa_cv_w2, nsa_w_out):
    b, t, d = x.shape
    depth = ada_w.shape[0]
    mod = ada_modulation(c, ada_w, ada_b).reshape(depth, b, 6, d)
    h = None
    for i in range(depth):
        sh_t, sc_t, gt_t, sh_c, sc_c, gt_c = (mod[i, :, j] for j in range(6))
        kind = i % 4
        if kind == 0:
            y = rwkv7_time_mix(x, norm_g[i, 0], sc_t, sh_t, rwkv_mu, rwkv_w_rkv, rwkv_w0, rwkv_w_la, rwkv_w_lb,
                               rwkv_a0, rwkv_a_la, rwkv_a_lb, rwkv_g_la, rwkv_g_lb, rwkv_k_k, rwkv_k_a,
                               rwkv_r_k.reshape(-1), rwkv_ln_g, rwkv_ln_b, rwkv_w_out)
        else:
            if h is None:
                (h,) = resid_norm(x, pre=(norm_g[i, 0], sc_t, sh_t))
            if kind == 1:
                y = retention_mix(h, ret_w_in, ret_gn_g, ret_gn_b, ret_w_out)
            elif kind == 2:
                y = conformer_conv_mix(h, conv_pw1_w, conv_pw1_b, conv_dw_w, conv_dw_b, conv_ln_g, conv_ln_b,
                                       conv_pw2_w, conv_pw2_b)
            else:
                y = nsa_mix(h, nsa_w_in, nsa_pe_k, nsa_pe_v, nsa_ck_w1, nsa_ck_w2, nsa_cv_w1, nsa_cv_w2, nsa_w_out)
        x, h = resid_norm(x, y, post=(norm_g[i, 1], gt_t), pre=(norm_g[i, 2], sc_c, sh_c))
        y = sqrelu_mlp(h.reshape(b * t, d), mlp_w1[i], mlp_w2[i]).reshape(b, t, d)
        nxt = i + 1
        if nxt < depth and nxt % 4 != 0:
            x, h = resid_norm(x, y, post=(norm_g[i, 3], gt_c),
                              pre=(norm_g[nxt, 0], mod[nxt, :, 1], mod[nxt, :, 0]))
        else:
            (x,) = resid_norm(x, y, post=(norm_g[i, 3], gt_c))
            h = None
    return x
```

```python
import functools
import math

import jax
import jax.numpy as jnp
from jax import lax
from jax.experimental import pallas as pl
from jax.experimental.pallas import tpu as pltpu

F32 = jnp.float32
BF16 = jnp.bfloat16
HIGHEST = lax.Precision.HIGHEST

NORM_EPS = 1e-6
NEG_INF = -1e30
POS_BIG = 1e30

RWKV_HEAD_DIM = 64
RWKV_GN_EPS = 64e-5
RWKV_CHUNK = 64
RWKV_HEADS_PER_STEP = 8

RET_HEADS = 8
RET_CHUNK = 128
RET_GN_EPS = 1e-5
ROPE_BASE = 10000.0

CONV_WIDTH = 31
CONV_HALO = 32
CONV_LN_EPS = 1e-5

NSA_HEADS = 16
NSA_KV_GROUPS = 4
NSA_HEAD_DIM = 128
NSA_CMP_BLOCK = 32
NSA_CMP_STRIDE = 16
NSA_SEL_BLOCK = 64
NSA_SEL_TOP = 16
NSA_WINDOW = 512

VMEM_LIMIT_BYTES = 56 * 1024 * 1024


def _cparams(*sem):
    return pltpu.CompilerParams(dimension_semantics=sem, vmem_limit_bytes=VMEM_LIMIT_BYTES)


def _dot(a, b, precision=None):
    return jnp.dot(a, b, preferred_element_type=F32, precision=precision)


def _dot_nt(a, b, precision=None):
    return lax.dot_general(a, b, (((1,), (1,)), ((), ())), preferred_element_type=F32, precision=precision)


def _pick(n, pref):
    if n <= pref:
        return n
    t = pref
    while n % t:
        t //= 2
    return t


def _mm_body(*refs, n_w, n_ex, nk, epi, precision, has_a_add, a_act):
    a_ref = refs[0]
    pos = 1
    a_add_ref = None
    if has_a_add:
        a_add_ref = refs[pos]
        pos += 1
    w_refs = refs[pos:pos + n_w]
    pos += n_w
    ex_refs = refs[pos:pos + n_ex]
    pos += n_ex
    o_ref = refs[pos]
    acc_refs = refs[pos + 1:]

    a = a_ref[...]
    if a_add_ref is not None:
        a = a.astype(F32) + a_add_ref[...]
    if a_act is not None:
        a = a_act(a)
    if precision is None:
        a = a.astype(BF16)
    parts = []
    for w_ref in w_refs:
        w = w_ref[...]
        if precision is None:
            w = w.astype(BF16)
        parts.append(_dot(a, w, precision))

    if nk == 1:
        o_ref[...] = epi(parts, [e[...] for e in ex_refs]).astype(o_ref.dtype)
    else:
        k = pl.program_id(2)

        @pl.when(k == 0)
        def _():
            for acc, p in zip(acc_refs, parts):
                acc[...] = p

        @pl.when(k > 0)
        def _():
            for acc, p in zip(acc_refs, parts):
                acc[...] += p

        @pl.when(k == nk - 1)
        def _():
            o_ref[...] = epi([acc[...] for acc in acc_refs], [e[...] for e in ex_refs]).astype(o_ref.dtype)


def _first(parts, extras):
    return parts[0]


def mm(a, ws, *, n_out=None, extras=(), epi=_first, out_dtype=F32, tm=1024, tn=1024, tk=2048, precision=None,
       a_add=None, a_act=None, name="mm"):
    m, kdim = a.shape
    if not isinstance(ws, (list, tuple)):
        ws = [(ws, 0)]
    if n_out is None:
        n_out = ws[0][0].shape[1]
    tm = _pick(m, tm)
    tn = _pick(n_out, tn)
    tk = _pick(kdim, tk)
    nk = kdim // tk
    assert m % tm == 0 and n_out % tn == 0 and kdim % tk == 0
    in_specs = [pl.BlockSpec((tm, tk), lambda i, j, k: (i, k))]
    args = [a]
    if a_add is not None:
        in_specs.append(pl.BlockSpec((1, tk), lambda i, j, k: (0, k)))
        args.append(a_add)
    for w, off in ws:
        in_specs.append(pl.BlockSpec((tk, tn), lambda i, j, k, off=off: (k, j + off)))
        args.append(w.astype(BF16) if precision is None else w)
    for arr, bshape, imap in extras:
        in_specs.append(pl.BlockSpec(bshape, imap))
        args.append(arr)
    scratch = [pltpu.VMEM((tm, tn), F32) for _ in ws] if nk > 1 else []
    body = functools.partial(_mm_body, n_w=len(ws), n_ex=len(extras), nk=nk, epi=epi, precision=precision,
                             has_a_add=a_add is not None, a_act=a_act)
    return pl.pallas_call(
        body,
        grid=(m // tm, n_out // tn, nk),
        in_specs=in_specs,
        out_specs=pl.BlockSpec((tm, tn), lambda i, j, k: (i, j)),
        out_shape=jax.ShapeDtypeStruct((m, n_out), out_dtype),
        scratch_shapes=scratch,
        compiler_params=_cparams("parallel", "parallel", "arbitrary"),
        name=name,
    )(*args)


def _row_extra(vec, tn):
    return (vec.reshape(1, -1), (1, tn), lambda i, j, k: (0, j))


def _rms(x, g):
    return x * lax.rsqrt(jnp.mean(x * x, axis=-1, keepdims=True) + NORM_EPS) * g


def _resid_norm_body(*refs, has_y, has_h):
    pos = 0
    x_ref = refs[pos]; pos += 1
    if has_y:
        y_ref, gpost_ref, gate_ref = refs[pos:pos + 3]; pos += 3
    if has_h:
        gpre_ref, scale_ref, shift_ref = refs[pos:pos + 3]; pos += 3
    outs = refs[pos:]
    x = x_ref[0]
    o = 0
    if has_y:
        y = y_ref[0].astype(F32)
        x = x + (1.0 + gate_ref[0]) * _rms(y, gpost_ref[...])
        outs[o][0] = x
        o += 1
    if has_h:
        h = _rms(x, gpre_ref[...]) * (1.0 + scale_ref[0]) + shift_ref[0]
        outs[o][0] = h.astype(outs[o].dtype)


def resid_norm(x, y=None, post=None, pre=None, tt=256):
    b, t, d = x.shape
    tt = _pick(t, tt)
    row = pl.BlockSpec((1, tt, d), lambda bi, ti: (bi, ti, 0))
    vec = pl.BlockSpec((1, d), lambda bi, ti: (0, 0))
    bvec = pl.BlockSpec((1, 1, d), lambda bi, ti: (bi, 0, 0))
    args, in_specs, out_shapes, out_specs = [x], [row], [], []
    if y is not None:
        args += [y, post[0].reshape(1, d), post[1].reshape(b, 1, d)]
        in_specs += [row, vec, bvec]
        out_shapes.append(jax.ShapeDtypeStruct((b, t, d), F32))
        out_specs.append(row)
    if pre is not None:
        args += [pre[0].reshape(1, d), pre[1].reshape(b, 1, d), pre[2].reshape(b, 1, d)]
        in_specs += [vec, bvec, bvec]
        out_shapes.append(jax.ShapeDtypeStruct((b, t, d), BF16))
        out_specs.append(row)
    res = pl.pallas_call(
        functools.partial(_resid_norm_body, has_y=y is not None, has_h=pre is not None),
        grid=(b, t // tt),
        in_specs=in_specs,
        out_specs=out_specs,
        out_shape=out_shapes,
        compiler_params=_cparams("parallel", "parallel"),
        name="resid_norm",
    )(*args)
    return tuple(res)


def _relu2(parts, extras):
    r = jnp.maximum(parts[0], 0.0)
    return r * r


def sqrelu_mlp(h2d, w1, w2):
    a = mm(h2d, w1, epi=_relu2, out_dtype=BF16, name="mlp_up")
    return mm(a, w2, name="mlp_down")


def _glu(parts, extras):
    return (parts[0] + extras[0]) * jax.nn.sigmoid(parts[1] + extras[1])


def _add_bias(parts, extras):
    return parts[0] + extras[0]


def _conv_ln_body(cur_ref, prev_ref, w_ref, b_ref, g_ref, beta_ref, o_ref, buf_ref, acc_ref, *, tt, d):
    ti = pl.program_id(1)
    halo = prev_ref[0]
    buf_ref[0:CONV_HALO, :] = jnp.where(ti == 0, jnp.zeros_like(halo), halo)
    buf_ref[CONV_HALO:, :] = cur_ref[0]
    lane_chunk = min(d, 256)
    row_chunk = min(tt, 64)
    base = CONV_HALO - (CONV_WIDTH - 1)
    for c0 in range(0, d, lane_chunk):
        for r0 in range(0, tt, row_chunk):
            acc = jnp.zeros((row_chunk, lane_chunk), F32)
            for j in range(CONV_WIDTH):
                tap = w_ref[j:j + 1, c0:c0 + lane_chunk]
                acc = acc + buf_ref[base + r0 + j:base + r0 + j + row_chunk, c0:c0 + lane_chunk] * tap
            acc_ref[r0:r0 + row_chunk, c0:c0 + lane_chunk] = acc
    y = acc_ref[...] + b_ref[...]
    mu = jnp.mean(y, axis=-1, keepdims=True)
    yc = y - mu
    var = jnp.mean(yc * yc, axis=-1, keepdims=True)
    z = yc * lax.rsqrt(var + CONV_LN_EPS) * g_ref[...] + beta_ref[...]
    o_ref[0] = (z * jax.nn.sigmoid(z)).astype(o_ref.dtype)


def conv_ln_silu(u, dw_w, dw_b, ln_g, ln_b, tt=128):
    b, t, d = u.shape
    tt = _pick(t, tt)
    hb = tt // CONV_HALO
    vec = pl.BlockSpec((1, d), lambda bi, ti: (0, 0))
    return pl.pallas_call(
        functools.partial(_conv_ln_body, tt=tt, d=d),
        grid=(b, t // tt),
        in_specs=[
            pl.BlockSpec((1, tt, d), lambda bi, ti: (bi, ti, 0)),
            pl.BlockSpec((1, CONV_HALO, d), lambda bi, ti: (bi, jnp.maximum(ti * hb - 1, 0), 0)),
            pl.BlockSpec((CONV_WIDTH, d), lambda bi, ti: (0, 0)),
            vec, vec, vec,
        ],
        out_specs=pl.BlockSpec((1, tt, d), lambda bi, ti: (bi, ti, 0)),
        out_shape=jax.ShapeDtypeStruct((b, t, d), BF16),
        scratch_shapes=[pltpu.VMEM((tt + CONV_HALO, d), F32), pltpu.VMEM((tt, d), F32)],
        compiler_params=_cparams("parallel", "parallel"),
        name="conv_ln_silu",
    )(u, u, dw_w, dw_b.reshape(1, d), ln_g.reshape(1, d), ln_b.reshape(1, d))


def conformer_conv_mix(h, pw1_w, pw1_b, dw_w, dw_b, ln_g, ln_b, pw2_w, pw2_b):
    b, t, d = h.shape
    tn = _pick(d, 1024)
    b1 = pw1_b.reshape(1, -1)
    u = mm(h.reshape(b * t, d), [(pw1_w, 0), (pw1_w, d // tn)], n_out=d, tn=tn, epi=_glu,
           extras=[(b1, (1, tn), lambda i, j, k: (0, j)),
                   (b1, (1, tn), lambda i, j, k, o=d // tn: (0, j + o))], name="conf_pw1_glu")
    z = conv_ln_silu(u.reshape(b, t, d), dw_w, dw_b, ln_g, ln_b)
    y = mm(z.reshape(b * t, d), pw2_w, epi=_add_bias, extras=[_row_extra(pw2_b, _pick(d, 1024))], name="conf_pw2")
    return y.reshape(b, t, d)


def _rope(x, cos, sin):
    half = x.shape[-1] // 2
    x1, x2 = x[:, :half], x[:, half:]
    return jnp.concatenate([x1 * cos - x2 * sin, x2 * cos + x1 * sin], axis=-1)


def _retention_body(q_ref, k_ref, v_ref, gate_ref, cos_ref, sin_ref, inner_ref, qdec_ref, kdec_ref, cdec_ref,
                    gng_ref, gnb_ref, o_ref, state_ref, *, dk):
    @pl.when(pl.program_id(2) == 0)
    def _():
        state_ref[...] = jnp.zeros_like(state_ref)

    cos, sin = cos_ref[...], sin_ref[...]
    q = _rope(q_ref[0].astype(F32), cos, sin)
    k = _rope(k_ref[0].astype(F32), cos, sin) * (dk ** -0.5)
    v = v_ref[0].astype(BF16)
    qb = q.astype(BF16)
    s = _dot_nt(qb, k.astype(BF16)) * inner_ref[0]
    state = state_ref[...]
    o = _dot(s.astype(BF16), v) + _dot(qb, state.astype(BF16)) * qdec_ref[0]
    kd_t = (k * kdec_ref[0]).T.astype(BF16)
    state_ref[...] = state * cdec_ref[0] + _dot(kd_t, v)
    mu = jnp.mean(o, axis=-1, keepdims=True)
    oc = o - mu
    var = jnp.mean(oc * oc, axis=-1, keepdims=True)
    on = oc * lax.rsqrt(var + RET_GN_EPS) * gng_ref[...] + gnb_ref[...]
    gate = gate_ref[0].astype(F32)
    o_ref[0] = (gate * jax.nn.sigmoid(gate) * on).astype(o_ref.dtype)


def retention_mix(h, w_in, gn_g, gn_b, w_out):
    b, t, d = h.shape
    nh, c = RET_HEADS, RET_CHUNK
    dk = d // nh
    dv = 2 * dk
    proj = mm(h.reshape(b * t, d), w_in, name="ret_in").reshape(b, t, 6 * d)
    pos = jnp.arange(t, dtype=F32)
    inv_freq = ROPE_BASE ** (-jnp.arange(0, dk, 2, dtype=F32) / dk)
    ang = pos[:, None] * inv_freq[None, :]
    cos, sin = jnp.cos(ang), jnp.sin(ang)
    log_gamma = jnp.log(1.0 - 2.0 ** (-5.0 - jnp.arange(nh, dtype=F32)))
    idx = jnp.arange(c, dtype=F32)
    diff = idx[:, None] - idx[None, :]
    inner = jnp.where(diff >= 0, jnp.exp(jnp.maximum(diff, 0.0)[None] * log_gamma[:, None, None]), 0.0)
    q_dec = jnp.exp((idx + 1.0)[None] * log_gamma[:, None])[:, :, None]
    k_dec = jnp.exp((c - 1.0 - idx)[None] * log_gamma[:, None])[:, :, None]
    c_dec = jnp.exp(c * log_gamma)[:, None, None]
    nq = (nh * dk) // dk
    o = pl.pallas_call(
        functools.partial(_retention_body, dk=dk),
        grid=(b, nh, t // c),
        in_specs=[
            pl.BlockSpec((1, c, dk), lambda bi, hi, ci: (bi, ci, hi)),
            pl.BlockSpec((1, c, dk), lambda bi, hi, ci: (bi, ci, nq + hi)),
            pl.BlockSpec((1, c, dv), lambda bi, hi, ci: (bi, ci, nq + hi)),
            pl.BlockSpec((1, c, dv), lambda bi, hi, ci: (bi, ci, 2 * nq + hi)),
            pl.BlockSpec((c, dk // 2), lambda bi, hi, ci: (ci, 0)),
            pl.BlockSpec((c, dk // 2), lambda bi, hi, ci: (ci, 0)),
            pl.BlockSpec((1, c, c), lambda bi, hi, ci: (hi, 0, 0)),
            pl.BlockSpec((1, c, 1), lambda bi, hi, ci: (hi, 0, 0)),
            pl.BlockSpec((1, c, 1), lambda bi, hi, ci: (hi, 0, 0)),
            pl.BlockSpec((1, 1, 1), lambda bi, hi, ci: (hi, 0, 0)),
            pl.BlockSpec((1, dv), lambda bi, hi, ci: (0, hi)),
            pl.BlockSpec((1, dv), lambda bi, hi, ci: (0, hi)),
        ],
        out_specs=pl.BlockSpec((1, c, dv), lambda bi, hi, ci: (bi, ci, hi)),
        out_shape=jax.ShapeDtypeStruct((b, t, nh * dv), BF16),
        scratch_shapes=[pltpu.VMEM((dk, dv), F32)],
        compiler_params=_cparams("parallel", "parallel", "arbitrary"),
        name="retention",
    )(proj, proj, proj, proj, cos, sin, inner, q_dec, k_dec, c_dec, gn_g.reshape(1, -1), gn_b.reshape(1, -1))
    return mm(o.reshape(b * t, nh * dv), w_out, name="ret_out").reshape(b, t, d)


def _rwkv_pre_body(x_ref, prev_ref, g_ref, scale_ref, shift_ref, mu_ref, o_ref):
    ti = pl.program_id(1)
    g, scale, shift = g_ref[...], scale_ref[0], shift_ref[0]
    h = _rms(x_ref[0], g) * (1.0 + scale) + shift
    hp = _rms(prev_ref[0], g) * (1.0 + scale) + shift
    last = jnp.where(ti == 0, 0.0, hp[7:8, :])
    row = lax.broadcasted_iota(jnp.int32, h.shape, 0)
    shifted = jnp.where(row == 0, last, pltpu.roll(h, 1, axis=0))
    xx = shifted - h
    for s in range(6):
        o_ref[s, 0] = (h + xx * mu_ref[s:s + 1, :]).astype(o_ref.dtype)


def rwkv_pre(x, g_pre, scale, shift, mu, tt=256):
    b, t, d = x.shape
    tt = _pick(t, tt)
    bvec = pl.BlockSpec((1, 1, d), lambda bi, ti: (bi, 0, 0))
    return pl.pallas_call(
        _rwkv_pre_body,
        grid=(b, t // tt),
        in_specs=[
            pl.BlockSpec((1, tt, d), lambda bi, ti: (bi, ti, 0)),
            pl.BlockSpec((1, 8, d), lambda bi, ti: (bi, jnp.maximum(ti * (tt // 8) - 1, 0), 0)),
            pl.BlockSpec((1, d), lambda bi, ti: (0, 0)),
            bvec, bvec,
            pl.BlockSpec((6, d), lambda bi, ti: (0, 0)),
        ],
        out_specs=pl.BlockSpec((6, 1, tt, d), lambda bi, ti: (0, bi, ti, 0)),
        out_shape=jax.ShapeDtypeStruct((6, b, t, d), BF16),
        compiler_params=_cparams("parallel", "parallel"),
        name="rwkv_pre",
    )(x, x, g_pre.reshape(1, d), scale.reshape(b, 1, d), shift.reshape(b, 1, d), mu)


def _rwkv_scan_body(r_ref, k_ref, v_ref, a_ref, lw_ref, g_ref, kk_ref, ka_ref, rk_ref, lng_ref, lnb_ref, o_ref,
                    state_ref, *, chunk, heads, n):
    @pl.when(pl.program_id(2) == 0)
    def _():
        state_ref[...] = jnp.zeros_like(state_ref)

    ln = chunk
    hs = range(heads)
    row = lax.broadcasted_iota(jnp.int32, (ln, ln), 0)
    col = lax.broadcasted_iota(jnp.int32, (ln, ln), 1)
    incl = col <= row
    strict = col < row
    tri = incl.astype(BF16)
    eye = (row == col).astype(F32)
    wid = heads * n
    seg = (lax.broadcasted_iota(jnp.int32, (wid, wid), 0) // n
           == lax.broadcasted_iota(jnp.int32, (wid, wid), 1) // n).astype(BF16)

    def split_dot(m, x):
        hi = x.astype(BF16)
        lo = (x - hi.astype(F32)).astype(BF16)
        return _dot(m, hi) + _dot(m, lo)

    def seg_sum(x):
        hi = x.astype(BF16)
        lo = (x - hi.astype(F32)).astype(BF16)
        return _dot(hi, seg) + _dot(lo, seg)

    heads_of = lambda x: [x[:, hd * n:(hd + 1) * n] for hd in hs]

    r, k, v, a, lw = r_ref[0], k_ref[0], v_ref[0], a_ref[0], lw_ref[0]
    kk = k * kk_ref[...]
    kk = kk / jnp.maximum(jnp.sqrt(seg_sum(kk * kk)), 1e-12)
    k2 = k * (1.0 + (a - 1.0) * ka_ref[...])
    beta = kk * a
    c = split_dot(tri, lw)
    c_last = c[ln - 1:ln, :]
    e_neg = jnp.exp(-c)
    e_rem = jnp.exp(c_last - c)
    e_last = heads_of(jnp.exp(c_last))
    a_t = heads_of(-kk * jnp.exp(c - lw))
    r_t = heads_of(r * jnp.exp(c))
    b_t = heads_of(beta * e_neg)
    k_t = heads_of(k2 * e_neg)
    b_rem = heads_of(beta * e_rem)
    k_rem = heads_of(k2 * e_rem)
    vs = heads_of(v)
    bonus = seg_sum(r * k2 * rk_ref[...]) * v

    ar = [jnp.concatenate([a_t[h], r_t[h]], axis=0).astype(BF16) for h in hs]
    bk = [jnp.concatenate([b_t[h], k_t[h]], axis=0).astype(BF16) for h in hs]
    bk_rem = [jnp.concatenate([b_rem[h], k_rem[h]], axis=0).astype(BF16) for h in hs]
    vb = [vs[h].astype(BF16) for h in hs]
    state = [state_ref[h] for h in hs]
    p = [_dot_nt(ar[h], bk[h]) for h in hs]
    q0 = [_dot_nt(ar[h], state[h].astype(BF16)) for h in hs]
    m_ab = [jnp.where(strict, p[h][:ln, :ln], 0.0) for h in hs]
    m_ak = [jnp.where(strict, p[h][:ln, ln:], 0.0).astype(BF16) for h in hs]
    m_r = [jnp.concatenate([jnp.where(incl, p[h][ln:, :ln], 0.0), jnp.where(incl, p[h][ln:, ln:], 0.0)],
                           axis=1).astype(BF16) for h in hs]
    rhs = [q0[h][:ln] + _dot(m_ak[h], vb[h]) for h in hs]
    inv = [eye + m_ab[h] for h in hs]
    pw = [m_ab[h].astype(BF16) for h in hs]
    for _ in range(int(math.log2(ln)) - 1):
        pw = [_dot(pw[h], pw[h]).astype(BF16) for h in hs]
        inv = [inv[h] + _dot(inv[h].astype(BF16), pw[h]) for h in hs]
    u = [_dot(inv[h].astype(BF16), rhs[h].astype(BF16)) for h in hs]
    uv = [jnp.concatenate([u[h], vs[h]], axis=0) for h in hs]
    y = [q0[h][ln:] + _dot(m_r[h], uv[h].astype(BF16)) for h in hs]
    for h in hs:
        state_ref[h] = state[h] * e_last[h] + _dot(uv[h].T.astype(BF16), bk_rem[h])
    yn = []
    for h in hs:
        mu = jnp.mean(y[h], axis=-1, keepdims=True)
        yc = y[h] - mu
        var = jnp.mean(yc * yc, axis=-1, keepdims=True)
        yn.append(yc * lax.rsqrt(var + RWKV_GN_EPS))
    yn = jnp.concatenate(yn, axis=-1) * lng_ref[...] + lnb_ref[...]
    o_ref[0] = ((yn + bonus) * g_ref[0]).astype(o_ref.dtype)


def rwkv_scan(r, k, v, a, lw, g, k_k, k_a, r_k, ln_g, ln_b):
    b, t, d = r.shape
    n = RWKV_HEAD_DIM
    hps = min(RWKV_HEADS_PER_STEP, d // n)
    w = hps * n
    ln = _pick(t, RWKV_CHUNK)
    tok = pl.BlockSpec((1, ln, w), lambda bi, hi, ci: (bi, ci, hi))
    vec = pl.BlockSpec((1, w), lambda bi, hi, ci: (0, hi))
    return pl.pallas_call(
        functools.partial(_rwkv_scan_body, chunk=ln, heads=hps, n=n),
        grid=(b, d // w, t // ln),
        in_specs=[tok] * 6 + [vec] * 5,
        out_specs=tok,
        out_shape=jax.ShapeDtypeStruct((b, t, d), BF16),
        scratch_shapes=[pltpu.VMEM((hps, n, n), F32)],
        compiler_params=_cparams("parallel", "parallel", "arbitrary"),
        name="rwkv_scan",
    )(r, k, v, a, lw, g, k_k.reshape(1, d), k_a.reshape(1, d), r_k.reshape(1, d), ln_g.reshape(1, d),
      ln_b.reshape(1, d))


def _tanh_epi(parts, extras):
    return jnp.tanh(parts[0])


def _sigmoid_epi(parts, extras):
    return jax.nn.sigmoid(parts[0])


def _sigmoid_bias_epi(parts, extras):
    return jax.nn.sigmoid(extras[0] + parts[0])


def _logdecay_epi(parts, extras):
    return -jnp.exp(-jax.nn.softplus(-(extras[0] + parts[0])) - 0.5)


def rwkv7_time_mix(x, g_pre, scale, shift, mu, w_rkv, w0, w_la, w_lb, a0, a_la, a_lb, g_la, g_lb, k_k, k_a, r_k,
                   ln_g, ln_b, w_out):
    b, t, d = x.shape
    xs = rwkv_pre(x, g_pre, scale, shift, mu).reshape(6, b * t, d)
    tn = _pick(d, 1024)
    r = mm(xs[0], w_rkv[0], name="rwkv_r")
    k = mm(xs[1], w_rkv[1], name="rwkv_k")
    v = mm(xs[2], w_rkv[2], name="rwkv_v")
    lw = mm(mm(xs[3], w_la, epi=_tanh_epi, out_dtype=BF16, name="rwkv_w_la"), w_lb, epi=_logdecay_epi,
            extras=[_row_extra(w0, tn)], name="rwkv_w_lb")
    a = mm(mm(xs[4], a_la, out_dtype=BF16, name="rwkv_a_la"), a_lb, epi=_sigmoid_bias_epi,
           extras=[_row_extra(a0, tn)], name="rwkv_a_lb")
    g = mm(mm(xs[5], g_la, epi=_sigmoid_epi, out_dtype=BF16, name="rwkv_g_la"), g_lb, name="rwkv_g_lb")
    sh = lambda z: z.reshape(b, t, d)
    z = rwkv_scan(sh(r), sh(k), sh(v), sh(a), sh(lw), sh(g), k_k, k_a, r_k, ln_g, ln_b)
    return mm(z.reshape(b * t, d), w_out, name="rwkv_out").reshape(b, t, d)


def _nsa_compress_body(kv_ref, pe_ref, w1_ref, w2_ref, o_ref, *, nchunk, dh):
    st = NSA_CMP_STRIDE
    hid = w1_ref.shape[1]
    first = jnp.zeros((nchunk, hid), F32)
    second = jnp.zeros((nchunk, hid), F32)
    for j in range(st):
        xj = kv_ref[0, pl.ds(j, nchunk, stride=st), :]
        first = first + _dot((xj + pe_ref[j:j + 1, :]).astype(BF16), w1_ref[j * dh:(j + 1) * dh, :])
        second = second + _dot((xj + pe_ref[st + j:st + j + 1, :]).astype(BF16),
                               w1_ref[(st + j) * dh:(st + j + 1) * dh, :])
    hidden = jax.nn.gelu(first + pltpu.roll(second, nchunk - 1, axis=0))
    out = _dot(hidden.astype(BF16), w2_ref[...])
    row = lax.broadcasted_iota(jnp.int32, out.shape, 0)
    o_ref[0, 0] = jnp.where(row < nchunk - 1, out, 0.0)


def nsa_compress(proj, col0, pe, w1, w2):
    b, t, _ = proj.shape
    g, dh = NSA_KV_GROUPS, NSA_HEAD_DIM
    nchunk = t // NSA_CMP_STRIDE
    hid = w1.shape[1]
    return pl.pallas_call(
        functools.partial(_nsa_compress_body, nchunk=nchunk, dh=dh),
        grid=(b, g),
        in_specs=[
            pl.BlockSpec((1, t, dh), lambda bi, gi: (bi, 0, col0 // dh + gi)),
            pl.BlockSpec((NSA_CMP_BLOCK, dh), lambda bi, gi: (0, 0)),
            pl.BlockSpec((NSA_CMP_BLOCK * dh, hid), lambda bi, gi: (0, 0)),
            pl.BlockSpec((hid, dh), lambda bi, gi: (0, 0)),
        ],
        out_specs=pl.BlockSpec((1, 1, nchunk, dh), lambda bi, gi: (bi, gi, 0, 0)),
        out_shape=jax.ShapeDtypeStruct((b, g, nchunk, dh), F32),
        compiler_params=_cparams("parallel", "parallel"),
        name="nsa_compress",
    )(proj, pe, w1.astype(BF16), w2.astype(BF16))


def _stack_heads(q, r, dh):
    return jnp.concatenate([q[:, i * dh:(i + 1) * dh] for i in range(r)], axis=0)


def _unstack_heads(o, r, tq):
    return jnp.concatenate([o[i * tq:(i + 1) * tq] for i in range(r)], axis=-1)


def _gate_rows(gates, branch, r, tq):
    return jnp.concatenate([gates[:, branch * r + i:branch * r + i + 1] for i in range(r)], axis=0)


def _nsa_cmp_body(q_ref, kc_ref, vc_ref, gates_ref, selmap_t_ref, o_ref, sel_ref, *, tq, r, dh, n_sel):
    q0 = pl.program_id(2) * tq
    qs = (_stack_heads(q_ref[0], r, dh) * (dh ** -0.5)).astype(BF16)
    kc, vc = kc_ref[0, 0].astype(BF16), vc_ref[0, 0].astype(BF16)
    ncp = kc.shape[0]
    s = _dot_nt(qs, kc)
    rows = lax.broadcasted_iota(jnp.int32, (r * tq, ncp), 0)
    cmp_end = lax.broadcasted_iota(jnp.int32, (r * tq, ncp), 1) * NSA_CMP_STRIDE + (NSA_CMP_BLOCK - 1)
    t_pos = q0 + rows % tq
    vis = cmp_end <= t_pos
    s = jnp.where(vis, s, NEG_INF)
    e = jnp.exp(s - jnp.max(s, axis=-1, keepdims=True))
    p = e / jnp.sum(e, axis=-1, keepdims=True) * vis.astype(F32)
    o = _dot(p.astype(BF16), vc)
    o = o * _gate_rows(gates_ref[0, 0], 0, r, tq)
    o_ref[0] = _unstack_heads(o, r, tq)
    p_sum = p[0:tq]
    for i in range(1, r):
        p_sum = p_sum + p[i * tq:(i + 1) * tq]
    hi = p_sum.astype(BF16)
    lo = (p_sum - hi.astype(F32)).astype(BF16)
    selmap_t = selmap_t_ref[...]
    imp = _dot_nt(selmap_t, hi) + _dot_nt(selmap_t, lo)
    ns = imp.shape[0]
    blk = lax.broadcasted_iota(jnp.int32, (ns, tq), 0)
    tq_pos = q0 + lax.broadcasted_iota(jnp.int32, (ns, tq), 1)
    cur = tq_pos // NSA_SEL_BLOCK
    valid = blk * NSA_SEL_BLOCK <= tq_pos
    forced = (blk == 0) | (blk == cur) | (blk == cur - 1)
    score = jnp.where(valid, jnp.where(forced, POS_BIG, imp), NEG_INF)
    rank = jnp.zeros((ns, tq), F32)
    for m in range(ns):
        sm = score[m:m + 1, :]
        beats = (sm > score) | ((sm == score) & (blk > m))
        rank = rank + beats.astype(F32)
    chosen = ((rank < n_sel) & valid).astype(F32)
    sel_ref[0, 0] = (chosen.T - 1.0).astype(sel_ref.dtype)


def _flash_update(s, v, m_ref, l_ref, acc_ref):
    m_old = m_ref[...]
    m_new = jnp.maximum(m_old, jnp.max(s, axis=-1, keepdims=True))
    alpha = jnp.exp(m_old - m_new)
    p = jnp.exp(s - m_new)
    l_ref[...] = alpha * l_ref[...] + jnp.sum(p, axis=-1, keepdims=True)
    acc_ref[...] = alpha * acc_ref[...] + _dot(p.astype(BF16), v)
    m_ref[...] = m_new


def _nsa_sel_body(qi_tbl, ki_tbl, q_ref, k_ref, v_ref, gates_ref, oin_ref, sel_ref, o_ref, qa_ref, m_ref, l_ref,
                  acc_ref, *, tq, tk, r, dh):
    step = pl.program_id(2)
    qi, ki = qi_tbl[step], ki_tbl[step]
    ns = sel_ref.shape[-1]
    assert ns <= dh

    @pl.when(ki == 0)
    def _():
        m_ref[...] = jnp.full_like(m_ref, NEG_INF)
        l_ref[...] = jnp.zeros_like(l_ref)
        acc_ref[...] = jnp.zeros_like(acc_ref)
        qs = _stack_heads(q_ref[0], r, dh) * (dh ** -0.5)
        pieces = [qs, jnp.concatenate([sel_ref[0, 0].astype(F32)] * r, axis=0)]
        if ns < dh:
            pieces.append(jnp.zeros((r * tq, dh - ns), F32))
        qa_ref[...] = jnp.concatenate(pieces, axis=1).astype(BF16)

    blk_of_key = (ki * tk + lax.broadcasted_iota(jnp.int32, (tk, dh), 0)) // NSA_SEL_BLOCK
    own_block = jnp.where(blk_of_key == lax.broadcasted_iota(jnp.int32, (tk, dh), 1), POS_BIG, 0.0)
    ka = jnp.concatenate([k_ref[0], own_block], axis=1).astype(BF16)
    s = _dot_nt(qa_ref[...], ka)
    v = v_ref[0].astype(BF16)

    @pl.when(ki < qi)
    def _():
        _flash_update(s, v, m_ref, l_ref, acc_ref)

    @pl.when(ki == qi)
    def _():
        t_loc = lax.broadcasted_iota(jnp.int32, (r * tq, tk), 0) % tq
        k_loc = lax.broadcasted_iota(jnp.int32, (r * tq, tk), 1)
        _flash_update(jnp.where(k_loc <= t_loc, s, NEG_INF), v, m_ref, l_ref, acc_ref)
        o = acc_ref[...] / l_ref[...] * _gate_rows(gates_ref[0, 0], 1, r, tq)
        o_ref[0] = (oin_ref[0] + _unstack_heads(o, r, tq)).astype(o_ref.dtype)


def _nsa_win_body(q_ref, *rest, tq, r, dh, nwin):
    k_refs, v_refs = rest[:nwin], rest[nwin:2 * nwin]
    gates_ref, oin_ref, o_ref = rest[2 * nwin:]
    qi = pl.program_id(2)
    qs = (_stack_heads(q_ref[0], r, dh) * (dh ** -0.5)).astype(BF16)
    t_loc = lax.broadcasted_iota(jnp.int32, (r * tq, tq), 0) % tq
    k_loc = lax.broadcasted_iota(jnp.int32, (r * tq, tq), 1)
    tiles = []
    for j in range(nwin):
        s = _dot_nt(qs, k_refs[j][0].astype(BF16))
        if j == nwin - 1:
            s = jnp.where(k_loc <= t_loc, s, NEG_INF)
        else:
            if j == 0:
                s = jnp.where(k_loc > t_loc, s, NEG_INF)
            s = jnp.where(qi - (nwin - 1) + j >= 0, s, NEG_INF)
        tiles.append(s)
    s = jnp.concatenate(tiles, axis=1)
    p = jnp.exp(s - jnp.max(s, axis=-1, keepdims=True))
    v = jnp.concatenate([v_refs[j][0] for j in range(nwin)], axis=0).astype(BF16)
    o = _dot(p.astype(BF16), v) / jnp.sum(p, axis=-1, keepdims=True)
    o = o * _gate_rows(gates_ref[0, 0], 2, r, tq)
    o_ref[0] = (oin_ref[0] + _unstack_heads(o, r, tq)).astype(o_ref.dtype)


def nsa_mix(h, w_in, pe_k, pe_v, ck_w1, ck_w2, cv_w1, cv_w2, w_out, tq=256):
    b, t, d = h.shape
    nh, g, dh = NSA_HEADS, NSA_KV_GROUPS, NSA_HEAD_DIM
    r = nh // g
    hd, kd = nh * dh, g * dh
    n_main = hd + 6 * kd
    h2 = h.reshape(b * t, d)
    proj = mm(h2, w_in[:, :n_main], name="nsa_in").reshape(b, t, n_main)
    n_gate = w_in.shape[1] - n_main
    w_gate = jnp.pad(w_in[:, n_main:], ((0, 0), (0, 128 - n_gate)))
    gates = mm(h2, w_gate, epi=_sigmoid_epi, name="nsa_gates")[:, :n_gate]
    gates = jnp.transpose(gates.reshape(b, t, 3, g, r), (0, 3, 1, 2, 4)).reshape(b, g, t, 3 * r)
    k_cmp = nsa_compress(proj, hd, pe_k, ck_w1, ck_w2)
    v_cmp = nsa_compress(proj, hd + kd, pe_v, cv_w1, cv_w2)
    ncp = k_cmp.shape[2]
    ns = t // NSA_SEL_BLOCK
    n_sel = min(NSA_SEL_TOP, ns)
    cs = jnp.arange(ncp)[None, :] * NSA_CMP_STRIDE
    ss = jnp.arange(ns)[:, None] * NSA_SEL_BLOCK
    sel_map_t = (jnp.maximum(jnp.minimum(cs + NSA_CMP_BLOCK, ss + NSA_SEL_BLOCK) - jnp.maximum(cs, ss), 0)
                 .astype(F32) / NSA_CMP_BLOCK).astype(BF16)
    tq = _pick(t, tq)
    assert NSA_WINDOW % tq == 0
    nq = t // tq
    wq = r * dh
    col = lambda base: base // dh
    q_spec3 = pl.BlockSpec((1, tq, wq), lambda bi, gi, qi: (bi, qi, gi))
    gate_spec3 = pl.BlockSpec((1, 1, tq, 3 * r), lambda bi, gi, qi: (bi, gi, qi, 0))
    o_cmp, sel = pl.pallas_call(
        functools.partial(_nsa_cmp_body, tq=tq, r=r, dh=dh, n_sel=n_sel),
        grid=(b, g, nq),
        in_specs=[
            q_spec3,
            pl.BlockSpec((1, 1, ncp, dh), lambda bi, gi, qi: (bi, gi, 0, 0)),
            pl.BlockSpec((1, 1, ncp, dh), lambda bi, gi, qi: (bi, gi, 0, 0)),
            gate_spec3,
            pl.BlockSpec((ns, ncp), lambda bi, gi, qi: (0, 0)),
        ],
        out_specs=[q_spec3, pl.BlockSpec((1, 1, tq, ns), lambda bi, gi, qi: (bi, gi, qi, 0))],
        out_shape=[jax.ShapeDtypeStruct((b, t, hd), F32), jax.ShapeDtypeStruct((b, g, t, ns), BF16)],
        compiler_params=_cparams("parallel", "parallel", "parallel"),
        name="nsa_cmp",
    )(proj, k_cmp, v_cmp, gates, sel_map_t)

    tk = tq
    pairs = [(qi, ki) for qi in range(nq) for ki in range(qi + 1)]
    qi_tbl = jnp.asarray([pq for pq, _ in pairs], jnp.int32)
    ki_tbl = jnp.asarray([pk for _, pk in pairs], jnp.int32)
    q_spec_p = pl.BlockSpec((1, tq, wq), lambda bi, gi, p, qt, kt: (bi, qt[p], gi))

    def kv_spec_p(base):
        return pl.BlockSpec((1, tk, dh), lambda bi, gi, p, qt, kt: (bi, kt[p], col(base) + gi))

    o_sel = pl.pallas_call(
        functools.partial(_nsa_sel_body, tq=tq, tk=tk, r=r, dh=dh),
        grid_spec=pltpu.PrefetchScalarGridSpec(
            num_scalar_prefetch=2,
            grid=(b, g, len(pairs)),
            in_specs=[q_spec_p, kv_spec_p(hd + 2 * kd), kv_spec_p(hd + 3 * kd),
                      pl.BlockSpec((1, 1, tq, 3 * r), lambda bi, gi, p, qt, kt: (bi, gi, qt[p], 0)),
                      q_spec_p,
                      pl.BlockSpec((1, 1, tq, ns), lambda bi, gi, p, qt, kt: (bi, gi, qt[p], 0))],
            out_specs=q_spec_p,
            scratch_shapes=[pltpu.VMEM((r * tq, 2 * dh), BF16), pltpu.VMEM((r * tq, 1), F32),
                            pltpu.VMEM((r * tq, 1), F32), pltpu.VMEM((r * tq, dh), F32)],
        ),
        out_shape=jax.ShapeDtypeStruct((b, t, hd), F32),
        compiler_params=_cparams("parallel", "parallel", "arbitrary"),
        name="nsa_sel",
    )(qi_tbl, ki_tbl, proj, proj, proj, gates, o_cmp, sel)

    nwin = NSA_WINDOW // tq + 1

    def kv_spec_w(base, j):
        return pl.BlockSpec((1, tq, dh),
                            lambda bi, gi, qi: (bi, jnp.maximum(qi - (nwin - 1) + j, 0), col(base) + gi))

    o_all = pl.pallas_call(
        functools.partial(_nsa_win_body, tq=tq, r=r, dh=dh, nwin=nwin),
        grid=(b, g, nq),
        in_specs=([q_spec3] + [kv_spec_w(hd + 4 * kd, j) for j in range(nwin)]
                  + [kv_spec_w(hd + 5 * kd, j) for j in range(nwin)] + [gate_spec3, q_spec3]),
        out_specs=q_spec3,
        out_shape=jax.ShapeDtypeStruct((b, t, hd), BF16),
        compiler_params=_cparams("parallel", "parallel", "parallel"),
        name="nsa_win",
    )(proj, *([proj] * (2 * nwin)), gates, o_sel)
    return mm(o_all.reshape(b * t, hd), w_out, name="nsa_out").reshape(b, t, d)


def _silu(a):
    return a * jax.nn.sigmoid(a)


def _ada_epi(parts, extras):
    return parts[0] + extras[0]


def ada_modulation(c, ada_w, ada_b):
    depth, d, n6 = ada_w.shape
    b = c.shape[0]
    rows = ((b + 7) // 8) * 8
    cond = jnp.pad(c, ((0, rows - b), (0, 0)))
    mods = []
    for i in range(depth):
        mods.append(mm(cond, ada_w[i], a_act=_silu, epi=_ada_epi, extras=[_row_extra(ada_b[i], _pick(n6, 1024))],
                       precision=HIGHEST, name="ada_mod")[:b])
    return jnp.stack(mods)


def kernel(x, c, ada_w, ada_b, norm_g, mlp_w1, mlp_w2, rwkv_mu, rwkv_w_rkv, rwkv_w0, rwkv_w_la, rwkv_w_lb, rwkv_a0, rwkv_a_la, rwkv_a_lb, rwkv_g_la, rwkv_g_lb, rwkv_k_k, rwkv_k_a, rwkv_r_k, rwkv_ln_g, rwkv_ln_b, rwkv_w_out, ret_w_in, ret_gn_g, ret_gn_b, ret_w_out, conv_pw1_w, conv_pw1_b, conv_dw_w, conv_dw_b, conv_ln_g, conv_ln_b, conv_pw2_w, conv_pw2_b, nsa_w_in, nsa_pe_k, nsa_pe_v, nsa_ck_w1, nsa_ck_w2, nsa_cv_w1, nsa_cv_w2, nsa_w_out):
    b, t, d = x.shape
    depth = ada_w.shape[0]
    mod = ada_modulation(c, ada_w, ada_b).reshape(depth, b, 6, d)
    h = None
    for i in range(depth):
        sh_t, sc_t, gt_t, sh_c, sc_c, gt_c = (mod[i, :, j] for j in range(6))
        kind = i % 4
        if kind == 0:
            y = rwkv7_time_mix(x, norm_g[i, 0], sc_t, sh_t, rwkv_mu, rwkv_w_rkv, rwkv_w0, rwkv_w_la, rwkv_w_lb,
                               rwkv_a0, rwkv_a_la, rwkv_a_lb, rwkv_g_la, rwkv_g_lb, rwkv_k_k, rwkv_k_a,
                               rwkv_r_k.reshape(-1), rwkv_ln_g, rwkv_ln_b, rwkv_w_out)
        else:
            if h is None:
                (h,) = resid_norm(x, pre=(norm_g[i, 0], sc_t, sh_t))
            if kind == 1:
                y = retention_mix(h, ret_w_in, ret_gn_g, ret_gn_b, ret_w_out)
            elif kind == 2:
                y = conformer_conv_mix(h, conv_pw1_w, conv_pw1_b, conv_dw_w, conv_dw_b, conv_ln_g, conv_ln_b,
                                       conv_pw2_w, conv_pw2_b)
            else:
                y = nsa_mix(h, nsa_w_in, nsa_pe_k, nsa_pe_v, nsa_ck_w1, nsa_ck_w2, nsa_cv_w1, nsa_cv_w2, nsa_w_out)
        x, h = resid_norm(x, y, post=(norm_g[i, 1], gt_t), pre=(norm_g[i, 2], sc_c, sh_c))
        y = sqrelu_mlp(h.reshape(b * t, d), mlp_w1[i], mlp_w2[i]).reshape(b, t, d)
        nxt = i + 1
        if nxt < depth and nxt % 4 != 0:
            x, h = resid_norm(x, y, post=(norm_g[i, 3], gt_c),
                              pre=(norm_g[nxt, 0], mod[nxt, :, 1], mod[nxt, :, 0]))
        else:
            (x,) = resid_norm(x, y, post=(norm_g[i, 3], gt_c))
            h = None
    return x
```

```python
import functools
import math

import jax
import jax.numpy as jnp
from jax import lax
from jax.experimental import pallas as pl
from jax.experimental.pallas import tpu as pltpu

F32 = jnp.float32
BF16 = jnp.bfloat16
HIGHEST = lax.Precision.HIGHEST

NORM_EPS = 1e-6
NEG_INF = -1e30
POS_BIG = 1e30

RWKV_HEAD_DIM = 64
RWKV_GN_EPS = 64e-5
RWKV_CHUNK = 64
RWKV_HEADS_PER_STEP = 8

RET_HEADS = 8
RET_CHUNK = 256
RET_GN_EPS = 1e-5
ROPE_BASE = 10000.0

CONV_WIDTH = 31
CONV_HALO = 32
CONV_LN_EPS = 1e-5

NSA_HEADS = 16
NSA_KV_GROUPS = 4
NSA_HEAD_DIM = 128
NSA_CMP_BLOCK = 32
NSA_CMP_STRIDE = 16
NSA_SEL_BLOCK = 64
NSA_SEL_TOP = 16
NSA_WINDOW = 512

VMEM_LIMIT_BYTES = 56 * 1024 * 1024


def _cparams(*sem):
    return pltpu.CompilerParams(dimension_semantics=sem, vmem_limit_bytes=VMEM_LIMIT_BYTES)


def _dot(a, b, precision=None):
    return jnp.dot(a, b, preferred_element_type=F32, precision=precision)


def _dot_nt(a, b, precision=None):
    return lax.dot_general(a, b, (((1,), (1,)), ((), ())), preferred_element_type=F32, precision=precision)


def _pick(n, pref):
    if n <= pref:
        return n
    t = pref
    while n % t:
        t //= 2
    return t


def _mm_body(*refs, n_w, n_ex, nk, epi, precision, has_a_add, a_act):
    a_ref = refs[0]
    pos = 1
    a_add_ref = None
    if has_a_add:
        a_add_ref = refs[pos]
        pos += 1
    w_refs = refs[pos:pos + n_w]
    pos += n_w
    ex_refs = refs[pos:pos + n_ex]
    pos += n_ex
    o_ref = refs[pos]
    acc_refs = refs[pos + 1:]

    a = a_ref[...]
    if a_add_ref is not None:
        a = a.astype(F32) + a_add_ref[...]
    if a_act is not None:
        a = a_act(a)
    if precision is None:
        a = a.astype(BF16)
    parts = []
    for w_ref in w_refs:
        w = w_ref[...]
        if precision is None:
            w = w.astype(BF16)
        parts.append(_dot(a, w, precision))

    if nk == 1:
        o_ref[...] = epi(parts, [e[...] for e in ex_refs]).astype(o_ref.dtype)
    else:
        k = pl.program_id(2)

        @pl.when(k == 0)
        def _():
            for acc, p in zip(acc_refs, parts):
                acc[...] = p

        @pl.when(k > 0)
        def _():
            for acc, p in zip(acc_refs, parts):
                acc[...] += p

        @pl.when(k == nk - 1)
        def _():
            o_ref[...] = epi([acc[...] for acc in acc_refs], [e[...] for e in ex_refs]).astype(o_ref.dtype)


def _first(parts, extras):
    return parts[0]


def mm(a, ws, *, n_out=None, extras=(), epi=_first, out_dtype=F32, tm=1024, tn=1024, tk=2048, precision=None,
       a_add=None, a_act=None, a_part=(0, 1), name="mm"):
    m, kdim = a.shape
    a_s, a_parts = a_part
    assert m % a_parts == 0
    m //= a_parts
    if not isinstance(ws, (list, tuple)):
        ws = [(ws, 0, 0)]
    ws = [tuple(w) + (0,) * (3 - len(w)) for w in ws]
    if n_out is None:
        n_out = ws[0][0].shape[1]
    tm = _pick(m, tm)
    tn = _pick(n_out, tn)
    tk = _pick(kdim, tk)
    nk = kdim // tk
    assert m % tm == 0 and n_out % tn == 0 and kdim % tk == 0
    in_specs = [pl.BlockSpec((tm, tk), lambda i, j, k, o=a_s * (m // tm): (i + o, k))]
    args = [a]
    if a_add is not None:
        in_specs.append(pl.BlockSpec((1, tk), lambda i, j, k: (0, k)))
        args.append(a_add)
    for w, off, koff in ws:
        in_specs.append(pl.BlockSpec((tk, tn), lambda i, j, k, off=off, ko=koff * nk: (k + ko, j + off)))
        args.append(w.astype(BF16) if precision is None else w)
    for arr, bshape, imap in extras:
        in_specs.append(pl.BlockSpec(bshape, imap))
        args.append(arr)
    scratch = [pltpu.VMEM((tm, tn), F32) for _ in ws] if nk > 1 else []
    body = functools.partial(_mm_body, n_w=len(ws), n_ex=len(extras), nk=nk, epi=epi, precision=precision,
                             has_a_add=a_add is not None, a_act=a_act)
    return pl.pallas_call(
        body,
        grid=(m // tm, n_out // tn, nk),
        in_specs=in_specs,
        out_specs=pl.BlockSpec((tm, tn), lambda i, j, k: (i, j)),
        out_shape=jax.ShapeDtypeStruct((m, n_out), out_dtype),
        scratch_shapes=scratch,
        compiler_params=_cparams("parallel", "parallel", "arbitrary"),
        name=name,
    )(*args)


def _row_extra(vec, tn):
    return (vec.reshape(1, -1), (1, tn), lambda i, j, k: (0, j))


def _rms(x, g):
    return x * lax.rsqrt(jnp.mean(x * x, axis=-1, keepdims=True) + NORM_EPS) * g


def _resid_norm_body(*refs, has_y, has_h):
    pos = 0
    x_ref = refs[pos]; pos += 1
    if has_y:
        y_ref, gpost_ref, gate_ref = refs[pos:pos + 3]; pos += 3
    if has_h:
        gpre_ref, scale_ref, shift_ref = refs[pos:pos + 3]; pos += 3
    outs = refs[pos:]
    x = x_ref[0]
    o = 0
    if has_y:
        y = y_ref[0].astype(F32)
        x = x + (1.0 + gate_ref[0]) * _rms(y, gpost_ref[...])
        outs[o][0] = x
        o += 1
    if has_h:
        h = _rms(x, gpre_ref[...]) * (1.0 + scale_ref[0]) + shift_ref[0]
        outs[o][0] = h.astype(outs[o].dtype)


def resid_norm(x, y=None, post=None, pre=None, tt=256):
    b, t, d = x.shape
    tt = _pick(t, tt)
    row = pl.BlockSpec((1, tt, d), lambda bi, ti: (bi, ti, 0))
    vec = pl.BlockSpec((1, d), lambda bi, ti: (0, 0))
    bvec = pl.BlockSpec((1, 1, d), lambda bi, ti: (bi, 0, 0))
    args, in_specs, out_shapes, out_specs = [x], [row], [], []
    if y is not None:
        args += [y, post[0].reshape(1, d), post[1].reshape(b, 1, d)]
        in_specs += [row, vec, bvec]
        out_shapes.append(jax.ShapeDtypeStruct((b, t, d), F32))
        out_specs.append(row)
    if pre is not None:
        args += [pre[0].reshape(1, d), pre[1].reshape(b, 1, d), pre[2].reshape(b, 1, d)]
        in_specs += [vec, bvec, bvec]
        out_shapes.append(jax.ShapeDtypeStruct((b, t, d), BF16))
        out_specs.append(row)
    res = pl.pallas_call(
        functools.partial(_resid_norm_body, has_y=y is not None, has_h=pre is not None),
        grid=(b, t // tt),
        in_specs=in_specs,
        out_specs=out_specs,
        out_shape=out_shapes,
        compiler_params=_cparams("parallel", "parallel"),
        name="resid_norm",
    )(*args)
    return tuple(res)


def _relu2(parts, extras):
    r = jnp.maximum(parts[0], 0.0)
    return r * r


def sqrelu_mlp(h2d, w1, w2, layer):
    nl, d, f = w1.shape
    a = mm(h2d, [(w1.reshape(nl * d, f), 0, layer)], epi=_relu2, out_dtype=BF16, name="mlp_up")
    return mm(a, [(w2.reshape(nl * f, d), 0, layer)], name="mlp_down")


def _glu(parts, extras):
    return (parts[0] + extras[0]) * jax.nn.sigmoid(parts[1] + extras[1])


def _add_bias(parts, extras):
    return parts[0] + extras[0]


def _conv_ln_body(cur_ref, prev_ref, w_ref, b_ref, g_ref, beta_ref, o_ref, buf_ref, acc_ref, *, tt, d):
    ti = pl.program_id(1)
    halo = prev_ref[0]
    buf_ref[0:CONV_HALO, :] = jnp.where(ti == 0, jnp.zeros_like(halo), halo)
    buf_ref[CONV_HALO:, :] = cur_ref[0]
    lane_chunk = min(d, 256)
    row_chunk = min(tt, 64)
    base = CONV_HALO - (CONV_WIDTH - 1)
    for c0 in range(0, d, lane_chunk):
        for r0 in range(0, tt, row_chunk):
            acc = jnp.zeros((row_chunk, lane_chunk), F32)
            for j in range(CONV_WIDTH):
                tap = w_ref[j:j + 1, c0:c0 + lane_chunk]
                acc = acc + buf_ref[base + r0 + j:base + r0 + j + row_chunk, c0:c0 + lane_chunk] * tap
            acc_ref[r0:r0 + row_chunk, c0:c0 + lane_chunk] = acc
    y = acc_ref[...] + b_ref[...]
    mu = jnp.mean(y, axis=-1, keepdims=True)
    yc = y - mu
    var = jnp.mean(yc * yc, axis=-1, keepdims=True)
    z = yc * lax.rsqrt(var + CONV_LN_EPS) * g_ref[...] + beta_ref[...]
    o_ref[0] = (z * jax.nn.sigmoid(z)).astype(o_ref.dtype)


def conv_ln_silu(u, dw_w, dw_b, ln_g, ln_b, tt=128):
    b, t, d = u.shape
    tt = _pick(t, tt)
    hb = tt // CONV_HALO
    vec = pl.BlockSpec((1, d), lambda bi, ti: (0, 0))
    return pl.pallas_call(
        functools.partial(_conv_ln_body, tt=tt, d=d),
        grid=(b, t // tt),
        in_specs=[
            pl.BlockSpec((1, tt, d), lambda bi, ti: (bi, ti, 0)),
            pl.BlockSpec((1, CONV_HALO, d), lambda bi, ti: (bi, jnp.maximum(ti * hb - 1, 0), 0)),
            pl.BlockSpec((CONV_WIDTH, d), lambda bi, ti: (0, 0)),
            vec, vec, vec,
        ],
        out_specs=pl.BlockSpec((1, tt, d), lambda bi, ti: (bi, ti, 0)),
        out_shape=jax.ShapeDtypeStruct((b, t, d), BF16),
        scratch_shapes=[pltpu.VMEM((tt + CONV_HALO, d), F32), pltpu.VMEM((tt, d), F32)],
        compiler_params=_cparams("parallel", "parallel"),
        name="conv_ln_silu",
    )(u, u, dw_w, dw_b.reshape(1, d), ln_g.reshape(1, d), ln_b.reshape(1, d))


def conformer_conv_mix(h, pw1_w, pw1_b, dw_w, dw_b, ln_g, ln_b, pw2_w, pw2_b):
    b, t, d = h.shape
    tn = _pick(d, 1024)
    b1 = pw1_b.reshape(1, -1)
    u = mm(h.reshape(b * t, d), [(pw1_w, 0), (pw1_w, d // tn)], n_out=d, tn=tn, epi=_glu,
           extras=[(b1, (1, tn), lambda i, j, k: (0, j)),
                   (b1, (1, tn), lambda i, j, k, o=d // tn: (0, j + o))], name="conf_pw1_glu")
    z = conv_ln_silu(u.reshape(b, t, d), dw_w, dw_b, ln_g, ln_b)
    y = mm(z.reshape(b * t, d), pw2_w, epi=_add_bias, extras=[_row_extra(pw2_b, _pick(d, 1024))], name="conf_pw2")
    return y.reshape(b, t, d)


def _rope(x, cos, sin):
    half = x.shape[-1] // 2
    x1, x2 = x[:, :half], x[:, half:]
    return jnp.concatenate([x1 * cos - x2 * sin, x2 * cos + x1 * sin], axis=-1)


def _retention_body(q_ref, k_ref, v_ref, gate_ref, cos_ref, sin_ref, inner_ref, qdec_ref, kdec_ref, cdec_ref,
                    gng_ref, gnb_ref, o_ref, state_ref, *, dk):
    @pl.when(pl.program_id(2) == 0)
    def _():
        state_ref[...] = jnp.zeros_like(state_ref)

    cos, sin = cos_ref[...], sin_ref[...]
    q = _rope(q_ref[0].astype(F32), cos, sin)
    k = _rope(k_ref[0].astype(F32), cos, sin) * (dk ** -0.5)
    v = v_ref[0].astype(BF16)
    qb = q.astype(BF16)
    s = _dot_nt(qb, k.astype(BF16)) * inner_ref[0]
    state = state_ref[...]
    o = _dot(s.astype(BF16), v) + _dot(qb, state.astype(BF16)) * qdec_ref[0]
    kd_t = (k * kdec_ref[0]).T.astype(BF16)
    state_ref[...] = state * cdec_ref[0] + _dot(kd_t, v)
    mu = jnp.mean(o, axis=-1, keepdims=True)
    oc = o - mu
    var = jnp.mean(oc * oc, axis=-1, keepdims=True)
    on = oc * lax.rsqrt(var + RET_GN_EPS) * gng_ref[...] + gnb_ref[...]
    gate = gate_ref[0].astype(F32)
    o_ref[0] = (gate * jax.nn.sigmoid(gate) * on).astype(o_ref.dtype)


def retention_mix(h, w_in, gn_g, gn_b, w_out):
    b, t, d = h.shape
    nh, c = RET_HEADS, _pick(t, RET_CHUNK)
    dk = d // nh
    dv = 2 * dk
    proj = mm(h.reshape(b * t, d), w_in, out_dtype=BF16, name="ret_in").reshape(b, t, 6 * d)
    pos = jnp.arange(t, dtype=F32)
    inv_freq = ROPE_BASE ** (-jnp.arange(0, dk, 2, dtype=F32) / dk)
    ang = pos[:, None] * inv_freq[None, :]
    cos, sin = jnp.cos(ang), jnp.sin(ang)
    log_gamma = jnp.log(1.0 - 2.0 ** (-5.0 - jnp.arange(nh, dtype=F32)))
    idx = jnp.arange(c, dtype=F32)
    diff = idx[:, None] - idx[None, :]
    inner = jnp.where(diff >= 0, jnp.exp(jnp.maximum(diff, 0.0)[None] * log_gamma[:, None, None]), 0.0)
    q_dec = jnp.exp((idx + 1.0)[None] * log_gamma[:, None])[:, :, None]
    k_dec = jnp.exp((c - 1.0 - idx)[None] * log_gamma[:, None])[:, :, None]
    c_dec = jnp.exp(c * log_gamma)[:, None, None]
    nq = (nh * dk) // dk
    o = pl.pallas_call(
        functools.partial(_retention_body, dk=dk),
        grid=(b, nh, t // c),
        in_specs=[
            pl.BlockSpec((1, c, dk), lambda bi, hi, ci: (bi, ci, hi)),
            pl.BlockSpec((1, c, dk), lambda bi, hi, ci: (bi, ci, nq + hi)),
            pl.BlockSpec((1, c, dv), lambda bi, hi, ci: (bi, ci, nq + hi)),
            pl.BlockSpec((1, c, dv), lambda bi, hi, ci: (bi, ci, 2 * nq + hi)),
            pl.BlockSpec((c, dk // 2), lambda bi, hi, ci: (ci, 0)),
            pl.BlockSpec((c, dk // 2), lambda bi, hi, ci: (ci, 0)),
            pl.BlockSpec((1, c, c), lambda bi, hi, ci: (hi, 0, 0)),
            pl.BlockSpec((1, c, 1), lambda bi, hi, ci: (hi, 0, 0)),
            pl.BlockSpec((1, c, 1), lambda bi, hi, ci: (hi, 0, 0)),
            pl.BlockSpec((1, 1, 1), lambda bi, hi, ci: (hi, 0, 0)),
            pl.BlockSpec((1, dv), lambda bi, hi, ci: (0, hi)),
            pl.BlockSpec((1, dv), lambda bi, hi, ci: (0, hi)),
        ],
        out_specs=pl.BlockSpec((1, c, dv), lambda bi, hi, ci: (bi, ci, hi)),
        out_shape=jax.ShapeDtypeStruct((b, t, nh * dv), BF16),
        scratch_shapes=[pltpu.VMEM((dk, dv), F32)],
        compiler_params=_cparams("parallel", "parallel", "arbitrary"),
        name="retention",
    )(proj, proj, proj, proj, cos, sin, inner, q_dec, k_dec, c_dec, gn_g.reshape(1, -1), gn_b.reshape(1, -1))
    return mm(o.reshape(b * t, nh * dv), w_out, name="ret_out").reshape(b, t, d)


def _rwkv_pre_body(x_ref, prev_ref, g_ref, scale_ref, shift_ref, mu_ref, o_ref):
    ti = pl.program_id(1)
    g, scale, shift = g_ref[...], scale_ref[0], shift_ref[0]
    h = _rms(x_ref[0], g) * (1.0 + scale) + shift
    hp = _rms(prev_ref[0], g) * (1.0 + scale) + shift
    last = jnp.where(ti == 0, 0.0, hp[7:8, :])
    row = lax.broadcasted_iota(jnp.int32, h.shape, 0)
    shifted = jnp.where(row == 0, last, pltpu.roll(h, 1, axis=0))
    xx = shifted - h
    for s in range(6):
        o_ref[s, 0] = (h + xx * mu_ref[s:s + 1, :]).astype(o_ref.dtype)


def rwkv_pre(x, g_pre, scale, shift, mu, tt=256):
    b, t, d = x.shape
    tt = _pick(t, tt)
    bvec = pl.BlockSpec((1, 1, d), lambda bi, ti: (bi, 0, 0))
    return pl.pallas_call(
        _rwkv_pre_body,
        grid=(b, t // tt),
        in_specs=[
            pl.BlockSpec((1, tt, d), lambda bi, ti: (bi, ti, 0)),
            pl.BlockSpec((1, 8, d), lambda bi, ti: (bi, jnp.maximum(ti * (tt // 8) - 1, 0), 0)),
            pl.BlockSpec((1, d), lambda bi, ti: (0, 0)),
            bvec, bvec,
            pl.BlockSpec((6, d), lambda bi, ti: (0, 0)),
        ],
        out_specs=pl.BlockSpec((6, 1, tt, d), lambda bi, ti: (0, bi, ti, 0)),
        out_shape=jax.ShapeDtypeStruct((6, b, t, d), BF16),
        compiler_params=_cparams("parallel", "parallel"),
        name="rwkv_pre",
    )(x, x, g_pre.reshape(1, d), scale.reshape(b, 1, d), shift.reshape(b, 1, d), mu)


def _rwkv_scan_body(r_ref, k_ref, v_ref, a_ref, lw_ref, g_ref, kk_ref, ka_ref, rk_ref, lng_ref, lnb_ref, o_ref,
                    state_ref, *, chunk, heads, n):
    @pl.when(pl.program_id(2) == 0)
    def _():
        state_ref[...] = jnp.zeros_like(state_ref)

    ln = chunk
    hs = range(heads)
    row = lax.broadcasted_iota(jnp.int32, (ln, ln), 0)
    col = lax.broadcasted_iota(jnp.int32, (ln, ln), 1)
    incl = col <= row
    strict = col < row
    tri = incl.astype(BF16)
    eye = (row == col).astype(F32)
    wid = heads * n
    seg = (lax.broadcasted_iota(jnp.int32, (wid, wid), 0) // n
           == lax.broadcasted_iota(jnp.int32, (wid, wid), 1) // n).astype(BF16)

    def split_dot(m, x):
        hi = x.astype(BF16)
        lo = (x - hi.astype(F32)).astype(BF16)
        return _dot(m, hi) + _dot(m, lo)

    def seg_sum(x):
        hi = x.astype(BF16)
        lo = (x - hi.astype(F32)).astype(BF16)
        return _dot(hi, seg) + _dot(lo, seg)

    heads_of = lambda x: [x[:, hd * n:(hd + 1) * n] for hd in hs]

    r, k, v, a, lw = r_ref[0], k_ref[0], v_ref[0], a_ref[0], lw_ref[0]
    kk = k * kk_ref[...]
    kk = kk / jnp.maximum(jnp.sqrt(seg_sum(kk * kk)), 1e-12)
    k2 = k * (1.0 + (a - 1.0) * ka_ref[...])
    beta = kk * a
    c = split_dot(tri, lw)
    c_last = c[ln - 1:ln, :]
    e_neg = jnp.exp(-c)
    e_rem = jnp.exp(c_last - c)
    e_last = heads_of(jnp.exp(c_last))
    a_t = heads_of(-kk * jnp.exp(c - lw))
    r_t = heads_of(r * jnp.exp(c))
    b_t = heads_of(beta * e_neg)
    k_t = heads_of(k2 * e_neg)
    b_rem = heads_of(beta * e_rem)
    k_rem = heads_of(k2 * e_rem)
    vs = heads_of(v)
    bonus = seg_sum(r * k2 * rk_ref[...]) * v

    ar = [jnp.concatenate([a_t[h], r_t[h]], axis=0).astype(BF16) for h in hs]
    bk = [jnp.concatenate([b_t[h], k_t[h]], axis=0).astype(BF16) for h in hs]
    bk_rem = [jnp.concatenate([b_rem[h], k_rem[h]], axis=0).astype(BF16) for h in hs]
    vb = [vs[h].astype(BF16) for h in hs]
    state = [state_ref[h] for h in hs]
    p = [_dot_nt(ar[h], bk[h]) for h in hs]
    q0 = [_dot_nt(ar[h], state[h].astype(BF16)) for h in hs]
    m_ab = [jnp.where(strict, p[h][:ln, :ln], 0.0) for h in hs]
    m_ak = [jnp.where(strict, p[h][:ln, ln:], 0.0).astype(BF16) for h in hs]
    m_r = [jnp.concatenate([jnp.where(incl, p[h][ln:, :ln], 0.0), jnp.where(incl, p[h][ln:, ln:], 0.0)],
                           axis=1).astype(BF16) for h in hs]
    rhs = [q0[h][:ln] + _dot(m_ak[h], vb[h]) for h in hs]
    inv = [eye + m_ab[h] for h in hs]
    pw = [m_ab[h].astype(BF16) for h in hs]
    for _ in range(int(math.log2(ln)) - 1):
        pw = [_dot(pw[h], pw[h]).astype(BF16) for h in hs]
        inv = [inv[h] + _dot(inv[h].astype(BF16), pw[h]) for h in hs]
    u = [_dot(inv[h].astype(BF16), rhs[h].astype(BF16)) for h in hs]
    uv = [jnp.concatenate([u[h], vs[h]], axis=0) for h in hs]
    y = [q0[h][ln:] + _dot(m_r[h], uv[h].astype(BF16)) for h in hs]
    for h in hs:
        state_ref[h] = state[h] * e_last[h] + _dot(uv[h].T.astype(BF16), bk_rem[h])
    yn = []
    for h in hs:
        mu = jnp.mean(y[h], axis=-1, keepdims=True)
        yc = y[h] - mu
        var = jnp.mean(yc * yc, axis=-1, keepdims=True)
        yn.append(yc * lax.rsqrt(var + RWKV_GN_EPS))
    yn = jnp.concatenate(yn, axis=-1) * lng_ref[...] + lnb_ref[...]
    o_ref[0] = ((yn + bonus) * g_ref[0]).astype(o_ref.dtype)


def rwkv_scan(r, k, v, a, lw, g, k_k, k_a, r_k, ln_g, ln_b):
    b, t, d = r.shape
    n = RWKV_HEAD_DIM
    hps = min(RWKV_HEADS_PER_STEP, d // n)
    w = hps * n
    ln = _pick(t, RWKV_CHUNK)
    tok = pl.BlockSpec((1, ln, w), lambda bi, hi, ci: (bi, ci, hi))
    vec = pl.BlockSpec((1, w), lambda bi, hi, ci: (0, hi))
    return pl.pallas_call(
        functools.partial(_rwkv_scan_body, chunk=ln, heads=hps, n=n),
        grid=(b, d // w, t // ln),
        in_specs=[tok] * 6 + [vec] * 5,
        out_specs=tok,
        out_shape=jax.ShapeDtypeStruct((b, t, d), BF16),
        scratch_shapes=[pltpu.VMEM((hps, n, n), F32)],
        compiler_params=_cparams("parallel", "parallel", "arbitrary"),
        name="rwkv_scan",
    )(r, k, v, a, lw, g, k_k.reshape(1, d), k_a.reshape(1, d), r_k.reshape(1, d), ln_g.reshape(1, d),
      ln_b.reshape(1, d))


def _tanh_epi(parts, extras):
    return jnp.tanh(parts[0])


def _sigmoid_epi(parts, extras):
    return jax.nn.sigmoid(parts[0])


def _sigmoid_bias_epi(parts, extras):
    return jax.nn.sigmoid(extras[0] + parts[0])


def _logdecay_epi(parts, extras):
    return -jnp.exp(-jax.nn.softplus(-(extras[0] + parts[0])) - 0.5)


def rwkv7_time_mix(x, g_pre, scale, shift, mu, w_rkv, w0, w_la, w_lb, a0, a_la, a_lb, g_la, g_lb, k_k, k_a, r_k,
                   ln_g, ln_b, w_out):
    b, t, d = x.shape
    xs = rwkv_pre(x, g_pre, scale, shift, mu).reshape(6 * b * t, d)
    tn = _pick(d, 1024)
    w_rkv2d = w_rkv.reshape(3 * d, d)
    r = mm(xs, [(w_rkv2d, 0, 0)], a_part=(0, 6), name="rwkv_r")
    k = mm(xs, [(w_rkv2d, 0, 1)], a_part=(1, 6), name="rwkv_k")
    v = mm(xs, [(w_rkv2d, 0, 2)], a_part=(2, 6), name="rwkv_v")
    lw = mm(mm(xs, w_la, a_part=(3, 6), epi=_tanh_epi, out_dtype=BF16, name="rwkv_w_la"), w_lb, epi=_logdecay_epi,
            extras=[_row_extra(w0, tn)], name="rwkv_w_lb")
    a = mm(mm(xs, a_la, a_part=(4, 6), out_dtype=BF16, name="rwkv_a_la"), a_lb, epi=_sigmoid_bias_epi,
           extras=[_row_extra(a0, tn)], name="rwkv_a_lb")
    g = mm(mm(xs, g_la, a_part=(5, 6), epi=_sigmoid_epi, out_dtype=BF16, name="rwkv_g_la"), g_lb, name="rwkv_g_lb")
    sh = lambda z: z.reshape(b, t, d)
    z = rwkv_scan(sh(r), sh(k), sh(v), sh(a), sh(lw), sh(g), k_k, k_a, r_k, ln_g, ln_b)
    return mm(z.reshape(b * t, d), w_out, name="rwkv_out").reshape(b, t, d)


def _nsa_compress_body(kv_ref, pe_ref, w1_ref, w2_ref, o_ref, *, nchunk, dh):
    st = NSA_CMP_STRIDE
    hid = w1_ref.shape[1]
    first = jnp.zeros((nchunk, hid), F32)
    second = jnp.zeros((nchunk, hid), F32)
    for j in range(st):
        xj = kv_ref[0, pl.ds(j, nchunk, stride=st), :]
        first = first + _dot((xj + pe_ref[j:j + 1, :]).astype(BF16), w1_ref[j * dh:(j + 1) * dh, :])
        second = second + _dot((xj + pe_ref[st + j:st + j + 1, :]).astype(BF16),
                               w1_ref[(st + j) * dh:(st + j + 1) * dh, :])
    hidden = jax.nn.gelu(first + pltpu.roll(second, nchunk - 1, axis=0))
    out = _dot(hidden.astype(BF16), w2_ref[...])
    row = lax.broadcasted_iota(jnp.int32, out.shape, 0)
    o_ref[0, 0] = jnp.where(row < nchunk - 1, out, 0.0)


def nsa_compress(proj, col0, pe, w1, w2):
    b, t, _ = proj.shape
    g, dh = NSA_KV_GROUPS, NSA_HEAD_DIM
    nchunk = t // NSA_CMP_STRIDE
    hid = w1.shape[1]
    return pl.pallas_call(
        functools.partial(_nsa_compress_body, nchunk=nchunk, dh=dh),
        grid=(b, g),
        in_specs=[
            pl.BlockSpec((1, t, dh), lambda bi, gi: (bi, 0, col0 // dh + gi)),
            pl.BlockSpec((NSA_CMP_BLOCK, dh), lambda bi, gi: (0, 0)),
            pl.BlockSpec((NSA_CMP_BLOCK * dh, hid), lambda bi, gi: (0, 0)),
            pl.BlockSpec((hid, dh), lambda bi, gi: (0, 0)),
        ],
        out_specs=pl.BlockSpec((1, 1, nchunk, dh), lambda bi, gi: (bi, gi, 0, 0)),
        out_shape=jax.ShapeDtypeStruct((b, g, nchunk, dh), F32),
        compiler_params=_cparams("parallel", "parallel"),
        name="nsa_compress",
    )(proj, pe, w1.astype(BF16), w2.astype(BF16))


def _stack_heads(q, r, dh):
    return jnp.concatenate([q[:, i * dh:(i + 1) * dh] for i in range(r)], axis=0)


def _unstack_heads(o, r, tq):
    return jnp.concatenate([o[i * tq:(i + 1) * tq] for i in range(r)], axis=-1)


def _gate_rows(gates, branch, r, tq):
    return jnp.concatenate([gates[:, branch * r + i:branch * r + i + 1] for i in range(r)], axis=0)


def _nsa_cmp_body(q_ref, kc_ref, vc_ref, gates_ref, selmap_t_ref, o_ref, sel_ref, *, tq, r, dh, n_sel):
    q0 = pl.program_id(2) * tq
    qs = (_stack_heads(q_ref[0], r, dh) * (dh ** -0.5)).astype(BF16)
    kc, vc = kc_ref[0, 0].astype(BF16), vc_ref[0, 0].astype(BF16)
    ncp = kc.shape[0]
    s = _dot_nt(qs, kc)
    rows = lax.broadcasted_iota(jnp.int32, (r * tq, ncp), 0)
    cmp_end = lax.broadcasted_iota(jnp.int32, (r * tq, ncp), 1) * NSA_CMP_STRIDE + (NSA_CMP_BLOCK - 1)
    t_pos = q0 + rows % tq
    vis = cmp_end <= t_pos
    s = jnp.where(vis, s, NEG_INF)
    e = jnp.exp(s - jnp.max(s, axis=-1, keepdims=True))
    p = e / jnp.sum(e, axis=-1, keepdims=True) * vis.astype(F32)
    o = _dot(p.astype(BF16), vc)
    o = o * _gate_rows(gates_ref[0, 0], 0, r, tq)
    o_ref[0] = _unstack_heads(o, r, tq)
    p_sum = p[0:tq]
    for i in range(1, r):
        p_sum = p_sum + p[i * tq:(i + 1) * tq]
    hi = p_sum.astype(BF16)
    lo = (p_sum - hi.astype(F32)).astype(BF16)
    selmap_t = selmap_t_ref[...]
    imp = _dot_nt(selmap_t, hi) + _dot_nt(selmap_t, lo)
    ns = imp.shape[0]
    blk = lax.broadcasted_iota(jnp.int32, (ns, tq), 0)
    tq_pos = q0 + lax.broadcasted_iota(jnp.int32, (ns, tq), 1)
    cur = tq_pos // NSA_SEL_BLOCK
    valid = blk * NSA_SEL_BLOCK <= tq_pos
    forced = (blk == 0) | (blk == cur) | (blk == cur - 1)
    score = jnp.where(valid, jnp.where(forced, POS_BIG, imp), NEG_INF)
    rank = jnp.zeros((ns, tq), F32)
    for m in range(ns):
        sm = score[m:m + 1, :]
        beats = (sm > score) | ((sm == score) & (blk > m))
        rank = rank + beats.astype(F32)
    chosen = ((rank < n_sel) & valid).astype(F32)
    sel_ref[0, 0] = (chosen.T - 1.0).astype(sel_ref.dtype)


def _flash_update(s, v, m_ref, l_ref, acc_ref):
    lanes = m_ref.shape[1]
    assert s.shape[1] % lanes == 0 and acc_ref.shape[1] == lanes
    m_old = m_ref[...]
    m_new = jnp.maximum(m_old, jnp.max(s, axis=-1, keepdims=True))
    alpha = jnp.exp(m_old - m_new)
    p = jnp.exp(s - jnp.concatenate([m_new] * (s.shape[1] // lanes), axis=1))
    l_ref[...] = alpha * l_ref[...] + jnp.sum(p, axis=-1, keepdims=True)
    acc_ref[...] = alpha * acc_ref[...] + _dot(p.astype(BF16), v)
    m_ref[...] = m_new


def _nsa_sel_body(qi_tbl, ki_tbl, q_ref, k_ref, v_ref, gates_ref, oin_ref, sel_ref, o_ref, qa_ref, m_ref, l_ref,
                  acc_ref, *, tq, tk, r, dh):
    step = pl.program_id(2)
    qi, ki = qi_tbl[step], ki_tbl[step]
    ns = sel_ref.shape[-1]
    assert ns <= dh

    @pl.when(ki == 0)
    def _():
        m_ref[...] = jnp.full_like(m_ref, NEG_INF)
        l_ref[...] = jnp.zeros_like(l_ref)
        acc_ref[...] = jnp.zeros_like(acc_ref)
        qs = _stack_heads(q_ref[0], r, dh) * (dh ** -0.5)
        pieces = [qs, jnp.concatenate([sel_ref[0, 0].astype(F32)] * r, axis=0)]
        if ns < dh:
            pieces.append(jnp.zeros((r * tq, dh - ns), F32))
        qa_ref[...] = jnp.concatenate(pieces, axis=1).astype(BF16)

    blk_of_key = (ki * tk + lax.broadcasted_iota(jnp.int32, (tk, dh), 0)) // NSA_SEL_BLOCK
    own_block = jnp.where(blk_of_key == lax.broadcasted_iota(jnp.int32, (tk, dh), 1), POS_BIG, 0.0)
    ka = jnp.concatenate([k_ref[0], own_block], axis=1).astype(BF16)
    s = _dot_nt(qa_ref[...], ka)
    v = v_ref[0].astype(BF16)

    @pl.when(ki < qi)
    def _():
        _flash_update(s, v, m_ref, l_ref, acc_ref)

    @pl.when(ki == qi)
    def _():
        t_loc = lax.broadcasted_iota(jnp.int32, (r * tq, tk), 0) % tq
        k_loc = lax.broadcasted_iota(jnp.int32, (r * tq, tk), 1)
        _flash_update(jnp.where(k_loc <= t_loc, s, NEG_INF), v, m_ref, l_ref, acc_ref)
        o = acc_ref[...] / l_ref[...] * _gate_rows(gates_ref[0, 0], 1, r, tq)
        o_ref[0] = (oin_ref[0] + _unstack_heads(o, r, tq)).astype(o_ref.dtype)


def _nsa_win_body(q_ref, *rest, tq, r, dh, nwin):
    k_refs, v_refs = rest[:nwin], rest[nwin:2 * nwin]
    gates_ref, oin_ref, o_ref = rest[2 * nwin:]
    qi = pl.program_id(2)
    qs = (_stack_heads(q_ref[0], r, dh) * (dh ** -0.5)).astype(BF16)
    t_loc = lax.broadcasted_iota(jnp.int32, (r * tq, tq), 0) % tq
    k_loc = lax.broadcasted_iota(jnp.int32, (r * tq, tq), 1)
    tiles = []
    for j in range(nwin):
        s = _dot_nt(qs, k_refs[j][0].astype(BF16))
        if j == nwin - 1:
            s = jnp.where(k_loc <= t_loc, s, NEG_INF)
        else:
            if j == 0:
                s = jnp.where(k_loc > t_loc, s, NEG_INF)
            s = jnp.where(qi - (nwin - 1) + j >= 0, s, NEG_INF)
        tiles.append(s)
    s = jnp.concatenate(tiles, axis=1)
    p = jnp.exp(s - jnp.max(s, axis=-1, keepdims=True))
    v = jnp.concatenate([v_refs[j][0] for j in range(nwin)], axis=0).astype(BF16)
    o = _dot(p.astype(BF16), v) / jnp.sum(p, axis=-1, keepdims=True)
    o = o * _gate_rows(gates_ref[0, 0], 2, r, tq)
    o_ref[0] = (oin_ref[0] + _unstack_heads(o, r, tq)).astype(o_ref.dtype)


def nsa_mix(h, w_in, pe_k, pe_v, ck_w1, ck_w2, cv_w1, cv_w2, w_out, tq=256):
    b, t, d = h.shape
    nh, g, dh = NSA_HEADS, NSA_KV_GROUPS, NSA_HEAD_DIM
    r = nh // g
    hd, kd = nh * dh, g * dh
    n_main = hd + 6 * kd
    h2 = h.reshape(b * t, d)
    proj = mm(h2, w_in[:, :n_main], name="nsa_in").reshape(b, t, n_main)
    n_gate = w_in.shape[1] - n_main
    w_gate = jnp.pad(w_in[:, n_main:], ((0, 0), (0, 128 - n_gate)))
    gates = mm(h2, w_gate, epi=_sigmoid_epi, name="nsa_gates")[:, :n_gate]
    gates = jnp.transpose(gates.reshape(b, t, 3, g, r), (0, 3, 1, 2, 4)).reshape(b, g, t, 3 * r)
    k_cmp = nsa_compress(proj, hd, pe_k, ck_w1, ck_w2)
    v_cmp = nsa_compress(proj, hd + kd, pe_v, cv_w1, cv_w2)
    ncp = k_cmp.shape[2]
    ns = t // NSA_SEL_BLOCK
    n_sel = min(NSA_SEL_TOP, ns)
    cs = jnp.arange(ncp)[None, :] * NSA_CMP_STRIDE
    ss = jnp.arange(ns)[:, None] * NSA_SEL_BLOCK
    sel_map_t = (jnp.maximum(jnp.minimum(cs + NSA_CMP_BLOCK, ss + NSA_SEL_BLOCK) - jnp.maximum(cs, ss), 0)
                 .astype(F32) / NSA_CMP_BLOCK).astype(BF16)
    tq = _pick(t, tq)
    assert NSA_WINDOW % tq == 0
    nq = t // tq
    wq = r * dh
    col = lambda base: base // dh
    q_spec3 = pl.BlockSpec((1, tq, wq), lambda bi, gi, qi: (bi, qi, gi))
    gate_spec3 = pl.BlockSpec((1, 1, tq, 3 * r), lambda bi, gi, qi: (bi, gi, qi, 0))
    o_cmp, sel = pl.pallas_call(
        functools.partial(_nsa_cmp_body, tq=tq, r=r, dh=dh, n_sel=n_sel),
        grid=(b, g, nq),
        in_specs=[
            q_spec3,
            pl.BlockSpec((1, 1, ncp, dh), lambda bi, gi, qi: (bi, gi, 0, 0)),
            pl.BlockSpec((1, 1, ncp, dh), lambda bi, gi, qi: (bi, gi, 0, 0)),
            gate_spec3,
            pl.BlockSpec((ns, ncp), lambda bi, gi, qi: (0, 0)),
        ],
        out_specs=[q_spec3, pl.BlockSpec((1, 1, tq, ns), lambda bi, gi, qi: (bi, gi, qi, 0))],
        out_shape=[jax.ShapeDtypeStruct((b, t, hd), F32), jax.ShapeDtypeStruct((b, g, t, ns), BF16)],
        compiler_params=_cparams("parallel", "parallel", "parallel"),
        name="nsa_cmp",
    )(proj, k_cmp, v_cmp, gates, sel_map_t)

    tk = tq
    pairs = [(qi, ki) for qi in range(nq) for ki in range(qi + 1)]
    qi_tbl = jnp.asarray([pq for pq, _ in pairs], jnp.int32)
    ki_tbl = jnp.asarray([pk for _, pk in pairs], jnp.int32)
    q_spec_p = pl.BlockSpec((1, tq, wq), lambda bi, gi, p, qt, kt: (bi, qt[p], gi))

    def kv_spec_p(base):
        return pl.BlockSpec((1, tk, dh), lambda bi, gi, p, qt, kt: (bi, kt[p], col(base) + gi))

    o_sel = pl.pallas_call(
        functools.partial(_nsa_sel_body, tq=tq, tk=tk, r=r, dh=dh),
        grid_spec=pltpu.PrefetchScalarGridSpec(
            num_scalar_prefetch=2,
            grid=(b, g, len(pairs)),
            in_specs=[q_spec_p, kv_spec_p(hd + 2 * kd), kv_spec_p(hd + 3 * kd),
                      pl.BlockSpec((1, 1, tq, 3 * r), lambda bi, gi, p, qt, kt: (bi, gi, qt[p], 0)),
                      q_spec_p,
                      pl.BlockSpec((1, 1, tq, ns), lambda bi, gi, p, qt, kt: (bi, gi, qt[p], 0))],
            out_specs=q_spec_p,
            scratch_shapes=[pltpu.VMEM((r * tq, 2 * dh), BF16), pltpu.VMEM((r * tq, dh), F32),
                            pltpu.VMEM((r * tq, dh), F32), pltpu.VMEM((r * tq, dh), F32)],
        ),
        out_shape=jax.ShapeDtypeStruct((b, t, hd), F32),
        compiler_params=_cparams("parallel", "parallel", "arbitrary"),
        name="nsa_sel",
    )(qi_tbl, ki_tbl, proj, proj, proj, gates, o_cmp, sel)

    nwin = NSA_WINDOW // tq + 1

    def kv_spec_w(base, j):
        return pl.BlockSpec((1, tq, dh),
                            lambda bi, gi, qi: (bi, jnp.maximum(qi - (nwin - 1) + j, 0), col(base) + gi))

    o_all = pl.pallas_call(
        functools.partial(_nsa_win_body, tq=tq, r=r, dh=dh, nwin=nwin),
        grid=(b, g, nq),
        in_specs=([q_spec3] + [kv_spec_w(hd + 4 * kd, j) for j in range(nwin)]
                  + [kv_spec_w(hd + 5 * kd, j) for j in range(nwin)] + [gate_spec3, q_spec3]),
        out_specs=q_spec3,
        out_shape=jax.ShapeDtypeStruct((b, t, hd), BF16),
        compiler_params=_cparams("parallel", "parallel", "parallel"),
        name="nsa_win",
    )(proj, *([proj] * (2 * nwin)), gates, o_sel)
    return mm(o_all.reshape(b * t, hd), w_out, name="nsa_out").reshape(b, t, d)


def _silu(a):
    return a * jax.nn.sigmoid(a)


def _ada_epi(parts, extras):
    return parts[0] + extras[0]


def ada_modulation(c, ada_w, ada_b):
    depth, d, n6 = ada_w.shape
    b = c.shape[0]
    rows = ((b + 7) // 8) * 8
    cond = jnp.pad(c, ((0, rows - b), (0, 0)))
    w2d = ada_w.reshape(depth * d, n6)
    mods = []
    for i in range(depth):
        mods.append(mm(cond, [(w2d, 0, i)], a_act=_silu, epi=_ada_epi, extras=[_row_extra(ada_b[i], _pick(n6, 1024))],
                       precision=HIGHEST, name="ada_mod")[:b])
    return jnp.stack(mods)


def kernel(x, c, ada_w, ada_b, norm_g, mlp_w1, mlp_w2, rwkv_mu, rwkv_w_rkv, rwkv_w0, rwkv_w_la, rwkv_w_lb, rwkv_a0, rwkv_a_la, rwkv_a_lb, rwkv_g_la, rwkv_g_lb, rwkv_k_k, rwkv_k_a, rwkv_r_k, rwkv_ln_g, rwkv_ln_b, rwkv_w_out, ret_w_in, ret_gn_g, ret_gn_b, ret_w_out, conv_pw1_w, conv_pw1_b, conv_dw_w, conv_dw_b, conv_ln_g, conv_ln_b, conv_pw2_w, conv_pw2_b, nsa_w_in, nsa_pe_k, nsa_pe_v, nsa_ck_w1, nsa_ck_w2, nsa_cv_w1, nsa_cv_w2, nsa_w_out):
    b, t, d = x.shape
    depth = ada_w.shape[0]
    mod = ada_modulation(c, ada_w, ada_b).reshape(depth, b, 6, d)
    h = None
    for i in range(depth):
        sh_t, sc_t, gt_t, sh_c, sc_c, gt_c = (mod[i, :, j] for j in range(6))
        kind = i % 4
        if kind == 0:
            y = rwkv7_time_mix(x, norm_g[i, 0], sc_t, sh_t, rwkv_mu, rwkv_w_rkv, rwkv_w0, rwkv_w_la, rwkv_w_lb,
                               rwkv_a0, rwkv_a_la, rwkv_a_lb, rwkv_g_la, rwkv_g_lb, rwkv_k_k, rwkv_k_a,
                               rwkv_r_k.reshape(-1), rwkv_ln_g, rwkv_ln_b, rwkv_w_out)
        else:
            if h is None:
                (h,) = resid_norm(x, pre=(norm_g[i, 0], sc_t, sh_t))
            if kind == 1:
                y = retention_mix(h, ret_w_in, ret_gn_g, ret_gn_b, ret_w_out)
            elif kind == 2:
                y = conformer_conv_mix(h, conv_pw1_w, conv_pw1_b, conv_dw_w, conv_dw_b, conv_ln_g, conv_ln_b,
                                       conv_pw2_w, conv_pw2_b)
            else:
                y = nsa_mix(h, nsa_w_in, nsa_pe_k, nsa_pe_v, nsa_ck_w1, nsa_ck_w2, nsa_cv_w1, nsa_cv_w2, nsa_w_out)
        x, h = resid_norm(x, y, post=(norm_g[i, 1], gt_t), pre=(norm_g[i, 2], sc_c, sh_c))
        y = sqrelu_mlp(h.reshape(b * t, d), mlp_w1, mlp_w2, i).reshape(b, t, d)
        nxt = i + 1
        if nxt < depth and nxt % 4 != 0:
            x, h = resid_norm(x, y, post=(norm_g[i, 3], gt_c),
                              pre=(norm_g[nxt, 0], mod[nxt, :, 1], mod[nxt, :, 0]))
        else:
            (x,) = resid_norm(x, y, post=(norm_g[i, 3], gt_c))
            h = None
    return x
```

```python
import functools
import math

import jax
import jax.numpy as jnp
from jax import lax
from jax.experimental import pallas as pl
from jax.experimental.pallas import tpu as pltpu

F32 = jnp.float32
BF16 = jnp.bfloat16
HIGHEST = lax.Precision.HIGHEST

NORM_EPS = 1e-6
NEG_INF = -1e30
POS_BIG = 1e30
LOG2_E = math.log2(math.e)

RWKV_HEAD_DIM = 64
RWKV_GN_EPS = 64e-5
RWKV_CHUNK = 64
RWKV_HEADS_PER_STEP = 32

RET_HEADS = 8
RET_CHUNK = 256
RET_GN_EPS = 1e-5
ROPE_BASE = 10000.0

CONV_WIDTH = 31
CONV_HALO = 32
CONV_LN_EPS = 1e-5

NSA_HEADS = 16
NSA_KV_GROUPS = 4
NSA_HEAD_DIM = 128
NSA_CMP_BLOCK = 32
NSA_CMP_STRIDE = 16
NSA_SEL_BLOCK = 64
NSA_SEL_TOP = 16
NSA_WINDOW = 512
NSA_SEL_KEY_TILE = 512

VMEM_LIMIT_BYTES = 56 * 1024 * 1024


def _cparams(*sem):
    return pltpu.CompilerParams(dimension_semantics=sem, vmem_limit_bytes=VMEM_LIMIT_BYTES)


def _dot(a, b, precision=None):
    return jnp.dot(a, b, preferred_element_type=F32, precision=precision)


def _dot_nt(a, b, precision=None):
    return lax.dot_general(a, b, (((1,), (1,)), ((), ())), preferred_element_type=F32, precision=precision)


def _pick(n, pref):
    if n <= pref:
        return n
    t = pref
    while n % t:
        t //= 2
    return t


def _mm_body(*refs, n_w, n_ex, n_out, nk, epi, precision, has_a_add, a_act):
    a_ref = refs[0]
    pos = 1
    a_add_ref = None
    if has_a_add:
        a_add_ref = refs[pos]
        pos += 1
    w_refs = refs[pos:pos + n_w]
    pos += n_w
    ex_refs = refs[pos:pos + n_ex]
    pos += n_ex
    o_refs = refs[pos:pos + n_out]
    acc_refs = refs[pos + n_out:]

    def finish(parts):
        res = epi(parts, [e[...] for e in ex_refs])
        if n_out == 1 and not isinstance(res, (tuple, list)):
            res = (res,)
        for o_ref, val in zip(o_refs, res, strict=True):
            o_ref[...] = val.astype(o_ref.dtype)

    a = a_ref[...]
    if a_add_ref is not None:
        a = a.astype(F32) + a_add_ref[...]
    if a_act is not None:
        a = a_act(a)
    if precision is None:
        a = a.astype(BF16)
    parts = []
    for w_ref in w_refs:
        w = w_ref[...]
        if precision is None:
            w = w.astype(BF16)
        parts.append(_dot(a, w, precision))

    if nk == 1:
        finish(parts)
    else:
        k = pl.program_id(2)

        @pl.when(k == 0)
        def _():
            for acc, p in zip(acc_refs, parts):
                acc[...] = p

        @pl.when(k > 0)
        def _():
            for acc, p in zip(acc_refs, parts):
                acc[...] += p

        @pl.when(k == nk - 1)
        def _():
            finish([acc[...] for acc in acc_refs])


def _first(parts, extras):
    return parts[0]


def mm(a, ws, *, n_out=None, extras=(), epi=_first, out_dtype=F32, tm=1024, tn=1024, tk=2048, precision=None,
       a_add=None, a_act=None, a_part=(0, 1), name="mm"):
    m, kdim = a.shape
    a_s, a_parts = a_part
    assert m % a_parts == 0
    m //= a_parts
    if not isinstance(ws, (list, tuple)):
        ws = [(ws, 0, 0)]
    ws = [tuple(w) + (0,) * (3 - len(w)) for w in ws]
    if n_out is None:
        n_out = ws[0][0].shape[1]
    tm = _pick(m, tm)
    tn = _pick(n_out, tn)
    tk = _pick(kdim, tk)
    nk = kdim // tk
    assert m % tm == 0 and n_out % tn == 0 and kdim % tk == 0
    in_specs = [pl.BlockSpec((tm, tk), lambda i, j, k, o=a_s * (m // tm): (i + o, k))]
    args = [a]
    if a_add is not None:
        in_specs.append(pl.BlockSpec((1, tk), lambda i, j, k: (0, k)))
        args.append(a_add)
    for w, off, koff in ws:
        in_specs.append(pl.BlockSpec((tk, tn), lambda i, j, k, off=off, ko=koff * nk: (k + ko, j + off)))
        args.append(w.astype(BF16) if precision is None else w)
    for arr, bshape, imap in extras:
        in_specs.append(pl.BlockSpec(bshape, imap))
        args.append(arr)
    scratch = [pltpu.VMEM((tm, tn), F32) for _ in ws] if nk > 1 else []
    multi = isinstance(out_dtype, (tuple, list))
    out_dtypes = tuple(out_dtype) if multi else (out_dtype,)
    body = functools.partial(_mm_body, n_w=len(ws), n_ex=len(extras), n_out=len(out_dtypes), nk=nk, epi=epi,
                             precision=precision, has_a_add=a_add is not None, a_act=a_act)
    outs = pl.pallas_call(
        body,
        grid=(m // tm, n_out // tn, nk),
        in_specs=in_specs,
        out_specs=[pl.BlockSpec((tm, tn), lambda i, j, k: (i, j)) for _ in out_dtypes],
        out_shape=[jax.ShapeDtypeStruct((m, n_out), dt) for dt in out_dtypes],
        scratch_shapes=scratch,
        compiler_params=_cparams("parallel", "parallel", "arbitrary"),
        name=name,
    )(*args)
    return tuple(outs) if multi else outs[0]


def _row_extra(vec, tn):
    return (vec.reshape(1, -1), (1, tn), lambda i, j, k: (0, j))


def _rms(x, g):
    return x * lax.rsqrt(jnp.mean(x * x, axis=-1, keepdims=True) + NORM_EPS) * g


def _resid_norm_body(*refs, has_y, has_h):
    pos = 0
    x_ref = refs[pos]; pos += 1
    if has_y:
        y_ref, gpost_ref, gate_ref = refs[pos:pos + 3]; pos += 3
    if has_h:
        gpre_ref, scale_ref, shift_ref = refs[pos:pos + 3]; pos += 3
    outs = refs[pos:]
    x = x_ref[0]
    o = 0
    if has_y:
        y = y_ref[0].astype(F32)
        x = x + (1.0 + gate_ref[0]) * _rms(y, gpost_ref[...])
        outs[o][0] = x
        o += 1
    if has_h:
        h = _rms(x, gpre_ref[...]) * (1.0 + scale_ref[0]) + shift_ref[0]
        outs[o][0] = h.astype(outs[o].dtype)


def resid_norm(x, y=None, post=None, pre=None, tt=256):
    b, t, d = x.shape
    tt = _pick(t, tt)
    row = pl.BlockSpec((1, tt, d), lambda bi, ti: (bi, ti, 0))
    vec = pl.BlockSpec((1, d), lambda bi, ti: (0, 0))
    bvec = pl.BlockSpec((1, 1, d), lambda bi, ti: (bi, 0, 0))
    args, in_specs, out_shapes, out_specs = [x], [row], [], []
    if y is not None:
        args += [y, post[0].reshape(1, d), post[1].reshape(b, 1, d)]
        in_specs += [row, vec, bvec]
        out_shapes.append(jax.ShapeDtypeStruct((b, t, d), F32))
        out_specs.append(row)
    if pre is not None:
        args += [pre[0].reshape(1, d), pre[1].reshape(b, 1, d), pre[2].reshape(b, 1, d)]
        in_specs += [vec, bvec, bvec]
        out_shapes.append(jax.ShapeDtypeStruct((b, t, d), BF16))
        out_specs.append(row)
    res = pl.pallas_call(
        functools.partial(_resid_norm_body, has_y=y is not None, has_h=pre is not None),
        grid=(b, t // tt),
        in_specs=in_specs,
        out_specs=out_specs,
        out_shape=out_shapes,
        compiler_params=_cparams("parallel", "parallel"),
        name="resid_norm",
    )(*args)
    return tuple(res)


def mm_resid(a, ws, x, post, pre=None, *, bias=None, tm=512, tk=1024, name="mm_resid"):
    b, t, d = x.shape
    tm = _pick(t, tm)
    tiles_per_batch = t // tm
    row = lambda vec: (vec.reshape(1, d), (1, d), lambda i, j, k: (0, 0))
    per_batch = lambda arr: (arr.reshape(b, 1, d), (1, 1, d), lambda i, j, k: (i // tiles_per_batch, 0, 0))
    extras = [(x.reshape(b * t, d), (tm, d), lambda i, j, k: (i, 0)), row(post[0]), per_batch(post[1])]
    if bias is not None:
        extras.append(row(bias))
    if pre is not None:
        extras += [row(pre[0]), per_batch(pre[1]), per_batch(pre[2])]

    def epi(parts, ex):
        y = parts[0]
        pos = 3
        if bias is not None:
            y = y + ex[pos]
            pos += 1
        x_new = ex[0] + (1.0 + ex[2][0]) * _rms(y, ex[1])
        if pre is None:
            return (x_new,)
        return x_new, _rms(x_new, ex[pos]) * (1.0 + ex[pos + 1][0]) + ex[pos + 2][0]

    outs = mm(a, ws, n_out=d, tn=d, tm=tm, tk=tk, extras=extras, epi=epi,
              out_dtype=(F32,) if pre is None else (F32, BF16), name=name)
    return tuple(o.reshape(b, t, d) for o in outs)


def _relu2(parts, extras):
    r = jnp.maximum(parts[0], 0.0)
    return r * r


def sqrelu_mlp(h2d, w1, w2, layer):
    nl, d, f = w1.shape
    a = mm(h2d, [(w1.reshape(nl * d, f), 0, layer)], epi=_relu2, out_dtype=BF16, name="mlp_up")
    return a, [(w2.reshape(nl * f, d), 0, layer)], None


def _glu(parts, extras):
    return (parts[0] + extras[0]) * jax.nn.sigmoid(parts[1] + extras[1])


def _add_bias(parts, extras):
    return parts[0] + extras[0]


def _conv_ln_body(cur_ref, prev_ref, w_ref, b_ref, g_ref, beta_ref, o_ref, buf_ref, acc_ref, *, tt, d):
    ti = pl.program_id(1)
    halo = prev_ref[0]
    buf_ref[0:CONV_HALO, :] = jnp.where(ti == 0, jnp.zeros_like(halo), halo)
    buf_ref[CONV_HALO:, :] = cur_ref[0]
    lane_chunk = min(d, 256)
    row_chunk = min(tt, 64)
    base = CONV_HALO - (CONV_WIDTH - 1)
    for c0 in range(0, d, lane_chunk):
        for r0 in range(0, tt, row_chunk):
            acc = jnp.zeros((row_chunk, lane_chunk), F32)
            for j in range(CONV_WIDTH):
                tap = w_ref[j:j + 1, c0:c0 + lane_chunk]
                acc = acc + buf_ref[base + r0 + j:base + r0 + j + row_chunk, c0:c0 + lane_chunk] * tap
            acc_ref[r0:r0 + row_chunk, c0:c0 + lane_chunk] = acc
    y = acc_ref[...] + b_ref[...]
    mu = jnp.mean(y, axis=-1, keepdims=True)
    yc = y - mu
    var = jnp.mean(yc * yc, axis=-1, keepdims=True)
    z = yc * lax.rsqrt(var + CONV_LN_EPS) * g_ref[...] + beta_ref[...]
    o_ref[0] = (z * jax.nn.sigmoid(z)).astype(o_ref.dtype)


def conv_ln_silu(u, dw_w, dw_b, ln_g, ln_b, tt=128):
    b, t, d = u.shape
    tt = _pick(t, tt)
    hb = tt // CONV_HALO
    vec = pl.BlockSpec((1, d), lambda bi, ti: (0, 0))
    return pl.pallas_call(
        functools.partial(_conv_ln_body, tt=tt, d=d),
        grid=(b, t // tt),
        in_specs=[
            pl.BlockSpec((1, tt, d), lambda bi, ti: (bi, ti, 0)),
            pl.BlockSpec((1, CONV_HALO, d), lambda bi, ti: (bi, jnp.maximum(ti * hb - 1, 0), 0)),
            pl.BlockSpec((CONV_WIDTH, d), lambda bi, ti: (0, 0)),
            vec, vec, vec,
        ],
        out_specs=pl.BlockSpec((1, tt, d), lambda bi, ti: (bi, ti, 0)),
        out_shape=jax.ShapeDtypeStruct((b, t, d), BF16),
        scratch_shapes=[pltpu.VMEM((tt + CONV_HALO, d), F32), pltpu.VMEM((tt, d), F32)],
        compiler_params=_cparams("parallel", "parallel"),
        name="conv_ln_silu",
    )(u, u, dw_w, dw_b.reshape(1, d), ln_g.reshape(1, d), ln_b.reshape(1, d))


def conformer_conv_mix(h, pw1_w, pw1_b, dw_w, dw_b, ln_g, ln_b, pw2_w, pw2_b):
    b, t, d = h.shape
    tn = _pick(d, 1024)
    b1 = pw1_b.reshape(1, -1)
    u = mm(h.reshape(b * t, d), [(pw1_w, 0), (pw1_w, d // tn)], n_out=d, tn=tn, epi=_glu,
           extras=[(b1, (1, tn), lambda i, j, k: (0, j)),
                   (b1, (1, tn), lambda i, j, k, o=d // tn: (0, j + o))], name="conf_pw1_glu")
    z = conv_ln_silu(u.reshape(b, t, d), dw_w, dw_b, ln_g, ln_b)
    return z.reshape(b * t, d), [(pw2_w, 0, 0)], pw2_b


def _rope(x, cos, sin):
    half = x.shape[-1] // 2
    x1, x2 = x[:, :half], x[:, half:]
    return jnp.concatenate([x1 * cos - x2 * sin, x2 * cos + x1 * sin], axis=-1)


def _retention_body(q_ref, k_ref, v_ref, gate_ref, cos_ref, sin_ref, inner_ref, qdec_ref, kdec_ref, cdec_ref,
                    gng_ref, gnb_ref, o_ref, state_ref, *, dk):
    @pl.when(pl.program_id(2) == 0)
    def _():
        state_ref[...] = jnp.zeros_like(state_ref)

    cos, sin = cos_ref[...], sin_ref[...]
    q = _rope(q_ref[0].astype(F32), cos, sin)
    k = _rope(k_ref[0].astype(F32), cos, sin) * (dk ** -0.5)
    v = v_ref[0].astype(BF16)
    qb = q.astype(BF16)
    s = _dot_nt(qb, k.astype(BF16)) * inner_ref[0]
    state = state_ref[...]
    o = _dot(s.astype(BF16), v) + _dot(qb, state.astype(BF16)) * qdec_ref[0]
    kd_t = (k * kdec_ref[0]).T.astype(BF16)
    state_ref[...] = state * cdec_ref[0] + _dot(kd_t, v)
    mu = jnp.mean(o, axis=-1, keepdims=True)
    oc = o - mu
    var = jnp.mean(oc * oc, axis=-1, keepdims=True)
    on = oc * lax.rsqrt(var + RET_GN_EPS) * gng_ref[...] + gnb_ref[...]
    gate = gate_ref[0].astype(F32)
    o_ref[0] = (gate * jax.nn.sigmoid(gate) * on).astype(o_ref.dtype)


def retention_mix(h, w_in, gn_g, gn_b, w_out):
    b, t, d = h.shape
    nh, c = RET_HEADS, _pick(t, RET_CHUNK)
    dk = d // nh
    dv = 2 * dk
    proj = mm(h.reshape(b * t, d), w_in, out_dtype=BF16, name="ret_in").reshape(b, t, 6 * d)
    pos = jnp.arange(t, dtype=F32)
    inv_freq = ROPE_BASE ** (-jnp.arange(0, dk, 2, dtype=F32) / dk)
    ang = pos[:, None] * inv_freq[None, :]
    cos, sin = jnp.cos(ang), jnp.sin(ang)
    log_gamma = jnp.log(1.0 - 2.0 ** (-5.0 - jnp.arange(nh, dtype=F32)))
    idx = jnp.arange(c, dtype=F32)
    diff = idx[:, None] - idx[None, :]
    inner = jnp.where(diff >= 0, jnp.exp(jnp.maximum(diff, 0.0)[None] * log_gamma[:, None, None]), 0.0)
    q_dec = jnp.exp((idx + 1.0)[None] * log_gamma[:, None])[:, :, None]
    k_dec = jnp.exp((c - 1.0 - idx)[None] * log_gamma[:, None])[:, :, None]
    c_dec = jnp.exp(c * log_gamma)[:, None, None]
    nq = (nh * dk) // dk
    o = pl.pallas_call(
        functools.partial(_retention_body, dk=dk),
        grid=(b, nh, t // c),
        in_specs=[
            pl.BlockSpec((1, c, dk), lambda bi, hi, ci: (bi, ci, hi)),
            pl.BlockSpec((1, c, dk), lambda bi, hi, ci: (bi, ci, nq + hi)),
            pl.BlockSpec((1, c, dv), lambda bi, hi, ci: (bi, ci, nq + hi)),
            pl.BlockSpec((1, c, dv), lambda bi, hi, ci: (bi, ci, 2 * nq + hi)),
            pl.BlockSpec((c, dk // 2), lambda bi, hi, ci: (ci, 0)),
            pl.BlockSpec((c, dk // 2), lambda bi, hi, ci: (ci, 0)),
            pl.BlockSpec((1, c, c), lambda bi, hi, ci: (hi, 0, 0)),
            pl.BlockSpec((1, c, 1), lambda bi, hi, ci: (hi, 0, 0)),
            pl.BlockSpec((1, c, 1), lambda bi, hi, ci: (hi, 0, 0)),
            pl.BlockSpec((1, 1, 1), lambda bi, hi, ci: (hi, 0, 0)),
            pl.BlockSpec((1, dv), lambda bi, hi, ci: (0, hi)),
            pl.BlockSpec((1, dv), lambda bi, hi, ci: (0, hi)),
        ],
        out_specs=pl.BlockSpec((1, c, dv), lambda bi, hi, ci: (bi, ci, hi)),
        out_shape=jax.ShapeDtypeStruct((b, t, nh * dv), BF16),
        scratch_shapes=[pltpu.VMEM((dk, dv), F32)],
        compiler_params=_cparams("parallel", "parallel", "arbitrary"),
        name="retention",
    )(proj, proj, proj, proj, cos, sin, inner, q_dec, k_dec, c_dec, gn_g.reshape(1, -1), gn_b.reshape(1, -1))
    return o.reshape(b * t, nh * dv), [(w_out, 0, 0)], None


def _rwkv_pre_body(x_ref, prev_ref, g_ref, scale_ref, shift_ref, mu_ref, o_ref):
    ti = pl.program_id(1)
    g, scale, shift = g_ref[...], scale_ref[0], shift_ref[0]
    h = _rms(x_ref[0], g) * (1.0 + scale) + shift
    hp = _rms(prev_ref[0], g) * (1.0 + scale) + shift
    last = jnp.where(ti == 0, 0.0, hp[7:8, :])
    row = lax.broadcasted_iota(jnp.int32, h.shape, 0)
    shifted = jnp.where(row == 0, last, pltpu.roll(h, 1, axis=0))
    xx = shifted - h
    for s in range(6):
        o_ref[s, 0] = (h + xx * mu_ref[s:s + 1, :]).astype(o_ref.dtype)


def rwkv_pre(x, g_pre, scale, shift, mu, tt=256):
    b, t, d = x.shape
    tt = _pick(t, tt)
    bvec = pl.BlockSpec((1, 1, d), lambda bi, ti: (bi, 0, 0))
    return pl.pallas_call(
        _rwkv_pre_body,
        grid=(b, t // tt),
        in_specs=[
            pl.BlockSpec((1, tt, d), lambda bi, ti: (bi, ti, 0)),
            pl.BlockSpec((1, 8, d), lambda bi, ti: (bi, jnp.maximum(ti * (tt // 8) - 1, 0), 0)),
            pl.BlockSpec((1, d), lambda bi, ti: (0, 0)),
            bvec, bvec,
            pl.BlockSpec((6, d), lambda bi, ti: (0, 0)),
        ],
        out_specs=pl.BlockSpec((6, 1, tt, d), lambda bi, ti: (0, bi, ti, 0)),
        out_shape=jax.ShapeDtypeStruct((6, b, t, d), BF16),
        compiler_params=_cparams("parallel", "parallel"),
        name="rwkv_pre",
    )(x, x, g_pre.reshape(1, d), scale.reshape(b, 1, d), shift.reshape(b, 1, d), mu)


def _rwkv_scan_body(r_ref, k_ref, v_ref, a_ref, lw_ref, g_ref, kk_ref, ka_ref, rk_ref, lng_ref, lnb_ref, o_ref,
                    state_ref, *, chunk, heads, n):
    @pl.when(pl.program_id(2) == 0)
    def _():
        state_ref[...] = jnp.zeros_like(state_ref)

    ln = chunk
    hs = range(heads)
    row = lax.broadcasted_iota(jnp.int32, (ln, ln), 0)
    col = lax.broadcasted_iota(jnp.int32, (ln, ln), 1)
    incl = col <= row
    strict = col < row
    tri = incl.astype(BF16)
    eye = (row == col).astype(F32)
    wid = heads * n
    seg_w = min(wid, 256)
    assert wid % seg_w == 0 and seg_w % n == 0
    seg = (lax.broadcasted_iota(jnp.int32, (seg_w, seg_w), 0) // n
           == lax.broadcasted_iota(jnp.int32, (seg_w, seg_w), 1) // n).astype(BF16)

    def split_dot(m, x):
        hi = x.astype(BF16)
        lo = (x - hi.astype(F32)).astype(BF16)
        return _dot(m, hi) + _dot(m, lo)

    def seg_sum(x):
        hi = x.astype(BF16)
        lo = (x - hi.astype(F32)).astype(BF16)
        return jnp.concatenate([_dot(hi[:, j:j + seg_w], seg) + _dot(lo[:, j:j + seg_w], seg)
                                for j in range(0, wid, seg_w)], axis=1)

    heads_of = lambda x: [x[:, hd * n:(hd + 1) * n] for hd in hs]

    r, k, v, a, lw = r_ref[0], k_ref[0], v_ref[0], a_ref[0], lw_ref[0]
    kk = k * kk_ref[...]
    kk = kk / jnp.maximum(jnp.sqrt(seg_sum(kk * kk)), 1e-12)
    k2 = k * (1.0 + (a - 1.0) * ka_ref[...])
    beta = kk * a
    c = split_dot(tri, lw)
    c_last = c[ln - 1:ln, :]
    e_neg = jnp.exp(-c)
    e_rem = jnp.exp(c_last - c)
    e_last = heads_of(jnp.exp(c_last))
    a_t = heads_of(-kk * jnp.exp(c - lw))
    r_t = heads_of(r * jnp.exp(c))
    b_t = heads_of(beta * e_neg)
    k_t = heads_of(k2 * e_neg)
    b_rem = heads_of(beta * e_rem)
    k_rem = heads_of(k2 * e_rem)
    vs = heads_of(v)
    bonus = seg_sum(r * k2 * rk_ref[...]) * v

    ar = [jnp.concatenate([a_t[h], r_t[h]], axis=0).astype(BF16) for h in hs]
    bk = [jnp.concatenate([b_t[h], k_t[h]], axis=0).astype(BF16) for h in hs]
    bk_rem = [jnp.concatenate([b_rem[h], k_rem[h]], axis=0).astype(BF16) for h in hs]
    vb = [vs[h].astype(BF16) for h in hs]
    state = [state_ref[h] for h in hs]
    p = [_dot_nt(ar[h], bk[h]) for h in hs]
    q0 = [_dot_nt(ar[h], state[h].astype(BF16)) for h in hs]
    m_ab = [jnp.where(strict, p[h][:ln, :ln], 0.0) for h in hs]
    m_ak = [jnp.where(strict, p[h][:ln, ln:], 0.0).astype(BF16) for h in hs]
    m_r = [jnp.concatenate([jnp.where(incl, p[h][ln:, :ln], 0.0), jnp.where(incl, p[h][ln:, ln:], 0.0)],
                           axis=1).astype(BF16) for h in hs]
    rhs = [q0[h][:ln] + _dot(m_ak[h], vb[h]) for h in hs]
    inv = [eye + m_ab[h] for h in hs]
    pw = [m_ab[h].astype(BF16) for h in hs]
    for _ in range(int(math.log2(ln)) - 1):
        pw = [_dot(pw[h], pw[h]).astype(BF16) for h in hs]
        inv = [inv[h] + _dot(inv[h].astype(BF16), pw[h]) for h in hs]
    u = [_dot(inv[h].astype(BF16), rhs[h].astype(BF16)) for h in hs]
    uv = [jnp.concatenate([u[h], vs[h]], axis=0) for h in hs]
    y = [q0[h][ln:] + _dot(m_r[h], uv[h].astype(BF16)) for h in hs]
    for h in hs:
        state_ref[h] = state[h] * e_last[h] + _dot(uv[h].T.astype(BF16), bk_rem[h])
    yn = []
    for h in hs:
        mu = jnp.mean(y[h], axis=-1, keepdims=True)
        yc = y[h] - mu
        var = jnp.mean(yc * yc, axis=-1, keepdims=True)
        yn.append(yc * lax.rsqrt(var + RWKV_GN_EPS))
    yn = jnp.concatenate(yn, axis=-1) * lng_ref[...] + lnb_ref[...]
    o_ref[0] = ((yn + bonus) * g_ref[0]).astype(o_ref.dtype)


def rwkv_scan(r, k, v, a, lw, g, k_k, k_a, r_k, ln_g, ln_b):
    b, t, d = r.shape
    n = RWKV_HEAD_DIM
    hps = min(RWKV_HEADS_PER_STEP, d // n)
    w = hps * n
    ln = _pick(t, RWKV_CHUNK)
    tok = pl.BlockSpec((1, ln, w), lambda bi, hi, ci: (bi, ci, hi))
    vec = pl.BlockSpec((1, w), lambda bi, hi, ci: (0, hi))
    return pl.pallas_call(
        functools.partial(_rwkv_scan_body, chunk=ln, heads=hps, n=n),
        grid=(b, d // w, t // ln),
        in_specs=[tok] * 6 + [vec] * 5,
        out_specs=tok,
        out_shape=jax.ShapeDtypeStruct((b, t, d), BF16),
        scratch_shapes=[pltpu.VMEM((hps, n, n), F32)],
        compiler_params=_cparams("parallel", "parallel", "arbitrary"),
        name="rwkv_scan",
    )(r, k, v, a, lw, g, k_k.reshape(1, d), k_a.reshape(1, d), r_k.reshape(1, d), ln_g.reshape(1, d),
      ln_b.reshape(1, d))


def _tanh_epi(parts, extras):
    return jnp.tanh(parts[0])


def _sigmoid_epi(parts, extras):
    return jax.nn.sigmoid(parts[0])


def _sigmoid_bias_epi(parts, extras):
    return jax.nn.sigmoid(extras[0] + parts[0])


def _logdecay_epi(parts, extras):
    return -jnp.exp(-jax.nn.softplus(-(extras[0] + parts[0])) - 0.5)


def rwkv7_time_mix(x, g_pre, scale, shift, mu, w_rkv, w0, w_la, w_lb, a0, a_la, a_lb, g_la, g_lb, k_k, k_a, r_k,
                   ln_g, ln_b, w_out):
    b, t, d = x.shape
    xs = rwkv_pre(x, g_pre, scale, shift, mu).reshape(6 * b * t, d)
    tn = _pick(d, 1024)
    w_rkv2d = w_rkv.reshape(3 * d, d)
    r = mm(xs, [(w_rkv2d, 0, 0)], a_part=(0, 6), name="rwkv_r")
    k = mm(xs, [(w_rkv2d, 0, 1)], a_part=(1, 6), name="rwkv_k")
    v = mm(xs, [(w_rkv2d, 0, 2)], a_part=(2, 6), name="rwkv_v")
    lw = mm(mm(xs, w_la, a_part=(3, 6), epi=_tanh_epi, out_dtype=BF16, name="rwkv_w_la"), w_lb, epi=_logdecay_epi,
            extras=[_row_extra(w0, tn)], name="rwkv_w_lb")
    a = mm(mm(xs, a_la, a_part=(4, 6), out_dtype=BF16, name="rwkv_a_la"), a_lb, epi=_sigmoid_bias_epi,
           extras=[_row_extra(a0, tn)], name="rwkv_a_lb")
    g = mm(mm(xs, g_la, a_part=(5, 6), epi=_sigmoid_epi, out_dtype=BF16, name="rwkv_g_la"), g_lb, name="rwkv_g_lb")
    sh = lambda z: z.reshape(b, t, d)
    z = rwkv_scan(sh(r), sh(k), sh(v), sh(a), sh(lw), sh(g), k_k, k_a, r_k, ln_g, ln_b)
    return z.reshape(b * t, d), [(w_out, 0, 0)], None


def _nsa_compress_body(kv_ref, pe_ref, w1_ref, w2_ref, o_ref, *, nchunk, dh):
    st = NSA_CMP_STRIDE
    hid = w1_ref.shape[1]
    first = jnp.zeros((nchunk, hid), F32)
    second = jnp.zeros((nchunk, hid), F32)
    for j in range(st):
        xj = kv_ref[0, pl.ds(j, nchunk, stride=st), :]
        first = first + _dot((xj + pe_ref[j:j + 1, :]).astype(BF16), w1_ref[j * dh:(j + 1) * dh, :])
        second = second + _dot((xj + pe_ref[st + j:st + j + 1, :]).astype(BF16),
                               w1_ref[(st + j) * dh:(st + j + 1) * dh, :])
    hidden = jax.nn.gelu(first + pltpu.roll(second, nchunk - 1, axis=0))
    out = _dot(hidden.astype(BF16), w2_ref[...])
    row = lax.broadcasted_iota(jnp.int32, out.shape, 0)
    o_ref[0, 0] = jnp.where(row < nchunk - 1, out, 0.0)


def nsa_compress(proj, col0, pe, w1, w2):
    b, t, _ = proj.shape
    g, dh = NSA_KV_GROUPS, NSA_HEAD_DIM
    nchunk = t // NSA_CMP_STRIDE
    hid = w1.shape[1]
    return pl.pallas_call(
        functools.partial(_nsa_compress_body, nchunk=nchunk, dh=dh),
        grid=(b, g),
        in_specs=[
            pl.BlockSpec((1, t, dh), lambda bi, gi: (bi, 0, col0 // dh + gi)),
            pl.BlockSpec((NSA_CMP_BLOCK, dh), lambda bi, gi: (0, 0)),
            pl.BlockSpec((NSA_CMP_BLOCK * dh, hid), lambda bi, gi: (0, 0)),
            pl.BlockSpec((hid, dh), lambda bi, gi: (0, 0)),
        ],
        out_specs=pl.BlockSpec((1, 1, nchunk, dh), lambda bi, gi: (bi, gi, 0, 0)),
        out_shape=jax.ShapeDtypeStruct((b, g, nchunk, dh), F32),
        compiler_params=_cparams("parallel", "parallel"),
        name="nsa_compress",
    )(proj, pe, w1.astype(BF16), w2.astype(BF16))


def _stack_heads(q, r, dh):
    return jnp.concatenate([q[:, i * dh:(i + 1) * dh] for i in range(r)], axis=0)


def _unstack_heads(o, r, tq):
    return jnp.concatenate([o[i * tq:(i + 1) * tq] for i in range(r)], axis=-1)


def _gate_rows(gates, branch, r, tq):
    return jnp.concatenate([gates[:, branch * r + i:branch * r + i + 1] for i in range(r)], axis=0)


def _nsa_cmp_body(q_ref, kc_ref, vc_ref, gates_ref, selmap_t_ref, o_ref, sel_ref, *, tq, r, dh, n_sel):
    q0 = pl.program_id(2) * tq
    qs = (_stack_heads(q_ref[0], r, dh) * (dh ** -0.5)).astype(BF16)
    kc, vc = kc_ref[0, 0].astype(BF16), vc_ref[0, 0].astype(BF16)
    ncp = kc.shape[0]
    s = _dot_nt(qs, kc)
    rows = lax.broadcasted_iota(jnp.int32, (r * tq, ncp), 0)
    cmp_end = lax.broadcasted_iota(jnp.int32, (r * tq, ncp), 1) * NSA_CMP_STRIDE + (NSA_CMP_BLOCK - 1)
    t_pos = q0 + rows % tq
    vis = cmp_end <= t_pos
    s = jnp.where(vis, s, NEG_INF)
    e = jnp.exp(s - jnp.max(s, axis=-1, keepdims=True))
    p = e / jnp.sum(e, axis=-1, keepdims=True) * vis.astype(F32)
    o = _dot(p.astype(BF16), vc)
    o = o * _gate_rows(gates_ref[0, 0], 0, r, tq)
    o_ref[0] = _unstack_heads(o, r, tq)
    p_sum = p[0:tq]
    for i in range(1, r):
        p_sum = p_sum + p[i * tq:(i + 1) * tq]
    hi = p_sum.astype(BF16)
    lo = (p_sum - hi.astype(F32)).astype(BF16)
    selmap_t = selmap_t_ref[...]
    imp = _dot_nt(selmap_t, hi) + _dot_nt(selmap_t, lo)
    ns = imp.shape[0]
    blk = lax.broadcasted_iota(jnp.int32, (ns, tq), 0)
    tq_pos = q0 + lax.broadcasted_iota(jnp.int32, (ns, tq), 1)
    cur = tq_pos // NSA_SEL_BLOCK
    valid = blk * NSA_SEL_BLOCK <= tq_pos
    forced = (blk == 0) | (blk == cur) | (blk == cur - 1)
    score = jnp.where(valid, jnp.where(forced, POS_BIG, imp), NEG_INF)
    rank = jnp.zeros((ns, tq), F32)
    for m in range(ns):
        sm = score[m:m + 1, :]
        beats = (sm > score) | ((sm == score) & (blk > m))
        rank = rank + beats.astype(F32)
    chosen = ((rank < n_sel) & valid).astype(F32)
    sel_ref[0, 0] = (chosen.T - 1.0).astype(sel_ref.dtype)


def _flash_update(s, v, m_ref, l_ref, acc_ref):
    lanes = m_ref.shape[1]
    assert s.shape[1] % lanes == 0 and acc_ref.shape[1] == lanes
    m_old = m_ref[...]
    m_new = jnp.maximum(m_old, jnp.max(s, axis=-1, keepdims=True))
    alpha = jnp.exp2(m_old - m_new)
    p = jnp.exp2(s - jnp.concatenate([m_new] * (s.shape[1] // lanes), axis=1))
    l_ref[...] = alpha * l_ref[...] + jnp.sum(p, axis=-1, keepdims=True)
    acc_ref[...] = alpha * acc_ref[...] + _dot(p.astype(BF16), v)
    m_ref[...] = m_new


def _nsa_sel_body(qi_tbl, ki_tbl, diag_tbl, q_ref, k_ref, v_ref, gates_ref, oin_ref, sel_ref, o_ref, qa_ref, m_ref,
                  l_ref, acc_ref, *, tq, tk, r, dh):
    step = pl.program_id(2)
    qi, ki, diag = qi_tbl[step], ki_tbl[step], diag_tbl[step]
    ns = sel_ref.shape[-1]
    assert ns <= dh

    @pl.when(ki == 0)
    def _():
        m_ref[...] = jnp.full_like(m_ref, NEG_INF)
        l_ref[...] = jnp.zeros_like(l_ref)
        acc_ref[...] = jnp.zeros_like(acc_ref)
        qs = _stack_heads(q_ref[0], r, dh) * (dh ** -0.5 * LOG2_E)
        pieces = [qs, jnp.concatenate([sel_ref[0, 0].astype(F32)] * r, axis=0)]
        if ns < dh:
            pieces.append(jnp.zeros((r * tq, dh - ns), F32))
        qa_ref[...] = jnp.concatenate(pieces, axis=1).astype(BF16)

    blk_of_key = (ki * tk + lax.broadcasted_iota(jnp.int32, (tk, dh), 0)) // NSA_SEL_BLOCK
    own_block = jnp.where(blk_of_key == lax.broadcasted_iota(jnp.int32, (tk, dh), 1), POS_BIG, 0.0)
    ka = jnp.concatenate([k_ref[0], own_block], axis=1).astype(BF16)
    s = _dot_nt(qa_ref[...], ka)
    v = v_ref[0].astype(BF16)

    @pl.when(diag == 0)
    def _():
        _flash_update(s, v, m_ref, l_ref, acc_ref)

    @pl.when(diag == 1)
    def _():
        t_pos = qi * tq + lax.broadcasted_iota(jnp.int32, (r * tq, tk), 0) % tq
        k_pos = ki * tk + lax.broadcasted_iota(jnp.int32, (r * tq, tk), 1)
        _flash_update(jnp.where(k_pos <= t_pos, s, NEG_INF), v, m_ref, l_ref, acc_ref)
        o = acc_ref[...] / l_ref[...] * _gate_rows(gates_ref[0, 0], 1, r, tq)
        o_ref[0] = (oin_ref[0] + _unstack_heads(o, r, tq)).astype(o_ref.dtype)


def _nsa_win_body(q_ref, *rest, tq, r, dh, nwin):
    k_refs, v_refs = rest[:nwin], rest[nwin:2 * nwin]
    gates_ref, oin_ref, o_ref = rest[2 * nwin:]
    qi = pl.program_id(2)
    qs = (_stack_heads(q_ref[0], r, dh) * (dh ** -0.5 * LOG2_E)).astype(BF16)
    t_loc = lax.broadcasted_iota(jnp.int32, (r * tq, tq), 0) % tq
    k_loc = lax.broadcasted_iota(jnp.int32, (r * tq, tq), 1)
    tiles = []
    for j in range(nwin):
        s = _dot_nt(qs, k_refs[j][0].astype(BF16))
        if j == nwin - 1:
            s = jnp.where(k_loc <= t_loc, s, NEG_INF)
        else:
            if j == 0:
                s = jnp.where(k_loc > t_loc, s, NEG_INF)
            s = jnp.where(qi - (nwin - 1) + j >= 0, s, NEG_INF)
        tiles.append(s)
    s = jnp.concatenate(tiles, axis=1)
    p = jnp.exp2(s - jnp.max(s, axis=-1, keepdims=True))
    v = jnp.concatenate([v_refs[j][0] for j in range(nwin)], axis=0).astype(BF16)
    o = _dot(p.astype(BF16), v) / jnp.sum(p, axis=-1, keepdims=True)
    o = o * _gate_rows(gates_ref[0, 0], 2, r, tq)
    o_ref[0] = (oin_ref[0] + _unstack_heads(o, r, tq)).astype(o_ref.dtype)


def nsa_mix(h, w_in, pe_k, pe_v, ck_w1, ck_w2, cv_w1, cv_w2, w_out, tq=256):
    b, t, d = h.shape
    nh, g, dh = NSA_HEADS, NSA_KV_GROUPS, NSA_HEAD_DIM
    r = nh // g
    hd, kd = nh * dh, g * dh
    n_main = hd + 6 * kd
    h2 = h.reshape(b * t, d)
    proj = mm(h2, w_in[:, :n_main], name="nsa_in").reshape(b, t, n_main)
    n_gate = w_in.shape[1] - n_main
    w_gate = jnp.pad(w_in[:, n_main:], ((0, 0), (0, 128 - n_gate)))
    gates = mm(h2, w_gate, epi=_sigmoid_epi, name="nsa_gates")[:, :n_gate]
    gates = jnp.transpose(gates.reshape(b, t, 3, g, r), (0, 3, 1, 2, 4)).reshape(b, g, t, 3 * r)
    k_cmp = nsa_compress(proj, hd, pe_k, ck_w1, ck_w2)
    v_cmp = nsa_compress(proj, hd + kd, pe_v, cv_w1, cv_w2)
    ncp = k_cmp.shape[2]
    ns = t // NSA_SEL_BLOCK
    n_sel = min(NSA_SEL_TOP, ns)
    cs = jnp.arange(ncp)[None, :] * NSA_CMP_STRIDE
    ss = jnp.arange(ns)[:, None] * NSA_SEL_BLOCK
    sel_map_t = (jnp.maximum(jnp.minimum(cs + NSA_CMP_BLOCK, ss + NSA_SEL_BLOCK) - jnp.maximum(cs, ss), 0)
                 .astype(F32) / NSA_CMP_BLOCK).astype(BF16)
    tq = _pick(t, tq)
    assert NSA_WINDOW % tq == 0
    nq = t // tq
    wq = r * dh
    col = lambda base: base // dh
    q_spec3 = pl.BlockSpec((1, tq, wq), lambda bi, gi, qi: (bi, qi, gi))
    gate_spec3 = pl.BlockSpec((1, 1, tq, 3 * r), lambda bi, gi, qi: (bi, gi, qi, 0))
    o_cmp, sel = pl.pallas_call(
        functools.partial(_nsa_cmp_body, tq=tq, r=r, dh=dh, n_sel=n_sel),
        grid=(b, g, nq),
        in_specs=[
            q_spec3,
            pl.BlockSpec((1, 1, ncp, dh), lambda bi, gi, qi: (bi, gi, 0, 0)),
            pl.BlockSpec((1, 1, ncp, dh), lambda bi, gi, qi: (bi, gi, 0, 0)),
            gate_spec3,
            pl.BlockSpec((ns, ncp), lambda bi, gi, qi: (0, 0)),
        ],
        out_specs=[q_spec3, pl.BlockSpec((1, 1, tq, ns), lambda bi, gi, qi: (bi, gi, qi, 0))],
        out_shape=[jax.ShapeDtypeStruct((b, t, hd), F32), jax.ShapeDtypeStruct((b, g, t, ns), BF16)],
        compiler_params=_cparams("parallel", "parallel", "parallel"),
        name="nsa_cmp",
    )(proj, k_cmp, v_cmp, gates, sel_map_t)

    tk = _pick(t, NSA_SEL_KEY_TILE)
    assert tk % tq == 0
    pairs = [(qi, ki) for qi in range(nq) for ki in range((qi * tq + tq - 1) // tk + 1)]
    qi_tbl = jnp.asarray([pq for pq, _ in pairs], jnp.int32)
    ki_tbl = jnp.asarray([pk for _, pk in pairs], jnp.int32)
    diag_tbl = jnp.asarray([int(pk == (pq * tq + tq - 1) // tk) for pq, pk in pairs], jnp.int32)
    q_spec_p = pl.BlockSpec((1, tq, wq), lambda bi, gi, p, qt, kt, dt: (bi, qt[p], gi))

    def kv_spec_p(base):
        return pl.BlockSpec((1, tk, dh), lambda bi, gi, p, qt, kt, dt: (bi, kt[p], col(base) + gi))

    o_sel = pl.pallas_call(
        functools.partial(_nsa_sel_body, tq=tq, tk=tk, r=r, dh=dh),
        grid_spec=pltpu.PrefetchScalarGridSpec(
            num_scalar_prefetch=3,
            grid=(b, g, len(pairs)),
            in_specs=[q_spec_p, kv_spec_p(hd + 2 * kd), kv_spec_p(hd + 3 * kd),
                      pl.BlockSpec((1, 1, tq, 3 * r), lambda bi, gi, p, qt, kt, dt: (bi, gi, qt[p], 0)),
                      q_spec_p,
                      pl.BlockSpec((1, 1, tq, ns), lambda bi, gi, p, qt, kt, dt: (bi, gi, qt[p], 0))],
            out_specs=q_spec_p,
            scratch_shapes=[pltpu.VMEM((r * tq, 2 * dh), BF16), pltpu.VMEM((r * tq, dh), F32),
                            pltpu.VMEM((r * tq, dh), F32), pltpu.VMEM((r * tq, dh), F32)],
        ),
        out_shape=jax.ShapeDtypeStruct((b, t, hd), F32),
        compiler_params=_cparams("parallel", "parallel", "arbitrary"),
        name="nsa_sel",
    )(qi_tbl, ki_tbl, diag_tbl, proj, proj, proj, gates, o_cmp, sel)

    nwin = NSA_WINDOW // tq + 1

    def kv_spec_w(base, j):
        return pl.BlockSpec((1, tq, dh),
                            lambda bi, gi, qi: (bi, jnp.maximum(qi - (nwin - 1) + j, 0), col(base) + gi))

    o_all = pl.pallas_call(
        functools.partial(_nsa_win_body, tq=tq, r=r, dh=dh, nwin=nwin),
        grid=(b, g, nq),
        in_specs=([q_spec3] + [kv_spec_w(hd + 4 * kd, j) for j in range(nwin)]
                  + [kv_spec_w(hd + 5 * kd, j) for j in range(nwin)] + [gate_spec3, q_spec3]),
        out_specs=q_spec3,
        out_shape=jax.ShapeDtypeStruct((b, t, hd), BF16),
        compiler_params=_cparams("parallel", "parallel", "parallel"),
        name="nsa_win",
    )(proj, *([proj] * (2 * nwin)), gates, o_sel)
    return o_all.reshape(b * t, hd), [(w_out, 0, 0)], None


def _silu(a):
    return a * jax.nn.sigmoid(a)


def _ada_epi(parts, extras):
    return parts[0] + extras[0]


def ada_modulation(c, ada_w, ada_b):
    depth, d, n6 = ada_w.shape
    b = c.shape[0]
    rows = ((b + 7) // 8) * 8
    cond = jnp.pad(c, ((0, rows - b), (0, 0)))
    w2d = ada_w.reshape(depth * d, n6)
    mods = []
    for i in range(depth):
        mods.append(mm(cond, [(w2d, 0, i)], a_act=_silu, epi=_ada_epi, extras=[_row_extra(ada_b[i], _pick(n6, 1024))],
                       precision=HIGHEST, name="ada_mod")[:b])
    return jnp.stack(mods)


def kernel(x, c, ada_w, ada_b, norm_g, mlp_w1, mlp_w2, rwkv_mu, rwkv_w_rkv, rwkv_w0, rwkv_w_la, rwkv_w_lb, rwkv_a0, rwkv_a_la, rwkv_a_lb, rwkv_g_la, rwkv_g_lb, rwkv_k_k, rwkv_k_a, rwkv_r_k, rwkv_ln_g, rwkv_ln_b, rwkv_w_out, ret_w_in, ret_gn_g, ret_gn_b, ret_w_out, conv_pw1_w, conv_pw1_b, conv_dw_w, conv_dw_b, conv_ln_g, conv_ln_b, conv_pw2_w, conv_pw2_b, nsa_w_in, nsa_pe_k, nsa_pe_v, nsa_ck_w1, nsa_ck_w2, nsa_cv_w1, nsa_cv_w2, nsa_w_out):
    b, t, d = x.shape
    depth = ada_w.shape[0]
    mod = ada_modulation(c, ada_w, ada_b).reshape(depth, b, 6, d)
    h = None
    for i in range(depth):
        sh_t, sc_t, gt_t, sh_c, sc_c, gt_c = (mod[i, :, j] for j in range(6))
        kind = i % 4
        if kind == 0:
            act, ws, bias = rwkv7_time_mix(x, norm_g[i, 0], sc_t, sh_t, rwkv_mu, rwkv_w_rkv, rwkv_w0, rwkv_w_la,
                                           rwkv_w_lb, rwkv_a0, rwkv_a_la, rwkv_a_lb, rwkv_g_la, rwkv_g_lb, rwkv_k_k,
                                           rwkv_k_a, rwkv_r_k.reshape(-1), rwkv_ln_g, rwkv_ln_b, rwkv_w_out)
        else:
            if h is None:
                (h,) = resid_norm(x, pre=(norm_g[i, 0], sc_t, sh_t))
            if kind == 1:
                act, ws, bias = retention_mix(h, ret_w_in, ret_gn_g, ret_gn_b, ret_w_out)
            elif kind == 2:
                act, ws, bias = conformer_conv_mix(h, conv_pw1_w, conv_pw1_b, conv_dw_w, conv_dw_b, conv_ln_g,
                                                   conv_ln_b, conv_pw2_w, conv_pw2_b)
            else:
                act, ws, bias = nsa_mix(h, nsa_w_in, nsa_pe_k, nsa_pe_v, nsa_ck_w1, nsa_ck_w2, nsa_cv_w1, nsa_cv_w2,
                                        nsa_w_out)
        x, h = mm_resid(act, ws, x, post=(norm_g[i, 1], gt_t), pre=(norm_g[i, 2], sc_c, sh_c), bias=bias,
                        name="mixer_out_resid")
        act, ws, bias = sqrelu_mlp(h.reshape(b * t, d), mlp_w1, mlp_w2, i)
        nxt = i + 1
        if nxt < depth and nxt % 4 != 0:
            x, h = mm_resid(act, ws, x, post=(norm_g[i, 3], gt_c),
                            pre=(norm_g[nxt, 0], mod[nxt, :, 1], mod[nxt, :, 0]), name="mlp_down_resid")
        else:
            (x,) = mm_resid(act, ws, x, post=(norm_g[i, 3], gt_c), name="mlp_down_resid")
            h = None
    return x
```

```python
import functools
import math

import jax
import jax.numpy as jnp
from jax import lax
from jax.experimental import pallas as pl
from jax.experimental.pallas import tpu as pltpu

F32 = jnp.float32
BF16 = jnp.bfloat16
HIGHEST = lax.Precision.HIGHEST

NORM_EPS = 1e-6
NEG_INF = -1e30
POS_BIG = 1e30
LOG2_E = math.log2(math.e)

RWKV_HEAD_DIM = 64
RWKV_GN_EPS = 64e-5
RWKV_CHUNK = 64
RWKV_HEADS_PER_STEP = 32

RET_HEADS = 8
RET_CHUNK = 256
RET_GN_EPS = 1e-5
ROPE_BASE = 10000.0

CONV_WIDTH = 31
CONV_HALO = 32
CONV_LN_EPS = 1e-5

NSA_HEADS = 16
NSA_KV_GROUPS = 4
NSA_HEAD_DIM = 128
NSA_CMP_BLOCK = 32
NSA_CMP_STRIDE = 16
NSA_SEL_BLOCK = 64
NSA_SEL_TOP = 16
NSA_WINDOW = 512
NSA_SEL_KEY_TILE = 512

VMEM_LIMIT_BYTES = 56 * 1024 * 1024


def _cparams(*sem):
    return pltpu.CompilerParams(dimension_semantics=sem, vmem_limit_bytes=VMEM_LIMIT_BYTES)


def _dot(a, b, precision=None):
    return jnp.dot(a, b, preferred_element_type=F32, precision=precision)


def _dot_nt(a, b, precision=None):
    return lax.dot_general(a, b, (((1,), (1,)), ((), ())), preferred_element_type=F32, precision=precision)


def _pick(n, pref):
    if n <= pref:
        return n
    t = pref
    while n % t:
        t //= 2
    return t


def _mm_body(*refs, n_w, n_ex, n_out, nk, epi, precision, has_a_add, a_act):
    a_ref = refs[0]
    pos = 1
    a_add_ref = None
    if has_a_add:
        a_add_ref = refs[pos]
        pos += 1
    w_refs = refs[pos:pos + n_w]
    pos += n_w
    ex_refs = refs[pos:pos + n_ex]
    pos += n_ex
    o_refs = refs[pos:pos + n_out]
    acc_refs = refs[pos + n_out:]

    def finish(parts):
        res = epi(parts, [e[...] for e in ex_refs])
        if n_out == 1 and not isinstance(res, (tuple, list)):
            res = (res,)
        for o_ref, val in zip(o_refs, res, strict=True):
            o_ref[...] = val.astype(o_ref.dtype)

    a = a_ref[...]
    if a_add_ref is not None:
        a = a.astype(F32) + a_add_ref[...]
    if a_act is not None:
        a = a_act(a)
    if precision is None:
        a = a.astype(BF16)
    parts = []
    for w_ref in w_refs:
        w = w_ref[...]
        if precision is None:
            w = w.astype(BF16)
        parts.append(_dot(a, w, precision))

    if nk == 1:
        finish(parts)
    else:
        k = pl.program_id(2)

        @pl.when(k == 0)
        def _():
            for acc, p in zip(acc_refs, parts):
                acc[...] = p

        @pl.when(k > 0)
        def _():
            for acc, p in zip(acc_refs, parts):
                acc[...] += p

        @pl.when(k == nk - 1)
        def _():
            finish([acc[...] for acc in acc_refs])


def _first(parts, extras):
    return parts[0]


def mm(a, ws, *, n_out=None, extras=(), epi=_first, out_dtype=F32, tm=1024, tn=1024, tk=2048, precision=None,
       a_add=None, a_act=None, a_part=(0, 1), name="mm"):
    m, kdim = a.shape
    a_s, a_parts = a_part
    assert m % a_parts == 0
    m //= a_parts
    if not isinstance(ws, (list, tuple)):
        ws = [(ws, 0, 0)]
    ws = [tuple(w) + (0,) * (3 - len(w)) for w in ws]
    if n_out is None:
        n_out = ws[0][0].shape[1]
    tm = _pick(m, tm)
    tn = _pick(n_out, tn)
    tk = _pick(kdim, tk)
    nk = kdim // tk
    assert m % tm == 0 and n_out % tn == 0 and kdim % tk == 0
    in_specs = [pl.BlockSpec((tm, tk), lambda i, j, k, o=a_s * (m // tm): (i + o, k))]
    args = [a]
    if a_add is not None:
        in_specs.append(pl.BlockSpec((1, tk), lambda i, j, k: (0, k)))
        args.append(a_add)
    for w, off, koff in ws:
        in_specs.append(pl.BlockSpec((tk, tn), lambda i, j, k, off=off, ko=koff * nk: (k + ko, j + off)))
        args.append(w.astype(BF16) if precision is None else w)
    for arr, bshape, imap in extras:
        in_specs.append(pl.BlockSpec(bshape, imap))
        args.append(arr)
    scratch = [pltpu.VMEM((tm, tn), F32) for _ in ws] if nk > 1 else []
    multi = isinstance(out_dtype, (tuple, list))
    out_dtypes = tuple(out_dtype) if multi else (out_dtype,)
    body = functools.partial(_mm_body, n_w=len(ws), n_ex=len(extras), n_out=len(out_dtypes), nk=nk, epi=epi,
                             precision=precision, has_a_add=a_add is not None, a_act=a_act)
    outs = pl.pallas_call(
        body,
        grid=(m // tm, n_out // tn, nk),
        in_specs=in_specs,
        out_specs=[pl.BlockSpec((tm, tn), lambda i, j, k: (i, j)) for _ in out_dtypes],
        out_shape=[jax.ShapeDtypeStruct((m, n_out), dt) for dt in out_dtypes],
        scratch_shapes=scratch,
        compiler_params=_cparams("parallel", "parallel", "arbitrary"),
        name=name,
    )(*args)
    return tuple(outs) if multi else outs[0]


def _row_extra(vec, tn):
    return (vec.reshape(1, -1), (1, tn), lambda i, j, k: (0, j))


def _rms(x, g):
    return x * lax.rsqrt(jnp.mean(x * x, axis=-1, keepdims=True) + NORM_EPS) * g


def _resid_norm_body(*refs, has_y, has_h):
    pos = 0
    x_ref = refs[pos]; pos += 1
    if has_y:
        y_ref, gpost_ref, gate_ref = refs[pos:pos + 3]; pos += 3
    if has_h:
        gpre_ref, scale_ref, shift_ref = refs[pos:pos + 3]; pos += 3
    outs = refs[pos:]
    x = x_ref[0]
    o = 0
    if has_y:
        y = y_ref[0].astype(F32)
        x = x + (1.0 + gate_ref[0]) * _rms(y, gpost_ref[...])
        outs[o][0] = x
        o += 1
    if has_h:
        h = _rms(x, gpre_ref[...]) * (1.0 + scale_ref[0]) + shift_ref[0]
        outs[o][0] = h.astype(outs[o].dtype)


def resid_norm(x, y=None, post=None, pre=None, tt=256):
    b, t, d = x.shape
    tt = _pick(t, tt)
    row = pl.BlockSpec((1, tt, d), lambda bi, ti: (bi, ti, 0))
    vec = pl.BlockSpec((1, d), lambda bi, ti: (0, 0))
    bvec = pl.BlockSpec((1, 1, d), lambda bi, ti: (bi, 0, 0))
    args, in_specs, out_shapes, out_specs = [x], [row], [], []
    if y is not None:
        args += [y, post[0].reshape(1, d), post[1].reshape(b, 1, d)]
        in_specs += [row, vec, bvec]
        out_shapes.append(jax.ShapeDtypeStruct((b, t, d), F32))
        out_specs.append(row)
    if pre is not None:
        args += [pre[0].reshape(1, d), pre[1].reshape(b, 1, d), pre[2].reshape(b, 1, d)]
        in_specs += [vec, bvec, bvec]
        out_shapes.append(jax.ShapeDtypeStruct((b, t, d), BF16))
        out_specs.append(row)
    res = pl.pallas_call(
        functools.partial(_resid_norm_body, has_y=y is not None, has_h=pre is not None),
        grid=(b, t // tt),
        in_specs=in_specs,
        out_specs=out_specs,
        out_shape=out_shapes,
        compiler_params=_cparams("parallel", "parallel"),
        name="resid_norm",
    )(*args)
    return tuple(res)


def _add_bias(parts, extras):
    return parts[0] + extras[0]


def project(act, ws, bias, name):
    if bias is None:
        return mm(act, ws, out_dtype=BF16, name=name)
    tn = _pick(ws[0][0].shape[1], 1024)
    return mm(act, ws, epi=_add_bias, extras=[_row_extra(bias, tn)], tn=tn, out_dtype=BF16, name=name)


def _relu2(parts, extras):
    r = jnp.maximum(parts[0], 0.0)
    return r * r


def sqrelu_mlp(h2d, w1, w2, layer):
    nl, d, f = w1.shape
    a = mm(h2d, [(w1.reshape(nl * d, f), 0, layer)], epi=_relu2, out_dtype=BF16, name="mlp_up")
    return a, [(w2.reshape(nl * f, d), 0, layer)], None


def _glu(parts, extras):
    return (parts[0] + extras[0]) * jax.nn.sigmoid(parts[1] + extras[1])


def _conv_ln_body(cur_ref, prev_ref, w_ref, b_ref, g_ref, beta_ref, o_ref, buf_ref, acc_ref, *, tt, d):
    ti = pl.program_id(1)
    halo = prev_ref[0]
    buf_ref[0:CONV_HALO, :] = jnp.where(ti == 0, jnp.zeros_like(halo), halo)
    buf_ref[CONV_HALO:, :] = cur_ref[0]
    lane_chunk = min(d, 256)
    row_chunk = min(tt, 64)
    base = CONV_HALO - (CONV_WIDTH - 1)
    sub = 8
    for c0 in range(0, d, lane_chunk):
        for r0 in range(0, tt, row_chunk):
            acc = None
            for s in range(sub):
                taps = [j for j in range(CONV_WIDTH) if (base + j) % sub == s]
                if not taps:
                    continue
                rows = row_chunk + (sub if s else 0)
                part = jnp.zeros((rows, lane_chunk), F32)
                for j in taps:
                    off = r0 + base + j - s
                    part = part + buf_ref[off:off + rows, c0:c0 + lane_chunk] * w_ref[j:j + 1, c0:c0 + lane_chunk]
                part = part[s:s + row_chunk]
                acc = part if acc is None else acc + part
            acc_ref[r0:r0 + row_chunk, c0:c0 + lane_chunk] = acc
    y = acc_ref[...] + b_ref[...]
    mu = jnp.mean(y, axis=-1, keepdims=True)
    yc = y - mu
    var = jnp.mean(yc * yc, axis=-1, keepdims=True)
    z = yc * lax.rsqrt(var + CONV_LN_EPS) * g_ref[...] + beta_ref[...]
    o_ref[0] = (z * jax.nn.sigmoid(z)).astype(o_ref.dtype)


def conv_ln_silu(u, dw_w, dw_b, ln_g, ln_b, tt=128):
    b, t, d = u.shape
    tt = _pick(t, tt)
    hb = tt // CONV_HALO
    vec = pl.BlockSpec((1, d), lambda bi, ti: (0, 0))
    return pl.pallas_call(
        functools.partial(_conv_ln_body, tt=tt, d=d),
        grid=(b, t // tt),
        in_specs=[
            pl.BlockSpec((1, tt, d), lambda bi, ti: (bi, ti, 0)),
            pl.BlockSpec((1, CONV_HALO, d), lambda bi, ti: (bi, jnp.maximum(ti * hb - 1, 0), 0)),
            pl.BlockSpec((CONV_WIDTH, d), lambda bi, ti: (0, 0)),
            vec, vec, vec,
        ],
        out_specs=pl.BlockSpec((1, tt, d), lambda bi, ti: (bi, ti, 0)),
        out_shape=jax.ShapeDtypeStruct((b, t, d), BF16),
        scratch_shapes=[pltpu.VMEM((tt + CONV_HALO, d), F32), pltpu.VMEM((tt, d), F32)],
        compiler_params=_cparams("parallel", "parallel"),
        name="conv_ln_silu",
    )(u, u, dw_w, dw_b.reshape(1, d), ln_g.reshape(1, d), ln_b.reshape(1, d))


def conformer_conv_mix(h, pw1_w, pw1_b, dw_w, dw_b, ln_g, ln_b, pw2_w, pw2_b):
    b, t, d = h.shape
    tn = _pick(d, 1024)
    b1 = pw1_b.reshape(1, -1)
    u = mm(h.reshape(b * t, d), [(pw1_w, 0), (pw1_w, d // tn)], n_out=d, tn=tn, epi=_glu,
           extras=[(b1, (1, tn), lambda i, j, k: (0, j)),
                   (b1, (1, tn), lambda i, j, k, o=d // tn: (0, j + o))], name="conf_pw1_glu")
    z = conv_ln_silu(u.reshape(b, t, d), dw_w, dw_b, ln_g, ln_b)
    return z.reshape(b * t, d), [(pw2_w, 0, 0)], pw2_b


def _rope(x, cos, sin):
    half = x.shape[-1] // 2
    x1, x2 = x[:, :half], x[:, half:]
    return jnp.concatenate([x1 * cos - x2 * sin, x2 * cos + x1 * sin], axis=-1)


def _retention_body(q_ref, k_ref, v_ref, gate_ref, cos_ref, sin_ref, inner_ref, qdec_ref, kdec_ref, cdec_ref,
                    gng_ref, gnb_ref, o_ref, state_ref, *, dk):
    @pl.when(pl.program_id(2) == 0)
    def _():
        state_ref[...] = jnp.zeros_like(state_ref)

    cos, sin = cos_ref[...], sin_ref[...]
    q = _rope(q_ref[0].astype(F32), cos, sin)
    k = _rope(k_ref[0].astype(F32), cos, sin) * (dk ** -0.5)
    v = v_ref[0].astype(BF16)
    qb = q.astype(BF16)
    s = _dot_nt(qb, k.astype(BF16)) * inner_ref[0]
    state = state_ref[...]
    o = _dot(s.astype(BF16), v) + _dot(qb, state.astype(BF16)) * qdec_ref[0]
    kd_t = (k * kdec_ref[0]).T.astype(BF16)
    state_ref[...] = state * cdec_ref[0] + _dot(kd_t, v)
    mu = jnp.mean(o, axis=-1, keepdims=True)
    oc = o - mu
    var = jnp.mean(oc * oc, axis=-1, keepdims=True)
    on = oc * lax.rsqrt(var + RET_GN_EPS) * gng_ref[...] + gnb_ref[...]
    gate = gate_ref[0].astype(F32)
    o_ref[0] = (gate * jax.nn.sigmoid(gate) * on).astype(o_ref.dtype)


def retention_mix(h, w_in, gn_g, gn_b, w_out):
    b, t, d = h.shape
    nh, c = RET_HEADS, _pick(t, RET_CHUNK)
    dk = d // nh
    dv = 2 * dk
    proj = mm(h.reshape(b * t, d), w_in, out_dtype=BF16, name="ret_in").reshape(b, t, 6 * d)
    pos = jnp.arange(t, dtype=F32)
    inv_freq = ROPE_BASE ** (-jnp.arange(0, dk, 2, dtype=F32) / dk)
    ang = pos[:, None] * inv_freq[None, :]
    cos, sin = jnp.cos(ang), jnp.sin(ang)
    log_gamma = jnp.log(1.0 - 2.0 ** (-5.0 - jnp.arange(nh, dtype=F32)))
    idx = jnp.arange(c, dtype=F32)
    diff = idx[:, None] - idx[None, :]
    inner = jnp.where(diff >= 0, jnp.exp(jnp.maximum(diff, 0.0)[None] * log_gamma[:, None, None]), 0.0)
    q_dec = jnp.exp((idx + 1.0)[None] * log_gamma[:, None])[:, :, None]
    k_dec = jnp.exp((c - 1.0 - idx)[None] * log_gamma[:, None])[:, :, None]
    c_dec = jnp.exp(c * log_gamma)[:, None, None]
    nq = (nh * dk) // dk
    o = pl.pallas_call(
        functools.partial(_retention_body, dk=dk),
        grid=(b, nh, t // c),
        in_specs=[
            pl.BlockSpec((1, c, dk), lambda bi, hi, ci: (bi, ci, hi)),
            pl.BlockSpec((1, c, dk), lambda bi, hi, ci: (bi, ci, nq + hi)),
            pl.BlockSpec((1, c, dv), lambda bi, hi, ci: (bi, ci, nq + hi)),
            pl.BlockSpec((1, c, dv), lambda bi, hi, ci: (bi, ci, 2 * nq + hi)),
            pl.BlockSpec((c, dk // 2), lambda bi, hi, ci: (ci, 0)),
            pl.BlockSpec((c, dk // 2), lambda bi, hi, ci: (ci, 0)),
            pl.BlockSpec((1, c, c), lambda bi, hi, ci: (hi, 0, 0)),
            pl.BlockSpec((1, c, 1), lambda bi, hi, ci: (hi, 0, 0)),
            pl.BlockSpec((1, c, 1), lambda bi, hi, ci: (hi, 0, 0)),
            pl.BlockSpec((1, 1, 1), lambda bi, hi, ci: (hi, 0, 0)),
            pl.BlockSpec((1, dv), lambda bi, hi, ci: (0, hi)),
            pl.BlockSpec((1, dv), lambda bi, hi, ci: (0, hi)),
        ],
        out_specs=pl.BlockSpec((1, c, dv), lambda bi, hi, ci: (bi, ci, hi)),
        out_shape=jax.ShapeDtypeStruct((b, t, nh * dv), BF16),
        scratch_shapes=[pltpu.VMEM((dk, dv), F32)],
        compiler_params=_cparams("parallel", "parallel", "arbitrary"),
        name="retention",
    )(proj, proj, proj, proj, cos, sin, inner, q_dec, k_dec, c_dec, gn_g.reshape(1, -1), gn_b.reshape(1, -1))
    return o.reshape(b * t, nh * dv), [(w_out, 0, 0)], None


def _rwkv_pre_body(x_ref, prev_ref, g_ref, scale_ref, shift_ref, mu_ref, o_ref):
    ti = pl.program_id(1)
    g, scale, shift = g_ref[...], scale_ref[0], shift_ref[0]
    h = _rms(x_ref[0], g) * (1.0 + scale) + shift
    hp = _rms(prev_ref[0], g) * (1.0 + scale) + shift
    last = jnp.where(ti == 0, 0.0, hp[7:8, :])
    row = lax.broadcasted_iota(jnp.int32, h.shape, 0)
    shifted = jnp.where(row == 0, last, pltpu.roll(h, 1, axis=0))
    xx = shifted - h
    for s in range(6):
        o_ref[s, 0] = (h + xx * mu_ref[s:s + 1, :]).astype(o_ref.dtype)


def rwkv_pre(x, g_pre, scale, shift, mu, tt=256):
    b, t, d = x.shape
    tt = _pick(t, tt)
    bvec = pl.BlockSpec((1, 1, d), lambda bi, ti: (bi, 0, 0))
    return pl.pallas_call(
        _rwkv_pre_body,
        grid=(b, t // tt),
        in_specs=[
            pl.BlockSpec((1, tt, d), lambda bi, ti: (bi, ti, 0)),
            pl.BlockSpec((1, 8, d), lambda bi, ti: (bi, jnp.maximum(ti * (tt // 8) - 1, 0), 0)),
            pl.BlockSpec((1, d), lambda bi, ti: (0, 0)),
            bvec, bvec,
            pl.BlockSpec((6, d), lambda bi, ti: (0, 0)),
        ],
        out_specs=pl.BlockSpec((6, 1, tt, d), lambda bi, ti: (0, bi, ti, 0)),
        out_shape=jax.ShapeDtypeStruct((6, b, t, d), BF16),
        compiler_params=_cparams("parallel", "parallel"),
        name="rwkv_pre",
    )(x, x, g_pre.reshape(1, d), scale.reshape(b, 1, d), shift.reshape(b, 1, d), mu)


def _rwkv_scan_body(r_ref, k_ref, v_ref, a_ref, lw_ref, g_ref, kk_ref, ka_ref, rk_ref, lng_ref, lnb_ref, o_ref,
                    state_ref, *, chunk, heads, n):
    @pl.when(pl.program_id(2) == 0)
    def _():
        state_ref[...] = jnp.zeros_like(state_ref)

    ln = chunk
    hs = range(heads)
    row = lax.broadcasted_iota(jnp.int32, (ln, ln), 0)
    col = lax.broadcasted_iota(jnp.int32, (ln, ln), 1)
    incl = col <= row
    strict = col < row
    tri = incl.astype(BF16)
    eye = (row == col).astype(F32)
    wid = heads * n
    seg_w = min(wid, 256)
    assert wid % seg_w == 0 and seg_w % n == 0
    seg = (lax.broadcasted_iota(jnp.int32, (seg_w, seg_w), 0) // n
           == lax.broadcasted_iota(jnp.int32, (seg_w, seg_w), 1) // n).astype(BF16)

    def split_dot(m, x):
        hi = x.astype(BF16)
        lo = (x - hi.astype(F32)).astype(BF16)
        return _dot(m, hi) + _dot(m, lo)

    def seg_sum(x):
        hi = x.astype(BF16)
        lo = (x - hi.astype(F32)).astype(BF16)
        return jnp.concatenate([_dot(hi[:, j:j + seg_w], seg) + _dot(lo[:, j:j + seg_w], seg)
                                for j in range(0, wid, seg_w)], axis=1)

    heads_of = lambda x: [x[:, hd * n:(hd + 1) * n] for hd in hs]

    r, k, v, a, lw = r_ref[0], k_ref[0], v_ref[0], a_ref[0], lw_ref[0]
    kk = k * kk_ref[...]
    kk = kk / jnp.maximum(jnp.sqrt(seg_sum(kk * kk)), 1e-12)
    k2 = k * (1.0 + (a - 1.0) * ka_ref[...])
    beta = kk * a
    c = split_dot(tri, lw)
    c_last = c[ln - 1:ln, :]
    e_neg = jnp.exp(-c)
    e_rem = jnp.exp(c_last - c)
    e_last = heads_of(jnp.exp(c_last))
    a_t = heads_of(-kk * jnp.exp(c - lw))
    r_t = heads_of(r * jnp.exp(c))
    b_t = heads_of(beta * e_neg)
    k_t = heads_of(k2 * e_neg)
    b_rem = heads_of(beta * e_rem)
    k_rem = heads_of(k2 * e_rem)
    vs = heads_of(v)
    bonus = seg_sum(r * k2 * rk_ref[...]) * v

    ar = [jnp.concatenate([a_t[h], r_t[h]], axis=0).astype(BF16) for h in hs]
    bk = [jnp.concatenate([b_t[h], k_t[h]], axis=0).astype(BF16) for h in hs]
    bk_rem = [jnp.concatenate([b_rem[h], k_rem[h]], axis=0).astype(BF16) for h in hs]
    vb = [vs[h].astype(BF16) for h in hs]
    state = [state_ref[h] for h in hs]
    p = [_dot_nt(ar[h], bk[h]) for h in hs]
    q0 = [_dot_nt(ar[h], state[h].astype(BF16)) for h in hs]
    m_ab = [jnp.where(strict, p[h][:ln, :ln], 0.0) for h in hs]
    m_ak = [jnp.where(strict, p[h][:ln, ln:], 0.0).astype(BF16) for h in hs]
    m_r = [jnp.concatenate([jnp.where(incl, p[h][ln:, :ln], 0.0), jnp.where(incl, p[h][ln:, ln:], 0.0)],
                           axis=1).astype(BF16) for h in hs]
    rhs = [q0[h][:ln] + _dot(m_ak[h], vb[h]) for h in hs]
    inv = [eye + m_ab[h] for h in hs]
    pw = [m_ab[h].astype(BF16) for h in hs]
    for _ in range(int(math.log2(ln)) - 1):
        pw = [_dot(pw[h], pw[h]).astype(BF16) for h in hs]
        inv = [inv[h] + _dot(inv[h].astype(BF16), pw[h]) for h in hs]
    u = [_dot(inv[h].astype(BF16), rhs[h].astype(BF16)) for h in hs]
    uv = [jnp.concatenate([u[h], vs[h]], axis=0) for h in hs]
    y = [q0[h][ln:] + _dot(m_r[h], uv[h].astype(BF16)) for h in hs]
    for h in hs:
        state_ref[h] = state[h] * e_last[h] + _dot(uv[h].T.astype(BF16), bk_rem[h])
    yn = []
    for h in hs:
        mu = jnp.mean(y[h], axis=-1, keepdims=True)
        yc = y[h] - mu
        var = jnp.mean(yc * yc, axis=-1, keepdims=True)
        yn.append(yc * lax.rsqrt(var + RWKV_GN_EPS))
    yn = jnp.concatenate(yn, axis=-1) * lng_ref[...] + lnb_ref[...]
    o_ref[0] = ((yn + bonus) * g_ref[0]).astype(o_ref.dtype)


def rwkv_scan(r, k, v, a, lw, g, k_k, k_a, r_k, ln_g, ln_b):
    b, t, d = r.shape
    n = RWKV_HEAD_DIM
    hps = min(RWKV_HEADS_PER_STEP, d // n)
    w = hps * n
    ln = _pick(t, RWKV_CHUNK)
    tok = pl.BlockSpec((1, ln, w), lambda bi, hi, ci: (bi, ci, hi))
    vec = pl.BlockSpec((1, w), lambda bi, hi, ci: (0, hi))
    return pl.pallas_call(
        functools.partial(_rwkv_scan_body, chunk=ln, heads=hps, n=n),
        grid=(b, d // w, t // ln),
        in_specs=[tok] * 6 + [vec] * 5,
        out_specs=tok,
        out_shape=jax.ShapeDtypeStruct((b, t, d), BF16),
        scratch_shapes=[pltpu.VMEM((hps, n, n), F32)],
        compiler_params=_cparams("parallel", "parallel", "arbitrary"),
        name="rwkv_scan",
    )(r, k, v, a, lw, g, k_k.reshape(1, d), k_a.reshape(1, d), r_k.reshape(1, d), ln_g.reshape(1, d),
      ln_b.reshape(1, d))


def _tanh_epi(parts, extras):
    return jnp.tanh(parts[0])


def _sigmoid_epi(parts, extras):
    return jax.nn.sigmoid(parts[0])


def _sigmoid_bias_epi(parts, extras):
    return jax.nn.sigmoid(extras[0] + parts[0])


def _logdecay_epi(parts, extras):
    return -jnp.exp(-jax.nn.softplus(-(extras[0] + parts[0])) - 0.5)


def rwkv7_time_mix(x, g_pre, scale, shift, mu, w_rkv, w0, w_la, w_lb, a0, a_la, a_lb, g_la, g_lb, k_k, k_a, r_k,
                   ln_g, ln_b, w_out):
    b, t, d = x.shape
    xs = rwkv_pre(x, g_pre, scale, shift, mu).reshape(6 * b * t, d)
    tn = _pick(d, 1024)
    w_rkv2d = w_rkv.reshape(3 * d, d)
    r = mm(xs, [(w_rkv2d, 0, 0)], a_part=(0, 6), name="rwkv_r")
    k = mm(xs, [(w_rkv2d, 0, 1)], a_part=(1, 6), name="rwkv_k")
    v = mm(xs, [(w_rkv2d, 0, 2)], a_part=(2, 6), name="rwkv_v")
    lw = mm(mm(xs, w_la, a_part=(3, 6), epi=_tanh_epi, out_dtype=BF16, name="rwkv_w_la"), w_lb, epi=_logdecay_epi,
            extras=[_row_extra(w0, tn)], name="rwkv_w_lb")
    a = mm(mm(xs, a_la, a_part=(4, 6), out_dtype=BF16, name="rwkv_a_la"), a_lb, epi=_sigmoid_bias_epi,
           extras=[_row_extra(a0, tn)], name="rwkv_a_lb")
    g = mm(mm(xs, g_la, a_part=(5, 6), epi=_sigmoid_epi, out_dtype=BF16, name="rwkv_g_la"), g_lb, name="rwkv_g_lb")
    sh = lambda z: z.reshape(b, t, d)
    z = rwkv_scan(sh(r), sh(k), sh(v), sh(a), sh(lw), sh(g), k_k, k_a, r_k, ln_g, ln_b)
    return z.reshape(b * t, d), [(w_out, 0, 0)], None


def _nsa_compress_body(kv_ref, pe_ref, w1_ref, w2_ref, o_ref, *, nchunk, dh):
    st = NSA_CMP_STRIDE
    hid = w1_ref.shape[1]
    first = jnp.zeros((nchunk, hid), F32)
    second = jnp.zeros((nchunk, hid), F32)
    for j in range(st):
        xj = kv_ref[0, pl.ds(j, nchunk, stride=st), :]
        first = first + _dot((xj + pe_ref[j:j + 1, :]).astype(BF16), w1_ref[j * dh:(j + 1) * dh, :])
        second = second + _dot((xj + pe_ref[st + j:st + j + 1, :]).astype(BF16),
                               w1_ref[(st + j) * dh:(st + j + 1) * dh, :])
    hidden = jax.nn.gelu(first + pltpu.roll(second, nchunk - 1, axis=0))
    out = _dot(hidden.astype(BF16), w2_ref[...])
    row = lax.broadcasted_iota(jnp.int32, out.shape, 0)
    o_ref[0, 0] = jnp.where(row < nchunk - 1, out, 0.0)


def nsa_compress(proj, col0, pe, w1, w2):
    b, t, _ = proj.shape
    g, dh = NSA_KV_GROUPS, NSA_HEAD_DIM
    nchunk = t // NSA_CMP_STRIDE
    hid = w1.shape[1]
    return pl.pallas_call(
        functools.partial(_nsa_compress_body, nchunk=nchunk, dh=dh),
        grid=(b, g),
        in_specs=[
            pl.BlockSpec((1, t, dh), lambda bi, gi: (bi, 0, col0 // dh + gi)),
            pl.BlockSpec((NSA_CMP_BLOCK, dh), lambda bi, gi: (0, 0)),
            pl.BlockSpec((NSA_CMP_BLOCK * dh, hid), lambda bi, gi: (0, 0)),
            pl.BlockSpec((hid, dh), lambda bi, gi: (0, 0)),
        ],
        out_specs=pl.BlockSpec((1, 1, nchunk, dh), lambda bi, gi: (bi, gi, 0, 0)),
        out_shape=jax.ShapeDtypeStruct((b, g, nchunk, dh), F32),
        compiler_params=_cparams("parallel", "parallel"),
        name="nsa_compress",
    )(proj, pe, w1.astype(BF16), w2.astype(BF16))


def _stack_heads(q, r, dh):
    return jnp.concatenate([q[:, i * dh:(i + 1) * dh] for i in range(r)], axis=0)


def _unstack_heads(o, r, tq):
    return jnp.concatenate([o[i * tq:(i + 1) * tq] for i in range(r)], axis=-1)


def _gate_rows(gates, branch, r, tq):
    return jnp.concatenate([gates[:, branch * r + i:branch * r + i + 1] for i in range(r)], axis=0)


def _nsa_cmp_body(q_ref, kc_ref, vc_ref, gates_ref, selmap_t_ref, o_ref, sel_ref, *, tq, r, dh, n_sel):
    q0 = pl.program_id(2) * tq
    qs = (_stack_heads(q_ref[0], r, dh) * (dh ** -0.5)).astype(BF16)
    kc, vc = kc_ref[0, 0].astype(BF16), vc_ref[0, 0].astype(BF16)
    ncp = kc.shape[0]
    s = _dot_nt(qs, kc)
    rows = lax.broadcasted_iota(jnp.int32, (r * tq, ncp), 0)
    cmp_end = lax.broadcasted_iota(jnp.int32, (r * tq, ncp), 1) * NSA_CMP_STRIDE + (NSA_CMP_BLOCK - 1)
    t_pos = q0 + rows % tq
    vis = cmp_end <= t_pos
    s = jnp.where(vis, s, NEG_INF)
    e = jnp.exp(s - jnp.max(s, axis=-1, keepdims=True))
    p = e / jnp.sum(e, axis=-1, keepdims=True) * vis.astype(F32)
    o = _dot(p.astype(BF16), vc)
    o = o * _gate_rows(gates_ref[0, 0], 0, r, tq)
    o_ref[0] = _unstack_heads(o, r, tq)
    p_sum = p[0:tq]
    for i in range(1, r):
        p_sum = p_sum + p[i * tq:(i + 1) * tq]
    hi = p_sum.astype(BF16)
    lo = (p_sum - hi.astype(F32)).astype(BF16)
    selmap_t = selmap_t_ref[...]
    imp = _dot_nt(selmap_t, hi) + _dot_nt(selmap_t, lo)
    ns = imp.shape[0]
    blk = lax.broadcasted_iota(jnp.int32, (ns, tq), 0)
    tq_pos = q0 + lax.broadcasted_iota(jnp.int32, (ns, tq), 1)
    cur = tq_pos // NSA_SEL_BLOCK
    valid = blk * NSA_SEL_BLOCK <= tq_pos
    forced = (blk == 0) | (blk == cur) | (blk == cur - 1)
    score = jnp.where(valid, jnp.where(forced, POS_BIG, imp), NEG_INF)
    rank = jnp.zeros((ns, tq), F32)
    for m in range(ns):
        sm = score[m:m + 1, :]
        beats = (sm > score) | ((sm == score) & (blk > m))
        rank = rank + beats.astype(F32)
    chosen = ((rank < n_sel) & valid).astype(F32)
    sel_ref[0, 0] = (chosen.T - 1.0).astype(sel_ref.dtype)


def _flash_update(s, v, m_ref, l_ref, acc_ref):
    lanes = m_ref.shape[1]
    assert s.shape[1] % lanes == 0 and acc_ref.shape[1] == lanes
    m_old = m_ref[...]
    m_new = jnp.maximum(m_old, jnp.max(s, axis=-1, keepdims=True))
    alpha = jnp.exp2(m_old - m_new)
    p = jnp.exp2(s - jnp.concatenate([m_new] * (s.shape[1] // lanes), axis=1))
    l_ref[...] = alpha * l_ref[...] + jnp.sum(p, axis=-1, keepdims=True)
    acc_ref[...] = alpha * acc_ref[...] + _dot(p.astype(BF16), v)
    m_ref[...] = m_new


def _nsa_sel_body(qi_tbl, ki_tbl, diag_tbl, q_ref, k_ref, v_ref, gates_ref, oin_ref, sel_ref, o_ref, qa_ref, m_ref,
                  l_ref, acc_ref, *, tq, tk, r, dh):
    step = pl.program_id(2)
    qi, ki, diag = qi_tbl[step], ki_tbl[step], diag_tbl[step]
    ns = sel_ref.shape[-1]
    assert ns <= dh

    @pl.when(ki == 0)
    def _():
        m_ref[...] = jnp.full_like(m_ref, NEG_INF)
        l_ref[...] = jnp.zeros_like(l_ref)
        acc_ref[...] = jnp.zeros_like(acc_ref)
        qs = _stack_heads(q_ref[0], r, dh) * (dh ** -0.5 * LOG2_E)
        pieces = [qs, jnp.concatenate([sel_ref[0, 0].astype(F32)] * r, axis=0)]
        if ns < dh:
            pieces.append(jnp.zeros((r * tq, dh - ns), F32))
        qa_ref[...] = jnp.concatenate(pieces, axis=1).astype(BF16)

    blk_of_key = (ki * tk + lax.broadcasted_iota(jnp.int32, (tk, dh), 0)) // NSA_SEL_BLOCK
    own_block = jnp.where(blk_of_key == lax.broadcasted_iota(jnp.int32, (tk, dh), 1), POS_BIG, 0.0)
    ka = jnp.concatenate([k_ref[0], own_block], axis=1).astype(BF16)
    s = _dot_nt(qa_ref[...], ka)
    v = v_ref[0].astype(BF16)

    @pl.when(diag == 0)
    def _():
        _flash_update(s, v, m_ref, l_ref, acc_ref)

    @pl.when(diag == 1)
    def _():
        t_pos = qi * tq + lax.broadcasted_iota(jnp.int32, (r * tq, tk), 0) % tq
        k_pos = ki * tk + lax.broadcasted_iota(jnp.int32, (r * tq, tk), 1)
        _flash_update(jnp.where(k_pos <= t_pos, s, NEG_INF), v, m_ref, l_ref, acc_ref)
        o = acc_ref[...] / l_ref[...] * _gate_rows(gates_ref[0, 0], 1, r, tq)
        o_ref[0] = (oin_ref[0] + _unstack_heads(o, r, tq)).astype(o_ref.dtype)


def _nsa_win_body(q_ref, *rest, tq, r, dh, nwin):
    k_refs, v_refs = rest[:nwin], rest[nwin:2 * nwin]
    gates_ref, oin_ref, o_ref = rest[2 * nwin:]
    qi = pl.program_id(2)
    qs = (_stack_heads(q_ref[0], r, dh) * (dh ** -0.5 * LOG2_E)).astype(BF16)
    t_loc = lax.broadcasted_iota(jnp.int32, (r * tq, tq), 0) % tq
    k_loc = lax.broadcasted_iota(jnp.int32, (r * tq, tq), 1)
    tiles = []
    for j in range(nwin):
        s = _dot_nt(qs, k_refs[j][0].astype(BF16))
        if j == nwin - 1:
            s = jnp.where(k_loc <= t_loc, s, NEG_INF)
        else:
            if j == 0:
                s = jnp.where(k_loc > t_loc, s, NEG_INF)
            s = jnp.where(qi - (nwin - 1) + j >= 0, s, NEG_INF)
        tiles.append(s)
    s = jnp.concatenate(tiles, axis=1)
    p = jnp.exp2(s - jnp.max(s, axis=-1, keepdims=True))
    v = jnp.concatenate([v_refs[j][0] for j in range(nwin)], axis=0).astype(BF16)
    o = _dot(p.astype(BF16), v) / jnp.sum(p, axis=-1, keepdims=True)
    o = o * _gate_rows(gates_ref[0, 0], 2, r, tq)
    o_ref[0] = (oin_ref[0] + _unstack_heads(o, r, tq)).astype(o_ref.dtype)


def nsa_mix(h, w_in, pe_k, pe_v, ck_w1, ck_w2, cv_w1, cv_w2, w_out, tq=256):
    b, t, d = h.shape
    nh, g, dh = NSA_HEADS, NSA_KV_GROUPS, NSA_HEAD_DIM
    r = nh // g
    hd, kd = nh * dh, g * dh
    n_main = hd + 6 * kd
    h2 = h.reshape(b * t, d)
    proj = mm(h2, w_in[:, :n_main], name="nsa_in").reshape(b, t, n_main)
    n_gate = w_in.shape[1] - n_main
    w_gate = jnp.pad(w_in[:, n_main:], ((0, 0), (0, 128 - n_gate)))
    gates = mm(h2, w_gate, epi=_sigmoid_epi, name="nsa_gates")[:, :n_gate]
    gates = jnp.transpose(gates.reshape(b, t, 3, g, r), (0, 3, 1, 2, 4)).reshape(b, g, t, 3 * r)
    k_cmp = nsa_compress(proj, hd, pe_k, ck_w1, ck_w2)
    v_cmp = nsa_compress(proj, hd + kd, pe_v, cv_w1, cv_w2)
    ncp = k_cmp.shape[2]
    ns = t // NSA_SEL_BLOCK
    n_sel = min(NSA_SEL_TOP, ns)
    cs = jnp.arange(ncp)[None, :] * NSA_CMP_STRIDE
    ss = jnp.arange(ns)[:, None] * NSA_SEL_BLOCK
    sel_map_t = (jnp.maximum(jnp.minimum(cs + NSA_CMP_BLOCK, ss + NSA_SEL_BLOCK) - jnp.maximum(cs, ss), 0)
                 .astype(F32) / NSA_CMP_BLOCK).astype(BF16)
    tq = _pick(t, tq)
    assert NSA_WINDOW % tq == 0
    nq = t // tq
    wq = r * dh
    col = lambda base: base // dh
    q_spec3 = pl.BlockSpec((1, tq, wq), lambda bi, gi, qi: (bi, qi, gi))
    gate_spec3 = pl.BlockSpec((1, 1, tq, 3 * r), lambda bi, gi, qi: (bi, gi, qi, 0))
    o_cmp, sel = pl.pallas_call(
        functools.partial(_nsa_cmp_body, tq=tq, r=r, dh=dh, n_sel=n_sel),
        grid=(b, g, nq),
        in_specs=[
            q_spec3,
            pl.BlockSpec((1, 1, ncp, dh), lambda bi, gi, qi: (bi, gi, 0, 0)),
            pl.BlockSpec((1, 1, ncp, dh), lambda bi, gi, qi: (bi, gi, 0, 0)),
            gate_spec3,
            pl.BlockSpec((ns, ncp), lambda bi, gi, qi: (0, 0)),
        ],
        out_specs=[q_spec3, pl.BlockSpec((1, 1, tq, ns), lambda bi, gi, qi: (bi, gi, qi, 0))],
        out_shape=[jax.ShapeDtypeStruct((b, t, hd), F32), jax.ShapeDtypeStruct((b, g, t, ns), BF16)],
        compiler_params=_cparams("parallel", "parallel", "parallel"),
        name="nsa_cmp",
    )(proj, k_cmp, v_cmp, gates, sel_map_t)

    tk = _pick(t, NSA_SEL_KEY_TILE)
    assert tk % tq == 0
    pairs = [(qi, ki) for qi in range(nq) for ki in range((qi * tq + tq - 1) // tk + 1)]
    qi_tbl = jnp.asarray([pq for pq, _ in pairs], jnp.int32)
    ki_tbl = jnp.asarray([pk for _, pk in pairs], jnp.int32)
    diag_tbl = jnp.asarray([int(pk == (pq * tq + tq - 1) // tk) for pq, pk in pairs], jnp.int32)
    q_spec_p = pl.BlockSpec((1, tq, wq), lambda bi, gi, p, qt, kt, dt: (bi, qt[p], gi))

    def kv_spec_p(base):
        return pl.BlockSpec((1, tk, dh), lambda bi, gi, p, qt, kt, dt: (bi, kt[p], col(base) + gi))

    o_sel = pl.pallas_call(
        functools.partial(_nsa_sel_body, tq=tq, tk=tk, r=r, dh=dh),
        grid_spec=pltpu.PrefetchScalarGridSpec(
            num_scalar_prefetch=3,
            grid=(b, g, len(pairs)),
            in_specs=[q_spec_p, kv_spec_p(hd + 2 * kd), kv_spec_p(hd + 3 * kd),
                      pl.BlockSpec((1, 1, tq, 3 * r), lambda bi, gi, p, qt, kt, dt: (bi, gi, qt[p], 0)),
                      q_spec_p,
                      pl.BlockSpec((1, 1, tq, ns), lambda bi, gi, p, qt, kt, dt: (bi, gi, qt[p], 0))],
            out_specs=q_spec_p,
            scratch_shapes=[pltpu.VMEM((r * tq, 2 * dh), BF16), pltpu.VMEM((r * tq, dh), F32),
                            pltpu.VMEM((r * tq, dh), F32), pltpu.VMEM((r * tq, dh), F32)],
        ),
        out_shape=jax.ShapeDtypeStruct((b, t, hd), F32),
        compiler_params=_cparams("parallel", "parallel", "arbitrary"),
        name="nsa_sel",
    )(qi_tbl, ki_tbl, diag_tbl, proj, proj, proj, gates, o_cmp, sel)

    nwin = NSA_WINDOW // tq + 1

    def kv_spec_w(base, j):
        return pl.BlockSpec((1, tq, dh),
                            lambda bi, gi, qi: (bi, jnp.maximum(qi - (nwin - 1) + j, 0), col(base) + gi))

    o_all = pl.pallas_call(
        functools.partial(_nsa_win_body, tq=tq, r=r, dh=dh, nwin=nwin),
        grid=(b, g, nq),
        in_specs=([q_spec3] + [kv_spec_w(hd + 4 * kd, j) for j in range(nwin)]
                  + [kv_spec_w(hd + 5 * kd, j) for j in range(nwin)] + [gate_spec3, q_spec3]),
        out_specs=q_spec3,
        out_shape=jax.ShapeDtypeStruct((b, t, hd), BF16),
        compiler_params=_cparams("parallel", "parallel", "parallel"),
        name="nsa_win",
    )(proj, *([proj] * (2 * nwin)), gates, o_sel)
    return o_all.reshape(b * t, hd), [(w_out, 0, 0)], None


def _silu(a):
    return a * jax.nn.sigmoid(a)


def _ada_epi(parts, extras):
    return parts[0] + extras[0]


def ada_modulation(c, ada_w, ada_b):
    depth, d, n6 = ada_w.shape
    b = c.shape[0]
    rows = ((b + 7) // 8) * 8
    cond = jnp.pad(c, ((0, rows - b), (0, 0)))
    w2d = ada_w.reshape(depth * d, n6)
    mods = []
    for i in range(depth):
        mods.append(mm(cond, [(w2d, 0, i)], a_act=_silu, epi=_ada_epi, extras=[_row_extra(ada_b[i], _pick(n6, 1024))],
                       precision=HIGHEST, name="ada_mod")[:b])
    return jnp.stack(mods)


def kernel(x, c, ada_w, ada_b, norm_g, mlp_w1, mlp_w2, rwkv_mu, rwkv_w_rkv, rwkv_w0, rwkv_w_la, rwkv_w_lb, rwkv_a0, rwkv_a_la, rwkv_a_lb, rwkv_g_la, rwkv_g_lb, rwkv_k_k, rwkv_k_a, rwkv_r_k, rwkv_ln_g, rwkv_ln_b, rwkv_w_out, ret_w_in, ret_gn_g, ret_gn_b, ret_w_out, conv_pw1_w, conv_pw1_b, conv_dw_w, conv_dw_b, conv_ln_g, conv_ln_b, conv_pw2_w, conv_pw2_b, nsa_w_in, nsa_pe_k, nsa_pe_v, nsa_ck_w1, nsa_ck_w2, nsa_cv_w1, nsa_cv_w2, nsa_w_out):
    b, t, d = x.shape
    depth = ada_w.shape[0]
    mod = ada_modulation(c, ada_w, ada_b).reshape(depth, b, 6, d)
    h = None
    for i in range(depth):
        sh_t, sc_t, gt_t, sh_c, sc_c, gt_c = (mod[i, :, j] for j in range(6))
        kind = i % 4
        if kind == 0:
            act, ws, bias = rwkv7_time_mix(x, norm_g[i, 0], sc_t, sh_t, rwkv_mu, rwkv_w_rkv, rwkv_w0, rwkv_w_la,
                                           rwkv_w_lb, rwkv_a0, rwkv_a_la, rwkv_a_lb, rwkv_g_la, rwkv_g_lb, rwkv_k_k,
                                           rwkv_k_a, rwkv_r_k.reshape(-1), rwkv_ln_g, rwkv_ln_b, rwkv_w_out)
        else:
            if h is None:
                (h,) = resid_norm(x, pre=(norm_g[i, 0], sc_t, sh_t))
            if kind == 1:
                act, ws, bias = retention_mix(h, ret_w_in, ret_gn_g, ret_gn_b, ret_w_out)
            elif kind == 2:
                act, ws, bias = conformer_conv_mix(h, conv_pw1_w, conv_pw1_b, conv_dw_w, conv_dw_b, conv_ln_g,
                                                   conv_ln_b, conv_pw2_w, conv_pw2_b)
            else:
                act, ws, bias = nsa_mix(h, nsa_w_in, nsa_pe_k, nsa_pe_v, nsa_ck_w1, nsa_ck_w2, nsa_cv_w1, nsa_cv_w2,
                                        nsa_w_out)
        y = project(act, ws, bias, "mixer_out").reshape(b, t, d)
        x, h = resid_norm(x, y, post=(norm_g[i, 1], gt_t), pre=(norm_g[i, 2], sc_c, sh_c))
        y = project(*sqrelu_mlp(h.reshape(b * t, d), mlp_w1, mlp_w2, i), "mlp_down").reshape(b, t, d)
        nxt = i + 1
        if nxt < depth and nxt % 4 != 0:
            x, h = resid_norm(x, y, post=(norm_g[i, 3], gt_c),
                              pre=(norm_g[nxt, 0], mod[nxt, :, 1], mod[nxt, :, 0]))
        else:
            (x,) = resid_norm(x, y, post=(norm_g[i, 3], gt_c))
            h = None
    return x
```

```python
import functools
import math

import jax
import jax.numpy as jnp
from jax import lax
from jax.experimental import pallas as pl
from jax.experimental.pallas import tpu as pltpu

F32 = jnp.float32
BF16 = jnp.bfloat16
HIGHEST = lax.Precision.HIGHEST

NORM_EPS = 1e-6
NEG_INF = -1e30
POS_BIG = 1e30
LOG2_E = math.log2(math.e)

RWKV_HEAD_DIM = 64
RWKV_GN_EPS = 64e-5
RWKV_CHUNK = 64
RWKV_HEADS_PER_STEP = 32

RET_HEADS = 8
RET_CHUNK = 256
RET_GN_EPS = 1e-5
ROPE_BASE = 10000.0

CONV_WIDTH = 31
CONV_HALO = 32
CONV_LN_EPS = 1e-5

NSA_HEADS = 16
NSA_KV_GROUPS = 4
NSA_HEAD_DIM = 128
NSA_CMP_BLOCK = 32
NSA_CMP_STRIDE = 16
NSA_SEL_BLOCK = 64
NSA_SEL_TOP = 16
NSA_WINDOW = 512
NSA_SEL_KEY_TILE = 512

VMEM_LIMIT_BYTES = 56 * 1024 * 1024


def _cparams(*sem):
    return pltpu.CompilerParams(dimension_semantics=sem, vmem_limit_bytes=VMEM_LIMIT_BYTES)


def _dot(a, b, precision=None):
    return jnp.dot(a, b, preferred_element_type=F32, precision=precision)


def _dot_nt(a, b, precision=None):
    return lax.dot_general(a, b, (((1,), (1,)), ((), ())), preferred_element_type=F32, precision=precision)


def _pick(n, pref):
    if n <= pref:
        return n
    t = pref
    while n % t:
        t //= 2
    return t


def _mm_body(*refs, n_w, n_ex, n_out, nk, epi, precision, has_a_add, a_act):
    a_ref = refs[0]
    pos = 1
    a_add_ref = None
    if has_a_add:
        a_add_ref = refs[pos]
        pos += 1
    w_refs = refs[pos:pos + n_w]
    pos += n_w
    ex_refs = refs[pos:pos + n_ex]
    pos += n_ex
    o_refs = refs[pos:pos + n_out]
    acc_refs = refs[pos + n_out:]

    def finish(parts):
        res = epi(parts, [e[...] for e in ex_refs])
        if n_out == 1 and not isinstance(res, (tuple, list)):
            res = (res,)
        for o_ref, val in zip(o_refs, res, strict=True):
            o_ref[...] = val.astype(o_ref.dtype)

    a = a_ref[...]
    if a_add_ref is not None:
        a = a.astype(F32) + a_add_ref[...]
    if a_act is not None:
        a = a_act(a)
    if precision is None:
        a = a.astype(BF16)
    parts = []
    for w_ref in w_refs:
        w = w_ref[...]
        if precision is None:
            w = w.astype(BF16)
        parts.append(_dot(a, w, precision))

    if nk == 1:
        finish(parts)
    else:
        k = pl.program_id(2)

        @pl.when(k == 0)
        def _():
            for acc, p in zip(acc_refs, parts):
                acc[...] = p

        @pl.when(k > 0)
        def _():
            for acc, p in zip(acc_refs, parts):
                acc[...] += p

        @pl.when(k == nk - 1)
        def _():
            finish([acc[...] for acc in acc_refs])


def _first(parts, extras):
    return parts[0]


def mm(a, ws, *, n_out=None, extras=(), epi=_first, out_dtype=F32, tm=2048, tn=1024, tk=2048, precision=None,
       a_add=None, a_act=None, a_part=(0, 1), name="mm"):
    m, kdim = a.shape
    a_s, a_parts = a_part
    assert m % a_parts == 0
    m //= a_parts
    if not isinstance(ws, (list, tuple)):
        ws = [(ws, 0, 0)]
    ws = [tuple(w) + (0,) * (3 - len(w)) for w in ws]
    if n_out is None:
        n_out = ws[0][0].shape[1]
    tm = _pick(m, tm)
    tn = _pick(n_out, tn)
    tk = _pick(kdim, tk)
    nk = kdim // tk
    assert m % tm == 0 and n_out % tn == 0 and kdim % tk == 0
    in_specs = [pl.BlockSpec((tm, tk), lambda i, j, k, o=a_s * (m // tm): (i + o, k))]
    args = [a]
    if a_add is not None:
        in_specs.append(pl.BlockSpec((1, tk), lambda i, j, k: (0, k)))
        args.append(a_add)
    for w, off, koff in ws:
        in_specs.append(pl.BlockSpec((tk, tn), lambda i, j, k, off=off, ko=koff * nk: (k + ko, j + off)))
        args.append(w.astype(BF16) if precision is None else w)
    for arr, bshape, imap in extras:
        in_specs.append(pl.BlockSpec(bshape, imap))
        args.append(arr)
    scratch = [pltpu.VMEM((tm, tn), F32) for _ in ws] if nk > 1 else []
    multi = isinstance(out_dtype, (tuple, list))
    out_dtypes = tuple(out_dtype) if multi else (out_dtype,)
    body = functools.partial(_mm_body, n_w=len(ws), n_ex=len(extras), n_out=len(out_dtypes), nk=nk, epi=epi,
                             precision=precision, has_a_add=a_add is not None, a_act=a_act)
    outs = pl.pallas_call(
        body,
        grid=(m // tm, n_out // tn, nk),
        in_specs=in_specs,
        out_specs=[pl.BlockSpec((tm, tn), lambda i, j, k: (i, j)) for _ in out_dtypes],
        out_shape=[jax.ShapeDtypeStruct((m, n_out), dt) for dt in out_dtypes],
        scratch_shapes=scratch,
        compiler_params=_cparams("parallel", "parallel", "arbitrary"),
        name=name,
    )(*args)
    return tuple(outs) if multi else outs[0]


def _row_extra(vec, tn):
    return (vec.reshape(1, -1), (1, tn), lambda i, j, k: (0, j))


def _rms(x, g):
    return x * lax.rsqrt(jnp.mean(x * x, axis=-1, keepdims=True) + NORM_EPS) * g


def _resid_norm_body(*refs, has_y, has_h):
    pos = 0
    x_ref = refs[pos]; pos += 1
    if has_y:
        y_ref, gpost_ref, gate_ref = refs[pos:pos + 3]; pos += 3
    if has_h:
        gpre_ref, scale_ref, shift_ref = refs[pos:pos + 3]; pos += 3
    outs = refs[pos:]
    x = x_ref[0]
    o = 0
    if has_y:
        y = y_ref[0].astype(F32)
        x = x + (1.0 + gate_ref[0]) * _rms(y, gpost_ref[...])
        outs[o][0] = x
        o += 1
    if has_h:
        h = _rms(x, gpre_ref[...]) * (1.0 + scale_ref[0]) + shift_ref[0]
        outs[o][0] = h.astype(outs[o].dtype)


def resid_norm(x, y=None, post=None, pre=None, tt=512):
    b, t, d = x.shape
    tt = _pick(t, tt)
    row = pl.BlockSpec((1, tt, d), lambda bi, ti: (bi, ti, 0))
    vec = pl.BlockSpec((1, d), lambda bi, ti: (0, 0))
    bvec = pl.BlockSpec((1, 1, d), lambda bi, ti: (bi, 0, 0))
    args, in_specs, out_shapes, out_specs = [x], [row], [], []
    if y is not None:
        args += [y, post[0].reshape(1, d), post[1].reshape(b, 1, d)]
        in_specs += [row, vec, bvec]
        out_shapes.append(jax.ShapeDtypeStruct((b, t, d), F32))
        out_specs.append(row)
    if pre is not None:
        args += [pre[0].reshape(1, d), pre[1].reshape(b, 1, d), pre[2].reshape(b, 1, d)]
        in_specs += [vec, bvec, bvec]
        out_shapes.append(jax.ShapeDtypeStruct((b, t, d), BF16))
        out_specs.append(row)
    res = pl.pallas_call(
        functools.partial(_resid_norm_body, has_y=y is not None, has_h=pre is not None),
        grid=(b, t // tt),
        in_specs=in_specs,
        out_specs=out_specs,
        out_shape=out_shapes,
        compiler_params=_cparams("parallel", "parallel"),
        name="resid_norm",
    )(*args)
    return tuple(res)


def _add_bias(parts, extras):
    return parts[0] + extras[0]


def project(act, ws, bias, name):
    if bias is None:
        return mm(act, ws, out_dtype=BF16, name=name)
    tn = _pick(ws[0][0].shape[1], 1024)
    return mm(act, ws, epi=_add_bias, extras=[_row_extra(bias, tn)], tn=tn, out_dtype=BF16, name=name)


def _relu2(parts, extras):
    r = jnp.maximum(parts[0], 0.0)
    return r * r


def sqrelu_mlp(h2d, w1, w2, layer):
    nl, d, f = w1.shape
    a = mm(h2d, [(w1.reshape(nl * d, f), 0, layer)], epi=_relu2, out_dtype=BF16, name="mlp_up")
    return a, [(w2.reshape(nl * f, d), 0, layer)], None


def _glu(parts, extras):
    return (parts[0] + extras[0]) * jax.nn.sigmoid(parts[1] + extras[1])


def _conv_ln_body(cur_ref, prev_ref, w_ref, b_ref, g_ref, beta_ref, o_ref, buf_ref, acc_ref, *, tt, d):
    ti = pl.program_id(1)
    halo = prev_ref[0].astype(F32)
    buf_ref[0:CONV_HALO, :] = jnp.where(ti == 0, jnp.zeros_like(halo), halo)
    buf_ref[CONV_HALO:, :] = cur_ref[0].astype(F32)
    lane_chunk = min(d, 256)
    row_chunk = min(tt, 64)
    base = CONV_HALO - (CONV_WIDTH - 1)
    sub = 8
    for c0 in range(0, d, lane_chunk):
        for r0 in range(0, tt, row_chunk):
            acc = None
            for s in range(sub):
                taps = [j for j in range(CONV_WIDTH) if (base + j) % sub == s]
                if not taps:
                    continue
                rows = row_chunk + (sub if s else 0)
                part = jnp.zeros((rows, lane_chunk), F32)
                for j in taps:
                    off = r0 + base + j - s
                    part = part + buf_ref[off:off + rows, c0:c0 + lane_chunk] * w_ref[j:j + 1, c0:c0 + lane_chunk]
                part = part[s:s + row_chunk]
                acc = part if acc is None else acc + part
            acc_ref[r0:r0 + row_chunk, c0:c0 + lane_chunk] = acc
    y = acc_ref[...] + b_ref[...]
    mu = jnp.mean(y, axis=-1, keepdims=True)
    yc = y - mu
    var = jnp.mean(yc * yc, axis=-1, keepdims=True)
    z = yc * lax.rsqrt(var + CONV_LN_EPS) * g_ref[...] + beta_ref[...]
    o_ref[0] = (z * jax.nn.sigmoid(z)).astype(o_ref.dtype)


def conv_ln_silu(u, dw_w, dw_b, ln_g, ln_b, tt=128):
    b, t, d = u.shape
    tt = _pick(t, tt)
    hb = tt // CONV_HALO
    vec = pl.BlockSpec((1, d), lambda bi, ti: (0, 0))
    return pl.pallas_call(
        functools.partial(_conv_ln_body, tt=tt, d=d),
        grid=(b, t // tt),
        in_specs=[
            pl.BlockSpec((1, tt, d), lambda bi, ti: (bi, ti, 0)),
            pl.BlockSpec((1, CONV_HALO, d), lambda bi, ti: (bi, jnp.maximum(ti * hb - 1, 0), 0)),
            pl.BlockSpec((CONV_WIDTH, d), lambda bi, ti: (0, 0)),
            vec, vec, vec,
        ],
        out_specs=pl.BlockSpec((1, tt, d), lambda bi, ti: (bi, ti, 0)),
        out_shape=jax.ShapeDtypeStruct((b, t, d), BF16),
        scratch_shapes=[pltpu.VMEM((tt + CONV_HALO, d), F32), pltpu.VMEM((tt, d), F32)],
        compiler_params=_cparams("parallel", "parallel"),
        name="conv_ln_silu",
    )(u, u, dw_w, dw_b.reshape(1, d), ln_g.reshape(1, d), ln_b.reshape(1, d))


def conformer_conv_mix(h, pw1_w, pw1_b, dw_w, dw_b, ln_g, ln_b, pw2_w, pw2_b):
    b, t, d = h.shape
    tn = _pick(d, 1024)
    b1 = pw1_b.reshape(1, -1)
    u = mm(h.reshape(b * t, d), [(pw1_w, 0), (pw1_w, d // tn)], n_out=d, tm=1024, tn=tn, epi=_glu,
           extras=[(b1, (1, tn), lambda i, j, k: (0, j)),
                   (b1, (1, tn), lambda i, j, k, o=d // tn: (0, j + o))], out_dtype=BF16, name="conf_pw1_glu")
    z = conv_ln_silu(u.reshape(b, t, d), dw_w, dw_b, ln_g, ln_b)
    return z.reshape(b * t, d), [(pw2_w, 0, 0)], pw2_b


def _rope(x, cos, sin):
    half = x.shape[-1] // 2
    x1, x2 = x[:, :half], x[:, half:]
    return jnp.concatenate([x1 * cos - x2 * sin, x2 * cos + x1 * sin], axis=-1)


def _retention_body(q_ref, k_ref, v_ref, gate_ref, cos_ref, sin_ref, inner_ref, qdec_ref, kdec_ref, cdec_ref,
                    gng_ref, gnb_ref, o_ref, state_ref, *, dk):
    @pl.when(pl.program_id(2) == 0)
    def _():
        state_ref[...] = jnp.zeros_like(state_ref)

    cos, sin = cos_ref[...], sin_ref[...]
    q = _rope(q_ref[0].astype(F32), cos, sin)
    k = _rope(k_ref[0].astype(F32), cos, sin) * (dk ** -0.5)
    v = v_ref[0].astype(BF16)
    qb = q.astype(BF16)
    s = _dot_nt(qb, k.astype(BF16)) * inner_ref[0]
    state = state_ref[...]
    o = _dot(s.astype(BF16), v) + _dot(qb, state.astype(BF16)) * qdec_ref[0]
    kd_t = (k * kdec_ref[0]).T.astype(BF16)
    state_ref[...] = state * cdec_ref[0] + _dot(kd_t, v)
    mu = jnp.mean(o, axis=-1, keepdims=True)
    oc = o - mu
    var = jnp.mean(oc * oc, axis=-1, keepdims=True)
    on = oc * lax.rsqrt(var + RET_GN_EPS) * gng_ref[...] + gnb_ref[...]
    gate = gate_ref[0].astype(F32)
    o_ref[0] = (gate * jax.nn.sigmoid(gate) * on).astype(o_ref.dtype)


def retention_mix(h, w_in, gn_g, gn_b, w_out):
    b, t, d = h.shape
    nh, c = RET_HEADS, _pick(t, RET_CHUNK)
    dk = d // nh
    dv = 2 * dk
    proj = mm(h.reshape(b * t, d), w_in, out_dtype=BF16, name="ret_in").reshape(b, t, 6 * d)
    pos = jnp.arange(t, dtype=F32)
    inv_freq = ROPE_BASE ** (-jnp.arange(0, dk, 2, dtype=F32) / dk)
    ang = pos[:, None] * inv_freq[None, :]
    cos, sin = jnp.cos(ang), jnp.sin(ang)
    log_gamma = jnp.log(1.0 - 2.0 ** (-5.0 - jnp.arange(nh, dtype=F32)))
    idx = jnp.arange(c, dtype=F32)
    diff = idx[:, None] - idx[None, :]
    inner = jnp.where(diff >= 0, jnp.exp(jnp.maximum(diff, 0.0)[None] * log_gamma[:, None, None]), 0.0)
    q_dec = jnp.exp((idx + 1.0)[None] * log_gamma[:, None])[:, :, None]
    k_dec = jnp.exp((c - 1.0 - idx)[None] * log_gamma[:, None])[:, :, None]
    c_dec = jnp.exp(c * log_gamma)[:, None, None]
    nq = (nh * dk) // dk
    o = pl.pallas_call(
        functools.partial(_retention_body, dk=dk),
        grid=(b, nh, t // c),
        in_specs=[
            pl.BlockSpec((1, c, dk), lambda bi, hi, ci: (bi, ci, hi)),
            pl.BlockSpec((1, c, dk), lambda bi, hi, ci: (bi, ci, nq + hi)),
            pl.BlockSpec((1, c, dv), lambda bi, hi, ci: (bi, ci, nq + hi)),
            pl.BlockSpec((1, c, dv), lambda bi, hi, ci: (bi, ci, 2 * nq + hi)),
            pl.BlockSpec((c, dk // 2), lambda bi, hi, ci: (ci, 0)),
            pl.BlockSpec((c, dk // 2), lambda bi, hi, ci: (ci, 0)),
            pl.BlockSpec((1, c, c), lambda bi, hi, ci: (hi, 0, 0)),
            pl.BlockSpec((1, c, 1), lambda bi, hi, ci: (hi, 0, 0)),
            pl.BlockSpec((1, c, 1), lambda bi, hi, ci: (hi, 0, 0)),
            pl.BlockSpec((1, 1, 1), lambda bi, hi, ci: (hi, 0, 0)),
            pl.BlockSpec((1, dv), lambda bi, hi, ci: (0, hi)),
            pl.BlockSpec((1, dv), lambda bi, hi, ci: (0, hi)),
        ],
        out_specs=pl.BlockSpec((1, c, dv), lambda bi, hi, ci: (bi, ci, hi)),
        out_shape=jax.ShapeDtypeStruct((b, t, nh * dv), BF16),
        scratch_shapes=[pltpu.VMEM((dk, dv), F32)],
        compiler_params=_cparams("parallel", "parallel", "arbitrary"),
        name="retention",
    )(proj, proj, proj, proj, cos, sin, inner, q_dec, k_dec, c_dec, gn_g.reshape(1, -1), gn_b.reshape(1, -1))
    return o.reshape(b * t, nh * dv), [(w_out, 0, 0)], None


def _rwkv_pre_body(x_ref, prev_ref, g_ref, scale_ref, shift_ref, mu_ref, o_ref):
    ti = pl.program_id(1)
    g, scale, shift = g_ref[...], scale_ref[0], shift_ref[0]
    h = _rms(x_ref[0], g) * (1.0 + scale) + shift
    hp = _rms(prev_ref[0], g) * (1.0 + scale) + shift
    last = jnp.where(ti == 0, 0.0, hp[7:8, :])
    row = lax.broadcasted_iota(jnp.int32, h.shape, 0)
    shifted = jnp.where(row == 0, last, pltpu.roll(h, 1, axis=0))
    xx = shifted - h
    for s in range(6):
        o_ref[s, 0] = (h + xx * mu_ref[s:s + 1, :]).astype(o_ref.dtype)


def rwkv_pre(x, g_pre, scale, shift, mu, tt=256):
    b, t, d = x.shape
    tt = _pick(t, tt)
    bvec = pl.BlockSpec((1, 1, d), lambda bi, ti: (bi, 0, 0))
    return pl.pallas_call(
        _rwkv_pre_body,
        grid=(b, t // tt),
        in_specs=[
            pl.BlockSpec((1, tt, d), lambda bi, ti: (bi, ti, 0)),
            pl.BlockSpec((1, 8, d), lambda bi, ti: (bi, jnp.maximum(ti * (tt // 8) - 1, 0), 0)),
            pl.BlockSpec((1, d), lambda bi, ti: (0, 0)),
            bvec, bvec,
            pl.BlockSpec((6, d), lambda bi, ti: (0, 0)),
        ],
        out_specs=pl.BlockSpec((6, 1, tt, d), lambda bi, ti: (0, bi, ti, 0)),
        out_shape=jax.ShapeDtypeStruct((6, b, t, d), BF16),
        compiler_params=_cparams("parallel", "parallel"),
        name="rwkv_pre",
    )(x, x, g_pre.reshape(1, d), scale.reshape(b, 1, d), shift.reshape(b, 1, d), mu)


def _rwkv_scan_body(r_ref, k_ref, v_ref, a_ref, lw_ref, g_ref, kk_ref, ka_ref, rk_ref, lng_ref, lnb_ref, o_ref,
                    state_ref, *, chunk, heads, n):
    @pl.when(pl.program_id(2) == 0)
    def _():
        state_ref[...] = jnp.zeros_like(state_ref)

    ln = chunk
    hs = range(heads)
    row = lax.broadcasted_iota(jnp.int32, (ln, ln), 0)
    col = lax.broadcasted_iota(jnp.int32, (ln, ln), 1)
    incl = col <= row
    strict = col < row
    tri = incl.astype(BF16)
    eye = (row == col).astype(F32)
    wid = heads * n
    seg_w = min(wid, 256)
    assert wid % seg_w == 0 and seg_w % n == 0
    seg = (lax.broadcasted_iota(jnp.int32, (seg_w, seg_w), 0) // n
           == lax.broadcasted_iota(jnp.int32, (seg_w, seg_w), 1) // n).astype(BF16)

    def split_dot(m, x):
        hi = x.astype(BF16)
        lo = (x - hi.astype(F32)).astype(BF16)
        return _dot(m, hi) + _dot(m, lo)

    def seg_sum(x):
        hi = x.astype(BF16)
        lo = (x - hi.astype(F32)).astype(BF16)
        return jnp.concatenate([_dot(hi[:, j:j + seg_w], seg) + _dot(lo[:, j:j + seg_w], seg)
                                for j in range(0, wid, seg_w)], axis=1)

    heads_of = lambda x: [x[:, hd * n:(hd + 1) * n] for hd in hs]

    r, k, v, a = (ref[0].astype(F32) for ref in (r_ref, k_ref, v_ref, a_ref))
    lw = lw_ref[0]
    kk = k * kk_ref[...]
    kk = kk / jnp.maximum(jnp.sqrt(seg_sum(kk * kk)), 1e-12)
    k2 = k * (1.0 + (a - 1.0) * ka_ref[...])
    beta = kk * a
    c = split_dot(tri, lw)
    c_last = c[ln - 1:ln, :]
    e_neg = jnp.exp(-c)
    e_rem = jnp.exp(c_last - c)
    e_last = heads_of(jnp.exp(c_last))
    a_t = heads_of(-kk * jnp.exp(c - lw))
    r_t = heads_of(r * jnp.exp(c))
    b_t = heads_of(beta * e_neg)
    k_t = heads_of(k2 * e_neg)
    b_rem = heads_of(beta * e_rem)
    k_rem = heads_of(k2 * e_rem)
    vs = heads_of(v)
    bonus = seg_sum(r * k2 * rk_ref[...]) * v

    ar = [jnp.concatenate([a_t[h], r_t[h]], axis=0).astype(BF16) for h in hs]
    bk = [jnp.concatenate([b_t[h], k_t[h]], axis=0).astype(BF16) for h in hs]
    bk_rem = [jnp.concatenate([b_rem[h], k_rem[h]], axis=0).astype(BF16) for h in hs]
    vb = [vs[h].astype(BF16) for h in hs]
    state = [state_ref[h] for h in hs]
    p = [_dot_nt(ar[h], bk[h]) for h in hs]
    q0 = [_dot_nt(ar[h], state[h].astype(BF16)) for h in hs]
    m_ab = [jnp.where(strict, p[h][:ln, :ln], 0.0) for h in hs]
    m_ak = [jnp.where(strict, p[h][:ln, ln:], 0.0).astype(BF16) for h in hs]
    m_r = [jnp.concatenate([jnp.where(incl, p[h][ln:, :ln], 0.0), jnp.where(incl, p[h][ln:, ln:], 0.0)],
                           axis=1).astype(BF16) for h in hs]
    rhs = [q0[h][:ln] + _dot(m_ak[h], vb[h]) for h in hs]
    inv = [eye + m_ab[h] for h in hs]
    pw = [m_ab[h].astype(BF16) for h in hs]
    for _ in range(int(math.log2(ln)) - 1):
        pw = [_dot(pw[h], pw[h]).astype(BF16) for h in hs]
        inv = [inv[h] + _dot(inv[h].astype(BF16), pw[h]) for h in hs]
    u = [_dot(inv[h].astype(BF16), rhs[h].astype(BF16)) for h in hs]
    uv = [jnp.concatenate([u[h], vs[h]], axis=0) for h in hs]
    y = [q0[h][ln:] + _dot(m_r[h], uv[h].astype(BF16)) for h in hs]
    for h in hs:
        state_ref[h] = state[h] * e_last[h] + _dot(uv[h].T.astype(BF16), bk_rem[h])
    yn = []
    for h in hs:
        mu = jnp.mean(y[h], axis=-1, keepdims=True)
        yc = y[h] - mu
        var = jnp.mean(yc * yc, axis=-1, keepdims=True)
        yn.append(yc * lax.rsqrt(var + RWKV_GN_EPS))
    yn = jnp.concatenate(yn, axis=-1) * lng_ref[...] + lnb_ref[...]
    o_ref[0] = ((yn + bonus) * g_ref[0]).astype(o_ref.dtype)


def rwkv_scan(r, k, v, a, lw, g, k_k, k_a, r_k, ln_g, ln_b):
    b, t, d = r.shape
    n = RWKV_HEAD_DIM
    hps = min(RWKV_HEADS_PER_STEP, d // n)
    w = hps * n
    ln = _pick(t, RWKV_CHUNK)
    tok = pl.BlockSpec((1, ln, w), lambda bi, hi, ci: (bi, ci, hi))
    vec = pl.BlockSpec((1, w), lambda bi, hi, ci: (0, hi))
    return pl.pallas_call(
        functools.partial(_rwkv_scan_body, chunk=ln, heads=hps, n=n),
        grid=(b, d // w, t // ln),
        in_specs=[tok] * 6 + [vec] * 5,
        out_specs=tok,
        out_shape=jax.ShapeDtypeStruct((b, t, d), BF16),
        scratch_shapes=[pltpu.VMEM((hps, n, n), F32)],
        compiler_params=_cparams("parallel", "parallel", "arbitrary"),
        name="rwkv_scan",
    )(r, k, v, a, lw, g, k_k.reshape(1, d), k_a.reshape(1, d), r_k.reshape(1, d), ln_g.reshape(1, d),
      ln_b.reshape(1, d))


def _tanh_epi(parts, extras):
    return jnp.tanh(parts[0])


def _sigmoid_epi(parts, extras):
    return jax.nn.sigmoid(parts[0])


def _sigmoid_bias_epi(parts, extras):
    return jax.nn.sigmoid(extras[0] + parts[0])


def _logdecay_epi(parts, extras):
    return -jnp.exp(-jax.nn.softplus(-(extras[0] + parts[0])) - 0.5)


def rwkv7_time_mix(x, g_pre, scale, shift, mu, w_rkv, w0, w_la, w_lb, a0, a_la, a_lb, g_la, g_lb, k_k, k_a, r_k,
                   ln_g, ln_b, w_out):
    b, t, d = x.shape
    xs = rwkv_pre(x, g_pre, scale, shift, mu).reshape(6 * b * t, d)
    tn = _pick(d, 1024)
    w_rkv2d = w_rkv.reshape(3 * d, d)
    r = mm(xs, [(w_rkv2d, 0, 0)], a_part=(0, 6), out_dtype=BF16, name="rwkv_r")
    k = mm(xs, [(w_rkv2d, 0, 1)], a_part=(1, 6), out_dtype=BF16, name="rwkv_k")
    v = mm(xs, [(w_rkv2d, 0, 2)], a_part=(2, 6), out_dtype=BF16, name="rwkv_v")
    lw = mm(mm(xs, w_la, a_part=(3, 6), epi=_tanh_epi, out_dtype=BF16, name="rwkv_w_la"), w_lb, epi=_logdecay_epi,
            extras=[_row_extra(w0, tn)], name="rwkv_w_lb")
    a = mm(mm(xs, a_la, a_part=(4, 6), out_dtype=BF16, name="rwkv_a_la"), a_lb, epi=_sigmoid_bias_epi,
           extras=[_row_extra(a0, tn)], out_dtype=BF16, name="rwkv_a_lb")
    g = mm(mm(xs, g_la, a_part=(5, 6), epi=_sigmoid_epi, out_dtype=BF16, name="rwkv_g_la"), g_lb, out_dtype=BF16,
           name="rwkv_g_lb")
    sh = lambda z: z.reshape(b, t, d)
    z = rwkv_scan(sh(r), sh(k), sh(v), sh(a), sh(lw), sh(g), k_k, k_a, r_k, ln_g, ln_b)
    return z.reshape(b * t, d), [(w_out, 0, 0)], None


def _nsa_compress_body(kv_ref, pe_ref, w1_ref, w2_ref, o_ref, buf_ref, *, nchunk, dh):
    st = NSA_CMP_STRIDE
    hid = w1_ref.shape[1]
    buf_ref[...] = kv_ref[0].astype(F32)
    first = jnp.zeros((nchunk, hid), F32)
    second = jnp.zeros((nchunk, hid), F32)
    for j in range(st):
        xj = buf_ref[pl.ds(j, nchunk, stride=st), :]
        first = first + _dot((xj + pe_ref[j:j + 1, :]).astype(BF16), w1_ref[j * dh:(j + 1) * dh, :])
        second = second + _dot((xj + pe_ref[st + j:st + j + 1, :]).astype(BF16),
                               w1_ref[(st + j) * dh:(st + j + 1) * dh, :])
    hidden = jax.nn.gelu(first + pltpu.roll(second, nchunk - 1, axis=0))
    out = _dot(hidden.astype(BF16), w2_ref[...])
    row = lax.broadcasted_iota(jnp.int32, out.shape, 0)
    o_ref[0, 0] = jnp.where(row < nchunk - 1, out, 0.0)


def nsa_compress(proj, col0, pe, w1, w2):
    b, t, _ = proj.shape
    g, dh = NSA_KV_GROUPS, NSA_HEAD_DIM
    nchunk = t // NSA_CMP_STRIDE
    hid = w1.shape[1]
    return pl.pallas_call(
        functools.partial(_nsa_compress_body, nchunk=nchunk, dh=dh),
        grid=(b, g),
        in_specs=[
            pl.BlockSpec((1, t, dh), lambda bi, gi: (bi, 0, col0 // dh + gi)),
            pl.BlockSpec((NSA_CMP_BLOCK, dh), lambda bi, gi: (0, 0)),
            pl.BlockSpec((NSA_CMP_BLOCK * dh, hid), lambda bi, gi: (0, 0)),
            pl.BlockSpec((hid, dh), lambda bi, gi: (0, 0)),
        ],
        out_specs=pl.BlockSpec((1, 1, nchunk, dh), lambda bi, gi: (bi, gi, 0, 0)),
        out_shape=jax.ShapeDtypeStruct((b, g, nchunk, dh), F32),
        scratch_shapes=[pltpu.VMEM((t, dh), F32)],
        compiler_params=_cparams("parallel", "parallel"),
        name="nsa_compress",
    )(proj, pe, w1.astype(BF16), w2.astype(BF16))


def _stack_heads(q, r, dh):
    return jnp.concatenate([q[:, i * dh:(i + 1) * dh] for i in range(r)], axis=0)


def _unstack_heads(o, r, tq):
    return jnp.concatenate([o[i * tq:(i + 1) * tq] for i in range(r)], axis=-1)


def _gate_rows(gates, branch, r, tq):
    return jnp.concatenate([gates[:, branch * r + i:branch * r + i + 1] for i in range(r)], axis=0)


def _nsa_cmp_body(q_ref, kc_ref, vc_ref, gates_ref, selmap_t_ref, o_ref, sel_ref, *, tq, r, dh, n_sel):
    q0 = pl.program_id(2) * tq
    qs = (_stack_heads(q_ref[0].astype(F32), r, dh) * (dh ** -0.5)).astype(BF16)
    kc, vc = kc_ref[0, 0].astype(BF16), vc_ref[0, 0].astype(BF16)
    ncp = kc.shape[0]
    s = _dot_nt(qs, kc)
    rows = lax.broadcasted_iota(jnp.int32, (r * tq, ncp), 0)
    cmp_end = lax.broadcasted_iota(jnp.int32, (r * tq, ncp), 1) * NSA_CMP_STRIDE + (NSA_CMP_BLOCK - 1)
    t_pos = q0 + rows % tq
    vis = cmp_end <= t_pos
    s = jnp.where(vis, s, NEG_INF)
    e = jnp.exp(s - jnp.max(s, axis=-1, keepdims=True))
    p = e / jnp.sum(e, axis=-1, keepdims=True) * vis.astype(F32)
    o = _dot(p.astype(BF16), vc)
    o = o * _gate_rows(gates_ref[0, 0], 0, r, tq)
    o_ref[0] = _unstack_heads(o, r, tq)
    p_sum = p[0:tq]
    for i in range(1, r):
        p_sum = p_sum + p[i * tq:(i + 1) * tq]
    hi = p_sum.astype(BF16)
    lo = (p_sum - hi.astype(F32)).astype(BF16)
    selmap_t = selmap_t_ref[...]
    imp = _dot_nt(selmap_t, hi) + _dot_nt(selmap_t, lo)
    ns = imp.shape[0]
    blk = lax.broadcasted_iota(jnp.int32, (ns, tq), 0)
    tq_pos = q0 + lax.broadcasted_iota(jnp.int32, (ns, tq), 1)
    cur = tq_pos // NSA_SEL_BLOCK
    valid = blk * NSA_SEL_BLOCK <= tq_pos
    forced = (blk == 0) | (blk == cur) | (blk == cur - 1)
    score = jnp.where(valid, jnp.where(forced, POS_BIG, imp), NEG_INF)
    rank = jnp.zeros((ns, tq), F32)
    for m in range(ns):
        sm = score[m:m + 1, :]
        beats = (sm > score) | ((sm == score) & (blk > m))
        rank = rank + beats.astype(F32)
    chosen = ((rank < n_sel) & valid).astype(F32)
    sel_ref[0, 0] = (chosen.T - 1.0).astype(sel_ref.dtype)


def _flash_update(s, v, m_ref, l_ref, acc_ref):
    lanes = m_ref.shape[1]
    assert s.shape[1] % lanes == 0 and acc_ref.shape[1] == lanes
    m_old = m_ref[...]
    m_new = jnp.maximum(m_old, jnp.max(s, axis=-1, keepdims=True))
    alpha = jnp.exp2(m_old - m_new)
    p = jnp.exp2(s - jnp.concatenate([m_new] * (s.shape[1] // lanes), axis=1))
    l_ref[...] = alpha * l_ref[...] + jnp.sum(p, axis=-1, keepdims=True)
    acc_ref[...] = alpha * acc_ref[...] + _dot(p.astype(BF16), v)
    m_ref[...] = m_new


def _nsa_sel_body(qi_tbl, ki_tbl, diag_tbl, q_ref, k_ref, v_ref, gates_ref, oin_ref, sel_ref, o_ref, qa_ref, m_ref,
                  l_ref, acc_ref, *, tq, tk, r, dh):
    step = pl.program_id(2)
    qi, ki, diag = qi_tbl[step], ki_tbl[step], diag_tbl[step]
    ns = sel_ref.shape[-1]
    assert ns <= dh

    @pl.when(ki == 0)
    def _():
        m_ref[...] = jnp.full_like(m_ref, NEG_INF)
        l_ref[...] = jnp.zeros_like(l_ref)
        acc_ref[...] = jnp.zeros_like(acc_ref)
        qs = _stack_heads(q_ref[0].astype(F32), r, dh) * (dh ** -0.5 * LOG2_E)
        pieces = [qs, jnp.concatenate([sel_ref[0, 0].astype(F32)] * r, axis=0)]
        if ns < dh:
            pieces.append(jnp.zeros((r * tq, dh - ns), F32))
        qa_ref[...] = jnp.concatenate(pieces, axis=1).astype(BF16)

    blk_of_key = (ki * tk + lax.broadcasted_iota(jnp.int32, (tk, dh), 0)) // NSA_SEL_BLOCK
    own_block = jnp.where(blk_of_key == lax.broadcasted_iota(jnp.int32, (tk, dh), 1), POS_BIG, 0.0).astype(BF16)
    ka = jnp.concatenate([k_ref[0].astype(BF16), own_block], axis=1)
    s = _dot_nt(qa_ref[...], ka)
    v = v_ref[0].astype(BF16)

    @pl.when(diag == 0)
    def _():
        _flash_update(s, v, m_ref, l_ref, acc_ref)

    @pl.when(diag == 1)
    def _():
        t_pos = qi * tq + lax.broadcasted_iota(jnp.int32, (r * tq, tk), 0) % tq
        k_pos = ki * tk + lax.broadcasted_iota(jnp.int32, (r * tq, tk), 1)
        _flash_update(jnp.where(k_pos <= t_pos, s, NEG_INF), v, m_ref, l_ref, acc_ref)
        o = acc_ref[...] / l_ref[...] * _gate_rows(gates_ref[0, 0], 1, r, tq)
        o_ref[0] = (oin_ref[0] + _unstack_heads(o, r, tq)).astype(o_ref.dtype)


def _nsa_win_body(q_ref, *rest, tq, r, dh, nwin):
    k_refs, v_refs = rest[:nwin], rest[nwin:2 * nwin]
    gates_ref, oin_ref, o_ref = rest[2 * nwin:]
    qi = pl.program_id(2)
    qs = (_stack_heads(q_ref[0].astype(F32), r, dh) * (dh ** -0.5 * LOG2_E)).astype(BF16)
    t_loc = lax.broadcasted_iota(jnp.int32, (r * tq, tq), 0) % tq
    k_loc = lax.broadcasted_iota(jnp.int32, (r * tq, tq), 1)
    tiles = []
    for j in range(nwin):
        s = _dot_nt(qs, k_refs[j][0].astype(BF16))
        if j == nwin - 1:
            s = jnp.where(k_loc <= t_loc, s, NEG_INF)
        else:
            if j == 0:
                s = jnp.where(k_loc > t_loc, s, NEG_INF)
            s = jnp.where(qi - (nwin - 1) + j >= 0, s, NEG_INF)
        tiles.append(s)
    s = jnp.concatenate(tiles, axis=1)
    p = jnp.exp2(s - jnp.max(s, axis=-1, keepdims=True))
    v = jnp.concatenate([v_refs[j][0] for j in range(nwin)], axis=0).astype(BF16)
    o = _dot(p.astype(BF16), v) / jnp.sum(p, axis=-1, keepdims=True)
    o = o * _gate_rows(gates_ref[0, 0], 2, r, tq)
    o_ref[0] = (oin_ref[0] + _unstack_heads(o, r, tq)).astype(o_ref.dtype)


def nsa_mix(h, w_in, pe_k, pe_v, ck_w1, ck_w2, cv_w1, cv_w2, w_out, tq=256):
    b, t, d = h.shape
    nh, g, dh = NSA_HEADS, NSA_KV_GROUPS, NSA_HEAD_DIM
    r = nh // g
    hd, kd = nh * dh, g * dh
    n_main = hd + 6 * kd
    h2 = h.reshape(b * t, d)
    proj = mm(h2, w_in[:, :n_main], out_dtype=BF16, name="nsa_in").reshape(b, t, n_main)
    n_gate = w_in.shape[1] - n_main
    w_gate = jnp.pad(w_in[:, n_main:], ((0, 0), (0, 128 - n_gate)))
    gates = mm(h2, w_gate, epi=_sigmoid_epi, name="nsa_gates")[:, :n_gate]
    gates = jnp.transpose(gates.reshape(b, t, 3, g, r), (0, 3, 1, 2, 4)).reshape(b, g, t, 3 * r)
    k_cmp = nsa_compress(proj, hd, pe_k, ck_w1, ck_w2)
    v_cmp = nsa_compress(proj, hd + kd, pe_v, cv_w1, cv_w2)
    ncp = k_cmp.shape[2]
    ns = t // NSA_SEL_BLOCK
    n_sel = min(NSA_SEL_TOP, ns)
    cs = jnp.arange(ncp)[None, :] * NSA_CMP_STRIDE
    ss = jnp.arange(ns)[:, None] * NSA_SEL_BLOCK
    sel_map_t = (jnp.maximum(jnp.minimum(cs + NSA_CMP_BLOCK, ss + NSA_SEL_BLOCK) - jnp.maximum(cs, ss), 0)
                 .astype(F32) / NSA_CMP_BLOCK).astype(BF16)
    tq = _pick(t, tq)
    assert NSA_WINDOW % tq == 0
    nq = t // tq
    wq = r * dh
    col = lambda base: base // dh
    q_spec3 = pl.BlockSpec((1, tq, wq), lambda bi, gi, qi: (bi, qi, gi))
    gate_spec3 = pl.BlockSpec((1, 1, tq, 3 * r), lambda bi, gi, qi: (bi, gi, qi, 0))
    o_cmp, sel = pl.pallas_call(
        functools.partial(_nsa_cmp_body, tq=tq, r=r, dh=dh, n_sel=n_sel),
        grid=(b, g, nq),
        in_specs=[
            q_spec3,
            pl.BlockSpec((1, 1, ncp, dh), lambda bi, gi, qi: (bi, gi, 0, 0)),
            pl.BlockSpec((1, 1, ncp, dh), lambda bi, gi, qi: (bi, gi, 0, 0)),
            gate_spec3,
            pl.BlockSpec((ns, ncp), lambda bi, gi, qi: (0, 0)),
        ],
        out_specs=[q_spec3, pl.BlockSpec((1, 1, tq, ns), lambda bi, gi, qi: (bi, gi, qi, 0))],
        out_shape=[jax.ShapeDtypeStruct((b, t, hd), F32), jax.ShapeDtypeStruct((b, g, t, ns), BF16)],
        compiler_params=_cparams("parallel", "parallel", "parallel"),
        name="nsa_cmp",
    )(proj, k_cmp, v_cmp, gates, sel_map_t)

    tk = _pick(t, NSA_SEL_KEY_TILE)
    assert tk % tq == 0
    pairs = [(qi, ki) for qi in range(nq) for ki in range((qi * tq + tq - 1) // tk + 1)]
    qi_tbl = jnp.asarray([pq for pq, _ in pairs], jnp.int32)
    ki_tbl = jnp.asarray([pk for _, pk in pairs], jnp.int32)
    diag_tbl = jnp.asarray([int(pk == (pq * tq + tq - 1) // tk) for pq, pk in pairs], jnp.int32)
    q_spec_p = pl.BlockSpec((1, tq, wq), lambda bi, gi, p, qt, kt, dt: (bi, qt[p], gi))

    def kv_spec_p(base):
        return pl.BlockSpec((1, tk, dh), lambda bi, gi, p, qt, kt, dt: (bi, kt[p], col(base) + gi))

    o_sel = pl.pallas_call(
        functools.partial(_nsa_sel_body, tq=tq, tk=tk, r=r, dh=dh),
        grid_spec=pltpu.PrefetchScalarGridSpec(
            num_scalar_prefetch=3,
            grid=(b, g, len(pairs)),
            in_specs=[q_spec_p, kv_spec_p(hd + 2 * kd), kv_spec_p(hd + 3 * kd),
                      pl.BlockSpec((1, 1, tq, 3 * r), lambda bi, gi, p, qt, kt, dt: (bi, gi, qt[p], 0)),
                      q_spec_p,
                      pl.BlockSpec((1, 1, tq, ns), lambda bi, gi, p, qt, kt, dt: (bi, gi, qt[p], 0))],
            out_specs=q_spec_p,
            scratch_shapes=[pltpu.VMEM((r * tq, 2 * dh), BF16), pltpu.VMEM((r * tq, dh), F32),
                            pltpu.VMEM((r * tq, dh), F32), pltpu.VMEM((r * tq, dh), F32)],
        ),
        out_shape=jax.ShapeDtypeStruct((b, t, hd), F32),
        compiler_params=_cparams("parallel", "parallel", "arbitrary"),
        name="nsa_sel",
    )(qi_tbl, ki_tbl, diag_tbl, proj, proj, proj, gates, o_cmp, sel)

    nwin = NSA_WINDOW // tq + 1

    def kv_spec_w(base, j):
        return pl.BlockSpec((1, tq, dh),
                            lambda bi, gi, qi: (bi, jnp.maximum(qi - (nwin - 1) + j, 0), col(base) + gi))

    o_all = pl.pallas_call(
        functools.partial(_nsa_win_body, tq=tq, r=r, dh=dh, nwin=nwin),
        grid=(b, g, nq),
        in_specs=([q_spec3] + [kv_spec_w(hd + 4 * kd, j) for j in range(nwin)]
                  + [kv_spec_w(hd + 5 * kd, j) for j in range(nwin)] + [gate_spec3, q_spec3]),
        out_specs=q_spec3,
        out_shape=jax.ShapeDtypeStruct((b, t, hd), BF16),
        compiler_params=_cparams("parallel", "parallel", "parallel"),
        name="nsa_win",
    )(proj, *([proj] * (2 * nwin)), gates, o_sel)
    return o_all.reshape(b * t, hd), [(w_out, 0, 0)], None


def _silu(a):
    return a * jax.nn.sigmoid(a)


def _ada_epi(parts, extras):
    return parts[0] + extras[0]


def ada_modulation(c, ada_w, ada_b):
    depth, d, n6 = ada_w.shape
    b = c.shape[0]
    rows = ((b + 7) // 8) * 8
    cond = jnp.pad(c, ((0, rows - b), (0, 0)))
    w2d = ada_w.reshape(depth * d, n6)
    mods = []
    for i in range(depth):
        mods.append(mm(cond, [(w2d, 0, i)], a_act=_silu, epi=_ada_epi, extras=[_row_extra(ada_b[i], _pick(n6, 1024))],
                       precision=HIGHEST, name="ada_mod")[:b])
    return jnp.stack(mods)


def kernel(x, c, ada_w, ada_b, norm_g, mlp_w1, mlp_w2, rwkv_mu, rwkv_w_rkv, rwkv_w0, rwkv_w_la, rwkv_w_lb, rwkv_a0, rwkv_a_la, rwkv_a_lb, rwkv_g_la, rwkv_g_lb, rwkv_k_k, rwkv_k_a, rwkv_r_k, rwkv_ln_g, rwkv_ln_b, rwkv_w_out, ret_w_in, ret_gn_g, ret_gn_b, ret_w_out, conv_pw1_w, conv_pw1_b, conv_dw_w, conv_dw_b, conv_ln_g, conv_ln_b, conv_pw2_w, conv_pw2_b, nsa_w_in, nsa_pe_k, nsa_pe_v, nsa_ck_w1, nsa_ck_w2, nsa_cv_w1, nsa_cv_w2, nsa_w_out):
    b, t, d = x.shape
    depth = ada_w.shape[0]
    mod = ada_modulation(c, ada_w, ada_b).reshape(depth, b, 6, d)
    h = None
    for i in range(depth):
        sh_t, sc_t, gt_t, sh_c, sc_c, gt_c = (mod[i, :, j] for j in range(6))
        kind = i % 4
        if kind == 0:
            act, ws, bias = rwkv7_time_mix(x, norm_g[i, 0], sc_t, sh_t, rwkv_mu, rwkv_w_rkv, rwkv_w0, rwkv_w_la,
                                           rwkv_w_lb, rwkv_a0, rwkv_a_la, rwkv_a_lb, rwkv_g_la, rwkv_g_lb, rwkv_k_k,
                                           rwkv_k_a, rwkv_r_k.reshape(-1), rwkv_ln_g, rwkv_ln_b, rwkv_w_out)
        else:
            if h is None:
                (h,) = resid_norm(x, pre=(norm_g[i, 0], sc_t, sh_t))
            if kind == 1:
                act, ws, bias = retention_mix(h, ret_w_in, ret_gn_g, ret_gn_b, ret_w_out)
            elif kind == 2:
                act, ws, bias = conformer_conv_mix(h, conv_pw1_w, conv_pw1_b, conv_dw_w, conv_dw_b, conv_ln_g,
                                                   conv_ln_b, conv_pw2_w, conv_pw2_b)
            else:
                act, ws, bias = nsa_mix(h, nsa_w_in, nsa_pe_k, nsa_pe_v, nsa_ck_w1, nsa_ck_w2, nsa_cv_w1, nsa_cv_w2,
                                        nsa_w_out)
        y = project(act, ws, bias, "mixer_out").reshape(b, t, d)
        x, h = resid_norm(x, y, post=(norm_g[i, 1], gt_t), pre=(norm_g[i, 2], sc_c, sh_c))
        y = project(*sqrelu_mlp(h.reshape(b * t, d), mlp_w1, mlp_w2, i), "mlp_down").reshape(b, t, d)
        nxt = i + 1
        if nxt < depth and nxt % 4 != 0:
            x, h = resid_norm(x, y, post=(norm_g[i, 3], gt_c),
                              pre=(norm_g[nxt, 0], mod[nxt, :, 1], mod[nxt, :, 0]))
        else:
            (x,) = resid_norm(x, y, post=(norm_g[i, 3], gt_c))
            h = None
    return x
```

```python
import functools
import math

import jax
import jax.numpy as jnp
from jax import lax
from jax.experimental import pallas as pl
from jax.experimental.pallas import tpu as pltpu

F32 = jnp.float32
BF16 = jnp.bfloat16
HIGHEST = lax.Precision.HIGHEST

NORM_EPS = 1e-6
NEG_INF = -1e30
POS_BIG = 1e30
LOG2_E = math.log2(math.e)

RWKV_HEAD_DIM = 64
RWKV_GN_EPS = 64e-5
RWKV_CHUNK = 64
RWKV_HEADS_PER_STEP = 32

RET_HEADS = 8
RET_CHUNK = 256
RET_GN_EPS = 1e-5
ROPE_BASE = 10000.0

CONV_WIDTH = 31
CONV_HALO = 32
CONV_LN_EPS = 1e-5

NSA_HEADS = 16
NSA_KV_GROUPS = 4
NSA_HEAD_DIM = 128
NSA_CMP_BLOCK = 32
NSA_CMP_STRIDE = 16
NSA_SEL_BLOCK = 64
NSA_SEL_TOP = 16
NSA_WINDOW = 512
NSA_SEL_KEY_TILE = 512

VMEM_LIMIT_BYTES = 56 * 1024 * 1024


def _cparams(*sem):
    return pltpu.CompilerParams(dimension_semantics=sem, vmem_limit_bytes=VMEM_LIMIT_BYTES)


def _dot(a, b, precision=None):
    return jnp.dot(a, b, preferred_element_type=F32, precision=precision)


def _dot_nt(a, b, precision=None):
    return lax.dot_general(a, b, (((1,), (1,)), ((), ())), preferred_element_type=F32, precision=precision)


def _pick(n, pref):
    if n <= pref:
        return n
    t = pref
    while n % t:
        t //= 2
    return t


def _mm_body(*refs, n_w, n_ex, n_out, nk, epi, precision, has_a_add, a_act):
    a_ref = refs[0]
    pos = 1
    a_add_ref = None
    if has_a_add:
        a_add_ref = refs[pos]
        pos += 1
    w_refs = refs[pos:pos + n_w]
    pos += n_w
    ex_refs = refs[pos:pos + n_ex]
    pos += n_ex
    o_refs = refs[pos:pos + n_out]
    acc_refs = refs[pos + n_out:]

    def finish(parts):
        res = epi(parts, [e[...] for e in ex_refs])
        if n_out == 1 and not isinstance(res, (tuple, list)):
            res = (res,)
        for o_ref, val in zip(o_refs, res, strict=True):
            o_ref[...] = val.astype(o_ref.dtype)

    a = a_ref[...]
    if a_add_ref is not None:
        a = a.astype(F32) + a_add_ref[...]
    if a_act is not None:
        a = a_act(a)
    if precision is None:
        a = a.astype(BF16)
    parts = []
    for w_ref in w_refs:
        w = w_ref[...]
        if precision is None:
            w = w.astype(BF16)
        parts.append(_dot(a, w, precision))

    if nk == 1:
        finish(parts)
    else:
        k = pl.program_id(2)

        @pl.when(k == 0)
        def _():
            for acc, p in zip(acc_refs, parts):
                acc[...] = p

        @pl.when(k > 0)
        def _():
            for acc, p in zip(acc_refs, parts):
                acc[...] += p

        @pl.when(k == nk - 1)
        def _():
            finish([acc[...] for acc in acc_refs])


def _first(parts, extras):
    return parts[0]


def mm(a, ws, *, n_out=None, extras=(), epi=_first, out_dtype=F32, tm=2048, tn=1024, tk=2048, precision=None,
       a_add=None, a_act=None, a_part=(0, 1), cast_in_kernel=False, name="mm"):
    m, kdim = a.shape
    a_s, a_parts = a_part
    assert m % a_parts == 0
    m //= a_parts
    if not isinstance(ws, (list, tuple)):
        ws = [(ws, 0, 0)]
    ws = [tuple(w) + (0,) * (3 - len(w)) for w in ws]
    if n_out is None:
        n_out = ws[0][0].shape[1]
    tm = _pick(m, tm)
    tn = _pick(n_out, tn)
    tk = _pick(kdim, tk)
    nk = kdim // tk
    assert m % tm == 0 and n_out % tn == 0 and kdim % tk == 0
    in_specs = [pl.BlockSpec((tm, tk), lambda i, j, k, o=a_s * (m // tm): (i + o, k))]
    args = [a]
    if a_add is not None:
        in_specs.append(pl.BlockSpec((1, tk), lambda i, j, k: (0, k)))
        args.append(a_add)
    for w, off, koff in ws:
        in_specs.append(pl.BlockSpec((tk, tn), lambda i, j, k, off=off, ko=koff * nk: (k + ko, j + off)))
        args.append(w.astype(BF16) if precision is None and not cast_in_kernel else w)
    for arr, bshape, imap in extras:
        in_specs.append(pl.BlockSpec(bshape, imap))
        args.append(arr)
    scratch = [pltpu.VMEM((tm, tn), F32) for _ in ws] if nk > 1 else []
    multi = isinstance(out_dtype, (tuple, list))
    out_dtypes = tuple(out_dtype) if multi else (out_dtype,)
    body = functools.partial(_mm_body, n_w=len(ws), n_ex=len(extras), n_out=len(out_dtypes), nk=nk, epi=epi,
                             precision=precision, has_a_add=a_add is not None, a_act=a_act)
    outs = pl.pallas_call(
        body,
        grid=(m // tm, n_out // tn, nk),
        in_specs=in_specs,
        out_specs=[pl.BlockSpec((tm, tn), lambda i, j, k: (i, j)) for _ in out_dtypes],
        out_shape=[jax.ShapeDtypeStruct((m, n_out), dt) for dt in out_dtypes],
        scratch_shapes=scratch,
        compiler_params=_cparams("parallel", "parallel", "arbitrary"),
        name=name,
    )(*args)
    return tuple(outs) if multi else outs[0]


def _row_extra(vec, tn):
    return (vec.reshape(1, -1), (1, tn), lambda i, j, k: (0, j))


def _rms(x, g):
    return x * lax.rsqrt(jnp.mean(x * x, axis=-1, keepdims=True) + NORM_EPS) * g


def _resid_norm_body(*refs, has_y, has_h):
    pos = 0
    x_ref = refs[pos]; pos += 1
    if has_y:
        y_ref, gpost_ref, gate_ref = refs[pos:pos + 3]; pos += 3
    if has_h:
        gpre_ref, scale_ref, shift_ref = refs[pos:pos + 3]; pos += 3
    outs = refs[pos:]
    x = x_ref[0]
    o = 0
    if has_y:
        y = y_ref[0].astype(F32)
        x = x + (1.0 + gate_ref[0]) * _rms(y, gpost_ref[...])
        outs[o][0] = x
        o += 1
    if has_h:
        h = _rms(x, gpre_ref[...]) * (1.0 + scale_ref[0]) + shift_ref[0]
        outs[o][0] = h.astype(outs[o].dtype)


def resid_norm(x, y=None, post=None, pre=None, tt=512):
    b, t, d = x.shape
    tt = _pick(t, tt)
    row = pl.BlockSpec((1, tt, d), lambda bi, ti: (bi, ti, 0))
    vec = pl.BlockSpec((1, d), lambda bi, ti: (0, 0))
    bvec = pl.BlockSpec((1, 1, d), lambda bi, ti: (bi, 0, 0))
    args, in_specs, out_shapes, out_specs = [x], [row], [], []
    if y is not None:
        args += [y, post[0].reshape(1, d), post[1].reshape(b, 1, d)]
        in_specs += [row, vec, bvec]
        out_shapes.append(jax.ShapeDtypeStruct((b, t, d), F32))
        out_specs.append(row)
    if pre is not None:
        args += [pre[0].reshape(1, d), pre[1].reshape(b, 1, d), pre[2].reshape(b, 1, d)]
        in_specs += [vec, bvec, bvec]
        out_shapes.append(jax.ShapeDtypeStruct((b, t, d), BF16))
        out_specs.append(row)
    res = pl.pallas_call(
        functools.partial(_resid_norm_body, has_y=y is not None, has_h=pre is not None),
        grid=(b, t // tt),
        in_specs=in_specs,
        out_specs=out_specs,
        out_shape=out_shapes,
        compiler_params=_cparams("parallel", "parallel"),
        name="resid_norm",
    )(*args)
    return tuple(res)


def _add_bias(parts, extras):
    return parts[0] + extras[0]


def project(act, ws, bias, name):
    if bias is None:
        return mm(act, ws, out_dtype=BF16, name=name)
    tn = _pick(ws[0][0].shape[1], 1024)
    return mm(act, ws, epi=_add_bias, extras=[_row_extra(bias, tn)], tn=tn, out_dtype=BF16, name=name)


def _relu2(parts, extras):
    r = jnp.maximum(parts[0], 0.0)
    return r * r


def sqrelu_mlp(h2d, w1, w2, layer):
    nl, d, f = w1.shape
    a = mm(h2d, [(w1.reshape(nl * d, f), 0, layer)], epi=_relu2, out_dtype=BF16, tm=2048, tn=512,
           cast_in_kernel=True, name="mlp_up")
    return a, [(w2.reshape(nl * f, d), 0, layer)], None


def _glu(parts, extras):
    return (parts[0] + extras[0]) * jax.nn.sigmoid(parts[1] + extras[1])


def _conv_ln_body(cur_ref, prev_ref, w_ref, b_ref, g_ref, beta_ref, o_ref, buf_ref, acc_ref, *, tt, d):
    ti = pl.program_id(1)
    halo = prev_ref[0].astype(F32)
    buf_ref[0:CONV_HALO, :] = jnp.where(ti == 0, jnp.zeros_like(halo), halo)
    buf_ref[CONV_HALO:, :] = cur_ref[0].astype(F32)
    lane_chunk = min(d, 256)
    row_chunk = min(tt, 64)
    base = CONV_HALO - (CONV_WIDTH - 1)
    sub = 8
    for c0 in range(0, d, lane_chunk):
        for r0 in range(0, tt, row_chunk):
            acc = None
            for s in range(sub):
                taps = [j for j in range(CONV_WIDTH) if (base + j) % sub == s]
                if not taps:
                    continue
                rows = row_chunk + (sub if s else 0)
                part = jnp.zeros((rows, lane_chunk), F32)
                for j in taps:
                    off = r0 + base + j - s
                    part = part + buf_ref[off:off + rows, c0:c0 + lane_chunk] * w_ref[j:j + 1, c0:c0 + lane_chunk]
                part = part[s:s + row_chunk]
                acc = part if acc is None else acc + part
            acc_ref[r0:r0 + row_chunk, c0:c0 + lane_chunk] = acc
    y = acc_ref[...] + b_ref[...]
    mu = jnp.mean(y, axis=-1, keepdims=True)
    yc = y - mu
    var = jnp.mean(yc * yc, axis=-1, keepdims=True)
    z = yc * lax.rsqrt(var + CONV_LN_EPS) * g_ref[...] + beta_ref[...]
    o_ref[0] = (z * jax.nn.sigmoid(z)).astype(o_ref.dtype)


def conv_ln_silu(u, dw_w, dw_b, ln_g, ln_b, tt=128):
    b, t, d = u.shape
    tt = _pick(t, tt)
    hb = tt // CONV_HALO
    vec = pl.BlockSpec((1, d), lambda bi, ti: (0, 0))
    return pl.pallas_call(
        functools.partial(_conv_ln_body, tt=tt, d=d),
        grid=(b, t // tt),
        in_specs=[
            pl.BlockSpec((1, tt, d), lambda bi, ti: (bi, ti, 0)),
            pl.BlockSpec((1, CONV_HALO, d), lambda bi, ti: (bi, jnp.maximum(ti * hb - 1, 0), 0)),
            pl.BlockSpec((CONV_WIDTH, d), lambda bi, ti: (0, 0)),
            vec, vec, vec,
        ],
        out_specs=pl.BlockSpec((1, tt, d), lambda bi, ti: (bi, ti, 0)),
        out_shape=jax.ShapeDtypeStruct((b, t, d), BF16),
        scratch_shapes=[pltpu.VMEM((tt + CONV_HALO, d), F32), pltpu.VMEM((tt, d), F32)],
        compiler_params=_cparams("parallel", "parallel"),
        name="conv_ln_silu",
    )(u, u, dw_w, dw_b.reshape(1, d), ln_g.reshape(1, d), ln_b.reshape(1, d))


def conformer_conv_mix(h, pw1_w, pw1_b, dw_w, dw_b, ln_g, ln_b, pw2_w, pw2_b):
    b, t, d = h.shape
    tn = _pick(d, 1024)
    b1 = pw1_b.reshape(1, -1)
    u = mm(h.reshape(b * t, d), [(pw1_w, 0), (pw1_w, d // tn)], n_out=d, tm=1024, tn=tn, epi=_glu,
           extras=[(b1, (1, tn), lambda i, j, k: (0, j)),
                   (b1, (1, tn), lambda i, j, k, o=d // tn: (0, j + o))], out_dtype=BF16, name="conf_pw1_glu")
    z = conv_ln_silu(u.reshape(b, t, d), dw_w, dw_b, ln_g, ln_b)
    return z.reshape(b * t, d), [(pw2_w, 0, 0)], pw2_b


def _rope(x, cos, sin):
    half = x.shape[-1] // 2
    x1, x2 = x[:, :half], x[:, half:]
    return jnp.concatenate([x1 * cos - x2 * sin, x2 * cos + x1 * sin], axis=-1)


def _retention_body(q_ref, k_ref, v_ref, gate_ref, cos_ref, sin_ref, inner_ref, qdec_ref, kdec_ref, cdec_ref,
                    gng_ref, gnb_ref, o_ref, state_ref, *, dk):
    @pl.when(pl.program_id(2) == 0)
    def _():
        state_ref[...] = jnp.zeros_like(state_ref)

    cos, sin = cos_ref[...], sin_ref[...]
    q = _rope(q_ref[0].astype(F32), cos, sin)
    k = _rope(k_ref[0].astype(F32), cos, sin) * (dk ** -0.5)
    v = v_ref[0].astype(BF16)
    qb = q.astype(BF16)
    s = _dot_nt(qb, k.astype(BF16)) * inner_ref[0]
    state = state_ref[...]
    o = _dot(s.astype(BF16), v) + _dot(qb, state.astype(BF16)) * qdec_ref[0]
    kd_t = (k * kdec_ref[0]).T.astype(BF16)
    state_ref[...] = state * cdec_ref[0] + _dot(kd_t, v)
    mu = jnp.mean(o, axis=-1, keepdims=True)
    oc = o - mu
    var = jnp.mean(oc * oc, axis=-1, keepdims=True)
    on = oc * lax.rsqrt(var + RET_GN_EPS) * gng_ref[...] + gnb_ref[...]
    gate = gate_ref[0].astype(F32)
    o_ref[0] = (gate * jax.nn.sigmoid(gate) * on).astype(o_ref.dtype)


def retention_mix(h, w_in, gn_g, gn_b, w_out):
    b, t, d = h.shape
    nh, c = RET_HEADS, _pick(t, RET_CHUNK)
    dk = d // nh
    dv = 2 * dk
    proj = mm(h.reshape(b * t, d), w_in, out_dtype=BF16, tm=2048, tn=512, cast_in_kernel=True,
              name="ret_in").reshape(b, t, 6 * d)
    pos = jnp.arange(t, dtype=F32)
    inv_freq = ROPE_BASE ** (-jnp.arange(0, dk, 2, dtype=F32) / dk)
    ang = pos[:, None] * inv_freq[None, :]
    cos, sin = jnp.cos(ang), jnp.sin(ang)
    log_gamma = jnp.log(1.0 - 2.0 ** (-5.0 - jnp.arange(nh, dtype=F32)))
    idx = jnp.arange(c, dtype=F32)
    diff = idx[:, None] - idx[None, :]
    inner = jnp.where(diff >= 0, jnp.exp(jnp.maximum(diff, 0.0)[None] * log_gamma[:, None, None]), 0.0)
    q_dec = jnp.exp((idx + 1.0)[None] * log_gamma[:, None])[:, :, None]
    k_dec = jnp.exp((c - 1.0 - idx)[None] * log_gamma[:, None])[:, :, None]
    c_dec = jnp.exp(c * log_gamma)[:, None, None]
    nq = (nh * dk) // dk
    o = pl.pallas_call(
        functools.partial(_retention_body, dk=dk),
        grid=(b, nh, t // c),
        in_specs=[
            pl.BlockSpec((1, c, dk), lambda bi, hi, ci: (bi, ci, hi)),
            pl.BlockSpec((1, c, dk), lambda bi, hi, ci: (bi, ci, nq + hi)),
            pl.BlockSpec((1, c, dv), lambda bi, hi, ci: (bi, ci, nq + hi)),
            pl.BlockSpec((1, c, dv), lambda bi, hi, ci: (bi, ci, 2 * nq + hi)),
            pl.BlockSpec((c, dk // 2), lambda bi, hi, ci: (ci, 0)),
            pl.BlockSpec((c, dk // 2), lambda bi, hi, ci: (ci, 0)),
            pl.BlockSpec((1, c, c), lambda bi, hi, ci: (hi, 0, 0)),
            pl.BlockSpec((1, c, 1), lambda bi, hi, ci: (hi, 0, 0)),
            pl.BlockSpec((1, c, 1), lambda bi, hi, ci: (hi, 0, 0)),
            pl.BlockSpec((1, 1, 1), lambda bi, hi, ci: (hi, 0, 0)),
            pl.BlockSpec((1, dv), lambda bi, hi, ci: (0, hi)),
            pl.BlockSpec((1, dv), lambda bi, hi, ci: (0, hi)),
        ],
        out_specs=pl.BlockSpec((1, c, dv), lambda bi, hi, ci: (bi, ci, hi)),
        out_shape=jax.ShapeDtypeStruct((b, t, nh * dv), BF16),
        scratch_shapes=[pltpu.VMEM((dk, dv), F32)],
        compiler_params=_cparams("parallel", "parallel", "arbitrary"),
        name="retention",
    )(proj, proj, proj, proj, cos, sin, inner, q_dec, k_dec, c_dec, gn_g.reshape(1, -1), gn_b.reshape(1, -1))
    return o.reshape(b * t, nh * dv), [(w_out, 0, 0)], None


def _rwkv_pre_body(x_ref, prev_ref, g_ref, scale_ref, shift_ref, mu_ref, o_ref):
    ti = pl.program_id(1)
    g, scale, shift = g_ref[...], scale_ref[0], shift_ref[0]
    h = _rms(x_ref[0], g) * (1.0 + scale) + shift
    hp = _rms(prev_ref[0], g) * (1.0 + scale) + shift
    last = jnp.where(ti == 0, 0.0, hp[7:8, :])
    row = lax.broadcasted_iota(jnp.int32, h.shape, 0)
    shifted = jnp.where(row == 0, last, pltpu.roll(h, 1, axis=0))
    xx = shifted - h
    for s in range(6):
        o_ref[s, 0] = (h + xx * mu_ref[s:s + 1, :]).astype(o_ref.dtype)


def rwkv_pre(x, g_pre, scale, shift, mu, tt=256):
    b, t, d = x.shape
    tt = _pick(t, tt)
    bvec = pl.BlockSpec((1, 1, d), lambda bi, ti: (bi, 0, 0))
    return pl.pallas_call(
        _rwkv_pre_body,
        grid=(b, t // tt),
        in_specs=[
            pl.BlockSpec((1, tt, d), lambda bi, ti: (bi, ti, 0)),
            pl.BlockSpec((1, 8, d), lambda bi, ti: (bi, jnp.maximum(ti * (tt // 8) - 1, 0), 0)),
            pl.BlockSpec((1, d), lambda bi, ti: (0, 0)),
            bvec, bvec,
            pl.BlockSpec((6, d), lambda bi, ti: (0, 0)),
        ],
        out_specs=pl.BlockSpec((6, 1, tt, d), lambda bi, ti: (0, bi, ti, 0)),
        out_shape=jax.ShapeDtypeStruct((6, b, t, d), BF16),
        compiler_params=_cparams("parallel", "parallel"),
        name="rwkv_pre",
    )(x, x, g_pre.reshape(1, d), scale.reshape(b, 1, d), shift.reshape(b, 1, d), mu)


def _rwkv_scan_body(r_ref, k_ref, v_ref, a_ref, lw_ref, g_ref, kk_ref, ka_ref, rk_ref, lng_ref, lnb_ref, o_ref,
                    state_ref, *, chunk, heads, n):
    @pl.when(pl.program_id(2) == 0)
    def _():
        state_ref[...] = jnp.zeros_like(state_ref)

    ln = chunk
    hs = range(heads)
    row = lax.broadcasted_iota(jnp.int32, (ln, ln), 0)
    col = lax.broadcasted_iota(jnp.int32, (ln, ln), 1)
    tri = (col <= row).astype(BF16)
    eye = (row == col).astype(F32)
    wid = heads * n
    seg_w = min(wid, 256)
    assert wid % seg_w == 0 and seg_w % n == 0
    seg = (lax.broadcasted_iota(jnp.int32, (seg_w, seg_w), 0) // n
           == lax.broadcasted_iota(jnp.int32, (seg_w, seg_w), 1) // n).astype(BF16)

    def split_dot(m, x):
        hi = x.astype(BF16)
        lo = (x - hi.astype(F32)).astype(BF16)
        return _dot(m, hi) + _dot(m, lo)

    def seg_sum(x):
        hi = x.astype(BF16)
        lo = (x - hi.astype(F32)).astype(BF16)
        return jnp.concatenate([_dot(hi[:, j:j + seg_w], seg) + _dot(lo[:, j:j + seg_w], seg)
                                for j in range(0, wid, seg_w)], axis=1)

    heads_of = lambda x: [x[:, hd * n:(hd + 1) * n] for hd in hs]

    r, k, v, a = (ref[0].astype(F32) for ref in (r_ref, k_ref, v_ref, a_ref))
    lw = lw_ref[0]
    kk = k * kk_ref[...]
    kk = kk / jnp.maximum(jnp.sqrt(seg_sum(kk * kk)), 1e-12)
    k2 = k * (1.0 + (a - 1.0) * ka_ref[...])
    beta = kk * a
    c = split_dot(tri, lw)
    c_last = c[ln - 1:ln, :]
    e_neg = jnp.exp(-c)
    e_rem = jnp.exp(c_last - c)
    e_last = heads_of(jnp.exp(c_last))
    a_t = heads_of(-kk * jnp.exp(c - lw))
    r_t = heads_of(r * jnp.exp(c))
    b_t = heads_of(beta * e_neg)
    k_t = heads_of(k2 * e_neg)
    b_rem = heads_of(beta * e_rem)
    k_rem = heads_of(k2 * e_rem)
    vs = heads_of(v)
    bonus = seg_sum(r * k2 * rk_ref[...]) * v

    ar = [jnp.concatenate([a_t[h], r_t[h]], axis=0).astype(BF16) for h in hs]
    bk = [jnp.concatenate([b_t[h], k_t[h]], axis=0).astype(BF16) for h in hs]
    bk_rem = [jnp.concatenate([b_rem[h], k_rem[h]], axis=0).astype(BF16) for h in hs]
    vb = [vs[h].astype(BF16) for h in hs]
    state = [state_ref[h] for h in hs]
    p = [_dot_nt(ar[h], bk[h]) for h in hs]
    q0 = [_dot_nt(ar[h], state[h].astype(BF16)) for h in hs]
    row2 = lax.broadcasted_iota(jnp.int32, (2 * ln, 2 * ln), 0)
    col2 = lax.broadcasted_iota(jnp.int32, (2 * ln, 2 * ln), 1)
    keep = col2 % ln < row2 % ln + row2 // ln
    pm = [jnp.where(keep, p[h], 0.0) for h in hs]
    m_ab = [pm[h][:ln, :ln] for h in hs]
    m_ak = [pm[h][:ln, ln:].astype(BF16) for h in hs]
    m_r = [pm[h][ln:].astype(BF16) for h in hs]
    rhs = [q0[h][:ln] + _dot(m_ak[h], vb[h]) for h in hs]
    inv = [eye + m_ab[h] for h in hs]
    pw = [m_ab[h].astype(BF16) for h in hs]
    for _ in range(int(math.log2(ln)) - 1):
        pw = [_dot(pw[h], pw[h]).astype(BF16) for h in hs]
        inv = [inv[h] + _dot(inv[h].astype(BF16), pw[h]) for h in hs]
    u = [_dot(inv[h].astype(BF16), rhs[h].astype(BF16)) for h in hs]
    uv = [jnp.concatenate([u[h], vs[h]], axis=0) for h in hs]
    y = [q0[h][ln:] + _dot(m_r[h], uv[h].astype(BF16)) for h in hs]
    for h in hs:
        state_ref[h] = state[h] * e_last[h] + _dot(uv[h].T.astype(BF16), bk_rem[h])
    yn = []
    for h in hs:
        mu = jnp.mean(y[h], axis=-1, keepdims=True)
        yc = y[h] - mu
        var = jnp.mean(yc * yc, axis=-1, keepdims=True)
        yn.append(yc * lax.rsqrt(var + RWKV_GN_EPS))
    yn = jnp.concatenate(yn, axis=-1) * lng_ref[...] + lnb_ref[...]
    o_ref[0] = ((yn + bonus) * g_ref[0]).astype(o_ref.dtype)


def rwkv_scan(r, k, v, a, lw, g, k_k, k_a, r_k, ln_g, ln_b):
    b, t, d = r.shape
    n = RWKV_HEAD_DIM
    hps = min(RWKV_HEADS_PER_STEP, d // n)
    w = hps * n
    ln = _pick(t, RWKV_CHUNK)
    tok = pl.BlockSpec((1, ln, w), lambda bi, hi, ci: (bi, ci, hi))
    vec = pl.BlockSpec((1, w), lambda bi, hi, ci: (0, hi))
    return pl.pallas_call(
        functools.partial(_rwkv_scan_body, chunk=ln, heads=hps, n=n),
        grid=(b, d // w, t // ln),
        in_specs=[tok] * 6 + [vec] * 5,
        out_specs=tok,
        out_shape=jax.ShapeDtypeStruct((b, t, d), BF16),
        scratch_shapes=[pltpu.VMEM((hps, n, n), F32)],
        compiler_params=_cparams("parallel", "parallel", "arbitrary"),
        name="rwkv_scan",
    )(r, k, v, a, lw, g, k_k.reshape(1, d), k_a.reshape(1, d), r_k.reshape(1, d), ln_g.reshape(1, d),
      ln_b.reshape(1, d))


def _tanh_epi(parts, extras):
    return jnp.tanh(parts[0])


def _sigmoid_epi(parts, extras):
    return jax.nn.sigmoid(parts[0])


def _sigmoid_bias_epi(parts, extras):
    return jax.nn.sigmoid(extras[0] + parts[0])


def _logdecay_epi(parts, extras):
    return -jnp.exp(-jax.nn.softplus(-(extras[0] + parts[0])) - 0.5)


def rwkv7_time_mix(x, g_pre, scale, shift, mu, w_rkv, w0, w_la, w_lb, a0, a_la, a_lb, g_la, g_lb, k_k, k_a, r_k,
                   ln_g, ln_b, w_out):
    b, t, d = x.shape
    xs = rwkv_pre(x, g_pre, scale, shift, mu).reshape(6 * b * t, d)
    tn = _pick(d, 1024)
    w_rkv2d = w_rkv.reshape(3 * d, d)
    r = mm(xs, [(w_rkv2d, 0, 0)], a_part=(0, 6), out_dtype=BF16, name="rwkv_r")
    k = mm(xs, [(w_rkv2d, 0, 1)], a_part=(1, 6), out_dtype=BF16, name="rwkv_k")
    v = mm(xs, [(w_rkv2d, 0, 2)], a_part=(2, 6), out_dtype=BF16, name="rwkv_v")
    lw = mm(mm(xs, w_la, a_part=(3, 6), epi=_tanh_epi, out_dtype=BF16, name="rwkv_w_la"), w_lb, epi=_logdecay_epi,
            extras=[_row_extra(w0, tn)], name="rwkv_w_lb")
    a = mm(mm(xs, a_la, a_part=(4, 6), out_dtype=BF16, name="rwkv_a_la"), a_lb, epi=_sigmoid_bias_epi,
           extras=[_row_extra(a0, tn)], out_dtype=BF16, name="rwkv_a_lb")
    g = mm(mm(xs, g_la, a_part=(5, 6), epi=_sigmoid_epi, out_dtype=BF16, name="rwkv_g_la"), g_lb, out_dtype=BF16,
           name="rwkv_g_lb")
    sh = lambda z: z.reshape(b, t, d)
    z = rwkv_scan(sh(r), sh(k), sh(v), sh(a), sh(lw), sh(g), k_k, k_a, r_k, ln_g, ln_b)
    return z.reshape(b * t, d), [(w_out, 0, 0)], None


def _nsa_compress_body(kv_ref, pe_ref, w1_ref, w2_ref, o_ref, buf_ref, *, nchunk, dh):
    st = NSA_CMP_STRIDE
    hid = w1_ref.shape[1]
    buf_ref[...] = kv_ref[0].astype(F32)
    first = jnp.zeros((nchunk, hid), F32)
    second = jnp.zeros((nchunk, hid), F32)
    for j in range(st):
        xj = buf_ref[pl.ds(j, nchunk, stride=st), :]
        first = first + _dot((xj + pe_ref[j:j + 1, :]).astype(BF16), w1_ref[j * dh:(j + 1) * dh, :])
        second = second + _dot((xj + pe_ref[st + j:st + j + 1, :]).astype(BF16),
                               w1_ref[(st + j) * dh:(st + j + 1) * dh, :])
    hidden = jax.nn.gelu(first + pltpu.roll(second, nchunk - 1, axis=0))
    out = _dot(hidden.astype(BF16), w2_ref[...])
    row = lax.broadcasted_iota(jnp.int32, out.shape, 0)
    o_ref[0, 0] = jnp.where(row < nchunk - 1, out, 0.0)


def nsa_compress(proj, col0, pe, w1, w2):
    b, t, _ = proj.shape
    g, dh = NSA_KV_GROUPS, NSA_HEAD_DIM
    nchunk = t // NSA_CMP_STRIDE
    hid = w1.shape[1]
    return pl.pallas_call(
        functools.partial(_nsa_compress_body, nchunk=nchunk, dh=dh),
        grid=(b, g),
        in_specs=[
            pl.BlockSpec((1, t, dh), lambda bi, gi: (bi, 0, col0 // dh + gi)),
            pl.BlockSpec((NSA_CMP_BLOCK, dh), lambda bi, gi: (0, 0)),
            pl.BlockSpec((NSA_CMP_BLOCK * dh, hid), lambda bi, gi: (0, 0)),
            pl.BlockSpec((hid, dh), lambda bi, gi: (0, 0)),
        ],
        out_specs=pl.BlockSpec((1, 1, nchunk, dh), lambda bi, gi: (bi, gi, 0, 0)),
        out_shape=jax.ShapeDtypeStruct((b, g, nchunk, dh), F32),
        scratch_shapes=[pltpu.VMEM((t, dh), F32)],
        compiler_params=_cparams("parallel", "parallel"),
        name="nsa_compress",
    )(proj, pe, w1.astype(BF16), w2.astype(BF16))


def _stack_heads(q, r, dh):
    return jnp.concatenate([q[:, i * dh:(i + 1) * dh] for i in range(r)], axis=0)


def _unstack_heads(o, r, tq):
    return jnp.concatenate([o[i * tq:(i + 1) * tq] for i in range(r)], axis=-1)


def _gate_rows(gates, branch, r, tq):
    return jnp.concatenate([gates[:, branch * r + i:branch * r + i + 1] for i in range(r)], axis=0)


def _nsa_cmp_body(q_ref, kc_ref, vc_ref, gates_ref, selmap_t_ref, o_ref, sel_ref, *, tq, r, dh, n_sel):
    q0 = pl.program_id(2) * tq
    qs = (_stack_heads(q_ref[0].astype(F32), r, dh) * (dh ** -0.5)).astype(BF16)
    kc, vc = kc_ref[0, 0].astype(BF16), vc_ref[0, 0].astype(BF16)
    ncp = kc.shape[0]
    s = _dot_nt(qs, kc)
    rows = lax.broadcasted_iota(jnp.int32, (r * tq, ncp), 0)
    cmp_end = lax.broadcasted_iota(jnp.int32, (r * tq, ncp), 1) * NSA_CMP_STRIDE + (NSA_CMP_BLOCK - 1)
    t_pos = q0 + rows % tq
    vis = cmp_end <= t_pos
    s = jnp.where(vis, s, NEG_INF)
    e = jnp.exp(s - jnp.max(s, axis=-1, keepdims=True))
    p = e / jnp.sum(e, axis=-1, keepdims=True) * vis.astype(F32)
    o = _dot(p.astype(BF16), vc)
    o = o * _gate_rows(gates_ref[0, 0], 0, r, tq)
    o_ref[0] = _unstack_heads(o, r, tq)
    p_sum = p[0:tq]
    for i in range(1, r):
        p_sum = p_sum + p[i * tq:(i + 1) * tq]
    hi = p_sum.astype(BF16)
    lo = (p_sum - hi.astype(F32)).astype(BF16)
    selmap_t = selmap_t_ref[...]
    imp = _dot_nt(selmap_t, hi) + _dot_nt(selmap_t, lo)
    ns = imp.shape[0]
    blk = lax.broadcasted_iota(jnp.int32, (ns, tq), 0)
    tq_pos = q0 + lax.broadcasted_iota(jnp.int32, (ns, tq), 1)
    cur = tq_pos // NSA_SEL_BLOCK
    valid = blk * NSA_SEL_BLOCK <= tq_pos
    forced = (blk == 0) | (blk == cur) | (blk == cur - 1)
    score = jnp.where(valid, jnp.where(forced, POS_BIG, imp), NEG_INF)
    rank = jnp.zeros((ns, tq), F32)
    for m in range(ns):
        sm = score[m:m + 1, :]
        beats = (sm > score) | ((sm == score) & (blk > m))
        rank = rank + beats.astype(F32)
    chosen = ((rank < n_sel) & valid).astype(F32)
    sel_ref[0, 0] = (chosen.T - 1.0).astype(sel_ref.dtype)


def _flash_update(s, v_ones, m_ref, acc_ref):
    lanes = m_ref.shape[1]
    assert s.shape[1] % lanes == 0 and acc_ref.shape[1] == 2 * lanes
    m_old = m_ref[...]
    m_new = jnp.maximum(m_old, jnp.max(s, axis=-1, keepdims=True))
    alpha = jnp.exp2(m_old - m_new)
    p = jnp.exp2((s - jnp.concatenate([m_new] * (s.shape[1] // lanes), axis=1)).astype(BF16))
    acc_ref[...] = jnp.concatenate([alpha, alpha], axis=1) * acc_ref[...] + _dot(p, v_ones)
    m_ref[...] = m_new


def _nsa_sel_body(qi_tbl, ki_tbl, diag_tbl, q_ref, k_ref, v_ref, gates_ref, oin_ref, sel_ref, o_ref, qa_ref, m_ref,
                  acc_ref, *, tq, tk, r, dh):
    step = pl.program_id(2)
    qi, ki, diag = qi_tbl[step], ki_tbl[step], diag_tbl[step]
    ns = sel_ref.shape[-1]
    assert ns <= dh

    @pl.when(ki == 0)
    def _():
        m_ref[...] = jnp.full_like(m_ref, NEG_INF)
        acc_ref[...] = jnp.zeros_like(acc_ref)
        qs = _stack_heads(q_ref[0].astype(F32), r, dh) * (dh ** -0.5 * LOG2_E)
        pieces = [qs, jnp.concatenate([sel_ref[0, 0].astype(F32)] * r, axis=0)]
        if ns < dh:
            pieces.append(jnp.zeros((r * tq, dh - ns), F32))
        qa_ref[...] = jnp.concatenate(pieces, axis=1).astype(BF16)

    blk_of_key = (ki * tk + lax.broadcasted_iota(jnp.int32, (tk, dh), 0)) // NSA_SEL_BLOCK
    own_block = jnp.where(blk_of_key == lax.broadcasted_iota(jnp.int32, (tk, dh), 1), POS_BIG, 0.0).astype(BF16)
    ka = jnp.concatenate([k_ref[0].astype(BF16), own_block], axis=1)
    s = _dot_nt(qa_ref[...], ka)
    v_ones = jnp.concatenate([v_ref[0].astype(BF16), jnp.ones((tk, dh), BF16)], axis=1)

    @pl.when(diag == 0)
    def _():
        _flash_update(s, v_ones, m_ref, acc_ref)

    @pl.when(diag == 1)
    def _():
        t_pos = qi * tq + lax.broadcasted_iota(jnp.int32, (r * tq, tk), 0) % tq
        k_pos = ki * tk + lax.broadcasted_iota(jnp.int32, (r * tq, tk), 1)
        _flash_update(jnp.where(k_pos <= t_pos, s, NEG_INF), v_ones, m_ref, acc_ref)
        o = acc_ref[:, :dh] / acc_ref[:, dh:] * _gate_rows(gates_ref[0, 0], 1, r, tq)
        o_ref[0] = (oin_ref[0] + _unstack_heads(o, r, tq)).astype(o_ref.dtype)


def _nsa_win_body(q_ref, *rest, tq, r, dh, nwin):
    k_refs, v_refs = rest[:nwin], rest[nwin:2 * nwin]
    gates_ref, oin_ref, o_ref = rest[2 * nwin:]
    qi = pl.program_id(2)
    qs = (_stack_heads(q_ref[0].astype(F32), r, dh) * (dh ** -0.5 * LOG2_E)).astype(BF16)
    t_loc = lax.broadcasted_iota(jnp.int32, (r * tq, tq), 0) % tq
    k_loc = lax.broadcasted_iota(jnp.int32, (r * tq, tq), 1)
    tiles = []
    for j in range(nwin):
        s = _dot_nt(qs, k_refs[j][0].astype(BF16))
        if j == nwin - 1:
            s = jnp.where(k_loc <= t_loc, s, NEG_INF)
        else:
            if j == 0:
                s = jnp.where(k_loc > t_loc, s, NEG_INF)
            s = jnp.where(qi - (nwin - 1) + j >= 0, s, NEG_INF)
        tiles.append(s)
    s = jnp.concatenate(tiles, axis=1)
    p = jnp.exp2((s - jnp.max(s, axis=-1, keepdims=True)).astype(BF16))
    v = jnp.concatenate([v_refs[j][0] for j in range(nwin)], axis=0).astype(BF16)
    on = _dot(p, jnp.concatenate([v, jnp.ones_like(v)], axis=1))
    o = on[:, :dh] / on[:, dh:] * _gate_rows(gates_ref[0, 0], 2, r, tq)
    o_ref[0] = (oin_ref[0] + _unstack_heads(o, r, tq)).astype(o_ref.dtype)


def nsa_mix(h, w_in, pe_k, pe_v, ck_w1, ck_w2, cv_w1, cv_w2, w_out, tq=256):
    b, t, d = h.shape
    nh, g, dh = NSA_HEADS, NSA_KV_GROUPS, NSA_HEAD_DIM
    r = nh // g
    hd, kd = nh * dh, g * dh
    n_main = hd + 6 * kd
    h2 = h.reshape(b * t, d)
    proj = mm(h2, w_in[:, :n_main], out_dtype=BF16, name="nsa_in").reshape(b, t, n_main)
    n_gate = w_in.shape[1] - n_main
    w_gate = jnp.pad(w_in[:, n_main:], ((0, 0), (0, 128 - n_gate)))
    gates = mm(h2, w_gate, epi=_sigmoid_epi, name="nsa_gates")[:, :n_gate]
    gates = jnp.transpose(gates.reshape(b, t, 3, g, r), (0, 3, 1, 2, 4)).reshape(b, g, t, 3 * r)
    k_cmp = nsa_compress(proj, hd, pe_k, ck_w1, ck_w2)
    v_cmp = nsa_compress(proj, hd + kd, pe_v, cv_w1, cv_w2)
    ncp = k_cmp.shape[2]
    ns = t // NSA_SEL_BLOCK
    n_sel = min(NSA_SEL_TOP, ns)
    cs = jnp.arange(ncp)[None, :] * NSA_CMP_STRIDE
    ss = jnp.arange(ns)[:, None] * NSA_SEL_BLOCK
    sel_map_t = (jnp.maximum(jnp.minimum(cs + NSA_CMP_BLOCK, ss + NSA_SEL_BLOCK) - jnp.maximum(cs, ss), 0)
                 .astype(F32) / NSA_CMP_BLOCK).astype(BF16)
    tq = _pick(t, tq)
    assert NSA_WINDOW % tq == 0
    nq = t // tq
    wq = r * dh
    col = lambda base: base // dh
    q_spec3 = pl.BlockSpec((1, tq, wq), lambda bi, gi, qi: (bi, qi, gi))
    gate_spec3 = pl.BlockSpec((1, 1, tq, 3 * r), lambda bi, gi, qi: (bi, gi, qi, 0))
    o_cmp, sel = pl.pallas_call(
        functools.partial(_nsa_cmp_body, tq=tq, r=r, dh=dh, n_sel=n_sel),
        grid=(b, g, nq),
        in_specs=[
            q_spec3,
            pl.BlockSpec((1, 1, ncp, dh), lambda bi, gi, qi: (bi, gi, 0, 0)),
            pl.BlockSpec((1, 1, ncp, dh), lambda bi, gi, qi: (bi, gi, 0, 0)),
            gate_spec3,
            pl.BlockSpec((ns, ncp), lambda bi, gi, qi: (0, 0)),
        ],
        out_specs=[q_spec3, pl.BlockSpec((1, 1, tq, ns), lambda bi, gi, qi: (bi, gi, qi, 0))],
        out_shape=[jax.ShapeDtypeStruct((b, t, hd), F32), jax.ShapeDtypeStruct((b, g, t, ns), BF16)],
        compiler_params=_cparams("parallel", "parallel", "parallel"),
        name="nsa_cmp",
    )(proj, k_cmp, v_cmp, gates, sel_map_t)

    tk = _pick(t, NSA_SEL_KEY_TILE)
    assert tk % tq == 0
    pairs = [(qi, ki) for qi in range(nq) for ki in range((qi * tq + tq - 1) // tk + 1)]
    qi_tbl = jnp.asarray([pq for pq, _ in pairs], jnp.int32)
    ki_tbl = jnp.asarray([pk for _, pk in pairs], jnp.int32)
    diag_tbl = jnp.asarray([int(pk == (pq * tq + tq - 1) // tk) for pq, pk in pairs], jnp.int32)
    q_spec_p = pl.BlockSpec((1, tq, wq), lambda bi, gi, p, qt, kt, dt: (bi, qt[p], gi))

    def kv_spec_p(base):
        return pl.BlockSpec((1, tk, dh), lambda bi, gi, p, qt, kt, dt: (bi, kt[p], col(base) + gi))

    o_sel = pl.pallas_call(
        functools.partial(_nsa_sel_body, tq=tq, tk=tk, r=r, dh=dh),
        grid_spec=pltpu.PrefetchScalarGridSpec(
            num_scalar_prefetch=3,
            grid=(b, g, len(pairs)),
            in_specs=[q_spec_p, kv_spec_p(hd + 2 * kd), kv_spec_p(hd + 3 * kd),
                      pl.BlockSpec((1, 1, tq, 3 * r), lambda bi, gi, p, qt, kt, dt: (bi, gi, qt[p], 0)),
                      q_spec_p,
                      pl.BlockSpec((1, 1, tq, ns), lambda bi, gi, p, qt, kt, dt: (bi, gi, qt[p], 0))],
            out_specs=q_spec_p,
            scratch_shapes=[pltpu.VMEM((r * tq, 2 * dh), BF16), pltpu.VMEM((r * tq, dh), F32),
                            pltpu.VMEM((r * tq, 2 * dh), F32)],
        ),
        out_shape=jax.ShapeDtypeStruct((b, t, hd), F32),
        compiler_params=_cparams("parallel", "parallel", "arbitrary"),
        name="nsa_sel",
    )(qi_tbl, ki_tbl, diag_tbl, proj, proj, proj, gates, o_cmp, sel)

    nwin = NSA_WINDOW // tq + 1

    def kv_spec_w(base, j):
        return pl.BlockSpec((1, tq, dh),
                            lambda bi, gi, qi: (bi, jnp.maximum(qi - (nwin - 1) + j, 0), col(base) + gi))

    o_all = pl.pallas_call(
        functools.partial(_nsa_win_body, tq=tq, r=r, dh=dh, nwin=nwin),
        grid=(b, g, nq),
        in_specs=([q_spec3] + [kv_spec_w(hd + 4 * kd, j) for j in range(nwin)]
                  + [kv_spec_w(hd + 5 * kd, j) for j in range(nwin)] + [gate_spec3, q_spec3]),
        out_specs=q_spec3,
        out_shape=jax.ShapeDtypeStruct((b, t, hd), BF16),
        compiler_params=_cparams("parallel", "parallel", "parallel"),
        name="nsa_win",
    )(proj, *([proj] * (2 * nwin)), gates, o_sel)
    return o_all.reshape(b * t, hd), [(w_out, 0, 0)], None


def _silu(a):
    return a * jax.nn.sigmoid(a)


def _ada_epi(parts, extras):
    return parts[0] + extras[0]


def ada_modulation(c, ada_w, ada_b):
    depth, d, n6 = ada_w.shape
    b = c.shape[0]
    rows = ((b + 7) // 8) * 8
    cond = jnp.pad(c, ((0, rows - b), (0, 0)))
    w2d = ada_w.reshape(depth * d, n6)
    mods = []
    for i in range(depth):
        mods.append(mm(cond, [(w2d, 0, i)], a_act=_silu, epi=_ada_epi, extras=[_row_extra(ada_b[i], _pick(n6, 1024))],
                       precision=HIGHEST, name="ada_mod")[:b])
    return jnp.stack(mods)


def kernel(x, c, ada_w, ada_b, norm_g, mlp_w1, mlp_w2, rwkv_mu, rwkv_w_rkv, rwkv_w0, rwkv_w_la, rwkv_w_lb, rwkv_a0, rwkv_a_la, rwkv_a_lb, rwkv_g_la, rwkv_g_lb, rwkv_k_k, rwkv_k_a, rwkv_r_k, rwkv_ln_g, rwkv_ln_b, rwkv_w_out, ret_w_in, ret_gn_g, ret_gn_b, ret_w_out, conv_pw1_w, conv_pw1_b, conv_dw_w, conv_dw_b, conv_ln_g, conv_ln_b, conv_pw2_w, conv_pw2_b, nsa_w_in, nsa_pe_k, nsa_pe_v, nsa_ck_w1, nsa_ck_w2, nsa_cv_w1, nsa_cv_w2, nsa_w_out):
    b, t, d = x.shape
    depth = ada_w.shape[0]
    mod = ada_modulation(c, ada_w, ada_b).reshape(depth, b, 6, d)
    h = None
    for i in range(depth):
        sh_t, sc_t, gt_t, sh_c, sc_c, gt_c = (mod[i, :, j] for j in range(6))
        kind = i % 4
        if kind == 0:
            act, ws, bias = rwkv7_time_mix(x, norm_g[i, 0], sc_t, sh_t, rwkv_mu, rwkv_w_rkv, rwkv_w0, rwkv_w_la,
                                           rwkv_w_lb, rwkv_a0, rwkv_a_la, rwkv_a_lb, rwkv_g_la, rwkv_g_lb, rwkv_k_k,
                                           rwkv_k_a, rwkv_r_k.reshape(-1), rwkv_ln_g, rwkv_ln_b, rwkv_w_out)
        else:
            if h is None:
                (h,) = resid_norm(x, pre=(norm_g[i, 0], sc_t, sh_t))
            if kind == 1:
                act, ws, bias = retention_mix(h, ret_w_in, ret_gn_g, ret_gn_b, ret_w_out)
            elif kind == 2:
                act, ws, bias = conformer_conv_mix(h, conv_pw1_w, conv_pw1_b, conv_dw_w, conv_dw_b, conv_ln_g,
                                                   conv_ln_b, conv_pw2_w, conv_pw2_b)
            else:
                act, ws, bias = nsa_mix(h, nsa_w_in, nsa_pe_k, nsa_pe_v, nsa_ck_w1, nsa_ck_w2, nsa_cv_w1, nsa_cv_w2,
                                        nsa_w_out)
        y = project(act, ws, bias, "mixer_out").reshape(b, t, d)
        x, h = resid_norm(x, y, post=(norm_g[i, 1], gt_t), pre=(norm_g[i, 2], sc_c, sh_c))
        y = project(*sqrelu_mlp(h.reshape(b * t, d), mlp_w1, mlp_w2, i), "mlp_down").reshape(b, t, d)
        nxt = i + 1
        if nxt < depth and nxt % 4 != 0:
            x, h = resid_norm(x, y, post=(norm_g[i, 3], gt_c),
                              pre=(norm_g[nxt, 0], mod[nxt, :, 1], mod[nxt, :, 0]))
        else:
            (x,) = resid_norm(x, y, post=(norm_g[i, 3], gt_c))
            h = None
    return x
```

```python
import functools
import math

import jax
import jax.numpy as jnp
from jax import lax
from jax.experimental import pallas as pl
from jax.experimental.pallas import tpu as pltpu

F32 = jnp.float32
BF16 = jnp.bfloat16
HIGHEST = lax.Precision.HIGHEST

NORM_EPS = 1e-6
NEG_INF = -1e30
POS_BIG = 1e30
LOG2_E = math.log2(math.e)

RWKV_HEAD_DIM = 64
RWKV_GN_EPS = 64e-5
RWKV_CHUNK = 64
RWKV_HEADS_PER_STEP = 32

RET_HEADS = 8
RET_CHUNK = 256
RET_GN_EPS = 1e-5
ROPE_BASE = 10000.0

CONV_WIDTH = 31
CONV_HALO = 32
CONV_LN_EPS = 1e-5

NSA_HEADS = 16
NSA_KV_GROUPS = 4
NSA_HEAD_DIM = 128
NSA_CMP_BLOCK = 32
NSA_CMP_STRIDE = 16
NSA_SEL_BLOCK = 64
NSA_SEL_TOP = 16
NSA_WINDOW = 512
NSA_SEL_KEY_TILE = 512

VMEM_LIMIT_BYTES = 56 * 1024 * 1024


def _cparams(*sem):
    return pltpu.CompilerParams(dimension_semantics=sem, vmem_limit_bytes=VMEM_LIMIT_BYTES)


def _dot(a, b, precision=None):
    return jnp.dot(a, b, preferred_element_type=F32, precision=precision)


def _dot_nt(a, b, precision=None):
    return lax.dot_general(a, b, (((1,), (1,)), ((), ())), preferred_element_type=F32, precision=precision)


def _pick(n, pref):
    if n <= pref:
        return n
    t = pref
    while n % t:
        t //= 2
    return t


def _mm_body(*refs, n_w, n_ex, n_out, nk, epi, precision, has_a_add, a_act):
    a_ref = refs[0]
    pos = 1
    a_add_ref = None
    if has_a_add:
        a_add_ref = refs[pos]
        pos += 1
    w_refs = refs[pos:pos + n_w]
    pos += n_w
    ex_refs = refs[pos:pos + n_ex]
    pos += n_ex
    o_refs = refs[pos:pos + n_out]
    acc_refs = refs[pos + n_out:]

    def finish(parts):
        res = epi(parts, [e[...] for e in ex_refs])
        if n_out == 1 and not isinstance(res, (tuple, list)):
            res = (res,)
        for o_ref, val in zip(o_refs, res, strict=True):
            o_ref[...] = val.astype(o_ref.dtype)

    a = a_ref[...]
    if a_add_ref is not None:
        a = a.astype(F32) + a_add_ref[...]
    if a_act is not None:
        a = a_act(a)
    if precision is None:
        a = a.astype(BF16)
    parts = []
    for w_ref in w_refs:
        w = w_ref[...]
        if precision is None:
            w = w.astype(BF16)
        parts.append(_dot(a, w, precision))

    if nk == 1:
        finish(parts)
    else:
        k = pl.program_id(2)

        @pl.when(k == 0)
        def _():
            for acc, p in zip(acc_refs, parts):
                acc[...] = p

        @pl.when(k > 0)
        def _():
            for acc, p in zip(acc_refs, parts):
                acc[...] += p

        @pl.when(k == nk - 1)
        def _():
            finish([acc[...] for acc in acc_refs])


def _first(parts, extras):
    return parts[0]


def mm(a, ws, *, n_out=None, extras=(), epi=_first, out_dtype=F32, tm=2048, tn=1024, tk=2048, precision=None,
       a_add=None, a_act=None, a_part=(0, 1), cast_in_kernel=False, name="mm"):
    m, kdim = a.shape
    a_s, a_parts = a_part
    assert m % a_parts == 0
    m //= a_parts
    if not isinstance(ws, (list, tuple)):
        ws = [(ws, 0, 0)]
    ws = [tuple(w) + (0,) * (3 - len(w)) for w in ws]
    if n_out is None:
        n_out = ws[0][0].shape[1]
    tm = _pick(m, tm)
    tn = _pick(n_out, tn)
    tk = _pick(kdim, tk)
    nk = kdim // tk
    assert m % tm == 0 and n_out % tn == 0 and kdim % tk == 0
    in_specs = [pl.BlockSpec((tm, tk), lambda i, j, k, o=a_s * (m // tm): (i + o, k))]
    args = [a]
    if a_add is not None:
        in_specs.append(pl.BlockSpec((1, tk), lambda i, j, k: (0, k)))
        args.append(a_add)
    for w, off, koff in ws:
        in_specs.append(pl.BlockSpec((tk, tn), lambda i, j, k, off=off, ko=koff * nk: (k + ko, j + off)))
        args.append(w.astype(BF16) if precision is None and not cast_in_kernel else w)
    for arr, bshape, imap in extras:
        in_specs.append(pl.BlockSpec(bshape, imap))
        args.append(arr)
    scratch = [pltpu.VMEM((tm, tn), F32) for _ in ws] if nk > 1 else []
    multi = isinstance(out_dtype, (tuple, list))
    out_dtypes = tuple(out_dtype) if multi else (out_dtype,)
    body = functools.partial(_mm_body, n_w=len(ws), n_ex=len(extras), n_out=len(out_dtypes), nk=nk, epi=epi,
                             precision=precision, has_a_add=a_add is not None, a_act=a_act)
    outs = pl.pallas_call(
        body,
        grid=(m // tm, n_out // tn, nk),
        in_specs=in_specs,
        out_specs=[pl.BlockSpec((tm, tn), lambda i, j, k: (i, j)) for _ in out_dtypes],
        out_shape=[jax.ShapeDtypeStruct((m, n_out), dt) for dt in out_dtypes],
        scratch_shapes=scratch,
        compiler_params=_cparams("parallel", "parallel", "arbitrary"),
        name=name,
    )(*args)
    return tuple(outs) if multi else outs[0]


def _row_extra(vec, tn):
    return (vec.reshape(1, -1), (1, tn), lambda i, j, k: (0, j))


def _rms(x, g):
    return x * lax.rsqrt(jnp.mean(x * x, axis=-1, keepdims=True) + NORM_EPS) * g


def _resid_norm_body(*refs, has_y, has_h):
    pos = 0
    x_ref = refs[pos]; pos += 1
    if has_y:
        y_ref, gpost_ref, gate_ref = refs[pos:pos + 3]; pos += 3
    if has_h:
        gpre_ref, scale_ref, shift_ref = refs[pos:pos + 3]; pos += 3
    outs = refs[pos:]
    x = x_ref[0]
    o = 0
    if has_y:
        y = y_ref[0].astype(F32)
        x = x + (1.0 + gate_ref[0]) * _rms(y, gpost_ref[...])
        outs[o][0] = x
        o += 1
    if has_h:
        h = _rms(x, gpre_ref[...]) * (1.0 + scale_ref[0]) + shift_ref[0]
        outs[o][0] = h.astype(outs[o].dtype)


def resid_norm(x, y=None, post=None, pre=None, tt=512):
    b, t, d = x.shape
    tt = _pick(t, tt)
    row = pl.BlockSpec((1, tt, d), lambda bi, ti: (bi, ti, 0))
    vec = pl.BlockSpec((1, d), lambda bi, ti: (0, 0))
    bvec = pl.BlockSpec((1, 1, d), lambda bi, ti: (bi, 0, 0))
    args, in_specs, out_shapes, out_specs = [x], [row], [], []
    if y is not None:
        args += [y, post[0].reshape(1, d), post[1].reshape(b, 1, d)]
        in_specs += [row, vec, bvec]
        out_shapes.append(jax.ShapeDtypeStruct((b, t, d), F32))
        out_specs.append(row)
    if pre is not None:
        args += [pre[0].reshape(1, d), pre[1].reshape(b, 1, d), pre[2].reshape(b, 1, d)]
        in_specs += [vec, bvec, bvec]
        out_shapes.append(jax.ShapeDtypeStruct((b, t, d), BF16))
        out_specs.append(row)
    res = pl.pallas_call(
        functools.partial(_resid_norm_body, has_y=y is not None, has_h=pre is not None),
        grid=(b, t // tt),
        in_specs=in_specs,
        out_specs=out_specs,
        out_shape=out_shapes,
        compiler_params=_cparams("parallel", "parallel"),
        name="resid_norm",
    )(*args)
    return tuple(res)


def _add_bias(parts, extras):
    return parts[0] + extras[0]


def project(act, ws, bias, name):
    if bias is None:
        return mm(act, ws, out_dtype=BF16, name=name)
    tn = _pick(ws[0][0].shape[1], 1024)
    return mm(act, ws, epi=_add_bias, extras=[_row_extra(bias, tn)], tn=tn, out_dtype=BF16, name=name)


def _mlp_up_body(x_ref, y_ref, gpost_ref, gate_ref, gpre_ref, scale_ref, shift_ref, w_ref, a_ref, xn_ref, h0_ref,
                 h1_ref, *, n_tiles, rows):
    i, j = pl.program_id(0), pl.program_id(1)
    bufs = (h0_ref, h1_ref)

    def norm_rows(dst_ref):
        xn = x_ref[...] + (1.0 + gate_ref[0]) * _rms(y_ref[...].astype(F32), gpost_ref[...])
        xn_ref[...] = xn
        h = _rms(xn, gpre_ref[...]) * (1.0 + scale_ref[0]) + shift_ref[0]
        dst_ref[pl.ds(pl.multiple_of(j * rows, rows), rows), :] = h.astype(dst_ref.dtype)

    def up_project(src_ref):
        r = jnp.maximum(_dot(src_ref[...], w_ref[...].astype(BF16)), 0.0)
        a_ref[...] = (r * r).astype(a_ref.dtype)

    @pl.when(i == 0)
    def _():
        norm_rows(bufs[0])

    for parity in (0, 1):
        @pl.when((i > 0) & (i < n_tiles) & (i % 2 == parity))
        def _():
            up_project(bufs[1 - parity])
            norm_rows(bufs[parity])

    @pl.when(i == n_tiles)
    def _():
        up_project(bufs[(n_tiles - 1) % 2])


def mlp_up_resid(x, y, post, pre, w1, layer, tm=2048, tn=512):
    b, t, d = x.shape
    nl, _, f = w1.shape
    m = b * t
    tm = _pick(t, tm)
    tn = _pick(f, tn)
    n_tiles, nj = m // tm, f // tn
    assert tm % nj == 0
    rows = tm // nj
    assert rows % 16 == 0
    tiles_per_batch = t // tm
    slice_idx = lambda i, j: jnp.where(i < n_tiles, i * nj + j, n_tiles * nj - 1)
    rows_spec = pl.BlockSpec((rows, d), lambda i, j: (slice_idx(i, j), 0))
    vec = pl.BlockSpec((1, d), lambda i, j: (0, 0))
    bvec = pl.BlockSpec((1, 1, d), lambda i, j: (jnp.minimum(i, n_tiles - 1) // tiles_per_batch, 0, 0))
    a, x_new = pl.pallas_call(
        functools.partial(_mlp_up_body, n_tiles=n_tiles, rows=rows),
        grid=(n_tiles + 1, nj),
        in_specs=[rows_spec, rows_spec, vec, bvec, vec, bvec, bvec,
                  pl.BlockSpec((d, tn), lambda i, j: (layer, j))],
        out_specs=[pl.BlockSpec((tm, tn), lambda i, j: (jnp.maximum(i - 1, 0), jnp.where(i > 0, j, 0))),
                   rows_spec],
        out_shape=[jax.ShapeDtypeStruct((m, f), BF16), jax.ShapeDtypeStruct((m, d), F32)],
        scratch_shapes=[pltpu.VMEM((tm, d), BF16), pltpu.VMEM((tm, d), BF16)],
        compiler_params=_cparams("arbitrary", "arbitrary"),
        name="mlp_up_resid",
    )(x.reshape(m, d), y.reshape(m, d), post[0].reshape(1, d), post[1].reshape(b, 1, d), pre[0].reshape(1, d),
      pre[1].reshape(b, 1, d), pre[2].reshape(b, 1, d), w1.reshape(nl * d, f))
    return a, x_new.reshape(b, t, d)


def _glu(parts, extras):
    return (parts[0] + extras[0]) * jax.nn.sigmoid(parts[1] + extras[1])


def _conv_ln_body(cur_ref, prev_ref, w_ref, b_ref, g_ref, beta_ref, o_ref, buf_ref, acc_ref, *, tt, d):
    ti = pl.program_id(1)
    halo = prev_ref[0].astype(F32)
    buf_ref[0:CONV_HALO, :] = jnp.where(ti == 0, jnp.zeros_like(halo), halo)
    buf_ref[CONV_HALO:, :] = cur_ref[0].astype(F32)
    lane_chunk = min(d, 256)
    row_chunk = min(tt, 64)
    base = CONV_HALO - (CONV_WIDTH - 1)
    sub = 8
    for c0 in range(0, d, lane_chunk):
        for r0 in range(0, tt, row_chunk):
            acc = None
            for s in range(sub):
                taps = [j for j in range(CONV_WIDTH) if (base + j) % sub == s]
                if not taps:
                    continue
                rows = row_chunk + (sub if s else 0)
                part = jnp.zeros((rows, lane_chunk), F32)
                for j in taps:
                    off = r0 + base + j - s
                    part = part + buf_ref[off:off + rows, c0:c0 + lane_chunk] * w_ref[j:j + 1, c0:c0 + lane_chunk]
                part = part[s:s + row_chunk]
                acc = part if acc is None else acc + part
            acc_ref[r0:r0 + row_chunk, c0:c0 + lane_chunk] = acc
    y = acc_ref[...] + b_ref[...]
    mu = jnp.mean(y, axis=-1, keepdims=True)
    yc = y - mu
    var = jnp.mean(yc * yc, axis=-1, keepdims=True)
    z = yc * lax.rsqrt(var + CONV_LN_EPS) * g_ref[...] + beta_ref[...]
    o_ref[0] = (z * jax.nn.sigmoid(z)).astype(o_ref.dtype)


def conv_ln_silu(u, dw_w, dw_b, ln_g, ln_b, tt=128):
    b, t, d = u.shape
    tt = _pick(t, tt)
    hb = tt // CONV_HALO
    vec = pl.BlockSpec((1, d), lambda bi, ti: (0, 0))
    return pl.pallas_call(
        functools.partial(_conv_ln_body, tt=tt, d=d),
        grid=(b, t // tt),
        in_specs=[
            pl.BlockSpec((1, tt, d), lambda bi, ti: (bi, ti, 0)),
            pl.BlockSpec((1, CONV_HALO, d), lambda bi, ti: (bi, jnp.maximum(ti * hb - 1, 0), 0)),
            pl.BlockSpec((CONV_WIDTH, d), lambda bi, ti: (0, 0)),
            vec, vec, vec,
        ],
        out_specs=pl.BlockSpec((1, tt, d), lambda bi, ti: (bi, ti, 0)),
        out_shape=jax.ShapeDtypeStruct((b, t, d), BF16),
        scratch_shapes=[pltpu.VMEM((tt + CONV_HALO, d), F32), pltpu.VMEM((tt, d), F32)],
        compiler_params=_cparams("parallel", "parallel"),
        name="conv_ln_silu",
    )(u, u, dw_w, dw_b.reshape(1, d), ln_g.reshape(1, d), ln_b.reshape(1, d))


def conformer_conv_mix(h, pw1_w, pw1_b, dw_w, dw_b, ln_g, ln_b, pw2_w, pw2_b):
    b, t, d = h.shape
    tn = _pick(d, 1024)
    b1 = pw1_b.reshape(1, -1)
    u = mm(h.reshape(b * t, d), [(pw1_w, 0), (pw1_w, d // tn)], n_out=d, tm=1024, tn=tn, epi=_glu,
           extras=[(b1, (1, tn), lambda i, j, k: (0, j)),
                   (b1, (1, tn), lambda i, j, k, o=d // tn: (0, j + o))], out_dtype=BF16, name="conf_pw1_glu")
    z = conv_ln_silu(u.reshape(b, t, d), dw_w, dw_b, ln_g, ln_b)
    return z.reshape(b * t, d), [(pw2_w, 0, 0)], pw2_b


def _rope(x, cos, sin):
    half = x.shape[-1] // 2
    x1, x2 = x[:, :half], x[:, half:]
    return jnp.concatenate([x1 * cos - x2 * sin, x2 * cos + x1 * sin], axis=-1)


def _retention_body(q_ref, k_ref, v_ref, gate_ref, cos_ref, sin_ref, inner_ref, qdec_ref, kdec_ref, cdec_ref,
                    gng_ref, gnb_ref, o_ref, state_ref, *, dk):
    @pl.when(pl.program_id(2) == 0)
    def _():
        state_ref[...] = jnp.zeros_like(state_ref)

    cos, sin = cos_ref[...], sin_ref[...]
    q = _rope(q_ref[0].astype(F32), cos, sin)
    k = _rope(k_ref[0].astype(F32), cos, sin) * (dk ** -0.5)
    v = v_ref[0].astype(BF16)
    qb = q.astype(BF16)
    s = _dot_nt(qb, k.astype(BF16)) * inner_ref[0]
    state = state_ref[...]
    o = _dot(s.astype(BF16), v) + _dot(qb, state.astype(BF16)) * qdec_ref[0]
    kd_t = (k * kdec_ref[0]).T.astype(BF16)
    state_ref[...] = state * cdec_ref[0] + _dot(kd_t, v)
    mu = jnp.mean(o, axis=-1, keepdims=True)
    oc = o - mu
    var = jnp.mean(oc * oc, axis=-1, keepdims=True)
    on = oc * lax.rsqrt(var + RET_GN_EPS) * gng_ref[...] + gnb_ref[...]
    gate = gate_ref[0].astype(F32)
    o_ref[0] = (gate * jax.nn.sigmoid(gate) * on).astype(o_ref.dtype)


def retention_mix(h, w_in, gn_g, gn_b, w_out):
    b, t, d = h.shape
    nh, c = RET_HEADS, _pick(t, RET_CHUNK)
    dk = d // nh
    dv = 2 * dk
    proj = mm(h.reshape(b * t, d), w_in, out_dtype=BF16, tm=2048, tn=512, cast_in_kernel=True,
              name="ret_in").reshape(b, t, 6 * d)
    pos = jnp.arange(t, dtype=F32)
    inv_freq = ROPE_BASE ** (-jnp.arange(0, dk, 2, dtype=F32) / dk)
    ang = pos[:, None] * inv_freq[None, :]
    cos, sin = jnp.cos(ang), jnp.sin(ang)
    log_gamma = jnp.log(1.0 - 2.0 ** (-5.0 - jnp.arange(nh, dtype=F32)))
    idx = jnp.arange(c, dtype=F32)
    diff = idx[:, None] - idx[None, :]
    inner = jnp.where(diff >= 0, jnp.exp(jnp.maximum(diff, 0.0)[None] * log_gamma[:, None, None]), 0.0)
    q_dec = jnp.exp((idx + 1.0)[None] * log_gamma[:, None])[:, :, None]
    k_dec = jnp.exp((c - 1.0 - idx)[None] * log_gamma[:, None])[:, :, None]
    c_dec = jnp.exp(c * log_gamma)[:, None, None]
    nq = (nh * dk) // dk
    o = pl.pallas_call(
        functools.partial(_retention_body, dk=dk),
        grid=(b, nh, t // c),
        in_specs=[
            pl.BlockSpec((1, c, dk), lambda bi, hi, ci: (bi, ci, hi)),
            pl.BlockSpec((1, c, dk), lambda bi, hi, ci: (bi, ci, nq + hi)),
            pl.BlockSpec((1, c, dv), lambda bi, hi, ci: (bi, ci, nq + hi)),
            pl.BlockSpec((1, c, dv), lambda bi, hi, ci: (bi, ci, 2 * nq + hi)),
            pl.BlockSpec((c, dk // 2), lambda bi, hi, ci: (ci, 0)),
            pl.BlockSpec((c, dk // 2), lambda bi, hi, ci: (ci, 0)),
            pl.BlockSpec((1, c, c), lambda bi, hi, ci: (hi, 0, 0)),
            pl.BlockSpec((1, c, 1), lambda bi, hi, ci: (hi, 0, 0)),
            pl.BlockSpec((1, c, 1), lambda bi, hi, ci: (hi, 0, 0)),
            pl.BlockSpec((1, 1, 1), lambda bi, hi, ci: (hi, 0, 0)),
            pl.BlockSpec((1, dv), lambda bi, hi, ci: (0, hi)),
            pl.BlockSpec((1, dv), lambda bi, hi, ci: (0, hi)),
        ],
        out_specs=pl.BlockSpec((1, c, dv), lambda bi, hi, ci: (bi, ci, hi)),
        out_shape=jax.ShapeDtypeStruct((b, t, nh * dv), BF16),
        scratch_shapes=[pltpu.VMEM((dk, dv), F32)],
        compiler_params=_cparams("parallel", "parallel", "arbitrary"),
        name="retention",
    )(proj, proj, proj, proj, cos, sin, inner, q_dec, k_dec, c_dec, gn_g.reshape(1, -1), gn_b.reshape(1, -1))
    return o.reshape(b * t, nh * dv), [(w_out, 0, 0)], None


def _rwkv_pre_body(x_ref, prev_ref, g_ref, scale_ref, shift_ref, mu_ref, o_ref):
    ti = pl.program_id(1)
    g, scale, shift = g_ref[...], scale_ref[0], shift_ref[0]
    h = _rms(x_ref[0], g) * (1.0 + scale) + shift
    hp = _rms(prev_ref[0], g) * (1.0 + scale) + shift
    last = jnp.where(ti == 0, 0.0, hp[7:8, :])
    row = lax.broadcasted_iota(jnp.int32, h.shape, 0)
    shifted = jnp.where(row == 0, last, pltpu.roll(h, 1, axis=0))
    xx = shifted - h
    for s in range(6):
        o_ref[s, 0] = (h + xx * mu_ref[s:s + 1, :]).astype(o_ref.dtype)


def rwkv_pre(x, g_pre, scale, shift, mu, tt=256):
    b, t, d = x.shape
    tt = _pick(t, tt)
    bvec = pl.BlockSpec((1, 1, d), lambda bi, ti: (bi, 0, 0))
    return pl.pallas_call(
        _rwkv_pre_body,
        grid=(b, t // tt),
        in_specs=[
            pl.BlockSpec((1, tt, d), lambda bi, ti: (bi, ti, 0)),
            pl.BlockSpec((1, 8, d), lambda bi, ti: (bi, jnp.maximum(ti * (tt // 8) - 1, 0), 0)),
            pl.BlockSpec((1, d), lambda bi, ti: (0, 0)),
            bvec, bvec,
            pl.BlockSpec((6, d), lambda bi, ti: (0, 0)),
        ],
        out_specs=pl.BlockSpec((6, 1, tt, d), lambda bi, ti: (0, bi, ti, 0)),
        out_shape=jax.ShapeDtypeStruct((6, b, t, d), BF16),
        compiler_params=_cparams("parallel", "parallel"),
        name="rwkv_pre",
    )(x, x, g_pre.reshape(1, d), scale.reshape(b, 1, d), shift.reshape(b, 1, d), mu)


def _rwkv_scan_body(r_ref, k_ref, v_ref, a_ref, lw_ref, g_ref, kk_ref, ka_ref, rk_ref, lng_ref, lnb_ref, o_ref,
                    state_ref, *, chunk, heads, n):
    @pl.when(pl.program_id(2) == 0)
    def _():
        state_ref[...] = jnp.zeros_like(state_ref)

    ln = chunk
    hs = range(heads)
    row = lax.broadcasted_iota(jnp.int32, (ln, ln), 0)
    col = lax.broadcasted_iota(jnp.int32, (ln, ln), 1)
    tri = (col <= row).astype(BF16)
    eye = (row == col).astype(F32)
    wid = heads * n
    seg_w = min(wid, 256)
    assert wid % seg_w == 0 and seg_w % n == 0
    seg = (lax.broadcasted_iota(jnp.int32, (seg_w, seg_w), 0) // n
           == lax.broadcasted_iota(jnp.int32, (seg_w, seg_w), 1) // n).astype(BF16)

    def split_dot(m, x):
        hi = x.astype(BF16)
        lo = (x - hi.astype(F32)).astype(BF16)
        return _dot(m, hi) + _dot(m, lo)

    def seg_sum(x):
        hi = x.astype(BF16)
        lo = (x - hi.astype(F32)).astype(BF16)
        return jnp.concatenate([_dot(hi[:, j:j + seg_w], seg) + _dot(lo[:, j:j + seg_w], seg)
                                for j in range(0, wid, seg_w)], axis=1)

    heads_of = lambda x: [x[:, hd * n:(hd + 1) * n] for hd in hs]

    r, k, v, a = (ref[0].astype(F32) for ref in (r_ref, k_ref, v_ref, a_ref))
    lw = lw_ref[0]
    kk = k * kk_ref[...]
    kk = kk / jnp.maximum(jnp.sqrt(seg_sum(kk * kk)), 1e-12)
    k2 = k * (1.0 + (a - 1.0) * ka_ref[...])
    beta = kk * a
    c = split_dot(tri, lw)
    c_last = c[ln - 1:ln, :]
    e_neg = jnp.exp(-c)
    e_rem = jnp.exp(c_last - c)
    e_last = heads_of(jnp.exp(c_last))
    a_t = heads_of(-kk * jnp.exp(c - lw))
    r_t = heads_of(r * jnp.exp(c))
    b_t = heads_of(beta * e_neg)
    k_t = heads_of(k2 * e_neg)
    b_rem = heads_of(beta * e_rem)
    k_rem = heads_of(k2 * e_rem)
    vs = heads_of(v)
    bonus = seg_sum(r * k2 * rk_ref[...]) * v

    ar = [jnp.concatenate([a_t[h], r_t[h]], axis=0).astype(BF16) for h in hs]
    bk = [jnp.concatenate([b_t[h], k_t[h]], axis=0).astype(BF16) for h in hs]
    bk_rem = [jnp.concatenate([b_rem[h], k_rem[h]], axis=0).astype(BF16) for h in hs]
    vb = [vs[h].astype(BF16) for h in hs]
    state = [state_ref[h] for h in hs]
    p = [_dot_nt(ar[h], bk[h]) for h in hs]
    q0 = [_dot_nt(ar[h], state[h].astype(BF16)) for h in hs]
    row2 = lax.broadcasted_iota(jnp.int32, (2 * ln, 2 * ln), 0)
    col2 = lax.broadcasted_iota(jnp.int32, (2 * ln, 2 * ln), 1)
    keep = col2 % ln < row2 % ln + row2 // ln
    pm = [jnp.where(keep, p[h], 0.0) for h in hs]
    m_ab = [pm[h][:ln, :ln] for h in hs]
    m_ak = [pm[h][:ln, ln:].astype(BF16) for h in hs]
    m_r = [pm[h][ln:].astype(BF16) for h in hs]
    rhs = [q0[h][:ln] + _dot(m_ak[h], vb[h]) for h in hs]
    inv = [eye + m_ab[h] for h in hs]
    pw = [m_ab[h].astype(BF16) for h in hs]
    for _ in range(int(math.log2(ln)) - 1):
        pw = [_dot(pw[h], pw[h]).astype(BF16) for h in hs]
        inv = [inv[h] + _dot(inv[h].astype(BF16), pw[h]) for h in hs]
    u = [_dot(inv[h].astype(BF16), rhs[h].astype(BF16)) for h in hs]
    uv = [jnp.concatenate([u[h], vs[h]], axis=0) for h in hs]
    y = [q0[h][ln:] + _dot(m_r[h], uv[h].astype(BF16)) for h in hs]
    for h in hs:
        state_ref[h] = state[h] * e_last[h] + _dot(uv[h].T.astype(BF16), bk_rem[h])
    yn = []
    for h in hs:
        mu = jnp.mean(y[h], axis=-1, keepdims=True)
        yc = y[h] - mu
        var = jnp.mean(yc * yc, axis=-1, keepdims=True)
        yn.append(yc * lax.rsqrt(var + RWKV_GN_EPS))
    yn = jnp.concatenate(yn, axis=-1) * lng_ref[...] + lnb_ref[...]
    o_ref[0] = ((yn + bonus) * g_ref[0]).astype(o_ref.dtype)


def rwkv_scan(r, k, v, a, lw, g, k_k, k_a, r_k, ln_g, ln_b):
    b, t, d = r.shape
    n = RWKV_HEAD_DIM
    hps = min(RWKV_HEADS_PER_STEP, d // n)
    w = hps * n
    ln = _pick(t, RWKV_CHUNK)
    tok = pl.BlockSpec((1, ln, w), lambda bi, hi, ci: (bi, ci, hi))
    vec = pl.BlockSpec((1, w), lambda bi, hi, ci: (0, hi))
    return pl.pallas_call(
        functools.partial(_rwkv_scan_body, chunk=ln, heads=hps, n=n),
        grid=(b, d // w, t // ln),
        in_specs=[tok] * 6 + [vec] * 5,
        out_specs=tok,
        out_shape=jax.ShapeDtypeStruct((b, t, d), BF16),
        scratch_shapes=[pltpu.VMEM((hps, n, n), F32)],
        compiler_params=_cparams("parallel", "parallel", "arbitrary"),
        name="rwkv_scan",
    )(r, k, v, a, lw, g, k_k.reshape(1, d), k_a.reshape(1, d), r_k.reshape(1, d), ln_g.reshape(1, d),
      ln_b.reshape(1, d))


def _tanh_epi(parts, extras):
    return jnp.tanh(parts[0])


def _sigmoid_epi(parts, extras):
    return jax.nn.sigmoid(parts[0])


def _sigmoid_bias_epi(parts, extras):
    return jax.nn.sigmoid(extras[0] + parts[0])


def _logdecay_epi(parts, extras):
    return -jnp.exp(-jax.nn.softplus(-(extras[0] + parts[0])) - 0.5)


def rwkv7_time_mix(x, g_pre, scale, shift, mu, w_rkv, w0, w_la, w_lb, a0, a_la, a_lb, g_la, g_lb, k_k, k_a, r_k,
                   ln_g, ln_b, w_out):
    b, t, d = x.shape
    xs = rwkv_pre(x, g_pre, scale, shift, mu).reshape(6 * b * t, d)
    tn = _pick(d, 1024)
    w_rkv2d = w_rkv.reshape(3 * d, d)
    r = mm(xs, [(w_rkv2d, 0, 0)], a_part=(0, 6), out_dtype=BF16, name="rwkv_r")
    k = mm(xs, [(w_rkv2d, 0, 1)], a_part=(1, 6), out_dtype=BF16, name="rwkv_k")
    v = mm(xs, [(w_rkv2d, 0, 2)], a_part=(2, 6), out_dtype=BF16, name="rwkv_v")
    lw = mm(mm(xs, w_la, a_part=(3, 6), epi=_tanh_epi, out_dtype=BF16, name="rwkv_w_la"), w_lb, epi=_logdecay_epi,
            extras=[_row_extra(w0, tn)], name="rwkv_w_lb")
    a = mm(mm(xs, a_la, a_part=(4, 6), out_dtype=BF16, name="rwkv_a_la"), a_lb, epi=_sigmoid_bias_epi,
           extras=[_row_extra(a0, tn)], out_dtype=BF16, name="rwkv_a_lb")
    g = mm(mm(xs, g_la, a_part=(5, 6), epi=_sigmoid_epi, out_dtype=BF16, name="rwkv_g_la"), g_lb, out_dtype=BF16,
           name="rwkv_g_lb")
    sh = lambda z: z.reshape(b, t, d)
    z = rwkv_scan(sh(r), sh(k), sh(v), sh(a), sh(lw), sh(g), k_k, k_a, r_k, ln_g, ln_b)
    return z.reshape(b * t, d), [(w_out, 0, 0)], None


def _nsa_compress_body(kv_ref, pe_ref, w1_ref, w2_ref, o_ref, buf_ref, *, nchunk, dh):
    st = NSA_CMP_STRIDE
    hid = w1_ref.shape[1]
    buf_ref[...] = kv_ref[0].astype(F32)
    first = jnp.zeros((nchunk, hid), F32)
    second = jnp.zeros((nchunk, hid), F32)
    for j in range(st):
        xj = buf_ref[pl.ds(j, nchunk, stride=st), :]
        first = first + _dot((xj + pe_ref[j:j + 1, :]).astype(BF16), w1_ref[j * dh:(j + 1) * dh, :])
        second = second + _dot((xj + pe_ref[st + j:st + j + 1, :]).astype(BF16),
                               w1_ref[(st + j) * dh:(st + j + 1) * dh, :])
    hidden = jax.nn.gelu(first + pltpu.roll(second, nchunk - 1, axis=0))
    out = _dot(hidden.astype(BF16), w2_ref[...])
    row = lax.broadcasted_iota(jnp.int32, out.shape, 0)
    o_ref[0, 0] = jnp.where(row < nchunk - 1, out, 0.0)


def nsa_compress(proj, col0, pe, w1, w2):
    b, t, _ = proj.shape
    g, dh = NSA_KV_GROUPS, NSA_HEAD_DIM
    nchunk = t // NSA_CMP_STRIDE
    hid = w1.shape[1]
    return pl.pallas_call(
        functools.partial(_nsa_compress_body, nchunk=nchunk, dh=dh),
        grid=(b, g),
        in_specs=[
            pl.BlockSpec((1, t, dh), lambda bi, gi: (bi, 0, col0 // dh + gi)),
            pl.BlockSpec((NSA_CMP_BLOCK, dh), lambda bi, gi: (0, 0)),
            pl.BlockSpec((NSA_CMP_BLOCK * dh, hid), lambda bi, gi: (0, 0)),
            pl.BlockSpec((hid, dh), lambda bi, gi: (0, 0)),
        ],
        out_specs=pl.BlockSpec((1, 1, nchunk, dh), lambda bi, gi: (bi, gi, 0, 0)),
        out_shape=jax.ShapeDtypeStruct((b, g, nchunk, dh), F32),
        scratch_shapes=[pltpu.VMEM((t, dh), F32)],
        compiler_params=_cparams("parallel", "parallel"),
        name="nsa_compress",
    )(proj, pe, w1.astype(BF16), w2.astype(BF16))


def _stack_heads(q, r, dh):
    return jnp.concatenate([q[:, i * dh:(i + 1) * dh] for i in range(r)], axis=0)


def _unstack_heads(o, r, tq):
    return jnp.concatenate([o[i * tq:(i + 1) * tq] for i in range(r)], axis=-1)


def _gate_rows(gates, branch, r, tq):
    return jnp.concatenate([gates[:, branch * r + i:branch * r + i + 1] for i in range(r)], axis=0)


def _nsa_cmp_body(q_ref, kc_ref, vc_ref, gates_ref, selmap_t_ref, o_ref, sel_ref, *, tq, r, dh, n_sel):
    q0 = pl.program_id(2) * tq
    qs = (_stack_heads(q_ref[0].astype(F32), r, dh) * (dh ** -0.5)).astype(BF16)
    kc, vc = kc_ref[0, 0].astype(BF16), vc_ref[0, 0].astype(BF16)
    ncp = kc.shape[0]
    s = _dot_nt(qs, kc)
    rows = lax.broadcasted_iota(jnp.int32, (r * tq, ncp), 0)
    cmp_end = lax.broadcasted_iota(jnp.int32, (r * tq, ncp), 1) * NSA_CMP_STRIDE + (NSA_CMP_BLOCK - 1)
    t_pos = q0 + rows % tq
    vis = cmp_end <= t_pos
    s = jnp.where(vis, s, NEG_INF)
    e = jnp.exp(s - jnp.max(s, axis=-1, keepdims=True))
    p = e / jnp.sum(e, axis=-1, keepdims=True) * vis.astype(F32)
    o = _dot(p.astype(BF16), vc)
    o = o * _gate_rows(gates_ref[0, 0], 0, r, tq)
    o_ref[0] = _unstack_heads(o, r, tq)
    p_sum = p[0:tq]
    for i in range(1, r):
        p_sum = p_sum + p[i * tq:(i + 1) * tq]
    hi = p_sum.astype(BF16)
    lo = (p_sum - hi.astype(F32)).astype(BF16)
    selmap_t = selmap_t_ref[...]
    imp = _dot_nt(selmap_t, hi) + _dot_nt(selmap_t, lo)
    ns = imp.shape[0]
    blk = lax.broadcasted_iota(jnp.int32, (ns, tq), 0)
    tq_pos = q0 + lax.broadcasted_iota(jnp.int32, (ns, tq), 1)
    cur = tq_pos // NSA_SEL_BLOCK
    valid = blk * NSA_SEL_BLOCK <= tq_pos
    forced = (blk == 0) | (blk == cur) | (blk == cur - 1)
    score = jnp.where(valid, jnp.where(forced, POS_BIG, imp), NEG_INF)
    rank = jnp.zeros((ns, tq), F32)
    for m in range(ns):
        sm = score[m:m + 1, :]
        beats = (sm > score) | ((sm == score) & (blk > m))
        rank = rank + beats.astype(F32)
    chosen = ((rank < n_sel) & valid).astype(F32)
    sel_ref[0, 0] = (chosen.T - 1.0).astype(sel_ref.dtype)


def _flash_update(s, v_ones, m_ref, acc_ref):
    lanes = m_ref.shape[1]
    assert s.shape[1] % lanes == 0 and acc_ref.shape[1] == 2 * lanes
    m_old = m_ref[...]
    m_new = jnp.maximum(m_old, jnp.max(s, axis=-1, keepdims=True))
    alpha = jnp.exp2(m_old - m_new)
    p = jnp.exp2((s - jnp.concatenate([m_new] * (s.shape[1] // lanes), axis=1)).astype(BF16))
    acc_ref[...] = jnp.concatenate([alpha, alpha], axis=1) * acc_ref[...] + _dot(p, v_ones)
    m_ref[...] = m_new


def _nsa_sel_body(qi_tbl, ki_tbl, diag_tbl, q_ref, k_ref, v_ref, gates_ref, oin_ref, sel_ref, o_ref, qa_ref, m_ref,
                  acc_ref, *, tq, tk, r, dh):
    step = pl.program_id(2)
    qi, ki, diag = qi_tbl[step], ki_tbl[step], diag_tbl[step]
    ns = sel_ref.shape[-1]
    assert ns <= dh

    @pl.when(ki == 0)
    def _():
        m_ref[...] = jnp.full_like(m_ref, NEG_INF)
        acc_ref[...] = jnp.zeros_like(acc_ref)
        qs = _stack_heads(q_ref[0].astype(F32), r, dh) * (dh ** -0.5 * LOG2_E)
        pieces = [qs, jnp.concatenate([sel_ref[0, 0].astype(F32)] * r, axis=0)]
        if ns < dh:
            pieces.append(jnp.zeros((r * tq, dh - ns), F32))
        qa_ref[...] = jnp.concatenate(pieces, axis=1).astype(BF16)

    blk_of_key = (ki * tk + lax.broadcasted_iota(jnp.int32, (tk, dh), 0)) // NSA_SEL_BLOCK
    own_block = jnp.where(blk_of_key == lax.broadcasted_iota(jnp.int32, (tk, dh), 1), POS_BIG, 0.0).astype(BF16)
    ka = jnp.concatenate([k_ref[0].astype(BF16), own_block], axis=1)
    s = _dot_nt(qa_ref[...], ka)
    v_ones = jnp.concatenate([v_ref[0].astype(BF16), jnp.ones((tk, dh), BF16)], axis=1)

    @pl.when(diag == 0)
    def _():
        _flash_update(s, v_ones, m_ref, acc_ref)

    @pl.when(diag == 1)
    def _():
        t_pos = qi * tq + lax.broadcasted_iota(jnp.int32, (r * tq, tk), 0) % tq
        k_pos = ki * tk + lax.broadcasted_iota(jnp.int32, (r * tq, tk), 1)
        _flash_update(jnp.where(k_pos <= t_pos, s, NEG_INF), v_ones, m_ref, acc_ref)
        o = acc_ref[:, :dh] / acc_ref[:, dh:] * _gate_rows(gates_ref[0, 0], 1, r, tq)
        o_ref[0] = (oin_ref[0] + _unstack_heads(o, r, tq)).astype(o_ref.dtype)


def _nsa_win_body(q_ref, *rest, tq, r, dh, nwin):
    k_refs, v_refs = rest[:nwin], rest[nwin:2 * nwin]
    gates_ref, oin_ref, o_ref = rest[2 * nwin:]
    qi = pl.program_id(2)
    qs = (_stack_heads(q_ref[0].astype(F32), r, dh) * (dh ** -0.5 * LOG2_E)).astype(BF16)
    t_loc = lax.broadcasted_iota(jnp.int32, (r * tq, tq), 0) % tq
    k_loc = lax.broadcasted_iota(jnp.int32, (r * tq, tq), 1)
    tiles = []
    for j in range(nwin):
        s = _dot_nt(qs, k_refs[j][0].astype(BF16))
        if j == nwin - 1:
            s = jnp.where(k_loc <= t_loc, s, NEG_INF)
        else:
            if j == 0:
                s = jnp.where(k_loc > t_loc, s, NEG_INF)
            s = jnp.where(qi - (nwin - 1) + j >= 0, s, NEG_INF)
        tiles.append(s)
    s = jnp.concatenate(tiles, axis=1)
    p = jnp.exp2((s - jnp.max(s, axis=-1, keepdims=True)).astype(BF16))
    v = jnp.concatenate([v_refs[j][0] for j in range(nwin)], axis=0).astype(BF16)
    on = _dot(p, jnp.concatenate([v, jnp.ones_like(v)], axis=1))
    o = on[:, :dh] / on[:, dh:] * _gate_rows(gates_ref[0, 0], 2, r, tq)
    o_ref[0] = (oin_ref[0] + _unstack_heads(o, r, tq)).astype(o_ref.dtype)


def nsa_mix(h, w_in, pe_k, pe_v, ck_w1, ck_w2, cv_w1, cv_w2, w_out, tq=256):
    b, t, d = h.shape
    nh, g, dh = NSA_HEADS, NSA_KV_GROUPS, NSA_HEAD_DIM
    r = nh // g
    hd, kd = nh * dh, g * dh
    n_main = hd + 6 * kd
    h2 = h.reshape(b * t, d)
    proj = mm(h2, w_in[:, :n_main], out_dtype=BF16, name="nsa_in").reshape(b, t, n_main)
    n_gate = w_in.shape[1] - n_main
    w_gate = jnp.pad(w_in[:, n_main:], ((0, 0), (0, 128 - n_gate)))
    gates = mm(h2, w_gate, epi=_sigmoid_epi, name="nsa_gates")[:, :n_gate]
    gates = jnp.transpose(gates.reshape(b, t, 3, g, r), (0, 3, 1, 2, 4)).reshape(b, g, t, 3 * r)
    k_cmp = nsa_compress(proj, hd, pe_k, ck_w1, ck_w2)
    v_cmp = nsa_compress(proj, hd + kd, pe_v, cv_w1, cv_w2)
    ncp = k_cmp.shape[2]
    ns = t // NSA_SEL_BLOCK
    n_sel = min(NSA_SEL_TOP, ns)
    cs = jnp.arange(ncp)[None, :] * NSA_CMP_STRIDE
    ss = jnp.arange(ns)[:, None] * NSA_SEL_BLOCK
    sel_map_t = (jnp.maximum(jnp.minimum(cs + NSA_CMP_BLOCK, ss + NSA_SEL_BLOCK) - jnp.maximum(cs, ss), 0)
                 .astype(F32) / NSA_CMP_BLOCK).astype(BF16)
    tq = _pick(t, tq)
    assert NSA_WINDOW % tq == 0
    nq = t // tq
    wq = r * dh
    col = lambda base: base // dh
    q_spec3 = pl.BlockSpec((1, tq, wq), lambda bi, gi, qi: (bi, qi, gi))
    gate_spec3 = pl.BlockSpec((1, 1, tq, 3 * r), lambda bi, gi, qi: (bi, gi, qi, 0))
    o_cmp, sel = pl.pallas_call(
        functools.partial(_nsa_cmp_body, tq=tq, r=r, dh=dh, n_sel=n_sel),
        grid=(b, g, nq),
        in_specs=[
            q_spec3,
            pl.BlockSpec((1, 1, ncp, dh), lambda bi, gi, qi: (bi, gi, 0, 0)),
            pl.BlockSpec((1, 1, ncp, dh), lambda bi, gi, qi: (bi, gi, 0, 0)),
            gate_spec3,
            pl.BlockSpec((ns, ncp), lambda bi, gi, qi: (0, 0)),
        ],
        out_specs=[q_spec3, pl.BlockSpec((1, 1, tq, ns), lambda bi, gi, qi: (bi, gi, qi, 0))],
        out_shape=[jax.ShapeDtypeStruct((b, t, hd), F32), jax.ShapeDtypeStruct((b, g, t, ns), BF16)],
        compiler_params=_cparams("parallel", "parallel", "parallel"),
        name="nsa_cmp",
    )(proj, k_cmp, v_cmp, gates, sel_map_t)

    tk = _pick(t, NSA_SEL_KEY_TILE)
    assert tk % tq == 0
    pairs = [(qi, ki) for qi in range(nq) for ki in range((qi * tq + tq - 1) // tk + 1)]
    qi_tbl = jnp.asarray([pq for pq, _ in pairs], jnp.int32)
    ki_tbl = jnp.asarray([pk for _, pk in pairs], jnp.int32)
    diag_tbl = jnp.asarray([int(pk == (pq * tq + tq - 1) // tk) for pq, pk in pairs], jnp.int32)
    q_spec_p = pl.BlockSpec((1, tq, wq), lambda bi, gi, p, qt, kt, dt: (bi, qt[p], gi))

    def kv_spec_p(base):
        return pl.BlockSpec((1, tk, dh), lambda bi, gi, p, qt, kt, dt: (bi, kt[p], col(base) + gi))

    o_sel = pl.pallas_call(
        functools.partial(_nsa_sel_body, tq=tq, tk=tk, r=r, dh=dh),
        grid_spec=pltpu.PrefetchScalarGridSpec(
            num_scalar_prefetch=3,
            grid=(b, g, len(pairs)),
            in_specs=[q_spec_p, kv_spec_p(hd + 2 * kd), kv_spec_p(hd + 3 * kd),
                      pl.BlockSpec((1, 1, tq, 3 * r), lambda bi, gi, p, qt, kt, dt: (bi, gi, qt[p], 0)),
                      q_spec_p,
                      pl.BlockSpec((1, 1, tq, ns), lambda bi, gi, p, qt, kt, dt: (bi, gi, qt[p], 0))],
            out_specs=q_spec_p,
            scratch_shapes=[pltpu.VMEM((r * tq, 2 * dh), BF16), pltpu.VMEM((r * tq, dh), F32),
                            pltpu.VMEM((r * tq, 2 * dh), F32)],
        ),
        out_shape=jax.ShapeDtypeStruct((b, t, hd), F32),
        compiler_params=_cparams("parallel", "parallel", "arbitrary"),
        name="nsa_sel",
    )(qi_tbl, ki_tbl, diag_tbl, proj, proj, proj, gates, o_cmp, sel)

    nwin = NSA_WINDOW // tq + 1

    def kv_spec_w(base, j):
        return pl.BlockSpec((1, tq, dh),
                            lambda bi, gi, qi: (bi, jnp.maximum(qi - (nwin - 1) + j, 0), col(base) + gi))

    o_all = pl.pallas_call(
        functools.partial(_nsa_win_body, tq=tq, r=r, dh=dh, nwin=nwin),
        grid=(b, g, nq),
        in_specs=([q_spec3] + [kv_spec_w(hd + 4 * kd, j) for j in range(nwin)]
                  + [kv_spec_w(hd + 5 * kd, j) for j in range(nwin)] + [gate_spec3, q_spec3]),
        out_specs=q_spec3,
        out_shape=jax.ShapeDtypeStruct((b, t, hd), BF16),
        compiler_params=_cparams("parallel", "parallel", "parallel"),
        name="nsa_win",
    )(proj, *([proj] * (2 * nwin)), gates, o_sel)
    return o_all.reshape(b * t, hd), [(w_out, 0, 0)], None


def _silu(a):
    return a * jax.nn.sigmoid(a)


def _ada_epi(parts, extras):
    return parts[0] + extras[0]


def ada_modulation(c, ada_w, ada_b):
    depth, d, n6 = ada_w.shape
    b = c.shape[0]
    rows = ((b + 7) // 8) * 8
    cond = jnp.pad(c, ((0, rows - b), (0, 0)))
    w2d = ada_w.reshape(depth * d, n6)
    mods = []
    for i in range(depth):
        mods.append(mm(cond, [(w2d, 0, i)], a_act=_silu, epi=_ada_epi, extras=[_row_extra(ada_b[i], _pick(n6, 1024))],
                       precision=HIGHEST, name="ada_mod")[:b])
    return jnp.stack(mods)


def kernel(x, c, ada_w, ada_b, norm_g, mlp_w1, mlp_w2, rwkv_mu, rwkv_w_rkv, rwkv_w0, rwkv_w_la, rwkv_w_lb, rwkv_a0, rwkv_a_la, rwkv_a_lb, rwkv_g_la, rwkv_g_lb, rwkv_k_k, rwkv_k_a, rwkv_r_k, rwkv_ln_g, rwkv_ln_b, rwkv_w_out, ret_w_in, ret_gn_g, ret_gn_b, ret_w_out, conv_pw1_w, conv_pw1_b, conv_dw_w, conv_dw_b, conv_ln_g, conv_ln_b, conv_pw2_w, conv_pw2_b, nsa_w_in, nsa_pe_k, nsa_pe_v, nsa_ck_w1, nsa_ck_w2, nsa_cv_w1, nsa_cv_w2, nsa_w_out):
    b, t, d = x.shape
    depth = ada_w.shape[0]
    mod = ada_modulation(c, ada_w, ada_b).reshape(depth, b, 6, d)
    h = None
    for i in range(depth):
        sh_t, sc_t, gt_t, sh_c, sc_c, gt_c = (mod[i, :, j] for j in range(6))
        kind = i % 4
        if kind == 0:
            act, ws, bias = rwkv7_time_mix(x, norm_g[i, 0], sc_t, sh_t, rwkv_mu, rwkv_w_rkv, rwkv_w0, rwkv_w_la,
                                           rwkv_w_lb, rwkv_a0, rwkv_a_la, rwkv_a_lb, rwkv_g_la, rwkv_g_lb, rwkv_k_k,
                                           rwkv_k_a, rwkv_r_k.reshape(-1), rwkv_ln_g, rwkv_ln_b, rwkv_w_out)
        else:
            if h is None:
                (h,) = resid_norm(x, pre=(norm_g[i, 0], sc_t, sh_t))
            if kind == 1:
                act, ws, bias = retention_mix(h, ret_w_in, ret_gn_g, ret_gn_b, ret_w_out)
            elif kind == 2:
                act, ws, bias = conformer_conv_mix(h, conv_pw1_w, conv_pw1_b, conv_dw_w, conv_dw_b, conv_ln_g,
                                                   conv_ln_b, conv_pw2_w, conv_pw2_b)
            else:
                act, ws, bias = nsa_mix(h, nsa_w_in, nsa_pe_k, nsa_pe_v, nsa_ck_w1, nsa_ck_w2, nsa_cv_w1, nsa_cv_w2,
                                        nsa_w_out)
        y = project(act, ws, bias, "mixer_out").reshape(b, t, d)
        act, x = mlp_up_resid(x, y, (norm_g[i, 1], gt_t), (norm_g[i, 2], sc_c, sh_c), mlp_w1, i)
        y = project(act, [(mlp_w2.reshape(-1, d), 0, i)], None, "mlp_down").reshape(b, t, d)
        nxt = i + 1
        if nxt < depth and nxt % 4 != 0:
            x, h = resid_norm(x, y, post=(norm_g[i, 3], gt_c),
                              pre=(norm_g[nxt, 0], mod[nxt, :, 1], mod[nxt, :, 0]))
        else:
            (x,) = resid_norm(x, y, post=(norm_g[i, 3], gt_c))
            h = None
    return x
```

```python
import functools
import math

import jax
import jax.numpy as jnp
from jax import lax
from jax.experimental import pallas as pl
from jax.experimental.pallas import tpu as pltpu

F32 = jnp.float32
BF16 = jnp.bfloat16
HIGHEST = lax.Precision.HIGHEST

NORM_EPS = 1e-6
NEG_INF = -1e30
POS_BIG = 1e30
LOG2_E = math.log2(math.e)

RWKV_HEAD_DIM = 64
RWKV_GN_EPS = 64e-5
RWKV_CHUNK = 64
RWKV_HEADS_PER_STEP = 32

RET_HEADS = 8
RET_CHUNK = 256
RET_GN_EPS = 1e-5
ROPE_BASE = 10000.0

CONV_WIDTH = 31
CONV_HALO = 32
CONV_LN_EPS = 1e-5

NSA_HEADS = 16
NSA_KV_GROUPS = 4
NSA_HEAD_DIM = 128
NSA_CMP_BLOCK = 32
NSA_CMP_STRIDE = 16
NSA_SEL_BLOCK = 64
NSA_SEL_TOP = 16
NSA_WINDOW = 512
NSA_SEL_KEY_TILE = 512

VMEM_LIMIT_BYTES = 56 * 1024 * 1024


def _cparams(*sem):
    return pltpu.CompilerParams(dimension_semantics=sem, vmem_limit_bytes=VMEM_LIMIT_BYTES)


def _dot(a, b, precision=None):
    return jnp.dot(a, b, preferred_element_type=F32, precision=precision)


def _dot_nt(a, b, precision=None):
    return lax.dot_general(a, b, (((1,), (1,)), ((), ())), preferred_element_type=F32, precision=precision)


def _pick(n, pref):
    if n <= pref:
        return n
    t = pref
    while n % t:
        t //= 2
    return t


def _mm_body(*refs, n_w, n_ex, n_out, nk, epi, precision, has_a_add, a_act):
    a_ref = refs[0]
    pos = 1
    a_add_ref = None
    if has_a_add:
        a_add_ref = refs[pos]
        pos += 1
    w_refs = refs[pos:pos + n_w]
    pos += n_w
    ex_refs = refs[pos:pos + n_ex]
    pos += n_ex
    o_refs = refs[pos:pos + n_out]
    acc_refs = refs[pos + n_out:]

    def finish(parts):
        res = epi(parts, [e[...] for e in ex_refs])
        if n_out == 1 and not isinstance(res, (tuple, list)):
            res = (res,)
        for o_ref, val in zip(o_refs, res, strict=True):
            o_ref[...] = val.astype(o_ref.dtype)

    a = a_ref[...]
    if a_add_ref is not None:
        a = a.astype(F32) + a_add_ref[...]
    if a_act is not None:
        a = a_act(a)
    if precision is None:
        a = a.astype(BF16)
    parts = []
    for w_ref in w_refs:
        w = w_ref[...]
        if precision is None:
            w = w.astype(BF16)
        parts.append(_dot(a, w, precision))

    if nk == 1:
        finish(parts)
    else:
        k = pl.program_id(2)

        @pl.when(k == 0)
        def _():
            for acc, p in zip(acc_refs, parts):
                acc[...] = p

        @pl.when(k > 0)
        def _():
            for acc, p in zip(acc_refs, parts):
                acc[...] += p

        @pl.when(k == nk - 1)
        def _():
            finish([acc[...] for acc in acc_refs])


def _first(parts, extras):
    return parts[0]


def mm(a, ws, *, n_out=None, extras=(), epi=_first, out_dtype=F32, tm=2048, tn=1024, tk=2048, precision=None,
       a_add=None, a_act=None, a_part=(0, 1), cast_in_kernel=False, name="mm"):
    m, kdim = a.shape
    a_s, a_parts = a_part
    assert m % a_parts == 0
    m //= a_parts
    if not isinstance(ws, (list, tuple)):
        ws = [(ws, 0, 0)]
    ws = [tuple(w) + (0,) * (3 - len(w)) for w in ws]
    if n_out is None:
        n_out = ws[0][0].shape[1]
    tm = _pick(m, tm)
    tn = _pick(n_out, tn)
    tk = _pick(kdim, tk)
    nk = kdim // tk
    assert m % tm == 0 and n_out % tn == 0 and kdim % tk == 0
    in_specs = [pl.BlockSpec((tm, tk), lambda i, j, k, o=a_s * (m // tm): (i + o, k))]
    args = [a]
    if a_add is not None:
        in_specs.append(pl.BlockSpec((1, tk), lambda i, j, k: (0, k)))
        args.append(a_add)
    for w, off, koff in ws:
        in_specs.append(pl.BlockSpec((tk, tn), lambda i, j, k, off=off, ko=koff * nk: (k + ko, j + off)))
        args.append(w.astype(BF16) if precision is None and not cast_in_kernel else w)
    for arr, bshape, imap in extras:
        in_specs.append(pl.BlockSpec(bshape, imap))
        args.append(arr)
    scratch = [pltpu.VMEM((tm, tn), F32) for _ in ws] if nk > 1 else []
    multi = isinstance(out_dtype, (tuple, list))
    out_dtypes = tuple(out_dtype) if multi else (out_dtype,)
    body = functools.partial(_mm_body, n_w=len(ws), n_ex=len(extras), n_out=len(out_dtypes), nk=nk, epi=epi,
                             precision=precision, has_a_add=a_add is not None, a_act=a_act)
    outs = pl.pallas_call(
        body,
        grid=(m // tm, n_out // tn, nk),
        in_specs=in_specs,
        out_specs=[pl.BlockSpec((tm, tn), lambda i, j, k: (i, j)) for _ in out_dtypes],
        out_shape=[jax.ShapeDtypeStruct((m, n_out), dt) for dt in out_dtypes],
        scratch_shapes=scratch,
        compiler_params=_cparams("parallel", "parallel", "arbitrary"),
        name=name,
    )(*args)
    return tuple(outs) if multi else outs[0]


def _row_extra(vec, tn):
    return (vec.reshape(1, -1), (1, tn), lambda i, j, k: (0, j))


def _rms(x, g):
    return x * lax.rsqrt(jnp.mean(x * x, axis=-1, keepdims=True) + NORM_EPS) * g


def _resid_norm_body(*refs, has_y, has_h):
    pos = 0
    x_ref = refs[pos]; pos += 1
    if has_y:
        y_ref, gpost_ref, gate_ref = refs[pos:pos + 3]; pos += 3
    if has_h:
        gpre_ref, scale_ref, shift_ref = refs[pos:pos + 3]; pos += 3
    outs = refs[pos:]
    x = x_ref[0]
    o = 0
    if has_y:
        y = y_ref[0].astype(F32)
        x = x + (1.0 + gate_ref[0]) * _rms(y, gpost_ref[...])
        outs[o][0] = x
        o += 1
    if has_h:
        h = _rms(x, gpre_ref[...]) * (1.0 + scale_ref[0]) + shift_ref[0]
        outs[o][0] = h.astype(outs[o].dtype)


def resid_norm(x, y=None, post=None, pre=None, tt=512):
    b, t, d = x.shape
    tt = _pick(t, tt)
    row = pl.BlockSpec((1, tt, d), lambda bi, ti: (bi, ti, 0))
    vec = pl.BlockSpec((1, d), lambda bi, ti: (0, 0))
    bvec = pl.BlockSpec((1, 1, d), lambda bi, ti: (bi, 0, 0))
    args, in_specs, out_shapes, out_specs = [x], [row], [], []
    if y is not None:
        args += [y, post[0].reshape(1, d), post[1].reshape(b, 1, d)]
        in_specs += [row, vec, bvec]
        out_shapes.append(jax.ShapeDtypeStruct((b, t, d), F32))
        out_specs.append(row)
    if pre is not None:
        args += [pre[0].reshape(1, d), pre[1].reshape(b, 1, d), pre[2].reshape(b, 1, d)]
        in_specs += [vec, bvec, bvec]
        out_shapes.append(jax.ShapeDtypeStruct((b, t, d), BF16))
        out_specs.append(row)
    res = pl.pallas_call(
        functools.partial(_resid_norm_body, has_y=y is not None, has_h=pre is not None),
        grid=(b, t // tt),
        in_specs=in_specs,
        out_specs=out_specs,
        out_shape=out_shapes,
        compiler_params=_cparams("parallel", "parallel"),
        name="resid_norm",
    )(*args)
    return tuple(res)


def _add_bias(parts, extras):
    return parts[0] + extras[0]


def project(act, ws, bias, name):
    if bias is None:
        return mm(act, ws, out_dtype=BF16, name=name)
    tn = _pick(ws[0][0].shape[1], 1024)
    return mm(act, ws, epi=_add_bias, extras=[_row_extra(bias, tn)], tn=tn, out_dtype=BF16, name=name)


def _mlp_up_body(x_ref, y_ref, gpost_ref, gate_ref, gpre_ref, scale_ref, shift_ref, w_ref, a_ref, xn_ref, h0_ref,
                 h1_ref, *, n_tiles, rows):
    i, j = pl.program_id(0), pl.program_id(1)
    bufs = (h0_ref, h1_ref)

    tm = a_ref.shape[0]
    pieces = 8
    nr, mr = rows // pieces, tm // pieces

    def norm_rows(dst_ref, s):
        sl = pl.ds(s * nr, nr)
        xn = x_ref[sl, :] + (1.0 + gate_ref[0]) * _rms(y_ref[sl, :].astype(F32), gpost_ref[...])
        xn_ref[sl, :] = xn
        h = _rms(xn, gpre_ref[...]) * (1.0 + scale_ref[0]) + shift_ref[0]
        dst_ref[pl.ds(pl.multiple_of(j * rows + s * nr, nr), nr), :] = h.astype(dst_ref.dtype)

    def up_project(src_ref, w, s):
        sl = pl.ds(s * mr, mr)
        r = jnp.maximum(_dot(src_ref[sl, :], w), 0.0)
        a_ref[sl, :] = (r * r).astype(a_ref.dtype)

    @pl.when(i == 0)
    def _():
        for s in range(pieces):
            norm_rows(bufs[0], s)

    for parity in (0, 1):
        @pl.when((i > 0) & (i < n_tiles) & (i % 2 == parity))
        def _():
            w = w_ref[...].astype(BF16)
            for s in range(pieces):
                up_project(bufs[1 - parity], w, s)
                norm_rows(bufs[parity], s)

    @pl.when(i == n_tiles)
    def _():
        w = w_ref[...].astype(BF16)
        for s in range(pieces):
            up_project(bufs[(n_tiles - 1) % 2], w, s)


def mlp_up_resid(x, y, post, pre, w1, layer, tm=2048, tn=512):
    b, t, d = x.shape
    nl, _, f = w1.shape
    m = b * t
    tm = _pick(t, tm)
    tn = _pick(f, tn)
    n_tiles, nj = m // tm, f // tn
    assert tm % nj == 0
    rows = tm // nj
    assert rows % 16 == 0
    tiles_per_batch = t // tm
    slice_idx = lambda i, j: jnp.where(i < n_tiles, i * nj + j, n_tiles * nj - 1)
    rows_spec = pl.BlockSpec((rows, d), lambda i, j: (slice_idx(i, j), 0))
    vec = pl.BlockSpec((1, d), lambda i, j: (0, 0))
    bvec = pl.BlockSpec((1, 1, d), lambda i, j: (jnp.minimum(i, n_tiles - 1) // tiles_per_batch, 0, 0))
    a, x_new = pl.pallas_call(
        functools.partial(_mlp_up_body, n_tiles=n_tiles, rows=rows),
        grid=(n_tiles + 1, nj),
        in_specs=[rows_spec, rows_spec, vec, bvec, vec, bvec, bvec,
                  pl.BlockSpec((d, tn), lambda i, j: (layer, j))],
        out_specs=[pl.BlockSpec((tm, tn), lambda i, j: (jnp.maximum(i - 1, 0), jnp.where(i > 0, j, 0))),
                   rows_spec],
        out_shape=[jax.ShapeDtypeStruct((m, f), BF16), jax.ShapeDtypeStruct((m, d), F32)],
        scratch_shapes=[pltpu.VMEM((tm, d), BF16), pltpu.VMEM((tm, d), BF16)],
        compiler_params=_cparams("arbitrary", "arbitrary"),
        name="mlp_up_resid",
    )(x.reshape(m, d), y.reshape(m, d), post[0].reshape(1, d), post[1].reshape(b, 1, d), pre[0].reshape(1, d),
      pre[1].reshape(b, 1, d), pre[2].reshape(b, 1, d), w1.reshape(nl * d, f))
    return a, x_new.reshape(b, t, d)


def _glu(parts, extras):
    return (parts[0] + extras[0]) * jax.nn.sigmoid(parts[1] + extras[1])


def _conv_ln_body(cur_ref, prev_ref, w_ref, b_ref, g_ref, beta_ref, o_ref, buf_ref, acc_ref, *, tt, d):
    ti = pl.program_id(1)
    halo = prev_ref[0].astype(F32)
    buf_ref[0:CONV_HALO, :] = jnp.where(ti == 0, jnp.zeros_like(halo), halo)
    buf_ref[CONV_HALO:, :] = cur_ref[0].astype(F32)
    lane_chunk = min(d, 256)
    row_chunk = min(tt, 64)
    base = CONV_HALO - (CONV_WIDTH - 1)
    sub = 8
    for c0 in range(0, d, lane_chunk):
        for r0 in range(0, tt, row_chunk):
            acc = None
            for s in range(sub):
                taps = [j for j in range(CONV_WIDTH) if (base + j) % sub == s]
                if not taps:
                    continue
                rows = row_chunk + (sub if s else 0)
                part = jnp.zeros((rows, lane_chunk), F32)
                for j in taps:
                    off = r0 + base + j - s
                    part = part + buf_ref[off:off + rows, c0:c0 + lane_chunk] * w_ref[j:j + 1, c0:c0 + lane_chunk]
                part = part[s:s + row_chunk]
                acc = part if acc is None else acc + part
            acc_ref[r0:r0 + row_chunk, c0:c0 + lane_chunk] = acc
    y = acc_ref[...] + b_ref[...]
    mu = jnp.mean(y, axis=-1, keepdims=True)
    yc = y - mu
    var = jnp.mean(yc * yc, axis=-1, keepdims=True)
    z = yc * lax.rsqrt(var + CONV_LN_EPS) * g_ref[...] + beta_ref[...]
    o_ref[0] = (z * jax.nn.sigmoid(z)).astype(o_ref.dtype)


def conv_ln_silu(u, dw_w, dw_b, ln_g, ln_b, tt=128):
    b, t, d = u.shape
    tt = _pick(t, tt)
    hb = tt // CONV_HALO
    vec = pl.BlockSpec((1, d), lambda bi, ti: (0, 0))
    return pl.pallas_call(
        functools.partial(_conv_ln_body, tt=tt, d=d),
        grid=(b, t // tt),
        in_specs=[
            pl.BlockSpec((1, tt, d), lambda bi, ti: (bi, ti, 0)),
            pl.BlockSpec((1, CONV_HALO, d), lambda bi, ti: (bi, jnp.maximum(ti * hb - 1, 0), 0)),
            pl.BlockSpec((CONV_WIDTH, d), lambda bi, ti: (0, 0)),
            vec, vec, vec,
        ],
        out_specs=pl.BlockSpec((1, tt, d), lambda bi, ti: (bi, ti, 0)),
        out_shape=jax.ShapeDtypeStruct((b, t, d), BF16),
        scratch_shapes=[pltpu.VMEM((tt + CONV_HALO, d), F32), pltpu.VMEM((tt, d), F32)],
        compiler_params=_cparams("parallel", "parallel"),
        name="conv_ln_silu",
    )(u, u, dw_w, dw_b.reshape(1, d), ln_g.reshape(1, d), ln_b.reshape(1, d))


def conformer_conv_mix(h, pw1_w, pw1_b, dw_w, dw_b, ln_g, ln_b, pw2_w, pw2_b):
    b, t, d = h.shape
    tn = _pick(d, 1024)
    b1 = pw1_b.reshape(1, -1)
    u = mm(h.reshape(b * t, d), [(pw1_w, 0), (pw1_w, d // tn)], n_out=d, tm=1024, tn=tn, epi=_glu,
           extras=[(b1, (1, tn), lambda i, j, k: (0, j)),
                   (b1, (1, tn), lambda i, j, k, o=d // tn: (0, j + o))], out_dtype=BF16, name="conf_pw1_glu")
    z = conv_ln_silu(u.reshape(b, t, d), dw_w, dw_b, ln_g, ln_b)
    return z.reshape(b * t, d), [(pw2_w, 0, 0)], pw2_b


def _rope(x, cos, sin):
    half = x.shape[-1] // 2
    x1, x2 = x[:, :half], x[:, half:]
    return jnp.concatenate([x1 * cos - x2 * sin, x2 * cos + x1 * sin], axis=-1)


def _retention_body(q_ref, k_ref, v_ref, gate_ref, cos_ref, sin_ref, inner_ref, qdec_ref, kdec_ref, cdec_ref,
                    gng_ref, gnb_ref, o_ref, state_ref, *, dk):
    @pl.when(pl.program_id(2) == 0)
    def _():
        state_ref[...] = jnp.zeros_like(state_ref)

    cos, sin = cos_ref[...], sin_ref[...]
    q = _rope(q_ref[0].astype(F32), cos, sin)
    k = _rope(k_ref[0].astype(F32), cos, sin) * (dk ** -0.5)
    v = v_ref[0].astype(BF16)
    qb = q.astype(BF16)
    s = _dot_nt(qb, k.astype(BF16)) * inner_ref[0]
    state = state_ref[...]
    o = _dot(s.astype(BF16), v) + _dot(qb, state.astype(BF16)) * qdec_ref[0]
    kd_t = (k * kdec_ref[0]).T.astype(BF16)
    state_ref[...] = state * cdec_ref[0] + _dot(kd_t, v)
    mu = jnp.mean(o, axis=-1, keepdims=True)
    oc = o - mu
    var = jnp.mean(oc * oc, axis=-1, keepdims=True)
    on = oc * lax.rsqrt(var + RET_GN_EPS) * gng_ref[...] + gnb_ref[...]
    gate = gate_ref[0].astype(F32)
    o_ref[0] = (gate * jax.nn.sigmoid(gate) * on).astype(o_ref.dtype)


def retention_mix(h, w_in, gn_g, gn_b, w_out):
    b, t, d = h.shape
    nh, c = RET_HEADS, _pick(t, RET_CHUNK)
    dk = d // nh
    dv = 2 * dk
    proj = mm(h.reshape(b * t, d), w_in, out_dtype=BF16, tm=2048, tn=512, cast_in_kernel=True,
              name="ret_in").reshape(b, t, 6 * d)
    pos = jnp.arange(t, dtype=F32)
    inv_freq = ROPE_BASE ** (-jnp.arange(0, dk, 2, dtype=F32) / dk)
    ang = pos[:, None] * inv_freq[None, :]
    cos, sin = jnp.cos(ang), jnp.sin(ang)
    log_gamma = jnp.log(1.0 - 2.0 ** (-5.0 - jnp.arange(nh, dtype=F32)))
    idx = jnp.arange(c, dtype=F32)
    diff = idx[:, None] - idx[None, :]
    inner = jnp.where(diff >= 0, jnp.exp(jnp.maximum(diff, 0.0)[None] * log_gamma[:, None, None]), 0.0)
    q_dec = jnp.exp((idx + 1.0)[None] * log_gamma[:, None])[:, :, None]
    k_dec = jnp.exp((c - 1.0 - idx)[None] * log_gamma[:, None])[:, :, None]
    c_dec = jnp.exp(c * log_gamma)[:, None, None]
    nq = (nh * dk) // dk
    o = pl.pallas_call(
        functools.partial(_retention_body, dk=dk),
        grid=(b, nh, t // c),
        in_specs=[
            pl.BlockSpec((1, c, dk), lambda bi, hi, ci: (bi, ci, hi)),
            pl.BlockSpec((1, c, dk), lambda bi, hi, ci: (bi, ci, nq + hi)),
            pl.BlockSpec((1, c, dv), lambda bi, hi, ci: (bi, ci, nq + hi)),
            pl.BlockSpec((1, c, dv), lambda bi, hi, ci: (bi, ci, 2 * nq + hi)),
            pl.BlockSpec((c, dk // 2), lambda bi, hi, ci: (ci, 0)),
            pl.BlockSpec((c, dk // 2), lambda bi, hi, ci: (ci, 0)),
            pl.BlockSpec((1, c, c), lambda bi, hi, ci: (hi, 0, 0)),
            pl.BlockSpec((1, c, 1), lambda bi, hi, ci: (hi, 0, 0)),
            pl.BlockSpec((1, c, 1), lambda bi, hi, ci: (hi, 0, 0)),
            pl.BlockSpec((1, 1, 1), lambda bi, hi, ci: (hi, 0, 0)),
            pl.BlockSpec((1, dv), lambda bi, hi, ci: (0, hi)),
            pl.BlockSpec((1, dv), lambda bi, hi, ci: (0, hi)),
        ],
        out_specs=pl.BlockSpec((1, c, dv), lambda bi, hi, ci: (bi, ci, hi)),
        out_shape=jax.ShapeDtypeStruct((b, t, nh * dv), BF16),
        scratch_shapes=[pltpu.VMEM((dk, dv), F32)],
        compiler_params=_cparams("parallel", "parallel", "arbitrary"),
        name="retention",
    )(proj, proj, proj, proj, cos, sin, inner, q_dec, k_dec, c_dec, gn_g.reshape(1, -1), gn_b.reshape(1, -1))
    return o.reshape(b * t, nh * dv), [(w_out, 0, 0)], None


def _rwkv_pre_body(x_ref, prev_ref, g_ref, scale_ref, shift_ref, mu_ref, o_ref):
    ti = pl.program_id(1)
    g, scale, shift = g_ref[...], scale_ref[0], shift_ref[0]
    h = _rms(x_ref[0], g) * (1.0 + scale) + shift
    hp = _rms(prev_ref[0], g) * (1.0 + scale) + shift
    last = jnp.where(ti == 0, 0.0, hp[7:8, :])
    row = lax.broadcasted_iota(jnp.int32, h.shape, 0)
    shifted = jnp.where(row == 0, last, pltpu.roll(h, 1, axis=0))
    xx = shifted - h
    for s in range(6):
        o_ref[s, 0] = (h + xx * mu_ref[s:s + 1, :]).astype(o_ref.dtype)


def rwkv_pre(x, g_pre, scale, shift, mu, tt=256):
    b, t, d = x.shape
    tt = _pick(t, tt)
    bvec = pl.BlockSpec((1, 1, d), lambda bi, ti: (bi, 0, 0))
    return pl.pallas_call(
        _rwkv_pre_body,
        grid=(b, t // tt),
        in_specs=[
            pl.BlockSpec((1, tt, d), lambda bi, ti: (bi, ti, 0)),
            pl.BlockSpec((1, 8, d), lambda bi, ti: (bi, jnp.maximum(ti * (tt // 8) - 1, 0), 0)),
            pl.BlockSpec((1, d), lambda bi, ti: (0, 0)),
            bvec, bvec,
            pl.BlockSpec((6, d), lambda bi, ti: (0, 0)),
        ],
        out_specs=pl.BlockSpec((6, 1, tt, d), lambda bi, ti: (0, bi, ti, 0)),
        out_shape=jax.ShapeDtypeStruct((6, b, t, d), BF16),
        compiler_params=_cparams("parallel", "parallel"),
        name="rwkv_pre",
    )(x, x, g_pre.reshape(1, d), scale.reshape(b, 1, d), shift.reshape(b, 1, d), mu)


def _rwkv_scan_body(r_ref, k_ref, v_ref, a_ref, lw_ref, g_ref, kk_ref, ka_ref, rk_ref, lng_ref, lnb_ref, o_ref,
                    state_ref, *, chunk, heads, n):
    @pl.when(pl.program_id(2) == 0)
    def _():
        state_ref[...] = jnp.zeros_like(state_ref)

    ln = chunk
    hs = range(heads)
    row = lax.broadcasted_iota(jnp.int32, (ln, ln), 0)
    col = lax.broadcasted_iota(jnp.int32, (ln, ln), 1)
    tri = (col <= row).astype(BF16)
    eye = (row == col).astype(F32)
    wid = heads * n
    seg_w = min(wid, 256)
    assert wid % seg_w == 0 and seg_w % n == 0
    seg = (lax.broadcasted_iota(jnp.int32, (seg_w, seg_w), 0) // n
           == lax.broadcasted_iota(jnp.int32, (seg_w, seg_w), 1) // n).astype(BF16)

    def split_dot(m, x):
        hi = x.astype(BF16)
        lo = (x - hi.astype(F32)).astype(BF16)
        return _dot(m, hi) + _dot(m, lo)

    def seg_sum(x):
        hi = x.astype(BF16)
        lo = (x - hi.astype(F32)).astype(BF16)
        return jnp.concatenate([_dot(hi[:, j:j + seg_w], seg) + _dot(lo[:, j:j + seg_w], seg)
                                for j in range(0, wid, seg_w)], axis=1)

    heads_of = lambda x: [x[:, hd * n:(hd + 1) * n] for hd in hs]

    r, k, v, a = (ref[0].astype(F32) for ref in (r_ref, k_ref, v_ref, a_ref))
    lw = lw_ref[0]
    kk = k * kk_ref[...]
    kk = kk / jnp.maximum(jnp.sqrt(seg_sum(kk * kk)), 1e-12)
    k2 = k * (1.0 + (a - 1.0) * ka_ref[...])
    beta = kk * a
    c = split_dot(tri, lw)
    c_last = c[ln - 1:ln, :]
    e_neg = jnp.exp(-c)
    e_rem = jnp.exp(c_last - c)
    e_last = heads_of(jnp.exp(c_last))
    a_t = heads_of(-kk * jnp.exp(c - lw))
    r_t = heads_of(r * jnp.exp(c))
    b_t = heads_of(beta * e_neg)
    k_t = heads_of(k2 * e_neg)
    b_rem = heads_of(beta * e_rem)
    k_rem = heads_of(k2 * e_rem)
    vs = heads_of(v)
    bonus = seg_sum(r * k2 * rk_ref[...]) * v

    ar = [jnp.concatenate([a_t[h], r_t[h]], axis=0).astype(BF16) for h in hs]
    bk = [jnp.concatenate([b_t[h], k_t[h]], axis=0).astype(BF16) for h in hs]
    bk_rem = [jnp.concatenate([b_rem[h], k_rem[h]], axis=0).astype(BF16) for h in hs]
    vb = [vs[h].astype(BF16) for h in hs]
    state = [state_ref[h] for h in hs]
    p = [_dot_nt(ar[h], bk[h]) for h in hs]
    q0 = [_dot_nt(ar[h], state[h].astype(BF16)) for h in hs]
    row2 = lax.broadcasted_iota(jnp.int32, (2 * ln, 2 * ln), 0)
    col2 = lax.broadcasted_iota(jnp.int32, (2 * ln, 2 * ln), 1)
    keep = col2 % ln < row2 % ln + row2 // ln
    pm = [jnp.where(keep, p[h], 0.0) for h in hs]
    m_ab = [pm[h][:ln, :ln] for h in hs]
    m_ak = [pm[h][:ln, ln:].astype(BF16) for h in hs]
    m_r = [pm[h][ln:].astype(BF16) for h in hs]
    rhs = [q0[h][:ln] + _dot(m_ak[h], vb[h]) for h in hs]
    inv = [eye + m_ab[h] for h in hs]
    pw = [m_ab[h].astype(BF16) for h in hs]
    for _ in range(int(math.log2(ln)) - 1):
        pw = [_dot(pw[h], pw[h]).astype(BF16) for h in hs]
        inv = [inv[h] + _dot(inv[h].astype(BF16), pw[h]) for h in hs]
    u = [_dot(inv[h].astype(BF16), rhs[h].astype(BF16)) for h in hs]
    uv = [jnp.concatenate([u[h], vs[h]], axis=0) for h in hs]
    y = [q0[h][ln:] + _dot(m_r[h], uv[h].astype(BF16)) for h in hs]
    for h in hs:
        state_ref[h] = state[h] * e_last[h] + _dot(uv[h].T.astype(BF16), bk_rem[h])
    yn = []
    for h in hs:
        mu = jnp.mean(y[h], axis=-1, keepdims=True)
        yc = y[h] - mu
        var = jnp.mean(yc * yc, axis=-1, keepdims=True)
        yn.append(yc * lax.rsqrt(var + RWKV_GN_EPS))
    yn = jnp.concatenate(yn, axis=-1) * lng_ref[...] + lnb_ref[...]
    o_ref[0] = ((yn + bonus) * g_ref[0]).astype(o_ref.dtype)


def rwkv_scan(r, k, v, a, lw, g, k_k, k_a, r_k, ln_g, ln_b):
    b, t, d = r.shape
    n = RWKV_HEAD_DIM
    hps = min(RWKV_HEADS_PER_STEP, d // n)
    w = hps * n
    ln = _pick(t, RWKV_CHUNK)
    tok = pl.BlockSpec((1, ln, w), lambda bi, hi, ci: (bi, ci, hi))
    vec = pl.BlockSpec((1, w), lambda bi, hi, ci: (0, hi))
    return pl.pallas_call(
        functools.partial(_rwkv_scan_body, chunk=ln, heads=hps, n=n),
        grid=(b, d // w, t // ln),
        in_specs=[tok] * 6 + [vec] * 5,
        out_specs=tok,
        out_shape=jax.ShapeDtypeStruct((b, t, d), BF16),
        scratch_shapes=[pltpu.VMEM((hps, n, n), F32)],
        compiler_params=_cparams("parallel", "parallel", "arbitrary"),
        name="rwkv_scan",
    )(r, k, v, a, lw, g, k_k.reshape(1, d), k_a.reshape(1, d), r_k.reshape(1, d), ln_g.reshape(1, d),
      ln_b.reshape(1, d))


def _tanh_epi(parts, extras):
    return jnp.tanh(parts[0])


def _sigmoid_epi(parts, extras):
    return jax.nn.sigmoid(parts[0])


def _sigmoid_bias_epi(parts, extras):
    return jax.nn.sigmoid(extras[0] + parts[0])


def _logdecay_epi(parts, extras):
    return -jnp.exp(-jax.nn.softplus(-(extras[0] + parts[0])) - 0.5)


def rwkv7_time_mix(x, g_pre, scale, shift, mu, w_rkv, w0, w_la, w_lb, a0, a_la, a_lb, g_la, g_lb, k_k, k_a, r_k,
                   ln_g, ln_b, w_out):
    b, t, d = x.shape
    xs = rwkv_pre(x, g_pre, scale, shift, mu).reshape(6 * b * t, d)
    tn = _pick(d, 1024)
    w_rkv2d = w_rkv.reshape(3 * d, d)
    r = mm(xs, [(w_rkv2d, 0, 0)], a_part=(0, 6), out_dtype=BF16, name="rwkv_r")
    k = mm(xs, [(w_rkv2d, 0, 1)], a_part=(1, 6), out_dtype=BF16, name="rwkv_k")
    v = mm(xs, [(w_rkv2d, 0, 2)], a_part=(2, 6), out_dtype=BF16, name="rwkv_v")
    lw = mm(mm(xs, w_la, a_part=(3, 6), epi=_tanh_epi, out_dtype=BF16, name="rwkv_w_la"), w_lb, epi=_logdecay_epi,
            extras=[_row_extra(w0, tn)], name="rwkv_w_lb")
    a = mm(mm(xs, a_la, a_part=(4, 6), out_dtype=BF16, name="rwkv_a_la"), a_lb, epi=_sigmoid_bias_epi,
           extras=[_row_extra(a0, tn)], out_dtype=BF16, name="rwkv_a_lb")
    g = mm(mm(xs, g_la, a_part=(5, 6), epi=_sigmoid_epi, out_dtype=BF16, name="rwkv_g_la"), g_lb, out_dtype=BF16,
           name="rwkv_g_lb")
    sh = lambda z: z.reshape(b, t, d)
    z = rwkv_scan(sh(r), sh(k), sh(v), sh(a), sh(lw), sh(g), k_k, k_a, r_k, ln_g, ln_b)
    return z.reshape(b * t, d), [(w_out, 0, 0)], None


def _nsa_compress_body(kv_ref, pe_ref, w1_ref, w2_ref, o_ref, buf_ref, *, nchunk, dh):
    st = NSA_CMP_STRIDE
    hid = w1_ref.shape[1]
    buf_ref[...] = kv_ref[0].astype(F32)
    first = jnp.zeros((nchunk, hid), F32)
    second = jnp.zeros((nchunk, hid), F32)
    for j in range(st):
        xj = buf_ref[pl.ds(j, nchunk, stride=st), :]
        first = first + _dot((xj + pe_ref[j:j + 1, :]).astype(BF16), w1_ref[j * dh:(j + 1) * dh, :])
        second = second + _dot((xj + pe_ref[st + j:st + j + 1, :]).astype(BF16),
                               w1_ref[(st + j) * dh:(st + j + 1) * dh, :])
    hidden = jax.nn.gelu(first + pltpu.roll(second, nchunk - 1, axis=0))
    out = _dot(hidden.astype(BF16), w2_ref[...])
    row = lax.broadcasted_iota(jnp.int32, out.shape, 0)
    o_ref[0, 0] = jnp.where(row < nchunk - 1, out, 0.0)


def nsa_compress(proj, col0, pe, w1, w2):
    b, t, _ = proj.shape
    g, dh = NSA_KV_GROUPS, NSA_HEAD_DIM
    nchunk = t // NSA_CMP_STRIDE
    hid = w1.shape[1]
    return pl.pallas_call(
        functools.partial(_nsa_compress_body, nchunk=nchunk, dh=dh),
        grid=(b, g),
        in_specs=[
            pl.BlockSpec((1, t, dh), lambda bi, gi: (bi, 0, col0 // dh + gi)),
            pl.BlockSpec((NSA_CMP_BLOCK, dh), lambda bi, gi: (0, 0)),
            pl.BlockSpec((NSA_CMP_BLOCK * dh, hid), lambda bi, gi: (0, 0)),
            pl.BlockSpec((hid, dh), lambda bi, gi: (0, 0)),
        ],
        out_specs=pl.BlockSpec((1, 1, nchunk, dh), lambda bi, gi: (bi, gi, 0, 0)),
        out_shape=jax.ShapeDtypeStruct((b, g, nchunk, dh), F32),
        scratch_shapes=[pltpu.VMEM((t, dh), F32)],
        compiler_params=_cparams("parallel", "parallel"),
        name="nsa_compress",
    )(proj, pe, w1.astype(BF16), w2.astype(BF16))


def _stack_heads(q, r, dh):
    return jnp.concatenate([q[:, i * dh:(i + 1) * dh] for i in range(r)], axis=0)


def _unstack_heads(o, r, tq):
    return jnp.concatenate([o[i * tq:(i + 1) * tq] for i in range(r)], axis=-1)


def _gate_rows(gates, branch, r, tq):
    return jnp.concatenate([gates[:, branch * r + i:branch * r + i + 1] for i in range(r)], axis=0)


def _nsa_cmp_body(q_ref, kc_ref, vc_ref, gates_ref, selmap_t_ref, o_ref, sel_ref, *, tq, r, dh, n_sel):
    q0 = pl.program_id(2) * tq
    qs = (_stack_heads(q_ref[0].astype(F32), r, dh) * (dh ** -0.5)).astype(BF16)
    kc, vc = kc_ref[0, 0].astype(BF16), vc_ref[0, 0].astype(BF16)
    ncp = kc.shape[0]
    s = _dot_nt(qs, kc)
    rows = lax.broadcasted_iota(jnp.int32, (r * tq, ncp), 0)
    cmp_end = lax.broadcasted_iota(jnp.int32, (r * tq, ncp), 1) * NSA_CMP_STRIDE + (NSA_CMP_BLOCK - 1)
    t_pos = q0 + rows % tq
    vis = cmp_end <= t_pos
    s = jnp.where(vis, s, NEG_INF)
    e = jnp.exp(s - jnp.max(s, axis=-1, keepdims=True))
    p = e / jnp.sum(e, axis=-1, keepdims=True) * vis.astype(F32)
    o = _dot(p.astype(BF16), vc)
    o = o * _gate_rows(gates_ref[0, 0], 0, r, tq)
    o_ref[0] = _unstack_heads(o, r, tq)
    p_sum = p[0:tq]
    for i in range(1, r):
        p_sum = p_sum + p[i * tq:(i + 1) * tq]
    hi = p_sum.astype(BF16)
    lo = (p_sum - hi.astype(F32)).astype(BF16)
    selmap_t = selmap_t_ref[...]
    imp = _dot_nt(selmap_t, hi) + _dot_nt(selmap_t, lo)
    ns = imp.shape[0]
    blk = lax.broadcasted_iota(jnp.int32, (ns, tq), 0)
    tq_pos = q0 + lax.broadcasted_iota(jnp.int32, (ns, tq), 1)
    cur = tq_pos // NSA_SEL_BLOCK
    valid = blk * NSA_SEL_BLOCK <= tq_pos
    forced = (blk == 0) | (blk == cur) | (blk == cur - 1)
    score = jnp.where(valid, jnp.where(forced, POS_BIG, imp), NEG_INF)
    rank = jnp.zeros((ns, tq), F32)
    for m in range(ns):
        sm = score[m:m + 1, :]
        beats = (sm > score) | ((sm == score) & (blk > m))
        rank = rank + beats.astype(F32)
    chosen = ((rank < n_sel) & valid).astype(F32)
    sel_ref[0, 0] = (chosen.T - 1.0).astype(sel_ref.dtype)


def _flash_update(s, v_ones, m_ref, acc_ref):
    lanes = m_ref.shape[1]
    assert s.shape[1] % lanes == 0 and acc_ref.shape[1] == 2 * lanes
    m_old = m_ref[...]
    m_new = jnp.maximum(m_old, jnp.max(s, axis=-1, keepdims=True))
    alpha = jnp.exp2(m_old - m_new)
    p = jnp.exp2((s - jnp.concatenate([m_new] * (s.shape[1] // lanes), axis=1)).astype(BF16))
    acc_ref[...] = jnp.concatenate([alpha, alpha], axis=1) * acc_ref[...] + _dot(p, v_ones)
    m_ref[...] = m_new


def _nsa_sel_body(qi_tbl, ki_tbl, diag_tbl, q_ref, k_ref, v_ref, gates_ref, oin_ref, sel_ref, o_ref, qa_ref, m_ref,
                  acc_ref, *, tq, tk, r, dh):
    step = pl.program_id(2)
    qi, ki, diag = qi_tbl[step], ki_tbl[step], diag_tbl[step]
    ns = sel_ref.shape[-1]
    assert ns <= dh

    @pl.when(ki == 0)
    def _():
        m_ref[...] = jnp.full_like(m_ref, NEG_INF)
        acc_ref[...] = jnp.zeros_like(acc_ref)
        qs = _stack_heads(q_ref[0].astype(F32), r, dh) * (dh ** -0.5 * LOG2_E)
        pieces = [qs, jnp.concatenate([sel_ref[0, 0].astype(F32)] * r, axis=0)]
        if ns < dh:
            pieces.append(jnp.zeros((r * tq, dh - ns), F32))
        qa_ref[...] = jnp.concatenate(pieces, axis=1).astype(BF16)

    blk_of_key = (ki * tk + lax.broadcasted_iota(jnp.int32, (tk, dh), 0)) // NSA_SEL_BLOCK
    own_block = jnp.where(blk_of_key == lax.broadcasted_iota(jnp.int32, (tk, dh), 1), POS_BIG, 0.0).astype(BF16)
    ka = jnp.concatenate([k_ref[0].astype(BF16), own_block], axis=1)
    s = _dot_nt(qa_ref[...], ka)
    v_ones = jnp.concatenate([v_ref[0].astype(BF16), jnp.ones((tk, dh), BF16)], axis=1)

    @pl.when(diag == 0)
    def _():
        _flash_update(s, v_ones, m_ref, acc_ref)

    @pl.when(diag == 1)
    def _():
        t_pos = qi * tq + lax.broadcasted_iota(jnp.int32, (r * tq, tk), 0) % tq
        k_pos = ki * tk + lax.broadcasted_iota(jnp.int32, (r * tq, tk), 1)
        _flash_update(jnp.where(k_pos <= t_pos, s, NEG_INF), v_ones, m_ref, acc_ref)
        o = acc_ref[:, :dh] / acc_ref[:, dh:] * _gate_rows(gates_ref[0, 0], 1, r, tq)
        o_ref[0] = (oin_ref[0] + _unstack_heads(o, r, tq)).astype(o_ref.dtype)


def _nsa_win_body(q_ref, *rest, tq, r, dh, nwin):
    k_refs, v_refs = rest[:nwin], rest[nwin:2 * nwin]
    gates_ref, oin_ref, o_ref = rest[2 * nwin:]
    qi = pl.program_id(2)
    qs = (_stack_heads(q_ref[0].astype(F32), r, dh) * (dh ** -0.5 * LOG2_E)).astype(BF16)
    t_loc = lax.broadcasted_iota(jnp.int32, (r * tq, tq), 0) % tq
    k_loc = lax.broadcasted_iota(jnp.int32, (r * tq, tq), 1)
    tiles = []
    for j in range(nwin):
        s = _dot_nt(qs, k_refs[j][0].astype(BF16))
        if j == nwin - 1:
            s = jnp.where(k_loc <= t_loc, s, NEG_INF)
        else:
            if j == 0:
                s = jnp.where(k_loc > t_loc, s, NEG_INF)
            s = jnp.where(qi - (nwin - 1) + j >= 0, s, NEG_INF)
        tiles.append(s)
    s = jnp.concatenate(tiles, axis=1)
    p = jnp.exp2((s - jnp.max(s, axis=-1, keepdims=True)).astype(BF16))
    v = jnp.concatenate([v_refs[j][0] for j in range(nwin)], axis=0).astype(BF16)
    on = _dot(p, jnp.concatenate([v, jnp.ones_like(v)], axis=1))
    o = on[:, :dh] / on[:, dh:] * _gate_rows(gates_ref[0, 0], 2, r, tq)
    o_ref[0] = (oin_ref[0] + _unstack_heads(o, r, tq)).astype(o_ref.dtype)


def nsa_mix(h, w_in, pe_k, pe_v, ck_w1, ck_w2, cv_w1, cv_w2, w_out, tq=256):
    b, t, d = h.shape
    nh, g, dh = NSA_HEADS, NSA_KV_GROUPS, NSA_HEAD_DIM
    r = nh // g
    hd, kd = nh * dh, g * dh
    n_main = hd + 6 * kd
    h2 = h.reshape(b * t, d)
    proj = mm(h2, w_in[:, :n_main], out_dtype=BF16, name="nsa_in").reshape(b, t, n_main)
    n_gate = w_in.shape[1] - n_main
    w_gate = jnp.pad(w_in[:, n_main:], ((0, 0), (0, 128 - n_gate)))
    gates = mm(h2, w_gate, epi=_sigmoid_epi, name="nsa_gates")[:, :n_gate]
    gates = jnp.transpose(gates.reshape(b, t, 3, g, r), (0, 3, 1, 2, 4)).reshape(b, g, t, 3 * r)
    k_cmp = nsa_compress(proj, hd, pe_k, ck_w1, ck_w2)
    v_cmp = nsa_compress(proj, hd + kd, pe_v, cv_w1, cv_w2)
    ncp = k_cmp.shape[2]
    ns = t // NSA_SEL_BLOCK
    n_sel = min(NSA_SEL_TOP, ns)
    cs = jnp.arange(ncp)[None, :] * NSA_CMP_STRIDE
    ss = jnp.arange(ns)[:, None] * NSA_SEL_BLOCK
    sel_map_t = (jnp.maximum(jnp.minimum(cs + NSA_CMP_BLOCK, ss + NSA_SEL_BLOCK) - jnp.maximum(cs, ss), 0)
                 .astype(F32) / NSA_CMP_BLOCK).astype(BF16)
    tq = _pick(t, tq)
    assert NSA_WINDOW % tq == 0
    nq = t // tq
    wq = r * dh
    col = lambda base: base // dh
    q_spec3 = pl.BlockSpec((1, tq, wq), lambda bi, gi, qi: (bi, qi, gi))
    gate_spec3 = pl.BlockSpec((1, 1, tq, 3 * r), lambda bi, gi, qi: (bi, gi, qi, 0))
    o_cmp, sel = pl.pallas_call(
        functools.partial(_nsa_cmp_body, tq=tq, r=r, dh=dh, n_sel=n_sel),
        grid=(b, g, nq),
        in_specs=[
            q_spec3,
            pl.BlockSpec((1, 1, ncp, dh), lambda bi, gi, qi: (bi, gi, 0, 0)),
            pl.BlockSpec((1, 1, ncp, dh), lambda bi, gi, qi: (bi, gi, 0, 0)),
            gate_spec3,
            pl.BlockSpec((ns, ncp), lambda bi, gi, qi: (0, 0)),
        ],
        out_specs=[q_spec3, pl.BlockSpec((1, 1, tq, ns), lambda bi, gi, qi: (bi, gi, qi, 0))],
        out_shape=[jax.ShapeDtypeStruct((b, t, hd), F32), jax.ShapeDtypeStruct((b, g, t, ns), BF16)],
        compiler_params=_cparams("parallel", "parallel", "parallel"),
        name="nsa_cmp",
    )(proj, k_cmp, v_cmp, gates, sel_map_t)

    tk = _pick(t, NSA_SEL_KEY_TILE)
    assert tk % tq == 0
    pairs = [(qi, ki) for qi in range(nq) for ki in range((qi * tq + tq - 1) // tk + 1)]
    qi_tbl = jnp.asarray([pq for pq, _ in pairs], jnp.int32)
    ki_tbl = jnp.asarray([pk for _, pk in pairs], jnp.int32)
    diag_tbl = jnp.asarray([int(pk == (pq * tq + tq - 1) // tk) for pq, pk in pairs], jnp.int32)
    q_spec_p = pl.BlockSpec((1, tq, wq), lambda bi, gi, p, qt, kt, dt: (bi, qt[p], gi))

    def kv_spec_p(base):
        return pl.BlockSpec((1, tk, dh), lambda bi, gi, p, qt, kt, dt: (bi, kt[p], col(base) + gi))

    o_sel = pl.pallas_call(
        functools.partial(_nsa_sel_body, tq=tq, tk=tk, r=r, dh=dh),
        grid_spec=pltpu.PrefetchScalarGridSpec(
            num_scalar_prefetch=3,
            grid=(b, g, len(pairs)),
            in_specs=[q_spec_p, kv_spec_p(hd + 2 * kd), kv_spec_p(hd + 3 * kd),
                      pl.BlockSpec((1, 1, tq, 3 * r), lambda bi, gi, p, qt, kt, dt: (bi, gi, qt[p], 0)),
                      q_spec_p,
                      pl.BlockSpec((1, 1, tq, ns), lambda bi, gi, p, qt, kt, dt: (bi, gi, qt[p], 0))],
            out_specs=q_spec_p,
            scratch_shapes=[pltpu.VMEM((r * tq, 2 * dh), BF16), pltpu.VMEM((r * tq, dh), F32),
                            pltpu.VMEM((r * tq, 2 * dh), F32)],
        ),
        out_shape=jax.ShapeDtypeStruct((b, t, hd), F32),
        compiler_params=_cparams("parallel", "parallel", "arbitrary"),
        name="nsa_sel",
    )(qi_tbl, ki_tbl, diag_tbl, proj, proj, proj, gates, o_cmp, sel)

    nwin = NSA_WINDOW // tq + 1

    def kv_spec_w(base, j):
        return pl.BlockSpec((1, tq, dh),
                            lambda bi, gi, qi: (bi, jnp.maximum(qi - (nwin - 1) + j, 0), col(base) + gi))

    o_all = pl.pallas_call(
        functools.partial(_nsa_win_body, tq=tq, r=r, dh=dh, nwin=nwin),
        grid=(b, g, nq),
        in_specs=([q_spec3] + [kv_spec_w(hd + 4 * kd, j) for j in range(nwin)]
                  + [kv_spec_w(hd + 5 * kd, j) for j in range(nwin)] + [gate_spec3, q_spec3]),
        out_specs=q_spec3,
        out_shape=jax.ShapeDtypeStruct((b, t, hd), BF16),
        compiler_params=_cparams("parallel", "parallel", "parallel"),
        name="nsa_win",
    )(proj, *([proj] * (2 * nwin)), gates, o_sel)
    return o_all.reshape(b * t, hd), [(w_out, 0, 0)], None


def _silu(a):
    return a * jax.nn.sigmoid(a)


def _ada_epi(parts, extras):
    return parts[0] + extras[0]


def ada_modulation(c, ada_w, ada_b):
    depth, d, n6 = ada_w.shape
    b = c.shape[0]
    rows = ((b + 7) // 8) * 8
    cond = jnp.pad(c, ((0, rows - b), (0, 0)))
    w2d = ada_w.reshape(depth * d, n6)
    mods = []
    for i in range(depth):
        mods.append(mm(cond, [(w2d, 0, i)], a_act=_silu, epi=_ada_epi, extras=[_row_extra(ada_b[i], _pick(n6, 1024))],
                       precision=HIGHEST, name="ada_mod")[:b])
    return jnp.stack(mods)


def kernel(x, c, ada_w, ada_b, norm_g, mlp_w1, mlp_w2, rwkv_mu, rwkv_w_rkv, rwkv_w0, rwkv_w_la, rwkv_w_lb, rwkv_a0, rwkv_a_la, rwkv_a_lb, rwkv_g_la, rwkv_g_lb, rwkv_k_k, rwkv_k_a, rwkv_r_k, rwkv_ln_g, rwkv_ln_b, rwkv_w_out, ret_w_in, ret_gn_g, ret_gn_b, ret_w_out, conv_pw1_w, conv_pw1_b, conv_dw_w, conv_dw_b, conv_ln_g, conv_ln_b, conv_pw2_w, conv_pw2_b, nsa_w_in, nsa_pe_k, nsa_pe_v, nsa_ck_w1, nsa_ck_w2, nsa_cv_w1, nsa_cv_w2, nsa_w_out):
    b, t, d = x.shape
    depth = ada_w.shape[0]
    mod = ada_modulation(c, ada_w, ada_b).reshape(depth, b, 6, d)
    h = None
    for i in range(depth):
        sh_t, sc_t, gt_t, sh_c, sc_c, gt_c = (mod[i, :, j] for j in range(6))
        kind = i % 4
        if kind == 0:
            act, ws, bias = rwkv7_time_mix(x, norm_g[i, 0], sc_t, sh_t, rwkv_mu, rwkv_w_rkv, rwkv_w0, rwkv_w_la,
                                           rwkv_w_lb, rwkv_a0, rwkv_a_la, rwkv_a_lb, rwkv_g_la, rwkv_g_lb, rwkv_k_k,
                                           rwkv_k_a, rwkv_r_k.reshape(-1), rwkv_ln_g, rwkv_ln_b, rwkv_w_out)
        else:
            if h is None:
                (h,) = resid_norm(x, pre=(norm_g[i, 0], sc_t, sh_t))
            if kind == 1:
                act, ws, bias = retention_mix(h, ret_w_in, ret_gn_g, ret_gn_b, ret_w_out)
            elif kind == 2:
                act, ws, bias = conformer_conv_mix(h, conv_pw1_w, conv_pw1_b, conv_dw_w, conv_dw_b, conv_ln_g,
                                                   conv_ln_b, conv_pw2_w, conv_pw2_b)
            else:
                act, ws, bias = nsa_mix(h, nsa_w_in, nsa_pe_k, nsa_pe_v, nsa_ck_w1, nsa_ck_w2, nsa_cv_w1, nsa_cv_w2,
                                        nsa_w_out)
        y = project(act, ws, bias, "mixer_out").reshape(b, t, d)
        act, x = mlp_up_resid(x, y, (norm_g[i, 1], gt_t), (norm_g[i, 2], sc_c, sh_c), mlp_w1, i)
        y = project(act, [(mlp_w2.reshape(-1, d), 0, i)], None, "mlp_down").reshape(b, t, d)
        nxt = i + 1
        if nxt < depth and nxt % 4 != 0:
            x, h = resid_norm(x, y, post=(norm_g[i, 3], gt_c),
                              pre=(norm_g[nxt, 0], mod[nxt, :, 1], mod[nxt, :, 0]))
        else:
            (x,) = resid_norm(x, y, post=(norm_g[i, 3], gt_c))
            h = None
    return x
```

```python
import functools
import math

import jax
import jax.numpy as jnp
from jax import lax
from jax.experimental import pallas as pl
from jax.experimental.pallas import tpu as pltpu

F32 = jnp.float32
BF16 = jnp.bfloat16
HIGHEST = lax.Precision.HIGHEST

NORM_EPS = 1e-6
NEG_INF = -1e30
POS_BIG = 1e30
LOG2_E = math.log2(math.e)

RWKV_HEAD_DIM = 64
RWKV_GN_EPS = 64e-5
RWKV_CHUNK = 64
RWKV_HEADS_PER_STEP = 32

RET_HEADS = 8
RET_CHUNK = 256
RET_GN_EPS = 1e-5
ROPE_BASE = 10000.0

CONV_WIDTH = 31
CONV_HALO = 32
CONV_LN_EPS = 1e-5

NSA_HEADS = 16
NSA_KV_GROUPS = 4
NSA_HEAD_DIM = 128
NSA_CMP_BLOCK = 32
NSA_CMP_STRIDE = 16
NSA_SEL_BLOCK = 64
NSA_SEL_TOP = 16
NSA_WINDOW = 512
NSA_SEL_KEY_TILE = 512

VMEM_LIMIT_BYTES = 56 * 1024 * 1024


def _cparams(*sem):
    return pltpu.CompilerParams(dimension_semantics=sem, vmem_limit_bytes=VMEM_LIMIT_BYTES)


def _dot(a, b, precision=None):
    return jnp.dot(a, b, preferred_element_type=F32, precision=precision)


def _dot_nt(a, b, precision=None):
    return lax.dot_general(a, b, (((1,), (1,)), ((), ())), preferred_element_type=F32, precision=precision)


def _pick(n, pref):
    if n <= pref:
        return n
    t = pref
    while n % t:
        t //= 2
    return t


def _mm_body(*refs, n_w, n_ex, n_out, nk, epi, precision, has_a_add, a_act):
    a_ref = refs[0]
    pos = 1
    a_add_ref = None
    if has_a_add:
        a_add_ref = refs[pos]
        pos += 1
    w_refs = refs[pos:pos + n_w]
    pos += n_w
    ex_refs = refs[pos:pos + n_ex]
    pos += n_ex
    o_refs = refs[pos:pos + n_out]
    acc_refs = refs[pos + n_out:]

    def finish(parts):
        res = epi(parts, [e[...] for e in ex_refs])
        if n_out == 1 and not isinstance(res, (tuple, list)):
            res = (res,)
        for o_ref, val in zip(o_refs, res, strict=True):
            o_ref[...] = val.astype(o_ref.dtype)

    a = a_ref[...]
    if a_add_ref is not None:
        a = a.astype(F32) + a_add_ref[...]
    if a_act is not None:
        a = a_act(a)
    if precision is None:
        a = a.astype(BF16)
    parts = []
    for w_ref in w_refs:
        w = w_ref[...]
        if precision is None:
            w = w.astype(BF16)
        parts.append(_dot(a, w, precision))

    if nk == 1:
        finish(parts)
    else:
        k = pl.program_id(2)

        @pl.when(k == 0)
        def _():
            for acc, p in zip(acc_refs, parts):
                acc[...] = p

        @pl.when(k > 0)
        def _():
            for acc, p in zip(acc_refs, parts):
                acc[...] += p

        @pl.when(k == nk - 1)
        def _():
            finish([acc[...] for acc in acc_refs])


def _first(parts, extras):
    return parts[0]


def mm(a, ws, *, n_out=None, extras=(), epi=_first, out_dtype=F32, tm=2048, tn=1024, tk=2048, precision=None,
       a_add=None, a_act=None, a_part=(0, 1), cast_in_kernel=False, name="mm"):
    m, kdim = a.shape
    a_s, a_parts = a_part
    assert m % a_parts == 0
    m //= a_parts
    if not isinstance(ws, (list, tuple)):
        ws = [(ws, 0, 0)]
    ws = [tuple(w) + (0,) * (3 - len(w)) for w in ws]
    if n_out is None:
        n_out = ws[0][0].shape[1]
    tm = _pick(m, tm)
    tn = _pick(n_out, tn)
    tk = _pick(kdim, tk)
    nk = kdim // tk
    assert m % tm == 0 and n_out % tn == 0 and kdim % tk == 0
    in_specs = [pl.BlockSpec((tm, tk), lambda i, j, k, o=a_s * (m // tm): (i + o, k))]
    args = [a]
    if a_add is not None:
        in_specs.append(pl.BlockSpec((1, tk), lambda i, j, k: (0, k)))
        args.append(a_add)
    for w, off, koff in ws:
        in_specs.append(pl.BlockSpec((tk, tn), lambda i, j, k, off=off, ko=koff * nk: (k + ko, j + off)))
        args.append(w.astype(BF16) if precision is None and not cast_in_kernel else w)
    for arr, bshape, imap in extras:
        in_specs.append(pl.BlockSpec(bshape, imap))
        args.append(arr)
    scratch = [pltpu.VMEM((tm, tn), F32) for _ in ws] if nk > 1 else []
    multi = isinstance(out_dtype, (tuple, list))
    out_dtypes = tuple(out_dtype) if multi else (out_dtype,)
    body = functools.partial(_mm_body, n_w=len(ws), n_ex=len(extras), n_out=len(out_dtypes), nk=nk, epi=epi,
                             precision=precision, has_a_add=a_add is not None, a_act=a_act)
    outs = pl.pallas_call(
        body,
        grid=(m // tm, n_out // tn, nk),
        in_specs=in_specs,
        out_specs=[pl.BlockSpec((tm, tn), lambda i, j, k: (i, j)) for _ in out_dtypes],
        out_shape=[jax.ShapeDtypeStruct((m, n_out), dt) for dt in out_dtypes],
        scratch_shapes=scratch,
        compiler_params=_cparams("parallel", "parallel", "arbitrary"),
        name=name,
    )(*args)
    return tuple(outs) if multi else outs[0]


def _row_extra(vec, tn):
    return (vec.reshape(1, -1), (1, tn), lambda i, j, k: (0, j))


def _rms(x, g):
    return x * lax.rsqrt(jnp.mean(x * x, axis=-1, keepdims=True) + NORM_EPS) * g


def _resid_norm_body(*refs, has_y, has_h):
    pos = 0
    x_ref = refs[pos]; pos += 1
    if has_y:
        y_ref, gpost_ref, gate_ref = refs[pos:pos + 3]; pos += 3
    if has_h:
        gpre_ref, scale_ref, shift_ref = refs[pos:pos + 3]; pos += 3
    outs = refs[pos:]
    x = x_ref[0]
    o = 0
    if has_y:
        y = y_ref[0].astype(F32)
        x = x + (1.0 + gate_ref[0]) * _rms(y, gpost_ref[...])
        outs[o][0] = x
        o += 1
    if has_h:
        h = _rms(x, gpre_ref[...]) * (1.0 + scale_ref[0]) + shift_ref[0]
        outs[o][0] = h.astype(outs[o].dtype)


def resid_norm(x, y=None, post=None, pre=None, tt=512):
    b, t, d = x.shape
    tt = _pick(t, tt)
    row = pl.BlockSpec((1, tt, d), lambda bi, ti: (bi, ti, 0))
    vec = pl.BlockSpec((1, d), lambda bi, ti: (0, 0))
    bvec = pl.BlockSpec((1, 1, d), lambda bi, ti: (bi, 0, 0))
    args, in_specs, out_shapes, out_specs = [x], [row], [], []
    if y is not None:
        args += [y, post[0].reshape(1, d), post[1].reshape(b, 1, d)]
        in_specs += [row, vec, bvec]
        out_shapes.append(jax.ShapeDtypeStruct((b, t, d), F32))
        out_specs.append(row)
    if pre is not None:
        args += [pre[0].reshape(1, d), pre[1].reshape(b, 1, d), pre[2].reshape(b, 1, d)]
        in_specs += [vec, bvec, bvec]
        out_shapes.append(jax.ShapeDtypeStruct((b, t, d), BF16))
        out_specs.append(row)
    res = pl.pallas_call(
        functools.partial(_resid_norm_body, has_y=y is not None, has_h=pre is not None),
        grid=(b, t // tt),
        in_specs=in_specs,
        out_specs=out_specs,
        out_shape=out_shapes,
        compiler_params=_cparams("parallel", "parallel"),
        name="resid_norm",
    )(*args)
    return tuple(res)


def _add_bias(parts, extras):
    return parts[0] + extras[0]


def project(act, ws, bias, name):
    if bias is None:
        return mm(act, ws, out_dtype=BF16, name=name)
    tn = _pick(ws[0][0].shape[1], 1024)
    return mm(act, ws, epi=_add_bias, extras=[_row_extra(bias, tn)], tn=tn, out_dtype=BF16, name=name)


def _mlp_up_body(x_ref, y_ref, gpost_ref, gate_ref, gpre_ref, scale_ref, shift_ref, w_ref, a_ref, xn_ref, h0_ref,
                 h1_ref, *, n_tiles, rows):
    i, j = pl.program_id(0), pl.program_id(1)
    bufs = (h0_ref, h1_ref)

    tm = a_ref.shape[0]
    pieces = 8
    nr, mr = rows // pieces, tm // pieces

    def norm_rows(dst_ref, s):
        sl = pl.ds(s * nr, nr)
        xn = x_ref[sl, :] + (1.0 + gate_ref[0]) * _rms(y_ref[sl, :].astype(F32), gpost_ref[...])
        xn_ref[sl, :] = xn
        h = _rms(xn, gpre_ref[...]) * (1.0 + scale_ref[0]) + shift_ref[0]
        dst_ref[pl.ds(pl.multiple_of(j * rows + s * nr, nr), nr), :] = h.astype(dst_ref.dtype)

    def up_project(src_ref, w, s):
        sl = pl.ds(s * mr, mr)
        r = jnp.maximum(_dot(src_ref[sl, :], w), 0.0)
        a_ref[sl, :] = (r * r).astype(a_ref.dtype)

    @pl.when(i == 0)
    def _():
        for s in range(pieces):
            norm_rows(bufs[0], s)

    for parity in (0, 1):
        @pl.when((i > 0) & (i < n_tiles) & (i % 2 == parity))
        def _():
            w = w_ref[...].astype(BF16)
            for s in range(pieces):
                up_project(bufs[1 - parity], w, s)
                norm_rows(bufs[parity], s)

    @pl.when(i == n_tiles)
    def _():
        w = w_ref[...].astype(BF16)
        for s in range(pieces):
            up_project(bufs[(n_tiles - 1) % 2], w, s)


def mlp_up_resid(x, y, post, pre, w1, layer, tm=2048, tn=512):
    b, t, d = x.shape
    nl, _, f = w1.shape
    m = b * t
    tm = _pick(t, tm)
    tn = _pick(f, tn)
    n_tiles, nj = m // tm, f // tn
    assert tm % nj == 0
    rows = tm // nj
    assert rows % 16 == 0
    tiles_per_batch = t // tm
    slice_idx = lambda i, j: jnp.where(i < n_tiles, i * nj + j, n_tiles * nj - 1)
    rows_spec = pl.BlockSpec((rows, d), lambda i, j: (slice_idx(i, j), 0))
    vec = pl.BlockSpec((1, d), lambda i, j: (0, 0))
    bvec = pl.BlockSpec((1, 1, d), lambda i, j: (jnp.minimum(i, n_tiles - 1) // tiles_per_batch, 0, 0))
    a, x_new = pl.pallas_call(
        functools.partial(_mlp_up_body, n_tiles=n_tiles, rows=rows),
        grid=(n_tiles + 1, nj),
        in_specs=[rows_spec, rows_spec, vec, bvec, vec, bvec, bvec,
                  pl.BlockSpec((d, tn), lambda i, j: (layer, j))],
        out_specs=[pl.BlockSpec((tm, tn), lambda i, j: (jnp.maximum(i - 1, 0), jnp.where(i > 0, j, 0))),
                   rows_spec],
        out_shape=[jax.ShapeDtypeStruct((m, f), BF16), jax.ShapeDtypeStruct((m, d), F32)],
        scratch_shapes=[pltpu.VMEM((tm, d), BF16), pltpu.VMEM((tm, d), BF16)],
        compiler_params=_cparams("arbitrary", "arbitrary"),
        name="mlp_up_resid",
    )(x.reshape(m, d), y.reshape(m, d), post[0].reshape(1, d), post[1].reshape(b, 1, d), pre[0].reshape(1, d),
      pre[1].reshape(b, 1, d), pre[2].reshape(b, 1, d), w1.reshape(nl * d, f))
    return a, x_new.reshape(b, t, d)


def _mlp_down_body(*refs, n_tiles, rows, has_pre):
    a_ref, w_ref, x_ref, gpost_ref, gate_ref = refs[:5]
    if has_pre:
        gpre_ref, scale_ref, shift_ref, xn_ref, h_ref, acc0_ref, acc1_ref = refs[5:]
    else:
        xn_ref, acc0_ref, acc1_ref = refs[5:]
    i, k = pl.program_id(0), pl.program_id(1)
    accs = (acc0_ref, acc1_ref)
    tm = a_ref.shape[0]
    pieces = 4
    mr = tm // pieces
    nr = rows // pieces

    @pl.when((i == 0) & (k == 0))
    def _():
        acc0_ref[...] = jnp.zeros_like(acc0_ref)
        acc1_ref[...] = jnp.zeros_like(acc1_ref)

    def accumulate(dst_ref, w, s):
        sl = pl.ds(s * mr, mr)
        part = _dot(a_ref[sl, :], w)
        dst_ref[sl, :] = jnp.where(k == 0, part, dst_ref[sl, :] + part)

    def finish_rows(src_ref, s):
        sl = pl.ds(s * nr, nr)
        y = src_ref[pl.ds(pl.multiple_of(k * rows + s * nr, nr), nr), :]
        xn = x_ref[sl, :] + (1.0 + gate_ref[0]) * _rms(y, gpost_ref[...])
        xn_ref[sl, :] = xn
        if has_pre:
            h_ref[sl, :] = (_rms(xn, gpre_ref[...]) * (1.0 + scale_ref[0]) + shift_ref[0]).astype(h_ref.dtype)

    @pl.when(i == 0)
    def _():
        w = w_ref[...]
        for s in range(pieces):
            accumulate(accs[0], w, s)

    for parity in (0, 1):
        @pl.when((i > 0) & (i < n_tiles) & (i % 2 == parity))
        def _():
            w = w_ref[...]
            for s in range(pieces):
                accumulate(accs[parity], w, s)
                finish_rows(accs[1 - parity], s)

    @pl.when(i == n_tiles)
    def _():
        for s in range(pieces):
            finish_rows(accs[(n_tiles - 1) % 2], s)


def mlp_down_resid(a, w2, layer, x, post, pre=None, tm=1024, tk=1024):
    b, t, d = x.shape
    nl, f, _ = w2.shape
    m = b * t
    tm = _pick(t, tm)
    tk = _pick(f, tk)
    n_tiles, nk = m // tm, f // tk
    assert tm % nk == 0
    rows = tm // nk
    assert rows % (8 * 16) == 0
    tiles_per_batch = t // tm
    slice_idx = lambda i, k: jnp.where(i > 0, (i - 1) * nk + k, 0)
    rows_spec = pl.BlockSpec((rows, d), lambda i, k: (slice_idx(i, k), 0))
    vec = pl.BlockSpec((1, d), lambda i, k: (0, 0))
    bvec = pl.BlockSpec((1, 1, d), lambda i, k: (jnp.maximum(i - 1, 0) // tiles_per_batch, 0, 0))
    in_specs = [pl.BlockSpec((tm, tk), lambda i, k: (jnp.minimum(i, n_tiles - 1), jnp.where(i < n_tiles, k, nk - 1))),
                pl.BlockSpec((tk, d), lambda i, k: (layer * nk + jnp.where(i < n_tiles, k, nk - 1), 0)),
                rows_spec, vec, bvec]
    args = [a, w2.reshape(nl * f, d).astype(BF16), x.reshape(m, d), post[0].reshape(1, d), post[1].reshape(b, 1, d)]
    out_specs, out_shapes = [rows_spec], [jax.ShapeDtypeStruct((m, d), F32)]
    if pre is not None:
        in_specs += [vec, bvec, bvec]
        args += [pre[0].reshape(1, d), pre[1].reshape(b, 1, d), pre[2].reshape(b, 1, d)]
        out_specs.append(rows_spec)
        out_shapes.append(jax.ShapeDtypeStruct((m, d), BF16))
    outs = pl.pallas_call(
        functools.partial(_mlp_down_body, n_tiles=n_tiles, rows=rows, has_pre=pre is not None),
        grid=(n_tiles + 1, nk),
        in_specs=in_specs,
        out_specs=out_specs,
        out_shape=out_shapes,
        scratch_shapes=[pltpu.VMEM((tm, d), F32), pltpu.VMEM((tm, d), F32)],
        compiler_params=_cparams("arbitrary", "arbitrary"),
        name="mlp_down_resid",
    )(*args)
    return tuple(o.reshape(b, t, d) for o in outs)


def _glu(parts, extras):
    return (parts[0] + extras[0]) * jax.nn.sigmoid(parts[1] + extras[1])


def _conv_ln_body(cur_ref, prev_ref, w_ref, b_ref, g_ref, beta_ref, o_ref, buf_ref, acc_ref, *, tt, d):
    ti = pl.program_id(1)
    halo = prev_ref[0].astype(F32)
    buf_ref[0:CONV_HALO, :] = jnp.where(ti == 0, jnp.zeros_like(halo), halo)
    buf_ref[CONV_HALO:, :] = cur_ref[0].astype(F32)
    lane_chunk = min(d, 256)
    row_chunk = min(tt, 64)
    base = CONV_HALO - (CONV_WIDTH - 1)
    sub = 8
    for c0 in range(0, d, lane_chunk):
        for r0 in range(0, tt, row_chunk):
            acc = None
            for s in range(sub):
                taps = [j for j in range(CONV_WIDTH) if (base + j) % sub == s]
                if not taps:
                    continue
                rows = row_chunk + (sub if s else 0)
                part = jnp.zeros((rows, lane_chunk), F32)
                for j in taps:
                    off = r0 + base + j - s
                    part = part + buf_ref[off:off + rows, c0:c0 + lane_chunk] * w_ref[j:j + 1, c0:c0 + lane_chunk]
                part = part[s:s + row_chunk]
                acc = part if acc is None else acc + part
            acc_ref[r0:r0 + row_chunk, c0:c0 + lane_chunk] = acc
    y = acc_ref[...] + b_ref[...]
    mu = jnp.mean(y, axis=-1, keepdims=True)
    yc = y - mu
    var = jnp.mean(yc * yc, axis=-1, keepdims=True)
    z = yc * lax.rsqrt(var + CONV_LN_EPS) * g_ref[...] + beta_ref[...]
    o_ref[0] = (z * jax.nn.sigmoid(z)).astype(o_ref.dtype)


def conv_ln_silu(u, dw_w, dw_b, ln_g, ln_b, tt=128):
    b, t, d = u.shape
    tt = _pick(t, tt)
    hb = tt // CONV_HALO
    vec = pl.BlockSpec((1, d), lambda bi, ti: (0, 0))
    return pl.pallas_call(
        functools.partial(_conv_ln_body, tt=tt, d=d),
        grid=(b, t // tt),
        in_specs=[
            pl.BlockSpec((1, tt, d), lambda bi, ti: (bi, ti, 0)),
            pl.BlockSpec((1, CONV_HALO, d), lambda bi, ti: (bi, jnp.maximum(ti * hb - 1, 0), 0)),
            pl.BlockSpec((CONV_WIDTH, d), lambda bi, ti: (0, 0)),
            vec, vec, vec,
        ],
        out_specs=pl.BlockSpec((1, tt, d), lambda bi, ti: (bi, ti, 0)),
        out_shape=jax.ShapeDtypeStruct((b, t, d), BF16),
        scratch_shapes=[pltpu.VMEM((tt + CONV_HALO, d), F32), pltpu.VMEM((tt, d), F32)],
        compiler_params=_cparams("parallel", "parallel"),
        name="conv_ln_silu",
    )(u, u, dw_w, dw_b.reshape(1, d), ln_g.reshape(1, d), ln_b.reshape(1, d))


def conformer_conv_mix(h, pw1_w, pw1_b, dw_w, dw_b, ln_g, ln_b, pw2_w, pw2_b):
    b, t, d = h.shape
    tn = _pick(d, 1024)
    b1 = pw1_b.reshape(1, -1)
    u = mm(h.reshape(b * t, d), [(pw1_w, 0), (pw1_w, d // tn)], n_out=d, tm=1024, tn=tn, epi=_glu,
           extras=[(b1, (1, tn), lambda i, j, k: (0, j)),
                   (b1, (1, tn), lambda i, j, k, o=d // tn: (0, j + o))], out_dtype=BF16, name="conf_pw1_glu")
    z = conv_ln_silu(u.reshape(b, t, d), dw_w, dw_b, ln_g, ln_b)
    return z.reshape(b * t, d), [(pw2_w, 0, 0)], pw2_b


def _rope(x, cos, sin):
    half = x.shape[-1] // 2
    x1, x2 = x[:, :half], x[:, half:]
    return jnp.concatenate([x1 * cos - x2 * sin, x2 * cos + x1 * sin], axis=-1)


def _retention_body(q_ref, k_ref, v_ref, gate_ref, cos_ref, sin_ref, inner_ref, qdec_ref, kdec_ref, cdec_ref,
                    gng_ref, gnb_ref, o_ref, state_ref, *, dk):
    @pl.when(pl.program_id(2) == 0)
    def _():
        state_ref[...] = jnp.zeros_like(state_ref)

    cos, sin = cos_ref[...], sin_ref[...]
    q = _rope(q_ref[0].astype(F32), cos, sin)
    k = _rope(k_ref[0].astype(F32), cos, sin) * (dk ** -0.5)
    v = v_ref[0].astype(BF16)
    qb = q.astype(BF16)
    s = _dot_nt(qb, k.astype(BF16)) * inner_ref[0]
    state = state_ref[...]
    o = _dot(s.astype(BF16), v) + _dot(qb, state.astype(BF16)) * qdec_ref[0]
    kd_t = (k * kdec_ref[0]).T.astype(BF16)
    state_ref[...] = state * cdec_ref[0] + _dot(kd_t, v)
    mu = jnp.mean(o, axis=-1, keepdims=True)
    oc = o - mu
    var = jnp.mean(oc * oc, axis=-1, keepdims=True)
    on = oc * lax.rsqrt(var + RET_GN_EPS) * gng_ref[...] + gnb_ref[...]
    gate = gate_ref[0].astype(F32)
    o_ref[0] = (gate * jax.nn.sigmoid(gate) * on).astype(o_ref.dtype)


def retention_mix(h, w_in, gn_g, gn_b, w_out):
    b, t, d = h.shape
    nh, c = RET_HEADS, _pick(t, RET_CHUNK)
    dk = d // nh
    dv = 2 * dk
    proj = mm(h.reshape(b * t, d), w_in, out_dtype=BF16, tm=2048, tn=512, cast_in_kernel=True,
              name="ret_in").reshape(b, t, 6 * d)
    pos = jnp.arange(t, dtype=F32)
    inv_freq = ROPE_BASE ** (-jnp.arange(0, dk, 2, dtype=F32) / dk)
    ang = pos[:, None] * inv_freq[None, :]
    cos, sin = jnp.cos(ang), jnp.sin(ang)
    log_gamma = jnp.log(1.0 - 2.0 ** (-5.0 - jnp.arange(nh, dtype=F32)))
    idx = jnp.arange(c, dtype=F32)
    diff = idx[:, None] - idx[None, :]
    inner = jnp.where(diff >= 0, jnp.exp(jnp.maximum(diff, 0.0)[None] * log_gamma[:, None, None]), 0.0)
    q_dec = jnp.exp((idx + 1.0)[None] * log_gamma[:, None])[:, :, None]
    k_dec = jnp.exp((c - 1.0 - idx)[None] * log_gamma[:, None])[:, :, None]
    c_dec = jnp.exp(c * log_gamma)[:, None, None]
    nq = (nh * dk) // dk
    o = pl.pallas_call(
        functools.partial(_retention_body, dk=dk),
        grid=(b, nh, t // c),
        in_specs=[
            pl.BlockSpec((1, c, dk), lambda bi, hi, ci: (bi, ci, hi)),
            pl.BlockSpec((1, c, dk), lambda bi, hi, ci: (bi, ci, nq + hi)),
            pl.BlockSpec((1, c, dv), lambda bi, hi, ci: (bi, ci, nq + hi)),
            pl.BlockSpec((1, c, dv), lambda bi, hi, ci: (bi, ci, 2 * nq + hi)),
            pl.BlockSpec((c, dk // 2), lambda bi, hi, ci: (ci, 0)),
            pl.BlockSpec((c, dk // 2), lambda bi, hi, ci: (ci, 0)),
            pl.BlockSpec((1, c, c), lambda bi, hi, ci: (hi, 0, 0)),
            pl.BlockSpec((1, c, 1), lambda bi, hi, ci: (hi, 0, 0)),
            pl.BlockSpec((1, c, 1), lambda bi, hi, ci: (hi, 0, 0)),
            pl.BlockSpec((1, 1, 1), lambda bi, hi, ci: (hi, 0, 0)),
            pl.BlockSpec((1, dv), lambda bi, hi, ci: (0, hi)),
            pl.BlockSpec((1, dv), lambda bi, hi, ci: (0, hi)),
        ],
        out_specs=pl.BlockSpec((1, c, dv), lambda bi, hi, ci: (bi, ci, hi)),
        out_shape=jax.ShapeDtypeStruct((b, t, nh * dv), BF16),
        scratch_shapes=[pltpu.VMEM((dk, dv), F32)],
        compiler_params=_cparams("parallel", "parallel", "arbitrary"),
        name="retention",
    )(proj, proj, proj, proj, cos, sin, inner, q_dec, k_dec, c_dec, gn_g.reshape(1, -1), gn_b.reshape(1, -1))
    return o.reshape(b * t, nh * dv), [(w_out, 0, 0)], None


def _rwkv_pre_body(x_ref, prev_ref, g_ref, scale_ref, shift_ref, mu_ref, o_ref):
    ti = pl.program_id(1)
    g, scale, shift = g_ref[...], scale_ref[0], shift_ref[0]
    h = _rms(x_ref[0], g) * (1.0 + scale) + shift
    hp = _rms(prev_ref[0], g) * (1.0 + scale) + shift
    last = jnp.where(ti == 0, 0.0, hp[7:8, :])
    row = lax.broadcasted_iota(jnp.int32, h.shape, 0)
    shifted = jnp.where(row == 0, last, pltpu.roll(h, 1, axis=0))
    xx = shifted - h
    for s in range(6):
        o_ref[s, 0] = (h + xx * mu_ref[s:s + 1, :]).astype(o_ref.dtype)


def rwkv_pre(x, g_pre, scale, shift, mu, tt=256):
    b, t, d = x.shape
    tt = _pick(t, tt)
    bvec = pl.BlockSpec((1, 1, d), lambda bi, ti: (bi, 0, 0))
    return pl.pallas_call(
        _rwkv_pre_body,
        grid=(b, t // tt),
        in_specs=[
            pl.BlockSpec((1, tt, d), lambda bi, ti: (bi, ti, 0)),
            pl.BlockSpec((1, 8, d), lambda bi, ti: (bi, jnp.maximum(ti * (tt // 8) - 1, 0), 0)),
            pl.BlockSpec((1, d), lambda bi, ti: (0, 0)),
            bvec, bvec,
            pl.BlockSpec((6, d), lambda bi, ti: (0, 0)),
        ],
        out_specs=pl.BlockSpec((6, 1, tt, d), lambda bi, ti: (0, bi, ti, 0)),
        out_shape=jax.ShapeDtypeStruct((6, b, t, d), BF16),
        compiler_params=_cparams("parallel", "parallel"),
        name="rwkv_pre",
    )(x, x, g_pre.reshape(1, d), scale.reshape(b, 1, d), shift.reshape(b, 1, d), mu)


def _rwkv_scan_body(r_ref, k_ref, v_ref, a_ref, lw_ref, g_ref, kk_ref, ka_ref, rk_ref, lng_ref, lnb_ref, o_ref,
                    state_ref, *, chunk, heads, n):
    @pl.when(pl.program_id(2) == 0)
    def _():
        state_ref[...] = jnp.zeros_like(state_ref)

    ln = chunk
    hs = range(heads)
    row = lax.broadcasted_iota(jnp.int32, (ln, ln), 0)
    col = lax.broadcasted_iota(jnp.int32, (ln, ln), 1)
    tri = (col <= row).astype(BF16)
    eye = (row == col).astype(F32)
    wid = heads * n
    seg_w = min(wid, 256)
    assert wid % seg_w == 0 and seg_w % n == 0
    seg = (lax.broadcasted_iota(jnp.int32, (seg_w, seg_w), 0) // n
           == lax.broadcasted_iota(jnp.int32, (seg_w, seg_w), 1) // n).astype(BF16)

    def split_dot(m, x):
        hi = x.astype(BF16)
        lo = (x - hi.astype(F32)).astype(BF16)
        return _dot(m, hi) + _dot(m, lo)

    def seg_sum(x):
        hi = x.astype(BF16)
        lo = (x - hi.astype(F32)).astype(BF16)
        return jnp.concatenate([_dot(hi[:, j:j + seg_w], seg) + _dot(lo[:, j:j + seg_w], seg)
                                for j in range(0, wid, seg_w)], axis=1)

    heads_of = lambda x: [x[:, hd * n:(hd + 1) * n] for hd in hs]

    r, k, v, a = (ref[0].astype(F32) for ref in (r_ref, k_ref, v_ref, a_ref))
    lw = lw_ref[0]
    kk = k * kk_ref[...]
    kk = kk / jnp.maximum(jnp.sqrt(seg_sum(kk * kk)), 1e-12)
    k2 = k * (1.0 + (a - 1.0) * ka_ref[...])
    beta = kk * a
    c = split_dot(tri, lw)
    c_last = c[ln - 1:ln, :]
    e_neg = jnp.exp(-c)
    e_rem = jnp.exp(c_last - c)
    e_last = heads_of(jnp.exp(c_last))
    a_t = heads_of(-kk * jnp.exp(c - lw))
    r_t = heads_of(r * jnp.exp(c))
    b_t = heads_of(beta * e_neg)
    k_t = heads_of(k2 * e_neg)
    b_rem = heads_of(beta * e_rem)
    k_rem = heads_of(k2 * e_rem)
    vs = heads_of(v)
    bonus = seg_sum(r * k2 * rk_ref[...]) * v

    ar = [jnp.concatenate([a_t[h], r_t[h]], axis=0).astype(BF16) for h in hs]
    bk = [jnp.concatenate([b_t[h], k_t[h]], axis=0).astype(BF16) for h in hs]
    bk_rem = [jnp.concatenate([b_rem[h], k_rem[h]], axis=0).astype(BF16) for h in hs]
    vb = [vs[h].astype(BF16) for h in hs]
    state = [state_ref[h] for h in hs]
    p = [_dot_nt(ar[h], bk[h]) for h in hs]
    q0 = [_dot_nt(ar[h], state[h].astype(BF16)) for h in hs]
    row2 = lax.broadcasted_iota(jnp.int32, (2 * ln, 2 * ln), 0)
    col2 = lax.broadcasted_iota(jnp.int32, (2 * ln, 2 * ln), 1)
    keep = col2 % ln < row2 % ln + row2 // ln
    pm = [jnp.where(keep, p[h], 0.0) for h in hs]
    m_ab = [pm[h][:ln, :ln] for h in hs]
    m_ak = [pm[h][:ln, ln:].astype(BF16) for h in hs]
    m_r = [pm[h][ln:].astype(BF16) for h in hs]
    rhs = [q0[h][:ln] + _dot(m_ak[h], vb[h]) for h in hs]
    inv = [eye + m_ab[h] for h in hs]
    pw = [m_ab[h].astype(BF16) for h in hs]
    for _ in range(int(math.log2(ln)) - 1):
        pw = [_dot(pw[h], pw[h]).astype(BF16) for h in hs]
        inv = [inv[h] + _dot(inv[h].astype(BF16), pw[h]) for h in hs]
    u = [_dot(inv[h].astype(BF16), rhs[h].astype(BF16)) for h in hs]
    uv = [jnp.concatenate([u[h], vs[h]], axis=0) for h in hs]
    y = [q0[h][ln:] + _dot(m_r[h], uv[h].astype(BF16)) for h in hs]
    for h in hs:
        state_ref[h] = state[h] * e_last[h] + _dot(uv[h].T.astype(BF16), bk_rem[h])
    yn = []
    for h in hs:
        mu = jnp.mean(y[h], axis=-1, keepdims=True)
        yc = y[h] - mu
        var = jnp.mean(yc * yc, axis=-1, keepdims=True)
        yn.append(yc * lax.rsqrt(var + RWKV_GN_EPS))
    yn = jnp.concatenate(yn, axis=-1) * lng_ref[...] + lnb_ref[...]
    o_ref[0] = ((yn + bonus) * g_ref[0]).astype(o_ref.dtype)


def rwkv_scan(r, k, v, a, lw, g, k_k, k_a, r_k, ln_g, ln_b):
    b, t, d = r.shape
    n = RWKV_HEAD_DIM
    hps = min(RWKV_HEADS_PER_STEP, d // n)
    w = hps * n
    ln = _pick(t, RWKV_CHUNK)
    tok = pl.BlockSpec((1, ln, w), lambda bi, hi, ci: (bi, ci, hi))
    vec = pl.BlockSpec((1, w), lambda bi, hi, ci: (0, hi))
    return pl.pallas_call(
        functools.partial(_rwkv_scan_body, chunk=ln, heads=hps, n=n),
        grid=(b, d // w, t // ln),
        in_specs=[tok] * 6 + [vec] * 5,
        out_specs=tok,
        out_shape=jax.ShapeDtypeStruct((b, t, d), BF16),
        scratch_shapes=[pltpu.VMEM((hps, n, n), F32)],
        compiler_params=_cparams("parallel", "parallel", "arbitrary"),
        name="rwkv_scan",
    )(r, k, v, a, lw, g, k_k.reshape(1, d), k_a.reshape(1, d), r_k.reshape(1, d), ln_g.reshape(1, d),
      ln_b.reshape(1, d))


def _tanh_epi(parts, extras):
    return jnp.tanh(parts[0])


def _sigmoid_epi(parts, extras):
    return jax.nn.sigmoid(parts[0])


def _sigmoid_bias_epi(parts, extras):
    return jax.nn.sigmoid(extras[0] + parts[0])


def _logdecay_epi(parts, extras):
    return -jnp.exp(-jax.nn.softplus(-(extras[0] + parts[0])) - 0.5)


def rwkv7_time_mix(x, g_pre, scale, shift, mu, w_rkv, w0, w_la, w_lb, a0, a_la, a_lb, g_la, g_lb, k_k, k_a, r_k,
                   ln_g, ln_b, w_out):
    b, t, d = x.shape
    xs = rwkv_pre(x, g_pre, scale, shift, mu).reshape(6 * b * t, d)
    tn = _pick(d, 1024)
    w_rkv2d = w_rkv.reshape(3 * d, d)
    r = mm(xs, [(w_rkv2d, 0, 0)], a_part=(0, 6), out_dtype=BF16, name="rwkv_r")
    k = mm(xs, [(w_rkv2d, 0, 1)], a_part=(1, 6), out_dtype=BF16, name="rwkv_k")
    v = mm(xs, [(w_rkv2d, 0, 2)], a_part=(2, 6), out_dtype=BF16, name="rwkv_v")
    lw = mm(mm(xs, w_la, a_part=(3, 6), epi=_tanh_epi, out_dtype=BF16, name="rwkv_w_la"), w_lb, epi=_logdecay_epi,
            extras=[_row_extra(w0, tn)], name="rwkv_w_lb")
    a = mm(mm(xs, a_la, a_part=(4, 6), out_dtype=BF16, name="rwkv_a_la"), a_lb, epi=_sigmoid_bias_epi,
           extras=[_row_extra(a0, tn)], out_dtype=BF16, name="rwkv_a_lb")
    g = mm(mm(xs, g_la, a_part=(5, 6), epi=_sigmoid_epi, out_dtype=BF16, name="rwkv_g_la"), g_lb, out_dtype=BF16,
           name="rwkv_g_lb")
    sh = lambda z: z.reshape(b, t, d)
    z = rwkv_scan(sh(r), sh(k), sh(v), sh(a), sh(lw), sh(g), k_k, k_a, r_k, ln_g, ln_b)
    return z.reshape(b * t, d), [(w_out, 0, 0)], None


def _nsa_compress_body(kv_ref, pe_ref, w1_ref, w2_ref, o_ref, buf_ref, *, nchunk, dh):
    st = NSA_CMP_STRIDE
    hid = w1_ref.shape[1]
    buf_ref[...] = kv_ref[0].astype(F32)
    first = jnp.zeros((nchunk, hid), F32)
    second = jnp.zeros((nchunk, hid), F32)
    for j in range(st):
        xj = buf_ref[pl.ds(j, nchunk, stride=st), :]
        first = first + _dot((xj + pe_ref[j:j + 1, :]).astype(BF16), w1_ref[j * dh:(j + 1) * dh, :])
        second = second + _dot((xj + pe_ref[st + j:st + j + 1, :]).astype(BF16),
                               w1_ref[(st + j) * dh:(st + j + 1) * dh, :])
    hidden = jax.nn.gelu(first + pltpu.roll(second, nchunk - 1, axis=0))
    out = _dot(hidden.astype(BF16), w2_ref[...])
    row = lax.broadcasted_iota(jnp.int32, out.shape, 0)
    o_ref[0, 0] = jnp.where(row < nchunk - 1, out, 0.0)


def nsa_compress(proj, col0, pe, w1, w2):
    b, t, _ = proj.shape
    g, dh = NSA_KV_GROUPS, NSA_HEAD_DIM
    nchunk = t // NSA_CMP_STRIDE
    hid = w1.shape[1]
    return pl.pallas_call(
        functools.partial(_nsa_compress_body, nchunk=nchunk, dh=dh),
        grid=(b, g),
        in_specs=[
            pl.BlockSpec((1, t, dh), lambda bi, gi: (bi, 0, col0 // dh + gi)),
            pl.BlockSpec((NSA_CMP_BLOCK, dh), lambda bi, gi: (0, 0)),
            pl.BlockSpec((NSA_CMP_BLOCK * dh, hid), lambda bi, gi: (0, 0)),
            pl.BlockSpec((hid, dh), lambda bi, gi: (0, 0)),
        ],
        out_specs=pl.BlockSpec((1, 1, nchunk, dh), lambda bi, gi: (bi, gi, 0, 0)),
        out_shape=jax.ShapeDtypeStruct((b, g, nchunk, dh), F32),
        scratch_shapes=[pltpu.VMEM((t, dh), F32)],
        compiler_params=_cparams("parallel", "parallel"),
        name="nsa_compress",
    )(proj, pe, w1.astype(BF16), w2.astype(BF16))


def _stack_heads(q, r, dh):
    return jnp.concatenate([q[:, i * dh:(i + 1) * dh] for i in range(r)], axis=0)


def _unstack_heads(o, r, tq):
    return jnp.concatenate([o[i * tq:(i + 1) * tq] for i in range(r)], axis=-1)


def _gate_rows(gates, branch, r, tq):
    return jnp.concatenate([gates[:, branch * r + i:branch * r + i + 1] for i in range(r)], axis=0)


def _nsa_cmp_body(q_ref, kc_ref, vc_ref, gates_ref, selmap_t_ref, o_ref, sel_ref, *, tq, r, dh, n_sel):
    q0 = pl.program_id(2) * tq
    qs = (_stack_heads(q_ref[0].astype(F32), r, dh) * (dh ** -0.5)).astype(BF16)
    kc, vc = kc_ref[0, 0].astype(BF16), vc_ref[0, 0].astype(BF16)
    ncp = kc.shape[0]
    s = _dot_nt(qs, kc)
    rows = lax.broadcasted_iota(jnp.int32, (r * tq, ncp), 0)
    cmp_end = lax.broadcasted_iota(jnp.int32, (r * tq, ncp), 1) * NSA_CMP_STRIDE + (NSA_CMP_BLOCK - 1)
    t_pos = q0 + rows % tq
    vis = cmp_end <= t_pos
    s = jnp.where(vis, s, NEG_INF)
    e = jnp.exp(s - jnp.max(s, axis=-1, keepdims=True))
    p = e / jnp.sum(e, axis=-1, keepdims=True) * vis.astype(F32)
    o = _dot(p.astype(BF16), vc)
    o = o * _gate_rows(gates_ref[0, 0], 0, r, tq)
    o_ref[0] = _unstack_heads(o, r, tq)
    p_sum = p[0:tq]
    for i in range(1, r):
        p_sum = p_sum + p[i * tq:(i + 1) * tq]
    hi = p_sum.astype(BF16)
    lo = (p_sum - hi.astype(F32)).astype(BF16)
    selmap_t = selmap_t_ref[...]
    imp = _dot_nt(selmap_t, hi) + _dot_nt(selmap_t, lo)
    ns = imp.shape[0]
    blk = lax.broadcasted_iota(jnp.int32, (ns, tq), 0)
    tq_pos = q0 + lax.broadcasted_iota(jnp.int32, (ns, tq), 1)
    cur = tq_pos // NSA_SEL_BLOCK
    valid = blk * NSA_SEL_BLOCK <= tq_pos
    forced = (blk == 0) | (blk == cur) | (blk == cur - 1)
    score = jnp.where(valid, jnp.where(forced, POS_BIG, imp), NEG_INF)
    rank = jnp.zeros((ns, tq), F32)
    for m in range(ns):
        sm = score[m:m + 1, :]
        beats = (sm > score) | ((sm == score) & (blk > m))
        rank = rank + beats.astype(F32)
    chosen = ((rank < n_sel) & valid).astype(F32)
    sel_ref[0, 0] = (chosen.T - 1.0).astype(sel_ref.dtype)


def _flash_update(s, v_ones, m_ref, acc_ref):
    lanes = m_ref.shape[1]
    assert s.shape[1] % lanes == 0 and acc_ref.shape[1] == 2 * lanes
    m_old = m_ref[...]
    m_new = jnp.maximum(m_old, jnp.max(s, axis=-1, keepdims=True))
    alpha = jnp.exp2(m_old - m_new)
    p = jnp.exp2((s - jnp.concatenate([m_new] * (s.shape[1] // lanes), axis=1)).astype(BF16))
    acc_ref[...] = jnp.concatenate([alpha, alpha], axis=1) * acc_ref[...] + _dot(p, v_ones)
    m_ref[...] = m_new


def _nsa_sel_body(qi_tbl, ki_tbl, diag_tbl, q_ref, k_ref, v_ref, gates_ref, oin_ref, sel_ref, o_ref, qa_ref, m_ref,
                  acc_ref, *, tq, tk, r, dh):
    step = pl.program_id(2)
    qi, ki, diag = qi_tbl[step], ki_tbl[step], diag_tbl[step]
    ns = sel_ref.shape[-1]
    assert ns <= dh

    @pl.when(ki == 0)
    def _():
        m_ref[...] = jnp.full_like(m_ref, NEG_INF)
        acc_ref[...] = jnp.zeros_like(acc_ref)
        qs = _stack_heads(q_ref[0].astype(F32), r, dh) * (dh ** -0.5 * LOG2_E)
        pieces = [qs, jnp.concatenate([sel_ref[0, 0].astype(F32)] * r, axis=0)]
        if ns < dh:
            pieces.append(jnp.zeros((r * tq, dh - ns), F32))
        qa_ref[...] = jnp.concatenate(pieces, axis=1).astype(BF16)

    blk_of_key = (ki * tk + lax.broadcasted_iota(jnp.int32, (tk, dh), 0)) // NSA_SEL_BLOCK
    own_block = jnp.where(blk_of_key == lax.broadcasted_iota(jnp.int32, (tk, dh), 1), POS_BIG, 0.0).astype(BF16)
    ka = jnp.concatenate([k_ref[0].astype(BF16), own_block], axis=1)
    s = _dot_nt(qa_ref[...], ka)
    v_ones = jnp.concatenate([v_ref[0].astype(BF16), jnp.ones((tk, dh), BF16)], axis=1)

    @pl.when(diag == 0)
    def _():
        _flash_update(s, v_ones, m_ref, acc_ref)

    @pl.when(diag == 1)
    def _():
        t_pos = qi * tq + lax.broadcasted_iota(jnp.int32, (r * tq, tk), 0) % tq
        k_pos = ki * tk + lax.broadcasted_iota(jnp.int32, (r * tq, tk), 1)
        _flash_update(jnp.where(k_pos <= t_pos, s, NEG_INF), v_ones, m_ref, acc_ref)
        o = acc_ref[:, :dh] / acc_ref[:, dh:] * _gate_rows(gates_ref[0, 0], 1, r, tq)
        o_ref[0] = (oin_ref[0] + _unstack_heads(o, r, tq)).astype(o_ref.dtype)


def _nsa_win_body(q_ref, *rest, tq, r, dh, nwin):
    k_refs, v_refs = rest[:nwin], rest[nwin:2 * nwin]
    gates_ref, oin_ref, o_ref = rest[2 * nwin:]
    qi = pl.program_id(2)
    qs = (_stack_heads(q_ref[0].astype(F32), r, dh) * (dh ** -0.5 * LOG2_E)).astype(BF16)
    t_loc = lax.broadcasted_iota(jnp.int32, (r * tq, tq), 0) % tq
    k_loc = lax.broadcasted_iota(jnp.int32, (r * tq, tq), 1)
    tiles = []
    for j in range(nwin):
        s = _dot_nt(qs, k_refs[j][0].astype(BF16))
        if j == nwin - 1:
            s = jnp.where(k_loc <= t_loc, s, NEG_INF)
        else:
            if j == 0:
                s = jnp.where(k_loc > t_loc, s, NEG_INF)
            s = jnp.where(qi - (nwin - 1) + j >= 0, s, NEG_INF)
        tiles.append(s)
    s = jnp.concatenate(tiles, axis=1)
    p = jnp.exp2((s - jnp.max(s, axis=-1, keepdims=True)).astype(BF16))
    v = jnp.concatenate([v_refs[j][0] for j in range(nwin)], axis=0).astype(BF16)
    on = _dot(p, jnp.concatenate([v, jnp.ones_like(v)], axis=1))
    o = on[:, :dh] / on[:, dh:] * _gate_rows(gates_ref[0, 0], 2, r, tq)
    o_ref[0] = (oin_ref[0] + _unstack_heads(o, r, tq)).astype(o_ref.dtype)


def nsa_mix(h, w_in, pe_k, pe_v, ck_w1, ck_w2, cv_w1, cv_w2, w_out, tq=256):
    b, t, d = h.shape
    nh, g, dh = NSA_HEADS, NSA_KV_GROUPS, NSA_HEAD_DIM
    r = nh // g
    hd, kd = nh * dh, g * dh
    n_main = hd + 6 * kd
    h2 = h.reshape(b * t, d)
    proj = mm(h2, w_in[:, :n_main], out_dtype=BF16, name="nsa_in").reshape(b, t, n_main)
    n_gate = w_in.shape[1] - n_main
    w_gate = jnp.pad(w_in[:, n_main:], ((0, 0), (0, 128 - n_gate)))
    gates = mm(h2, w_gate, epi=_sigmoid_epi, name="nsa_gates")[:, :n_gate]
    gates = jnp.transpose(gates.reshape(b, t, 3, g, r), (0, 3, 1, 2, 4)).reshape(b, g, t, 3 * r)
    k_cmp = nsa_compress(proj, hd, pe_k, ck_w1, ck_w2)
    v_cmp = nsa_compress(proj, hd + kd, pe_v, cv_w1, cv_w2)
    ncp = k_cmp.shape[2]
    ns = t // NSA_SEL_BLOCK
    n_sel = min(NSA_SEL_TOP, ns)
    cs = jnp.arange(ncp)[None, :] * NSA_CMP_STRIDE
    ss = jnp.arange(ns)[:, None] * NSA_SEL_BLOCK
    sel_map_t = (jnp.maximum(jnp.minimum(cs + NSA_CMP_BLOCK, ss + NSA_SEL_BLOCK) - jnp.maximum(cs, ss), 0)
                 .astype(F32) / NSA_CMP_BLOCK).astype(BF16)
    tq = _pick(t, tq)
    assert NSA_WINDOW % tq == 0
    nq = t // tq
    wq = r * dh
    col = lambda base: base // dh
    q_spec3 = pl.BlockSpec((1, tq, wq), lambda bi, gi, qi: (bi, qi, gi))
    gate_spec3 = pl.BlockSpec((1, 1, tq, 3 * r), lambda bi, gi, qi: (bi, gi, qi, 0))
    o_cmp, sel = pl.pallas_call(
        functools.partial(_nsa_cmp_body, tq=tq, r=r, dh=dh, n_sel=n_sel),
        grid=(b, g, nq),
        in_specs=[
            q_spec3,
            pl.BlockSpec((1, 1, ncp, dh), lambda bi, gi, qi: (bi, gi, 0, 0)),
            pl.BlockSpec((1, 1, ncp, dh), lambda bi, gi, qi: (bi, gi, 0, 0)),
            gate_spec3,
            pl.BlockSpec((ns, ncp), lambda bi, gi, qi: (0, 0)),
        ],
        out_specs=[q_spec3, pl.BlockSpec((1, 1, tq, ns), lambda bi, gi, qi: (bi, gi, qi, 0))],
        out_shape=[jax.ShapeDtypeStruct((b, t, hd), F32), jax.ShapeDtypeStruct((b, g, t, ns), BF16)],
        compiler_params=_cparams("parallel", "parallel", "parallel"),
        name="nsa_cmp",
    )(proj, k_cmp, v_cmp, gates, sel_map_t)

    tk = _pick(t, NSA_SEL_KEY_TILE)
    assert tk % tq == 0
    pairs = [(qi, ki) for qi in range(nq) for ki in range((qi * tq + tq - 1) // tk + 1)]
    qi_tbl = jnp.asarray([pq for pq, _ in pairs], jnp.int32)
    ki_tbl = jnp.asarray([pk for _, pk in pairs], jnp.int32)
    diag_tbl = jnp.asarray([int(pk == (pq * tq + tq - 1) // tk) for pq, pk in pairs], jnp.int32)
    q_spec_p = pl.BlockSpec((1, tq, wq), lambda bi, gi, p, qt, kt, dt: (bi, qt[p], gi))

    def kv_spec_p(base):
        return pl.BlockSpec((1, tk, dh), lambda bi, gi, p, qt, kt, dt: (bi, kt[p], col(base) + gi))

    o_sel = pl.pallas_call(
        functools.partial(_nsa_sel_body, tq=tq, tk=tk, r=r, dh=dh),
        grid_spec=pltpu.PrefetchScalarGridSpec(
            num_scalar_prefetch=3,
            grid=(b, g, len(pairs)),
            in_specs=[q_spec_p, kv_spec_p(hd + 2 * kd), kv_spec_p(hd + 3 * kd),
                      pl.BlockSpec((1, 1, tq, 3 * r), lambda bi, gi, p, qt, kt, dt: (bi, gi, qt[p], 0)),
                      q_spec_p,
                      pl.BlockSpec((1, 1, tq, ns), lambda bi, gi, p, qt, kt, dt: (bi, gi, qt[p], 0))],
            out_specs=q_spec_p,
            scratch_shapes=[pltpu.VMEM((r * tq, 2 * dh), BF16), pltpu.VMEM((r * tq, dh), F32),
                            pltpu.VMEM((r * tq, 2 * dh), F32)],
        ),
        out_shape=jax.ShapeDtypeStruct((b, t, hd), F32),
        compiler_params=_cparams("parallel", "parallel", "arbitrary"),
        name="nsa_sel",
    )(qi_tbl, ki_tbl, diag_tbl, proj, proj, proj, gates, o_cmp, sel)

    nwin = NSA_WINDOW // tq + 1

    def kv_spec_w(base, j):
        return pl.BlockSpec((1, tq, dh),
                            lambda bi, gi, qi: (bi, jnp.maximum(qi - (nwin - 1) + j, 0), col(base) + gi))

    o_all = pl.pallas_call(
        functools.partial(_nsa_win_body, tq=tq, r=r, dh=dh, nwin=nwin),
        grid=(b, g, nq),
        in_specs=([q_spec3] + [kv_spec_w(hd + 4 * kd, j) for j in range(nwin)]
                  + [kv_spec_w(hd + 5 * kd, j) for j in range(nwin)] + [gate_spec3, q_spec3]),
        out_specs=q_spec3,
        out_shape=jax.ShapeDtypeStruct((b, t, hd), BF16),
        compiler_params=_cparams("parallel", "parallel", "parallel"),
        name="nsa_win",
    )(proj, *([proj] * (2 * nwin)), gates, o_sel)
    return o_all.reshape(b * t, hd), [(w_out, 0, 0)], None


def _silu(a):
    return a * jax.nn.sigmoid(a)


def _ada_epi(parts, extras):
    return parts[0] + extras[0]


def ada_modulation(c, ada_w, ada_b):
    depth, d, n6 = ada_w.shape
    b = c.shape[0]
    rows = ((b + 7) // 8) * 8
    cond = jnp.pad(c, ((0, rows - b), (0, 0)))
    w2d = ada_w.reshape(depth * d, n6)
    mods = []
    for i in range(depth):
        mods.append(mm(cond, [(w2d, 0, i)], a_act=_silu, epi=_ada_epi, extras=[_row_extra(ada_b[i], _pick(n6, 1024))],
                       precision=HIGHEST, name="ada_mod")[:b])
    return jnp.stack(mods)


def kernel(x, c, ada_w, ada_b, norm_g, mlp_w1, mlp_w2, rwkv_mu, rwkv_w_rkv, rwkv_w0, rwkv_w_la, rwkv_w_lb, rwkv_a0, rwkv_a_la, rwkv_a_lb, rwkv_g_la, rwkv_g_lb, rwkv_k_k, rwkv_k_a, rwkv_r_k, rwkv_ln_g, rwkv_ln_b, rwkv_w_out, ret_w_in, ret_gn_g, ret_gn_b, ret_w_out, conv_pw1_w, conv_pw1_b, conv_dw_w, conv_dw_b, conv_ln_g, conv_ln_b, conv_pw2_w, conv_pw2_b, nsa_w_in, nsa_pe_k, nsa_pe_v, nsa_ck_w1, nsa_ck_w2, nsa_cv_w1, nsa_cv_w2, nsa_w_out):
    b, t, d = x.shape
    depth = ada_w.shape[0]
    mod = ada_modulation(c, ada_w, ada_b).reshape(depth, b, 6, d)
    h = None
    for i in range(depth):
        sh_t, sc_t, gt_t, sh_c, sc_c, gt_c = (mod[i, :, j] for j in range(6))
        kind = i % 4
        if kind == 0:
            act, ws, bias = rwkv7_time_mix(x, norm_g[i, 0], sc_t, sh_t, rwkv_mu, rwkv_w_rkv, rwkv_w0, rwkv_w_la,
                                           rwkv_w_lb, rwkv_a0, rwkv_a_la, rwkv_a_lb, rwkv_g_la, rwkv_g_lb, rwkv_k_k,
                                           rwkv_k_a, rwkv_r_k.reshape(-1), rwkv_ln_g, rwkv_ln_b, rwkv_w_out)
        else:
            if h is None:
                (h,) = resid_norm(x, pre=(norm_g[i, 0], sc_t, sh_t))
            if kind == 1:
                act, ws, bias = retention_mix(h, ret_w_in, ret_gn_g, ret_gn_b, ret_w_out)
            elif kind == 2:
                act, ws, bias = conformer_conv_mix(h, conv_pw1_w, conv_pw1_b, conv_dw_w, conv_dw_b, conv_ln_g,
                                                   conv_ln_b, conv_pw2_w, conv_pw2_b)
            else:
                act, ws, bias = nsa_mix(h, nsa_w_in, nsa_pe_k, nsa_pe_v, nsa_ck_w1, nsa_ck_w2, nsa_cv_w1, nsa_cv_w2,
                                        nsa_w_out)
        y = project(act, ws, bias, "mixer_out").reshape(b, t, d)
        act, x = mlp_up_resid(x, y, (norm_g[i, 1], gt_t), (norm_g[i, 2], sc_c, sh_c), mlp_w1, i)
        nxt = i + 1
        if nxt < depth and nxt % 4 != 0:
            x, h = mlp_down_resid(act, mlp_w2, i, x, (norm_g[i, 3], gt_c),
                                  (norm_g[nxt, 0], mod[nxt, :, 1], mod[nxt, :, 0]))
        else:
            (x,) = mlp_down_resid(act, mlp_w2, i, x, (norm_g[i, 3], gt_c))
            h = None
    return x
```

```python
import functools
import math

import jax
import jax.numpy as jnp
from jax import lax
from jax.experimental import pallas as pl
from jax.experimental.pallas import tpu as pltpu

F32 = jnp.float32
BF16 = jnp.bfloat16
HIGHEST = lax.Precision.HIGHEST

NORM_EPS = 1e-6
NEG_INF = -1e30
POS_BIG = 1e30
LOG2_E = math.log2(math.e)

RWKV_HEAD_DIM = 64
RWKV_GN_EPS = 64e-5
RWKV_CHUNK = 64
RWKV_HEADS_PER_STEP = 32

RET_HEADS = 8
RET_CHUNK = 256
RET_GN_EPS = 1e-5
ROPE_BASE = 10000.0

CONV_WIDTH = 31
CONV_HALO = 32
CONV_LN_EPS = 1e-5

NSA_HEADS = 16
NSA_KV_GROUPS = 4
NSA_HEAD_DIM = 128
NSA_CMP_BLOCK = 32
NSA_CMP_STRIDE = 16
NSA_SEL_BLOCK = 64
NSA_SEL_TOP = 16
NSA_WINDOW = 512
NSA_SEL_KEY_TILE = 512
NSA_SEL_QUERY_TILE = 512

VMEM_LIMIT_BYTES = 56 * 1024 * 1024


def _cparams(*sem):
    return pltpu.CompilerParams(dimension_semantics=sem, vmem_limit_bytes=VMEM_LIMIT_BYTES)


def _dot(a, b, precision=None):
    return jnp.dot(a, b, preferred_element_type=F32, precision=precision)


def _dot_nt(a, b, precision=None):
    return lax.dot_general(a, b, (((1,), (1,)), ((), ())), preferred_element_type=F32, precision=precision)


def _pick(n, pref):
    if n <= pref:
        return n
    t = pref
    while n % t:
        t //= 2
    return t


def _mm_body(*refs, n_w, n_ex, n_out, nk, epi, precision, has_a_add, a_act):
    a_ref = refs[0]
    pos = 1
    a_add_ref = None
    if has_a_add:
        a_add_ref = refs[pos]
        pos += 1
    w_refs = refs[pos:pos + n_w]
    pos += n_w
    ex_refs = refs[pos:pos + n_ex]
    pos += n_ex
    o_refs = refs[pos:pos + n_out]
    acc_refs = refs[pos + n_out:]

    def finish(parts):
        res = epi(parts, [e[...] for e in ex_refs])
        if n_out == 1 and not isinstance(res, (tuple, list)):
            res = (res,)
        for o_ref, val in zip(o_refs, res, strict=True):
            o_ref[...] = val.astype(o_ref.dtype)

    a = a_ref[...]
    if a_add_ref is not None:
        a = a.astype(F32) + a_add_ref[...]
    if a_act is not None:
        a = a_act(a)
    if precision is None:
        a = a.astype(BF16)
    parts = []
    for w_ref in w_refs:
        w = w_ref[...]
        if precision is None:
            w = w.astype(BF16)
        parts.append(_dot(a, w, precision))

    if nk == 1:
        finish(parts)
    else:
        k = pl.program_id(2)

        @pl.when(k == 0)
        def _():
            for acc, p in zip(acc_refs, parts):
                acc[...] = p

        @pl.when(k > 0)
        def _():
            for acc, p in zip(acc_refs, parts):
                acc[...] += p

        @pl.when(k == nk - 1)
        def _():
            finish([acc[...] for acc in acc_refs])


def _first(parts, extras):
    return parts[0]


def mm(a, ws, *, n_out=None, extras=(), epi=_first, out_dtype=F32, tm=2048, tn=1024, tk=2048, precision=None,
       a_add=None, a_act=None, a_part=(0, 1), cast_in_kernel=False, name="mm"):
    m, kdim = a.shape
    a_s, a_parts = a_part
    assert m % a_parts == 0
    m //= a_parts
    if not isinstance(ws, (list, tuple)):
        ws = [(ws, 0, 0)]
    ws = [tuple(w) + (0,) * (3 - len(w)) for w in ws]
    if n_out is None:
        n_out = ws[0][0].shape[1]
    tm = _pick(m, tm)
    tn = _pick(n_out, tn)
    tk = _pick(kdim, tk)
    nk = kdim // tk
    assert m % tm == 0 and n_out % tn == 0 and kdim % tk == 0
    in_specs = [pl.BlockSpec((tm, tk), lambda i, j, k, o=a_s * (m // tm): (i + o, k))]
    args = [a]
    if a_add is not None:
        in_specs.append(pl.BlockSpec((1, tk), lambda i, j, k: (0, k)))
        args.append(a_add)
    for w, off, koff in ws:
        in_specs.append(pl.BlockSpec((tk, tn), lambda i, j, k, off=off, ko=koff * nk: (k + ko, j + off)))
        args.append(w.astype(BF16) if precision is None and not cast_in_kernel else w)
    for arr, bshape, imap in extras:
        in_specs.append(pl.BlockSpec(bshape, imap))
        args.append(arr)
    scratch = [pltpu.VMEM((tm, tn), F32) for _ in ws] if nk > 1 else []
    multi = isinstance(out_dtype, (tuple, list))
    out_dtypes = tuple(out_dtype) if multi else (out_dtype,)
    body = functools.partial(_mm_body, n_w=len(ws), n_ex=len(extras), n_out=len(out_dtypes), nk=nk, epi=epi,
                             precision=precision, has_a_add=a_add is not None, a_act=a_act)
    outs = pl.pallas_call(
        body,
        grid=(m // tm, n_out // tn, nk),
        in_specs=in_specs,
        out_specs=[pl.BlockSpec((tm, tn), lambda i, j, k: (i, j)) for _ in out_dtypes],
        out_shape=[jax.ShapeDtypeStruct((m, n_out), dt) for dt in out_dtypes],
        scratch_shapes=scratch,
        compiler_params=_cparams("parallel", "parallel", "arbitrary"),
        name=name,
    )(*args)
    return tuple(outs) if multi else outs[0]


def _row_extra(vec, tn):
    return (vec.reshape(1, -1), (1, tn), lambda i, j, k: (0, j))


def _rms(x, g):
    return x * lax.rsqrt(jnp.mean(x * x, axis=-1, keepdims=True) + NORM_EPS) * g


def _resid_norm_body(*refs, has_y, has_h):
    pos = 0
    x_ref = refs[pos]; pos += 1
    if has_y:
        y_ref, gpost_ref, gate_ref = refs[pos:pos + 3]; pos += 3
    if has_h:
        gpre_ref, scale_ref, shift_ref = refs[pos:pos + 3]; pos += 3
    outs = refs[pos:]
    x = x_ref[0]
    o = 0
    if has_y:
        y = y_ref[0].astype(F32)
        x = x + (1.0 + gate_ref[0]) * _rms(y, gpost_ref[...])
        outs[o][0] = x
        o += 1
    if has_h:
        h = _rms(x, gpre_ref[...]) * (1.0 + scale_ref[0]) + shift_ref[0]
        outs[o][0] = h.astype(outs[o].dtype)


def resid_norm(x, y=None, post=None, pre=None, tt=512):
    b, t, d = x.shape
    tt = _pick(t, tt)
    row = pl.BlockSpec((1, tt, d), lambda bi, ti: (bi, ti, 0))
    vec = pl.BlockSpec((1, d), lambda bi, ti: (0, 0))
    bvec = pl.BlockSpec((1, 1, d), lambda bi, ti: (bi, 0, 0))
    args, in_specs, out_shapes, out_specs = [x], [row], [], []
    if y is not None:
        args += [y, post[0].reshape(1, d), post[1].reshape(b, 1, d)]
        in_specs += [row, vec, bvec]
        out_shapes.append(jax.ShapeDtypeStruct((b, t, d), F32))
        out_specs.append(row)
    if pre is not None:
        args += [pre[0].reshape(1, d), pre[1].reshape(b, 1, d), pre[2].reshape(b, 1, d)]
        in_specs += [vec, bvec, bvec]
        out_shapes.append(jax.ShapeDtypeStruct((b, t, d), BF16))
        out_specs.append(row)
    res = pl.pallas_call(
        functools.partial(_resid_norm_body, has_y=y is not None, has_h=pre is not None),
        grid=(b, t // tt),
        in_specs=in_specs,
        out_specs=out_specs,
        out_shape=out_shapes,
        compiler_params=_cparams("parallel", "parallel"),
        name="resid_norm",
    )(*args)
    return tuple(res)


def _add_bias(parts, extras):
    return parts[0] + extras[0]


def project(act, ws, bias, name):
    if bias is None:
        return mm(act, ws, out_dtype=BF16, name=name)
    tn = _pick(ws[0][0].shape[1], 1024)
    return mm(act, ws, epi=_add_bias, extras=[_row_extra(bias, tn)], tn=tn, out_dtype=BF16, name=name)


def _mlp_up_body(x_ref, y_ref, gpost_ref, gate_ref, gpre_ref, scale_ref, shift_ref, w_ref, a_ref, xn_ref, h0_ref,
                 h1_ref, *, n_tiles, rows):
    i, j = pl.program_id(0), pl.program_id(1)
    bufs = (h0_ref, h1_ref)

    tm = a_ref.shape[0]
    pieces = 8
    nr, mr = rows // pieces, tm // pieces

    def norm_rows(dst_ref, s):
        sl = pl.ds(s * nr, nr)
        xn = x_ref[sl, :] + (1.0 + gate_ref[0]) * _rms(y_ref[sl, :].astype(F32), gpost_ref[...])
        xn_ref[sl, :] = xn
        h = _rms(xn, gpre_ref[...]) * (1.0 + scale_ref[0]) + shift_ref[0]
        dst_ref[pl.ds(pl.multiple_of(j * rows + s * nr, nr), nr), :] = h.astype(dst_ref.dtype)

    def up_project(src_ref, w, s):
        sl = pl.ds(s * mr, mr)
        r = jnp.maximum(_dot(src_ref[sl, :], w), 0.0)
        a_ref[sl, :] = (r * r).astype(a_ref.dtype)

    @pl.when(i == 0)
    def _():
        for s in range(pieces):
            norm_rows(bufs[0], s)

    for parity in (0, 1):
        @pl.when((i > 0) & (i < n_tiles) & (i % 2 == parity))
        def _():
            w = w_ref[...].astype(BF16)
            for s in range(pieces):
                up_project(bufs[1 - parity], w, s)
                norm_rows(bufs[parity], s)

    @pl.when(i == n_tiles)
    def _():
        w = w_ref[...].astype(BF16)
        for s in range(pieces):
            up_project(bufs[(n_tiles - 1) % 2], w, s)


def mlp_up_resid(x, y, post, pre, w1, layer, tm=2048, tn=512):
    b, t, d = x.shape
    nl, _, f = w1.shape
    m = b * t
    tm = _pick(t, tm)
    tn = _pick(f, tn)
    n_tiles, nj = m // tm, f // tn
    assert tm % nj == 0
    rows = tm // nj
    assert rows % 16 == 0
    tiles_per_batch = t // tm
    slice_idx = lambda i, j: jnp.where(i < n_tiles, i * nj + j, n_tiles * nj - 1)
    rows_spec = pl.BlockSpec((rows, d), lambda i, j: (slice_idx(i, j), 0))
    vec = pl.BlockSpec((1, d), lambda i, j: (0, 0))
    bvec = pl.BlockSpec((1, 1, d), lambda i, j: (jnp.minimum(i, n_tiles - 1) // tiles_per_batch, 0, 0))
    a, x_new = pl.pallas_call(
        functools.partial(_mlp_up_body, n_tiles=n_tiles, rows=rows),
        grid=(n_tiles + 1, nj),
        in_specs=[rows_spec, rows_spec, vec, bvec, vec, bvec, bvec,
                  pl.BlockSpec((d, tn), lambda i, j: (layer, j))],
        out_specs=[pl.BlockSpec((tm, tn), lambda i, j: (jnp.maximum(i - 1, 0), jnp.where(i > 0, j, 0))),
                   rows_spec],
        out_shape=[jax.ShapeDtypeStruct((m, f), BF16), jax.ShapeDtypeStruct((m, d), F32)],
        scratch_shapes=[pltpu.VMEM((tm, d), BF16), pltpu.VMEM((tm, d), BF16)],
        compiler_params=_cparams("arbitrary", "arbitrary"),
        name="mlp_up_resid",
    )(x.reshape(m, d), y.reshape(m, d), post[0].reshape(1, d), post[1].reshape(b, 1, d), pre[0].reshape(1, d),
      pre[1].reshape(b, 1, d), pre[2].reshape(b, 1, d), w1.reshape(nl * d, f))
    return a, x_new.reshape(b, t, d)


def _mlp_down_body(*refs, n_tiles, rows, has_pre):
    a_ref, w_ref, x_ref, gpost_ref, gate_ref = refs[:5]
    if has_pre:
        gpre_ref, scale_ref, shift_ref, xn_ref, h_ref, acc0_ref, acc1_ref = refs[5:]
    else:
        xn_ref, acc0_ref, acc1_ref = refs[5:]
    i, k = pl.program_id(0), pl.program_id(1)
    accs = (acc0_ref, acc1_ref)
    tm = a_ref.shape[0]
    pieces = 4
    mr = tm // pieces
    nr = rows // pieces

    @pl.when((i == 0) & (k == 0))
    def _():
        acc0_ref[...] = jnp.zeros_like(acc0_ref)
        acc1_ref[...] = jnp.zeros_like(acc1_ref)

    def accumulate(dst_ref, w, s):
        sl = pl.ds(s * mr, mr)
        part = _dot(a_ref[sl, :], w)
        dst_ref[sl, :] = jnp.where(k == 0, part, dst_ref[sl, :] + part)

    def finish_rows(src_ref, s):
        sl = pl.ds(s * nr, nr)
        y = src_ref[pl.ds(pl.multiple_of(k * rows + s * nr, nr), nr), :]
        xn = x_ref[sl, :] + (1.0 + gate_ref[0]) * _rms(y, gpost_ref[...])
        xn_ref[sl, :] = xn
        if has_pre:
            h_ref[sl, :] = (_rms(xn, gpre_ref[...]) * (1.0 + scale_ref[0]) + shift_ref[0]).astype(h_ref.dtype)

    @pl.when(i == 0)
    def _():
        w = w_ref[...]
        for s in range(pieces):
            accumulate(accs[0], w, s)

    for parity in (0, 1):
        @pl.when((i > 0) & (i < n_tiles) & (i % 2 == parity))
        def _():
            w = w_ref[...]
            for s in range(pieces):
                accumulate(accs[parity], w, s)
                finish_rows(accs[1 - parity], s)

    @pl.when(i == n_tiles)
    def _():
        for s in range(pieces):
            finish_rows(accs[(n_tiles - 1) % 2], s)


def mlp_down_resid(a, w2, layer, x, post, pre=None, tm=1024, tk=1024):
    b, t, d = x.shape
    nl, f, _ = w2.shape
    m = b * t
    tm = _pick(t, tm)
    tk = _pick(f, tk)
    n_tiles, nk = m // tm, f // tk
    assert tm % nk == 0
    rows = tm // nk
    assert rows % (8 * 16) == 0
    tiles_per_batch = t // tm
    slice_idx = lambda i, k: jnp.where(i > 0, (i - 1) * nk + k, 0)
    rows_spec = pl.BlockSpec((rows, d), lambda i, k: (slice_idx(i, k), 0))
    vec = pl.BlockSpec((1, d), lambda i, k: (0, 0))
    bvec = pl.BlockSpec((1, 1, d), lambda i, k: (jnp.maximum(i - 1, 0) // tiles_per_batch, 0, 0))
    in_specs = [pl.BlockSpec((tm, tk), lambda i, k: (jnp.minimum(i, n_tiles - 1), jnp.where(i < n_tiles, k, nk - 1))),
                pl.BlockSpec((tk, d), lambda i, k: (layer * nk + jnp.where(i < n_tiles, k, nk - 1), 0)),
                rows_spec, vec, bvec]
    args = [a, w2.reshape(nl * f, d).astype(BF16), x.reshape(m, d), post[0].reshape(1, d), post[1].reshape(b, 1, d)]
    out_specs, out_shapes = [rows_spec], [jax.ShapeDtypeStruct((m, d), F32)]
    if pre is not None:
        in_specs += [vec, bvec, bvec]
        args += [pre[0].reshape(1, d), pre[1].reshape(b, 1, d), pre[2].reshape(b, 1, d)]
        out_specs.append(rows_spec)
        out_shapes.append(jax.ShapeDtypeStruct((m, d), BF16))
    outs = pl.pallas_call(
        functools.partial(_mlp_down_body, n_tiles=n_tiles, rows=rows, has_pre=pre is not None),
        grid=(n_tiles + 1, nk),
        in_specs=in_specs,
        out_specs=out_specs,
        out_shape=out_shapes,
        scratch_shapes=[pltpu.VMEM((tm, d), F32), pltpu.VMEM((tm, d), F32)],
        compiler_params=_cparams("arbitrary", "arbitrary"),
        name="mlp_down_resid",
    )(*args)
    return tuple(o.reshape(b, t, d) for o in outs)


def _glu(parts, extras):
    return (parts[0] + extras[0]) * jax.nn.sigmoid(parts[1] + extras[1])


def _conv_ln_body(cur_ref, prev_ref, w_ref, b_ref, g_ref, beta_ref, o_ref, buf_ref, acc_ref, *, tt, d):
    ti = pl.program_id(1)
    halo = prev_ref[0].astype(F32)
    buf_ref[0:CONV_HALO, :] = jnp.where(ti == 0, jnp.zeros_like(halo), halo)
    buf_ref[CONV_HALO:, :] = cur_ref[0].astype(F32)
    lane_chunk = min(d, 256)
    row_chunk = min(tt, 64)
    base = CONV_HALO - (CONV_WIDTH - 1)
    sub = 8
    for c0 in range(0, d, lane_chunk):
        for r0 in range(0, tt, row_chunk):
            acc = None
            for s in range(sub):
                taps = [j for j in range(CONV_WIDTH) if (base + j) % sub == s]
                if not taps:
                    continue
                rows = row_chunk + (sub if s else 0)
                part = jnp.zeros((rows, lane_chunk), F32)
                for j in taps:
                    off = r0 + base + j - s
                    part = part + buf_ref[off:off + rows, c0:c0 + lane_chunk] * w_ref[j:j + 1, c0:c0 + lane_chunk]
                part = part[s:s + row_chunk]
                acc = part if acc is None else acc + part
            acc_ref[r0:r0 + row_chunk, c0:c0 + lane_chunk] = acc
    y = acc_ref[...] + b_ref[...]
    mu = jnp.mean(y, axis=-1, keepdims=True)
    yc = y - mu
    var = jnp.mean(yc * yc, axis=-1, keepdims=True)
    z = yc * lax.rsqrt(var + CONV_LN_EPS) * g_ref[...] + beta_ref[...]
    o_ref[0] = (z * jax.nn.sigmoid(z)).astype(o_ref.dtype)


def conv_ln_silu(u, dw_w, dw_b, ln_g, ln_b, tt=128):
    b, t, d = u.shape
    tt = _pick(t, tt)
    hb = tt // CONV_HALO
    vec = pl.BlockSpec((1, d), lambda bi, ti: (0, 0))
    return pl.pallas_call(
        functools.partial(_conv_ln_body, tt=tt, d=d),
        grid=(b, t // tt),
        in_specs=[
            pl.BlockSpec((1, tt, d), lambda bi, ti: (bi, ti, 0)),
            pl.BlockSpec((1, CONV_HALO, d), lambda bi, ti: (bi, jnp.maximum(ti * hb - 1, 0), 0)),
            pl.BlockSpec((CONV_WIDTH, d), lambda bi, ti: (0, 0)),
            vec, vec, vec,
        ],
        out_specs=pl.BlockSpec((1, tt, d), lambda bi, ti: (bi, ti, 0)),
        out_shape=jax.ShapeDtypeStruct((b, t, d), BF16),
        scratch_shapes=[pltpu.VMEM((tt + CONV_HALO, d), F32), pltpu.VMEM((tt, d), F32)],
        compiler_params=_cparams("parallel", "parallel"),
        name="conv_ln_silu",
    )(u, u, dw_w, dw_b.reshape(1, d), ln_g.reshape(1, d), ln_b.reshape(1, d))


def conformer_conv_mix(h, pw1_w, pw1_b, dw_w, dw_b, ln_g, ln_b, pw2_w, pw2_b):
    b, t, d = h.shape
    tn = _pick(d, 1024)
    b1 = pw1_b.reshape(1, -1)
    u = mm(h.reshape(b * t, d), [(pw1_w, 0), (pw1_w, d // tn)], n_out=d, tm=1024, tn=tn, epi=_glu,
           extras=[(b1, (1, tn), lambda i, j, k: (0, j)),
                   (b1, (1, tn), lambda i, j, k, o=d // tn: (0, j + o))], out_dtype=BF16, name="conf_pw1_glu")
    z = conv_ln_silu(u.reshape(b, t, d), dw_w, dw_b, ln_g, ln_b)
    return z.reshape(b * t, d), [(pw2_w, 0, 0)], pw2_b


def _rope(x, cos, sin):
    half = x.shape[-1] // 2
    x1, x2 = x[:, :half], x[:, half:]
    return jnp.concatenate([x1 * cos - x2 * sin, x2 * cos + x1 * sin], axis=-1)


def _retention_body(q_ref, k_ref, v_ref, gate_ref, cos_ref, sin_ref, inner_ref, qdec_ref, kdec_ref, cdec_ref,
                    gng_ref, gnb_ref, o_ref, state_ref, *, dk):
    @pl.when(pl.program_id(2) == 0)
    def _():
        state_ref[...] = jnp.zeros_like(state_ref)

    cos, sin = cos_ref[...], sin_ref[...]
    q = _rope(q_ref[0].astype(F32), cos, sin)
    k = _rope(k_ref[0].astype(F32), cos, sin) * (dk ** -0.5)
    v = v_ref[0].astype(BF16)
    qb = q.astype(BF16)
    s = _dot_nt(qb, k.astype(BF16)) * inner_ref[0]
    state = state_ref[...]
    o = _dot(s.astype(BF16), v) + _dot(qb, state.astype(BF16)) * qdec_ref[0]
    kd_t = (k * kdec_ref[0]).T.astype(BF16)
    state_ref[...] = state * cdec_ref[0] + _dot(kd_t, v)
    mu = jnp.mean(o, axis=-1, keepdims=True)
    oc = o - mu
    var = jnp.mean(oc * oc, axis=-1, keepdims=True)
    on = oc * lax.rsqrt(var + RET_GN_EPS) * gng_ref[...] + gnb_ref[...]
    gate = gate_ref[0].astype(F32)
    o_ref[0] = (gate * jax.nn.sigmoid(gate) * on).astype(o_ref.dtype)


def retention_mix(h, w_in, gn_g, gn_b, w_out):
    b, t, d = h.shape
    nh, c = RET_HEADS, _pick(t, RET_CHUNK)
    dk = d // nh
    dv = 2 * dk
    proj = mm(h.reshape(b * t, d), w_in, out_dtype=BF16, tm=2048, tn=512, cast_in_kernel=True,
              name="ret_in").reshape(b, t, 6 * d)
    pos = jnp.arange(t, dtype=F32)
    inv_freq = ROPE_BASE ** (-jnp.arange(0, dk, 2, dtype=F32) / dk)
    ang = pos[:, None] * inv_freq[None, :]
    cos, sin = jnp.cos(ang), jnp.sin(ang)
    log_gamma = jnp.log(1.0 - 2.0 ** (-5.0 - jnp.arange(nh, dtype=F32)))
    idx = jnp.arange(c, dtype=F32)
    diff = idx[:, None] - idx[None, :]
    inner = jnp.where(diff >= 0, jnp.exp(jnp.maximum(diff, 0.0)[None] * log_gamma[:, None, None]), 0.0)
    q_dec = jnp.exp((idx + 1.0)[None] * log_gamma[:, None])[:, :, None]
    k_dec = jnp.exp((c - 1.0 - idx)[None] * log_gamma[:, None])[:, :, None]
    c_dec = jnp.exp(c * log_gamma)[:, None, None]
    nq = (nh * dk) // dk
    o = pl.pallas_call(
        functools.partial(_retention_body, dk=dk),
        grid=(b, nh, t // c),
        in_specs=[
            pl.BlockSpec((1, c, dk), lambda bi, hi, ci: (bi, ci, hi)),
            pl.BlockSpec((1, c, dk), lambda bi, hi, ci: (bi, ci, nq + hi)),
            pl.BlockSpec((1, c, dv), lambda bi, hi, ci: (bi, ci, nq + hi)),
            pl.BlockSpec((1, c, dv), lambda bi, hi, ci: (bi, ci, 2 * nq + hi)),
            pl.BlockSpec((c, dk // 2), lambda bi, hi, ci: (ci, 0)),
            pl.BlockSpec((c, dk // 2), lambda bi, hi, ci: (ci, 0)),
            pl.BlockSpec((1, c, c), lambda bi, hi, ci: (hi, 0, 0)),
            pl.BlockSpec((1, c, 1), lambda bi, hi, ci: (hi, 0, 0)),
            pl.BlockSpec((1, c, 1), lambda bi, hi, ci: (hi, 0, 0)),
            pl.BlockSpec((1, 1, 1), lambda bi, hi, ci: (hi, 0, 0)),
            pl.BlockSpec((1, dv), lambda bi, hi, ci: (0, hi)),
            pl.BlockSpec((1, dv), lambda bi, hi, ci: (0, hi)),
        ],
        out_specs=pl.BlockSpec((1, c, dv), lambda bi, hi, ci: (bi, ci, hi)),
        out_shape=jax.ShapeDtypeStruct((b, t, nh * dv), BF16),
        scratch_shapes=[pltpu.VMEM((dk, dv), F32)],
        compiler_params=_cparams("parallel", "parallel", "arbitrary"),
        name="retention",
    )(proj, proj, proj, proj, cos, sin, inner, q_dec, k_dec, c_dec, gn_g.reshape(1, -1), gn_b.reshape(1, -1))
    return o.reshape(b * t, nh * dv), [(w_out, 0, 0)], None


def _rwkv_pre_body(x_ref, prev_ref, g_ref, scale_ref, shift_ref, mu_ref, o_ref):
    ti = pl.program_id(1)
    g, scale, shift = g_ref[...], scale_ref[0], shift_ref[0]
    h = _rms(x_ref[0], g) * (1.0 + scale) + shift
    hp = _rms(prev_ref[0], g) * (1.0 + scale) + shift
    last = jnp.where(ti == 0, 0.0, hp[7:8, :])
    row = lax.broadcasted_iota(jnp.int32, h.shape, 0)
    shifted = jnp.where(row == 0, last, pltpu.roll(h, 1, axis=0))
    xx = shifted - h
    for s in range(6):
        o_ref[s, 0] = (h + xx * mu_ref[s:s + 1, :]).astype(o_ref.dtype)


def rwkv_pre(x, g_pre, scale, shift, mu, tt=256):
    b, t, d = x.shape
    tt = _pick(t, tt)
    bvec = pl.BlockSpec((1, 1, d), lambda bi, ti: (bi, 0, 0))
    return pl.pallas_call(
        _rwkv_pre_body,
        grid=(b, t // tt),
        in_specs=[
            pl.BlockSpec((1, tt, d), lambda bi, ti: (bi, ti, 0)),
            pl.BlockSpec((1, 8, d), lambda bi, ti: (bi, jnp.maximum(ti * (tt // 8) - 1, 0), 0)),
            pl.BlockSpec((1, d), lambda bi, ti: (0, 0)),
            bvec, bvec,
            pl.BlockSpec((6, d), lambda bi, ti: (0, 0)),
        ],
        out_specs=pl.BlockSpec((6, 1, tt, d), lambda bi, ti: (0, bi, ti, 0)),
        out_shape=jax.ShapeDtypeStruct((6, b, t, d), BF16),
        compiler_params=_cparams("parallel", "parallel"),
        name="rwkv_pre",
    )(x, x, g_pre.reshape(1, d), scale.reshape(b, 1, d), shift.reshape(b, 1, d), mu)


def _rwkv_scan_body(r_ref, k_ref, v_ref, a_ref, lw_ref, g_ref, kk_ref, ka_ref, rk_ref, lng_ref, lnb_ref, o_ref,
                    state_ref, *, chunk, heads, n):
    @pl.when(pl.program_id(2) == 0)
    def _():
        state_ref[...] = jnp.zeros_like(state_ref)

    ln = chunk
    hs = range(heads)
    row = lax.broadcasted_iota(jnp.int32, (ln, ln), 0)
    col = lax.broadcasted_iota(jnp.int32, (ln, ln), 1)
    tri = (col <= row).astype(BF16)
    eye = (row == col).astype(F32)
    wid = heads * n
    seg_w = min(wid, 256)
    assert wid % seg_w == 0 and seg_w % n == 0
    seg = (lax.broadcasted_iota(jnp.int32, (seg_w, seg_w), 0) // n
           == lax.broadcasted_iota(jnp.int32, (seg_w, seg_w), 1) // n).astype(BF16)

    def split_dot(m, x):
        hi = x.astype(BF16)
        lo = (x - hi.astype(F32)).astype(BF16)
        return _dot(m, hi) + _dot(m, lo)

    def seg_sum(x):
        hi = x.astype(BF16)
        lo = (x - hi.astype(F32)).astype(BF16)
        return jnp.concatenate([_dot(hi[:, j:j + seg_w], seg) + _dot(lo[:, j:j + seg_w], seg)
                                for j in range(0, wid, seg_w)], axis=1)

    heads_of = lambda x: [x[:, hd * n:(hd + 1) * n] for hd in hs]

    r, k, v, a = (ref[0].astype(F32) for ref in (r_ref, k_ref, v_ref, a_ref))
    lw = lw_ref[0]
    kk = k * kk_ref[...]
    kk = kk / jnp.maximum(jnp.sqrt(seg_sum(kk * kk)), 1e-12)
    k2 = k * (1.0 + (a - 1.0) * ka_ref[...])
    beta = kk * a
    c = split_dot(tri, lw)
    c_last = c[ln - 1:ln, :]
    e_neg = jnp.exp(-c)
    e_rem = jnp.exp(c_last - c)
    e_last = heads_of(jnp.exp(c_last))
    a_t = heads_of(-kk * jnp.exp(c - lw))
    r_t = heads_of(r * jnp.exp(c))
    b_t = heads_of(beta * e_neg)
    k_t = heads_of(k2 * e_neg)
    b_rem = heads_of(beta * e_rem)
    k_rem = heads_of(k2 * e_rem)
    vs = heads_of(v)
    bonus = seg_sum(r * k2 * rk_ref[...]) * v

    ar = [jnp.concatenate([a_t[h], r_t[h]], axis=0).astype(BF16) for h in hs]
    bk = [jnp.concatenate([b_t[h], k_t[h]], axis=0).astype(BF16) for h in hs]
    bk_rem = [jnp.concatenate([b_rem[h], k_rem[h]], axis=0).astype(BF16) for h in hs]
    vb = [vs[h].astype(BF16) for h in hs]
    state = [state_ref[h] for h in hs]
    p = [_dot_nt(ar[h], bk[h]) for h in hs]
    q0 = [_dot_nt(ar[h], state[h].astype(BF16)) for h in hs]
    row2 = lax.broadcasted_iota(jnp.int32, (2 * ln, 2 * ln), 0)
    col2 = lax.broadcasted_iota(jnp.int32, (2 * ln, 2 * ln), 1)
    keep = col2 % ln < row2 % ln + row2 // ln
    pm = [jnp.where(keep, p[h], 0.0) for h in hs]
    m_ab = [pm[h][:ln, :ln] for h in hs]
    m_ak = [pm[h][:ln, ln:].astype(BF16) for h in hs]
    m_r = [pm[h][ln:].astype(BF16) for h in hs]
    rhs = [q0[h][:ln] + _dot(m_ak[h], vb[h]) for h in hs]
    inv = [eye + m_ab[h] for h in hs]
    pw = [m_ab[h].astype(BF16) for h in hs]
    for _ in range(int(math.log2(ln)) - 1):
        pw = [_dot(pw[h], pw[h]).astype(BF16) for h in hs]
        inv = [inv[h] + _dot(inv[h].astype(BF16), pw[h]) for h in hs]
    u = [_dot(inv[h].astype(BF16), rhs[h].astype(BF16)) for h in hs]
    uv = [jnp.concatenate([u[h], vs[h]], axis=0) for h in hs]
    y = [q0[h][ln:] + _dot(m_r[h], uv[h].astype(BF16)) for h in hs]
    for h in hs:
        state_ref[h] = state[h] * e_last[h] + _dot(uv[h].T.astype(BF16), bk_rem[h])
    yn = []
    for h in hs:
        mu = jnp.mean(y[h], axis=-1, keepdims=True)
        yc = y[h] - mu
        var = jnp.mean(yc * yc, axis=-1, keepdims=True)
        yn.append(yc * lax.rsqrt(var + RWKV_GN_EPS))
    yn = jnp.concatenate(yn, axis=-1) * lng_ref[...] + lnb_ref[...]
    o_ref[0] = ((yn + bonus) * g_ref[0]).astype(o_ref.dtype)


def rwkv_scan(r, k, v, a, lw, g, k_k, k_a, r_k, ln_g, ln_b):
    b, t, d = r.shape
    n = RWKV_HEAD_DIM
    hps = min(RWKV_HEADS_PER_STEP, d // n)
    w = hps * n
    ln = _pick(t, RWKV_CHUNK)
    tok = pl.BlockSpec((1, ln, w), lambda bi, hi, ci: (bi, ci, hi))
    vec = pl.BlockSpec((1, w), lambda bi, hi, ci: (0, hi))
    return pl.pallas_call(
        functools.partial(_rwkv_scan_body, chunk=ln, heads=hps, n=n),
        grid=(b, d // w, t // ln),
        in_specs=[tok] * 6 + [vec] * 5,
        out_specs=tok,
        out_shape=jax.ShapeDtypeStruct((b, t, d), BF16),
        scratch_shapes=[pltpu.VMEM((hps, n, n), F32)],
        compiler_params=_cparams("parallel", "parallel", "arbitrary"),
        name="rwkv_scan",
    )(r, k, v, a, lw, g, k_k.reshape(1, d), k_a.reshape(1, d), r_k.reshape(1, d), ln_g.reshape(1, d),
      ln_b.reshape(1, d))


def _tanh_epi(parts, extras):
    return jnp.tanh(parts[0])


def _sigmoid_epi(parts, extras):
    return jax.nn.sigmoid(parts[0])


def _sigmoid_bias_epi(parts, extras):
    return jax.nn.sigmoid(extras[0] + parts[0])


def _logdecay_epi(parts, extras):
    return -jnp.exp(-jax.nn.softplus(-(extras[0] + parts[0])) - 0.5)


def rwkv7_time_mix(x, g_pre, scale, shift, mu, w_rkv, w0, w_la, w_lb, a0, a_la, a_lb, g_la, g_lb, k_k, k_a, r_k,
                   ln_g, ln_b, w_out):
    b, t, d = x.shape
    xs = rwkv_pre(x, g_pre, scale, shift, mu).reshape(6 * b * t, d)
    tn = _pick(d, 1024)
    w_rkv2d = w_rkv.reshape(3 * d, d)
    r = mm(xs, [(w_rkv2d, 0, 0)], a_part=(0, 6), out_dtype=BF16, name="rwkv_r")
    k = mm(xs, [(w_rkv2d, 0, 1)], a_part=(1, 6), out_dtype=BF16, name="rwkv_k")
    v = mm(xs, [(w_rkv2d, 0, 2)], a_part=(2, 6), out_dtype=BF16, name="rwkv_v")
    lw = mm(mm(xs, w_la, a_part=(3, 6), epi=_tanh_epi, out_dtype=BF16, name="rwkv_w_la"), w_lb, epi=_logdecay_epi,
            extras=[_row_extra(w0, tn)], name="rwkv_w_lb")
    a = mm(mm(xs, a_la, a_part=(4, 6), out_dtype=BF16, name="rwkv_a_la"), a_lb, epi=_sigmoid_bias_epi,
           extras=[_row_extra(a0, tn)], out_dtype=BF16, name="rwkv_a_lb")
    g = mm(mm(xs, g_la, a_part=(5, 6), epi=_sigmoid_epi, out_dtype=BF16, name="rwkv_g_la"), g_lb, out_dtype=BF16,
           name="rwkv_g_lb")
    sh = lambda z: z.reshape(b, t, d)
    z = rwkv_scan(sh(r), sh(k), sh(v), sh(a), sh(lw), sh(g), k_k, k_a, r_k, ln_g, ln_b)
    return z.reshape(b * t, d), [(w_out, 0, 0)], None


def _nsa_compress_body(kv_ref, pe_ref, w1_ref, w2_ref, o_ref, buf_ref, *, nchunk, dh):
    st = NSA_CMP_STRIDE
    hid = w1_ref.shape[1]
    buf_ref[...] = kv_ref[0].astype(F32)
    first = jnp.zeros((nchunk, hid), F32)
    second = jnp.zeros((nchunk, hid), F32)
    for j in range(st):
        xj = buf_ref[pl.ds(j, nchunk, stride=st), :]
        first = first + _dot((xj + pe_ref[j:j + 1, :]).astype(BF16), w1_ref[j * dh:(j + 1) * dh, :])
        second = second + _dot((xj + pe_ref[st + j:st + j + 1, :]).astype(BF16),
                               w1_ref[(st + j) * dh:(st + j + 1) * dh, :])
    hidden = jax.nn.gelu(first + pltpu.roll(second, nchunk - 1, axis=0))
    out = _dot(hidden.astype(BF16), w2_ref[...])
    row = lax.broadcasted_iota(jnp.int32, out.shape, 0)
    o_ref[0, 0] = jnp.where(row < nchunk - 1, out, 0.0)


def nsa_compress(proj, col0, pe, w1, w2):
    b, t, _ = proj.shape
    g, dh = NSA_KV_GROUPS, NSA_HEAD_DIM
    nchunk = t // NSA_CMP_STRIDE
    hid = w1.shape[1]
    return pl.pallas_call(
        functools.partial(_nsa_compress_body, nchunk=nchunk, dh=dh),
        grid=(b, g),
        in_specs=[
            pl.BlockSpec((1, t, dh), lambda bi, gi: (bi, 0, col0 // dh + gi)),
            pl.BlockSpec((NSA_CMP_BLOCK, dh), lambda bi, gi: (0, 0)),
            pl.BlockSpec((NSA_CMP_BLOCK * dh, hid), lambda bi, gi: (0, 0)),
            pl.BlockSpec((hid, dh), lambda bi, gi: (0, 0)),
        ],
        out_specs=pl.BlockSpec((1, 1, nchunk, dh), lambda bi, gi: (bi, gi, 0, 0)),
        out_shape=jax.ShapeDtypeStruct((b, g, nchunk, dh), F32),
        scratch_shapes=[pltpu.VMEM((t, dh), F32)],
        compiler_params=_cparams("parallel", "parallel"),
        name="nsa_compress",
    )(proj, pe, w1.astype(BF16), w2.astype(BF16))


def _stack_heads(q, r, dh):
    return jnp.concatenate([q[:, i * dh:(i + 1) * dh] for i in range(r)], axis=0)


def _unstack_heads(o, r, tq):
    return jnp.concatenate([o[i * tq:(i + 1) * tq] for i in range(r)], axis=-1)


def _gate_rows(gates, branch, r, tq):
    return jnp.concatenate([gates[:, branch * r + i:branch * r + i + 1] for i in range(r)], axis=0)


def _nsa_cmp_body(q_ref, kc_ref, vc_ref, gates_ref, selmap_t_ref, o_ref, sel_ref, *, tq, r, dh, n_sel):
    q0 = pl.program_id(2) * tq
    qs = (_stack_heads(q_ref[0].astype(F32), r, dh) * (dh ** -0.5)).astype(BF16)
    kc, vc = kc_ref[0, 0].astype(BF16), vc_ref[0, 0].astype(BF16)
    ncp = kc.shape[0]
    s = _dot_nt(qs, kc)
    rows = lax.broadcasted_iota(jnp.int32, (r * tq, ncp), 0)
    cmp_end = lax.broadcasted_iota(jnp.int32, (r * tq, ncp), 1) * NSA_CMP_STRIDE + (NSA_CMP_BLOCK - 1)
    t_pos = q0 + rows % tq
    vis = cmp_end <= t_pos
    s = jnp.where(vis, s, NEG_INF)
    e = jnp.exp(s - jnp.max(s, axis=-1, keepdims=True))
    p = e / jnp.sum(e, axis=-1, keepdims=True) * vis.astype(F32)
    o = _dot(p.astype(BF16), vc)
    o = o * _gate_rows(gates_ref[0, 0], 0, r, tq)
    o_ref[0] = _unstack_heads(o, r, tq)
    p_sum = p[0:tq]
    for i in range(1, r):
        p_sum = p_sum + p[i * tq:(i + 1) * tq]
    hi = p_sum.astype(BF16)
    lo = (p_sum - hi.astype(F32)).astype(BF16)
    selmap_t = selmap_t_ref[...]
    imp = _dot_nt(selmap_t, hi) + _dot_nt(selmap_t, lo)
    ns = imp.shape[0]
    blk = lax.broadcasted_iota(jnp.int32, (ns, tq), 0)
    tq_pos = q0 + lax.broadcasted_iota(jnp.int32, (ns, tq), 1)
    cur = tq_pos // NSA_SEL_BLOCK
    valid = blk * NSA_SEL_BLOCK <= tq_pos
    forced = (blk == 0) | (blk == cur) | (blk == cur - 1)
    score = jnp.where(valid, jnp.where(forced, POS_BIG, imp), NEG_INF)
    rank = jnp.zeros((ns, tq), F32)
    for m in range(ns):
        sm = score[m:m + 1, :]
        beats = (sm > score) | ((sm == score) & (blk > m))
        rank = rank + beats.astype(F32)
    chosen = ((rank < n_sel) & valid).astype(F32)
    sel_ref[0, 0] = (chosen.T - 1.0).astype(sel_ref.dtype)


def _flash_update(s, v_ones, m_ref, acc_ref):
    lanes = m_ref.shape[1]
    assert s.shape[1] % lanes == 0 and acc_ref.shape[1] == 2 * lanes
    m_old = m_ref[...]
    m_new = jnp.maximum(m_old, jnp.max(s, axis=-1, keepdims=True))
    alpha = jnp.exp2(m_old - m_new)
    p = jnp.exp2((s - jnp.concatenate([m_new] * (s.shape[1] // lanes), axis=1)).astype(BF16))
    acc_ref[...] = jnp.concatenate([alpha, alpha], axis=1) * acc_ref[...] + _dot(p, v_ones)
    m_ref[...] = m_new


def _nsa_sel_body(qi_tbl, ki_tbl, diag_tbl, q_ref, k_ref, v_ref, gates_ref, oin_ref, sel_ref, o_ref, qa_ref, m_ref,
                  acc_ref, *, tq, tk, r, dh):
    step = pl.program_id(2)
    qi, ki, diag = qi_tbl[step], ki_tbl[step], diag_tbl[step]
    ns = sel_ref.shape[-1]
    assert ns <= dh

    @pl.when(ki == 0)
    def _():
        m_ref[...] = jnp.full_like(m_ref, NEG_INF)
        acc_ref[...] = jnp.zeros_like(acc_ref)
        qs = _stack_heads(q_ref[0].astype(F32), r, dh) * (dh ** -0.5 * LOG2_E)
        pieces = [qs, jnp.concatenate([sel_ref[0, 0].astype(F32)] * r, axis=0)]
        if ns < dh:
            pieces.append(jnp.zeros((r * tq, dh - ns), F32))
        qa_ref[...] = jnp.concatenate(pieces, axis=1).astype(BF16)

    blk_of_key = (ki * tk + lax.broadcasted_iota(jnp.int32, (tk, dh), 0)) // NSA_SEL_BLOCK
    own_block = jnp.where(blk_of_key == lax.broadcasted_iota(jnp.int32, (tk, dh), 1), POS_BIG, 0.0).astype(BF16)
    ka = jnp.concatenate([k_ref[0].astype(BF16), own_block], axis=1)
    s = _dot_nt(qa_ref[...], ka)
    v_ones = jnp.concatenate([v_ref[0].astype(BF16), jnp.ones((tk, dh), BF16)], axis=1)

    @pl.when(diag == 0)
    def _():
        _flash_update(s, v_ones, m_ref, acc_ref)

    @pl.when(diag == 1)
    def _():
        t_pos = qi * tq + lax.broadcasted_iota(jnp.int32, (r * tq, tk), 0) % tq
        k_pos = ki * tk + lax.broadcasted_iota(jnp.int32, (r * tq, tk), 1)
        _flash_update(jnp.where(k_pos <= t_pos, s, NEG_INF), v_ones, m_ref, acc_ref)
        o = acc_ref[:, :dh] / acc_ref[:, dh:] * _gate_rows(gates_ref[0, 0], 1, r, tq)
        o_ref[0] = (oin_ref[0] + _unstack_heads(o, r, tq)).astype(o_ref.dtype)


def _nsa_win_body(q_ref, *rest, tq, r, dh, nwin):
    k_refs, v_refs = rest[:nwin], rest[nwin:2 * nwin]
    gates_ref, oin_ref, o_ref = rest[2 * nwin:]
    qi = pl.program_id(2)
    qs = (_stack_heads(q_ref[0].astype(F32), r, dh) * (dh ** -0.5 * LOG2_E)).astype(BF16)
    t_loc = lax.broadcasted_iota(jnp.int32, (r * tq, tq), 0) % tq
    k_loc = lax.broadcasted_iota(jnp.int32, (r * tq, tq), 1)
    tiles = []
    for j in range(nwin):
        s = _dot_nt(qs, k_refs[j][0].astype(BF16))
        if j == nwin - 1:
            s = jnp.where(k_loc <= t_loc, s, NEG_INF)
        else:
            if j == 0:
                s = jnp.where(k_loc > t_loc, s, NEG_INF)
            s = jnp.where(qi - (nwin - 1) + j >= 0, s, NEG_INF)
        tiles.append(s)
    s = jnp.concatenate(tiles, axis=1)
    p = jnp.exp2((s - jnp.max(s, axis=-1, keepdims=True)).astype(BF16))
    v = jnp.concatenate([v_refs[j][0] for j in range(nwin)], axis=0).astype(BF16)
    on = _dot(p, jnp.concatenate([v, jnp.ones_like(v)], axis=1))
    o = on[:, :dh] / on[:, dh:] * _gate_rows(gates_ref[0, 0], 2, r, tq)
    o_ref[0] = (oin_ref[0] + _unstack_heads(o, r, tq)).astype(o_ref.dtype)


def nsa_mix(h, w_in, pe_k, pe_v, ck_w1, ck_w2, cv_w1, cv_w2, w_out, tq=256):
    b, t, d = h.shape
    nh, g, dh = NSA_HEADS, NSA_KV_GROUPS, NSA_HEAD_DIM
    r = nh // g
    hd, kd = nh * dh, g * dh
    n_main = hd + 6 * kd
    h2 = h.reshape(b * t, d)
    proj = mm(h2, w_in[:, :n_main], out_dtype=BF16, name="nsa_in").reshape(b, t, n_main)
    n_gate = w_in.shape[1] - n_main
    w_gate = jnp.pad(w_in[:, n_main:], ((0, 0), (0, 128 - n_gate)))
    gates = mm(h2, w_gate, epi=_sigmoid_epi, name="nsa_gates")[:, :n_gate]
    gates = jnp.transpose(gates.reshape(b, t, 3, g, r), (0, 3, 1, 2, 4)).reshape(b, g, t, 3 * r)
    k_cmp = nsa_compress(proj, hd, pe_k, ck_w1, ck_w2)
    v_cmp = nsa_compress(proj, hd + kd, pe_v, cv_w1, cv_w2)
    ncp = k_cmp.shape[2]
    ns = t // NSA_SEL_BLOCK
    n_sel = min(NSA_SEL_TOP, ns)
    cs = jnp.arange(ncp)[None, :] * NSA_CMP_STRIDE
    ss = jnp.arange(ns)[:, None] * NSA_SEL_BLOCK
    sel_map_t = (jnp.maximum(jnp.minimum(cs + NSA_CMP_BLOCK, ss + NSA_SEL_BLOCK) - jnp.maximum(cs, ss), 0)
                 .astype(F32) / NSA_CMP_BLOCK).astype(BF16)
    tq = _pick(t, tq)
    assert NSA_WINDOW % tq == 0
    nq = t // tq
    wq = r * dh
    col = lambda base: base // dh
    q_spec3 = pl.BlockSpec((1, tq, wq), lambda bi, gi, qi: (bi, qi, gi))
    gate_spec3 = pl.BlockSpec((1, 1, tq, 3 * r), lambda bi, gi, qi: (bi, gi, qi, 0))
    o_cmp, sel = pl.pallas_call(
        functools.partial(_nsa_cmp_body, tq=tq, r=r, dh=dh, n_sel=n_sel),
        grid=(b, g, nq),
        in_specs=[
            q_spec3,
            pl.BlockSpec((1, 1, ncp, dh), lambda bi, gi, qi: (bi, gi, 0, 0)),
            pl.BlockSpec((1, 1, ncp, dh), lambda bi, gi, qi: (bi, gi, 0, 0)),
            gate_spec3,
            pl.BlockSpec((ns, ncp), lambda bi, gi, qi: (0, 0)),
        ],
        out_specs=[q_spec3, pl.BlockSpec((1, 1, tq, ns), lambda bi, gi, qi: (bi, gi, qi, 0))],
        out_shape=[jax.ShapeDtypeStruct((b, t, hd), F32), jax.ShapeDtypeStruct((b, g, t, ns), BF16)],
        compiler_params=_cparams("parallel", "parallel", "parallel"),
        name="nsa_cmp",
    )(proj, k_cmp, v_cmp, gates, sel_map_t)

    tk = _pick(t, NSA_SEL_KEY_TILE)
    tqs = _pick(t, NSA_SEL_QUERY_TILE)
    assert tk % tqs == 0
    pairs = [(qi, ki) for qi in range(t // tqs) for ki in range((qi * tqs + tqs - 1) // tk + 1)]
    qi_tbl = jnp.asarray([pq for pq, _ in pairs], jnp.int32)
    ki_tbl = jnp.asarray([pk for _, pk in pairs], jnp.int32)
    diag_tbl = jnp.asarray([int(pk == (pq * tqs + tqs - 1) // tk) for pq, pk in pairs], jnp.int32)
    q_spec_p = pl.BlockSpec((1, tqs, wq), lambda bi, gi, p, qt, kt, dt: (bi, qt[p], gi))

    def kv_spec_p(base):
        return pl.BlockSpec((1, tk, dh), lambda bi, gi, p, qt, kt, dt: (bi, kt[p], col(base) + gi))

    o_sel = pl.pallas_call(
        functools.partial(_nsa_sel_body, tq=tqs, tk=tk, r=r, dh=dh),
        grid_spec=pltpu.PrefetchScalarGridSpec(
            num_scalar_prefetch=3,
            grid=(b, g, len(pairs)),
            in_specs=[q_spec_p, kv_spec_p(hd + 2 * kd), kv_spec_p(hd + 3 * kd),
                      pl.BlockSpec((1, 1, tqs, 3 * r), lambda bi, gi, p, qt, kt, dt: (bi, gi, qt[p], 0)),
                      q_spec_p,
                      pl.BlockSpec((1, 1, tqs, ns), lambda bi, gi, p, qt, kt, dt: (bi, gi, qt[p], 0))],
            out_specs=q_spec_p,
            scratch_shapes=[pltpu.VMEM((r * tqs, 2 * dh), BF16), pltpu.VMEM((r * tqs, dh), F32),
                            pltpu.VMEM((r * tqs, 2 * dh), F32)],
        ),
        out_shape=jax.ShapeDtypeStruct((b, t, hd), F32),
        compiler_params=_cparams("parallel", "parallel", "arbitrary"),
        name="nsa_sel",
    )(qi_tbl, ki_tbl, diag_tbl, proj, proj, proj, gates, o_cmp, sel)

    nwin = NSA_WINDOW // tq + 1

    def kv_spec_w(base, j):
        return pl.BlockSpec((1, tq, dh),
                            lambda bi, gi, qi: (bi, jnp.maximum(qi - (nwin - 1) + j, 0), col(base) + gi))

    o_all = pl.pallas_call(
        functools.partial(_nsa_win_body, tq=tq, r=r, dh=dh, nwin=nwin),
        grid=(b, g, nq),
        in_specs=([q_spec3] + [kv_spec_w(hd + 4 * kd, j) for j in range(nwin)]
                  + [kv_spec_w(hd + 5 * kd, j) for j in range(nwin)] + [gate_spec3, q_spec3]),
        out_specs=q_spec3,
        out_shape=jax.ShapeDtypeStruct((b, t, hd), BF16),
        compiler_params=_cparams("parallel", "parallel", "parallel"),
        name="nsa_win",
    )(proj, *([proj] * (2 * nwin)), gates, o_sel)
    return o_all.reshape(b * t, hd), [(w_out, 0, 0)], None


def _silu(a):
    return a * jax.nn.sigmoid(a)


def _ada_epi(parts, extras):
    return parts[0] + extras[0]


def ada_modulation(c, ada_w, ada_b):
    depth, d, n6 = ada_w.shape
    b = c.shape[0]
    rows = ((b + 7) // 8) * 8
    cond = jnp.pad(c, ((0, rows - b), (0, 0)))
    w2d = ada_w.reshape(depth * d, n6)
    mods = []
    for i in range(depth):
        mods.append(mm(cond, [(w2d, 0, i)], a_act=_silu, epi=_ada_epi, extras=[_row_extra(ada_b[i], _pick(n6, 1024))],
                       precision=HIGHEST, name="ada_mod")[:b])
    return jnp.stack(mods)


def kernel(x, c, ada_w, ada_b, norm_g, mlp_w1, mlp_w2, rwkv_mu, rwkv_w_rkv, rwkv_w0, rwkv_w_la, rwkv_w_lb, rwkv_a0, rwkv_a_la, rwkv_a_lb, rwkv_g_la, rwkv_g_lb, rwkv_k_k, rwkv_k_a, rwkv_r_k, rwkv_ln_g, rwkv_ln_b, rwkv_w_out, ret_w_in, ret_gn_g, ret_gn_b, ret_w_out, conv_pw1_w, conv_pw1_b, conv_dw_w, conv_dw_b, conv_ln_g, conv_ln_b, conv_pw2_w, conv_pw2_b, nsa_w_in, nsa_pe_k, nsa_pe_v, nsa_ck_w1, nsa_ck_w2, nsa_cv_w1, nsa_cv_w2, nsa_w_out):
    b, t, d = x.shape
    depth = ada_w.shape[0]
    mod = ada_modulation(c, ada_w, ada_b).reshape(depth, b, 6, d)
    h = None
    for i in range(depth):
        sh_t, sc_t, gt_t, sh_c, sc_c, gt_c = (mod[i, :, j] for j in range(6))
        kind = i % 4
        if kind == 0:
            act, ws, bias = rwkv7_time_mix(x, norm_g[i, 0], sc_t, sh_t, rwkv_mu, rwkv_w_rkv, rwkv_w0, rwkv_w_la,
                                           rwkv_w_lb, rwkv_a0, rwkv_a_la, rwkv_a_lb, rwkv_g_la, rwkv_g_lb, rwkv_k_k,
                                           rwkv_k_a, rwkv_r_k.reshape(-1), rwkv_ln_g, rwkv_ln_b, rwkv_w_out)
        else:
            if h is None:
                (h,) = resid_norm(x, pre=(norm_g[i, 0], sc_t, sh_t))
            if kind == 1:
                act, ws, bias = retention_mix(h, ret_w_in, ret_gn_g, ret_gn_b, ret_w_out)
            elif kind == 2:
                act, ws, bias = conformer_conv_mix(h, conv_pw1_w, conv_pw1_b, conv_dw_w, conv_dw_b, conv_ln_g,
                                                   conv_ln_b, conv_pw2_w, conv_pw2_b)
            else:
                act, ws, bias = nsa_mix(h, nsa_w_in, nsa_pe_k, nsa_pe_v, nsa_ck_w1, nsa_ck_w2, nsa_cv_w1, nsa_cv_w2,
                                        nsa_w_out)
        y = project(act, ws, bias, "mixer_out").reshape(b, t, d)
        act, x = mlp_up_resid(x, y, (norm_g[i, 1], gt_t), (norm_g[i, 2], sc_c, sh_c), mlp_w1, i)
        nxt = i + 1
        if nxt < depth and nxt % 4 != 0:
            x, h = mlp_down_resid(act, mlp_w2, i, x, (norm_g[i, 3], gt_c),
                                  (norm_g[nxt, 0], mod[nxt, :, 1], mod[nxt, :, 0]))
        else:
            (x,) = mlp_down_resid(act, mlp_w2, i, x, (norm_g[i, 3], gt_c))
            h = None
    return x
```

```python
import functools
import math

import jax
import jax.numpy as jnp
from jax import lax
from jax.experimental import pallas as pl
from jax.experimental.pallas import tpu as pltpu

F32 = jnp.float32
BF16 = jnp.bfloat16
HIGHEST = lax.Precision.HIGHEST

NORM_EPS = 1e-6
NEG_INF = -1e30
POS_BIG = 1e30
LOG2_E = math.log2(math.e)

RWKV_HEAD_DIM = 64
RWKV_GN_EPS = 64e-5
RWKV_CHUNK = 64
RWKV_HEADS_PER_STEP = 32

RET_HEADS = 8
RET_HEADS_PER_STEP = 4
RET_CHUNK = 256
RET_GN_EPS = 1e-5
ROPE_BASE = 10000.0

CONV_WIDTH = 31
CONV_HALO = 32
CONV_LN_EPS = 1e-5

NSA_HEADS = 16
NSA_KV_GROUPS = 4
NSA_HEAD_DIM = 128
NSA_CMP_BLOCK = 32
NSA_CMP_STRIDE = 16
NSA_SEL_BLOCK = 64
NSA_SEL_TOP = 16
NSA_WINDOW = 512
NSA_SEL_KEY_TILE = 512
NSA_SEL_QUERY_TILE = 512

VMEM_LIMIT_BYTES = 56 * 1024 * 1024


def _cparams(*sem):
    return pltpu.CompilerParams(dimension_semantics=sem, vmem_limit_bytes=VMEM_LIMIT_BYTES)


def _dot(a, b, precision=None):
    return jnp.dot(a, b, preferred_element_type=F32, precision=precision)


def _dot_nt(a, b, precision=None):
    return lax.dot_general(a, b, (((1,), (1,)), ((), ())), preferred_element_type=F32, precision=precision)


def _pick(n, pref):
    if n <= pref:
        return n
    t = pref
    while n % t:
        t //= 2
    return t


def _mm_body(*refs, n_w, n_ex, n_out, nk, epi, precision, has_a_add, a_act):
    a_ref = refs[0]
    pos = 1
    a_add_ref = None
    if has_a_add:
        a_add_ref = refs[pos]
        pos += 1
    w_refs = refs[pos:pos + n_w]
    pos += n_w
    ex_refs = refs[pos:pos + n_ex]
    pos += n_ex
    o_refs = refs[pos:pos + n_out]
    acc_refs = refs[pos + n_out:]

    def finish(parts):
        res = epi(parts, [e[...] for e in ex_refs])
        if n_out == 1 and not isinstance(res, (tuple, list)):
            res = (res,)
        for o_ref, val in zip(o_refs, res, strict=True):
            o_ref[...] = val.astype(o_ref.dtype)

    a = a_ref[...]
    if a_add_ref is not None:
        a = a.astype(F32) + a_add_ref[...]
    if a_act is not None:
        a = a_act(a)
    if precision is None:
        a = a.astype(BF16)
    parts = []
    for w_ref in w_refs:
        w = w_ref[...]
        if precision is None:
            w = w.astype(BF16)
        parts.append(_dot(a, w, precision))

    if nk == 1:
        finish(parts)
    else:
        k = pl.program_id(2)

        @pl.when(k == 0)
        def _():
            for acc, p in zip(acc_refs, parts):
                acc[...] = p

        @pl.when(k > 0)
        def _():
            for acc, p in zip(acc_refs, parts):
                acc[...] += p

        @pl.when(k == nk - 1)
        def _():
            finish([acc[...] for acc in acc_refs])


def _first(parts, extras):
    return parts[0]


def mm(a, ws, *, n_out=None, extras=(), epi=_first, out_dtype=F32, tm=2048, tn=1024, tk=2048, precision=None,
       a_add=None, a_act=None, a_part=(0, 1), cast_in_kernel=False, name="mm"):
    m, kdim = a.shape
    a_s, a_parts = a_part
    assert m % a_parts == 0
    m //= a_parts
    if not isinstance(ws, (list, tuple)):
        ws = [(ws, 0, 0)]
    ws = [tuple(w) + (0,) * (3 - len(w)) for w in ws]
    if n_out is None:
        n_out = ws[0][0].shape[1]
    tm = _pick(m, tm)
    tn = _pick(n_out, tn)
    tk = _pick(kdim, tk)
    nk = kdim // tk
    assert m % tm == 0 and n_out % tn == 0 and kdim % tk == 0
    in_specs = [pl.BlockSpec((tm, tk), lambda i, j, k, o=a_s * (m // tm): (i + o, k))]
    args = [a]
    if a_add is not None:
        in_specs.append(pl.BlockSpec((1, tk), lambda i, j, k: (0, k)))
        args.append(a_add)
    for w, off, koff in ws:
        in_specs.append(pl.BlockSpec((tk, tn), lambda i, j, k, off=off, ko=koff * nk: (k + ko, j + off)))
        args.append(w.astype(BF16) if precision is None and not cast_in_kernel else w)
    for arr, bshape, imap in extras:
        in_specs.append(pl.BlockSpec(bshape, imap))
        args.append(arr)
    scratch = [pltpu.VMEM((tm, tn), F32) for _ in ws] if nk > 1 else []
    multi = isinstance(out_dtype, (tuple, list))
    out_dtypes = tuple(out_dtype) if multi else (out_dtype,)
    body = functools.partial(_mm_body, n_w=len(ws), n_ex=len(extras), n_out=len(out_dtypes), nk=nk, epi=epi,
                             precision=precision, has_a_add=a_add is not None, a_act=a_act)
    outs = pl.pallas_call(
        body,
        grid=(m // tm, n_out // tn, nk),
        in_specs=in_specs,
        out_specs=[pl.BlockSpec((tm, tn), lambda i, j, k: (i, j)) for _ in out_dtypes],
        out_shape=[jax.ShapeDtypeStruct((m, n_out), dt) for dt in out_dtypes],
        scratch_shapes=scratch,
        compiler_params=_cparams("parallel", "parallel", "arbitrary"),
        name=name,
    )(*args)
    return tuple(outs) if multi else outs[0]


def _row_extra(vec, tn):
    return (vec.reshape(1, -1), (1, tn), lambda i, j, k: (0, j))


def _rms(x, g):
    return x * lax.rsqrt(jnp.mean(x * x, axis=-1, keepdims=True) + NORM_EPS) * g


def _resid_norm_body(*refs, has_y, has_h):
    pos = 0
    x_ref = refs[pos]; pos += 1
    if has_y:
        y_ref, gpost_ref, gate_ref = refs[pos:pos + 3]; pos += 3
    if has_h:
        gpre_ref, scale_ref, shift_ref = refs[pos:pos + 3]; pos += 3
    outs = refs[pos:]
    x = x_ref[0]
    o = 0
    if has_y:
        y = y_ref[0].astype(F32)
        x = x + (1.0 + gate_ref[0]) * _rms(y, gpost_ref[...])
        outs[o][0] = x
        o += 1
    if has_h:
        h = _rms(x, gpre_ref[...]) * (1.0 + scale_ref[0]) + shift_ref[0]
        outs[o][0] = h.astype(outs[o].dtype)


def resid_norm(x, y=None, post=None, pre=None, tt=512):
    b, t, d = x.shape
    tt = _pick(t, tt)
    row = pl.BlockSpec((1, tt, d), lambda bi, ti: (bi, ti, 0))
    vec = pl.BlockSpec((1, d), lambda bi, ti: (0, 0))
    bvec = pl.BlockSpec((1, 1, d), lambda bi, ti: (bi, 0, 0))
    args, in_specs, out_shapes, out_specs = [x], [row], [], []
    if y is not None:
        args += [y, post[0].reshape(1, d), post[1].reshape(b, 1, d)]
        in_specs += [row, vec, bvec]
        out_shapes.append(jax.ShapeDtypeStruct((b, t, d), F32))
        out_specs.append(row)
    if pre is not None:
        args += [pre[0].reshape(1, d), pre[1].reshape(b, 1, d), pre[2].reshape(b, 1, d)]
        in_specs += [vec, bvec, bvec]
        out_shapes.append(jax.ShapeDtypeStruct((b, t, d), BF16))
        out_specs.append(row)
    res = pl.pallas_call(
        functools.partial(_resid_norm_body, has_y=y is not None, has_h=pre is not None),
        grid=(b, t // tt),
        in_specs=in_specs,
        out_specs=out_specs,
        out_shape=out_shapes,
        compiler_params=_cparams("parallel", "parallel"),
        name="resid_norm",
    )(*args)
    return tuple(res)


def _add_bias(parts, extras):
    return parts[0] + extras[0]


def project(act, ws, bias, name):
    if bias is None:
        return mm(act, ws, out_dtype=BF16, name=name)
    tn = _pick(ws[0][0].shape[1], 1024)
    return mm(act, ws, epi=_add_bias, extras=[_row_extra(bias, tn)], tn=tn, out_dtype=BF16, name=name)


def _mlp_up_body(x_ref, y_ref, gpost_ref, gate_ref, gpre_ref, scale_ref, shift_ref, w_ref, a_ref, xn_ref, h0_ref,
                 h1_ref, *, n_tiles, rows):
    i, j = pl.program_id(0), pl.program_id(1)
    bufs = (h0_ref, h1_ref)

    tm = a_ref.shape[0]
    pieces = 8
    nr, mr = rows // pieces, tm // pieces

    def norm_rows(dst_ref, s):
        sl = pl.ds(s * nr, nr)
        xn = x_ref[sl, :] + (1.0 + gate_ref[0]) * _rms(y_ref[sl, :].astype(F32), gpost_ref[...])
        xn_ref[sl, :] = xn
        h = _rms(xn, gpre_ref[...]) * (1.0 + scale_ref[0]) + shift_ref[0]
        dst_ref[pl.ds(pl.multiple_of(j * rows + s * nr, nr), nr), :] = h.astype(dst_ref.dtype)

    def up_project(src_ref, w, s):
        sl = pl.ds(s * mr, mr)
        r = jnp.maximum(_dot(src_ref[sl, :], w), 0.0)
        a_ref[sl, :] = (r * r).astype(a_ref.dtype)

    @pl.when(i == 0)
    def _():
        for s in range(pieces):
            norm_rows(bufs[0], s)

    for parity in (0, 1):
        @pl.when((i > 0) & (i < n_tiles) & (i % 2 == parity))
        def _():
            w = w_ref[...].astype(BF16)
            for s in range(pieces):
                up_project(bufs[1 - parity], w, s)
                norm_rows(bufs[parity], s)

    @pl.when(i == n_tiles)
    def _():
        w = w_ref[...].astype(BF16)
        for s in range(pieces):
            up_project(bufs[(n_tiles - 1) % 2], w, s)


def mlp_up_resid(x, y, post, pre, w1, layer, tm=2048, tn=512):
    b, t, d = x.shape
    nl, _, f = w1.shape
    m = b * t
    tm = _pick(t, tm)
    tn = _pick(f, tn)
    n_tiles, nj = m // tm, f // tn
    assert tm % nj == 0
    rows = tm // nj
    assert rows % 16 == 0
    tiles_per_batch = t // tm
    slice_idx = lambda i, j: jnp.where(i < n_tiles, i * nj + j, n_tiles * nj - 1)
    rows_spec = pl.BlockSpec((rows, d), lambda i, j: (slice_idx(i, j), 0))
    vec = pl.BlockSpec((1, d), lambda i, j: (0, 0))
    bvec = pl.BlockSpec((1, 1, d), lambda i, j: (jnp.minimum(i, n_tiles - 1) // tiles_per_batch, 0, 0))
    a, x_new = pl.pallas_call(
        functools.partial(_mlp_up_body, n_tiles=n_tiles, rows=rows),
        grid=(n_tiles + 1, nj),
        in_specs=[rows_spec, rows_spec, vec, bvec, vec, bvec, bvec,
                  pl.BlockSpec((d, tn), lambda i, j: (layer, j))],
        out_specs=[pl.BlockSpec((tm, tn), lambda i, j: (jnp.maximum(i - 1, 0), jnp.where(i > 0, j, 0))),
                   rows_spec],
        out_shape=[jax.ShapeDtypeStruct((m, f), BF16), jax.ShapeDtypeStruct((m, d), F32)],
        scratch_shapes=[pltpu.VMEM((tm, d), BF16), pltpu.VMEM((tm, d), BF16)],
        compiler_params=_cparams("arbitrary", "arbitrary"),
        name="mlp_up_resid",
    )(x.reshape(m, d), y.reshape(m, d), post[0].reshape(1, d), post[1].reshape(b, 1, d), pre[0].reshape(1, d),
      pre[1].reshape(b, 1, d), pre[2].reshape(b, 1, d), w1.reshape(nl * d, f))
    return a, x_new.reshape(b, t, d)


def _mlp_down_body(*refs, n_tiles, rows, has_pre):
    a_ref, w_ref, x_ref, gpost_ref, gate_ref = refs[:5]
    if has_pre:
        gpre_ref, scale_ref, shift_ref, xn_ref, h_ref, acc0_ref, acc1_ref = refs[5:]
    else:
        xn_ref, acc0_ref, acc1_ref = refs[5:]
    i, k = pl.program_id(0), pl.program_id(1)
    accs = (acc0_ref, acc1_ref)
    tm = a_ref.shape[0]
    pieces = 4
    mr = tm // pieces
    nr = rows // pieces

    @pl.when((i == 0) & (k == 0))
    def _():
        acc0_ref[...] = jnp.zeros_like(acc0_ref)
        acc1_ref[...] = jnp.zeros_like(acc1_ref)

    def accumulate(dst_ref, w, s):
        sl = pl.ds(s * mr, mr)
        part = _dot(a_ref[sl, :], w)
        dst_ref[sl, :] = jnp.where(k == 0, part, dst_ref[sl, :] + part)

    def finish_rows(src_ref, s):
        sl = pl.ds(s * nr, nr)
        y = src_ref[pl.ds(pl.multiple_of(k * rows + s * nr, nr), nr), :]
        xn = x_ref[sl, :] + (1.0 + gate_ref[0]) * _rms(y, gpost_ref[...])
        xn_ref[sl, :] = xn
        if has_pre:
            h_ref[sl, :] = (_rms(xn, gpre_ref[...]) * (1.0 + scale_ref[0]) + shift_ref[0]).astype(h_ref.dtype)

    @pl.when(i == 0)
    def _():
        w = w_ref[...]
        for s in range(pieces):
            accumulate(accs[0], w, s)

    for parity in (0, 1):
        @pl.when((i > 0) & (i < n_tiles) & (i % 2 == parity))
        def _():
            w = w_ref[...]
            for s in range(pieces):
                accumulate(accs[parity], w, s)
                finish_rows(accs[1 - parity], s)

    @pl.when(i == n_tiles)
    def _():
        for s in range(pieces):
            finish_rows(accs[(n_tiles - 1) % 2], s)


def mlp_down_resid(a, w2, layer, x, post, pre=None, tm=1024, tk=1024):
    b, t, d = x.shape
    nl, f, _ = w2.shape
    m = b * t
    tm = _pick(t, tm)
    tk = _pick(f, tk)
    n_tiles, nk = m // tm, f // tk
    assert tm % nk == 0
    rows = tm // nk
    assert rows % (8 * 16) == 0
    tiles_per_batch = t // tm
    slice_idx = lambda i, k: jnp.where(i > 0, (i - 1) * nk + k, 0)
    rows_spec = pl.BlockSpec((rows, d), lambda i, k: (slice_idx(i, k), 0))
    vec = pl.BlockSpec((1, d), lambda i, k: (0, 0))
    bvec = pl.BlockSpec((1, 1, d), lambda i, k: (jnp.maximum(i - 1, 0) // tiles_per_batch, 0, 0))
    in_specs = [pl.BlockSpec((tm, tk), lambda i, k: (jnp.minimum(i, n_tiles - 1), jnp.where(i < n_tiles, k, nk - 1))),
                pl.BlockSpec((tk, d), lambda i, k: (layer * nk + jnp.where(i < n_tiles, k, nk - 1), 0)),
                rows_spec, vec, bvec]
    args = [a, w2.reshape(nl * f, d).astype(BF16), x.reshape(m, d), post[0].reshape(1, d), post[1].reshape(b, 1, d)]
    out_specs, out_shapes = [rows_spec], [jax.ShapeDtypeStruct((m, d), F32)]
    if pre is not None:
        in_specs += [vec, bvec, bvec]
        args += [pre[0].reshape(1, d), pre[1].reshape(b, 1, d), pre[2].reshape(b, 1, d)]
        out_specs.append(rows_spec)
        out_shapes.append(jax.ShapeDtypeStruct((m, d), BF16))
    outs = pl.pallas_call(
        functools.partial(_mlp_down_body, n_tiles=n_tiles, rows=rows, has_pre=pre is not None),
        grid=(n_tiles + 1, nk),
        in_specs=in_specs,
        out_specs=out_specs,
        out_shape=out_shapes,
        scratch_shapes=[pltpu.VMEM((tm, d), F32), pltpu.VMEM((tm, d), F32)],
        compiler_params=_cparams("arbitrary", "arbitrary"),
        name="mlp_down_resid",
    )(*args)
    return tuple(o.reshape(b, t, d) for o in outs)


def _glu(parts, extras):
    return (parts[0] + extras[0]) * jax.nn.sigmoid(parts[1] + extras[1])


def _conv_ln_body(cur_ref, prev_ref, w_ref, b_ref, g_ref, beta_ref, o_ref, buf_ref, acc_ref, *, tt, d):
    ti = pl.program_id(1)
    halo = prev_ref[0].astype(F32)
    buf_ref[0:CONV_HALO, :] = jnp.where(ti == 0, jnp.zeros_like(halo), halo)
    buf_ref[CONV_HALO:, :] = cur_ref[0].astype(F32)
    lane_chunk = min(d, 256)
    row_chunk = min(tt, 64)
    base = CONV_HALO - (CONV_WIDTH - 1)
    sub = 8
    for c0 in range(0, d, lane_chunk):
        for r0 in range(0, tt, row_chunk):
            acc = None
            for s in range(sub):
                taps = [j for j in range(CONV_WIDTH) if (base + j) % sub == s]
                if not taps:
                    continue
                rows = row_chunk + (sub if s else 0)
                part = jnp.zeros((rows, lane_chunk), F32)
                for j in taps:
                    off = r0 + base + j - s
                    part = part + buf_ref[off:off + rows, c0:c0 + lane_chunk] * w_ref[j:j + 1, c0:c0 + lane_chunk]
                part = part[s:s + row_chunk]
                acc = part if acc is None else acc + part
            acc_ref[r0:r0 + row_chunk, c0:c0 + lane_chunk] = acc
    y = acc_ref[...] + b_ref[...]
    mu = jnp.mean(y, axis=-1, keepdims=True)
    yc = y - mu
    var = jnp.mean(yc * yc, axis=-1, keepdims=True)
    z = yc * lax.rsqrt(var + CONV_LN_EPS) * g_ref[...] + beta_ref[...]
    o_ref[0] = (z * jax.nn.sigmoid(z)).astype(o_ref.dtype)


def conv_ln_silu(u, dw_w, dw_b, ln_g, ln_b, tt=128):
    b, t, d = u.shape
    tt = _pick(t, tt)
    hb = tt // CONV_HALO
    vec = pl.BlockSpec((1, d), lambda bi, ti: (0, 0))
    return pl.pallas_call(
        functools.partial(_conv_ln_body, tt=tt, d=d),
        grid=(b, t // tt),
        in_specs=[
            pl.BlockSpec((1, tt, d), lambda bi, ti: (bi, ti, 0)),
            pl.BlockSpec((1, CONV_HALO, d), lambda bi, ti: (bi, jnp.maximum(ti * hb - 1, 0), 0)),
            pl.BlockSpec((CONV_WIDTH, d), lambda bi, ti: (0, 0)),
            vec, vec, vec,
        ],
        out_specs=pl.BlockSpec((1, tt, d), lambda bi, ti: (bi, ti, 0)),
        out_shape=jax.ShapeDtypeStruct((b, t, d), BF16),
        scratch_shapes=[pltpu.VMEM((tt + CONV_HALO, d), F32), pltpu.VMEM((tt, d), F32)],
        compiler_params=_cparams("parallel", "parallel"),
        name="conv_ln_silu",
    )(u, u, dw_w, dw_b.reshape(1, d), ln_g.reshape(1, d), ln_b.reshape(1, d))


def conformer_conv_mix(h, pw1_w, pw1_b, dw_w, dw_b, ln_g, ln_b, pw2_w, pw2_b):
    b, t, d = h.shape
    tn = _pick(d, 1024)
    b1 = pw1_b.reshape(1, -1)
    u = mm(h.reshape(b * t, d), [(pw1_w, 0), (pw1_w, d // tn)], n_out=d, tm=1024, tn=tn, epi=_glu,
           extras=[(b1, (1, tn), lambda i, j, k: (0, j)),
                   (b1, (1, tn), lambda i, j, k, o=d // tn: (0, j + o))], out_dtype=BF16, name="conf_pw1_glu")
    z = conv_ln_silu(u.reshape(b, t, d), dw_w, dw_b, ln_g, ln_b)
    return z.reshape(b * t, d), [(pw2_w, 0, 0)], pw2_b


def _rope(x, cos, sin):
    half = x.shape[-1] // 2
    x1, x2 = x[:, :half], x[:, half:]
    return jnp.concatenate([x1 * cos - x2 * sin, x2 * cos + x1 * sin], axis=-1)


def _retention_body(q_ref, k_ref, v_ref, gate_ref, cos_ref, sin_ref, inner_ref, qdec_ref, kdec_ref, cdec_ref,
                    gng_ref, gnb_ref, o_ref, state_ref, *, dk, heads):
    @pl.when(pl.program_id(2) == 0)
    def _():
        state_ref[...] = jnp.zeros_like(state_ref)

    cos, sin = cos_ref[...], sin_ref[...]
    dv = v_ref.shape[2] // heads
    for hh in range(heads):
        ks, vs = slice(hh * dk, (hh + 1) * dk), slice(hh * dv, (hh + 1) * dv)
        q = _rope(q_ref[0, :, ks].astype(F32), cos, sin)
        k = _rope(k_ref[0, :, ks].astype(F32), cos, sin) * (dk ** -0.5)
        v = v_ref[0, :, vs].astype(BF16)
        qb = q.astype(BF16)
        s = _dot_nt(qb, k.astype(BF16)) * inner_ref[hh]
        state = state_ref[hh]
        o = _dot(s.astype(BF16), v) + _dot(qb, state.astype(BF16)) * qdec_ref[hh]
        kd_t = (k * kdec_ref[hh]).T.astype(BF16)
        state_ref[hh] = state * cdec_ref[hh] + _dot(kd_t, v)
        mu = jnp.mean(o, axis=-1, keepdims=True)
        oc = o - mu
        var = jnp.mean(oc * oc, axis=-1, keepdims=True)
        on = oc * lax.rsqrt(var + RET_GN_EPS) * gng_ref[:, vs] + gnb_ref[:, vs]
        gate = gate_ref[0, :, vs].astype(F32)
        o_ref[0, :, vs] = (gate * jax.nn.sigmoid(gate) * on).astype(o_ref.dtype)


def retention_mix(h, w_in, gn_g, gn_b, w_out):
    b, t, d = h.shape
    nh, c = RET_HEADS, _pick(t, RET_CHUNK)
    dk = d // nh
    dv = 2 * dk
    proj = mm(h.reshape(b * t, d), w_in, out_dtype=BF16, tm=2048, tn=512, cast_in_kernel=True,
              name="ret_in").reshape(b, t, 6 * d)
    pos = jnp.arange(t, dtype=F32)
    inv_freq = ROPE_BASE ** (-jnp.arange(0, dk, 2, dtype=F32) / dk)
    ang = pos[:, None] * inv_freq[None, :]
    cos, sin = jnp.cos(ang), jnp.sin(ang)
    log_gamma = jnp.log(1.0 - 2.0 ** (-5.0 - jnp.arange(nh, dtype=F32)))
    idx = jnp.arange(c, dtype=F32)
    diff = idx[:, None] - idx[None, :]
    inner = jnp.where(diff >= 0, jnp.exp(jnp.maximum(diff, 0.0)[None] * log_gamma[:, None, None]), 0.0)
    q_dec = jnp.exp((idx + 1.0)[None] * log_gamma[:, None])[:, :, None]
    k_dec = jnp.exp((c - 1.0 - idx)[None] * log_gamma[:, None])[:, :, None]
    c_dec = jnp.exp(c * log_gamma)[:, None, None]
    hp = RET_HEADS_PER_STEP
    assert nh % hp == 0
    ng = nh // hp
    o = pl.pallas_call(
        functools.partial(_retention_body, dk=dk, heads=hp),
        grid=(b, ng, t // c),
        in_specs=[
            pl.BlockSpec((1, c, hp * dk), lambda bi, hi, ci: (bi, ci, hi)),
            pl.BlockSpec((1, c, hp * dk), lambda bi, hi, ci: (bi, ci, ng + hi)),
            pl.BlockSpec((1, c, hp * dv), lambda bi, hi, ci: (bi, ci, ng + hi)),
            pl.BlockSpec((1, c, hp * dv), lambda bi, hi, ci: (bi, ci, 2 * ng + hi)),
            pl.BlockSpec((c, dk // 2), lambda bi, hi, ci: (ci, 0)),
            pl.BlockSpec((c, dk // 2), lambda bi, hi, ci: (ci, 0)),
            pl.BlockSpec((hp, c, c), lambda bi, hi, ci: (hi, 0, 0)),
            pl.BlockSpec((hp, c, 1), lambda bi, hi, ci: (hi, 0, 0)),
            pl.BlockSpec((hp, c, 1), lambda bi, hi, ci: (hi, 0, 0)),
            pl.BlockSpec((hp, 1, 1), lambda bi, hi, ci: (hi, 0, 0)),
            pl.BlockSpec((1, hp * dv), lambda bi, hi, ci: (0, hi)),
            pl.BlockSpec((1, hp * dv), lambda bi, hi, ci: (0, hi)),
        ],
        out_specs=pl.BlockSpec((1, c, hp * dv), lambda bi, hi, ci: (bi, ci, hi)),
        out_shape=jax.ShapeDtypeStruct((b, t, nh * dv), BF16),
        scratch_shapes=[pltpu.VMEM((hp, dk, dv), F32)],
        compiler_params=_cparams("parallel", "parallel", "arbitrary"),
        name="retention",
    )(proj, proj, proj, proj, cos, sin, inner, q_dec, k_dec, c_dec, gn_g.reshape(1, -1), gn_b.reshape(1, -1))
    return o.reshape(b * t, nh * dv), [(w_out, 0, 0)], None


def _rwkv_pre_body(x_ref, prev_ref, g_ref, scale_ref, shift_ref, mu_ref, o_ref):
    ti = pl.program_id(1)
    g, scale, shift = g_ref[...], scale_ref[0], shift_ref[0]
    h = _rms(x_ref[0], g) * (1.0 + scale) + shift
    hp = _rms(prev_ref[0], g) * (1.0 + scale) + shift
    last = jnp.where(ti == 0, 0.0, hp[7:8, :])
    row = lax.broadcasted_iota(jnp.int32, h.shape, 0)
    shifted = jnp.where(row == 0, last, pltpu.roll(h, 1, axis=0))
    xx = shifted - h
    for s in range(6):
        o_ref[s, 0] = (h + xx * mu_ref[s:s + 1, :]).astype(o_ref.dtype)


def rwkv_pre(x, g_pre, scale, shift, mu, tt=256):
    b, t, d = x.shape
    tt = _pick(t, tt)
    bvec = pl.BlockSpec((1, 1, d), lambda bi, ti: (bi, 0, 0))
    return pl.pallas_call(
        _rwkv_pre_body,
        grid=(b, t // tt),
        in_specs=[
            pl.BlockSpec((1, tt, d), lambda bi, ti: (bi, ti, 0)),
            pl.BlockSpec((1, 8, d), lambda bi, ti: (bi, jnp.maximum(ti * (tt // 8) - 1, 0), 0)),
            pl.BlockSpec((1, d), lambda bi, ti: (0, 0)),
            bvec, bvec,
            pl.BlockSpec((6, d), lambda bi, ti: (0, 0)),
        ],
        out_specs=pl.BlockSpec((6, 1, tt, d), lambda bi, ti: (0, bi, ti, 0)),
        out_shape=jax.ShapeDtypeStruct((6, b, t, d), BF16),
        compiler_params=_cparams("parallel", "parallel"),
        name="rwkv_pre",
    )(x, x, g_pre.reshape(1, d), scale.reshape(b, 1, d), shift.reshape(b, 1, d), mu)


def _rwkv_scan_body(r_ref, k_ref, v_ref, a_ref, lw_ref, g_ref, kk_ref, ka_ref, rk_ref, lng_ref, lnb_ref, o_ref,
                    state_ref, *, chunk, heads, n):
    @pl.when(pl.program_id(2) == 0)
    def _():
        state_ref[...] = jnp.zeros_like(state_ref)

    ln = chunk
    hs = range(heads)
    row = lax.broadcasted_iota(jnp.int32, (ln, ln), 0)
    col = lax.broadcasted_iota(jnp.int32, (ln, ln), 1)
    tri = (col <= row).astype(BF16)
    eye = (row == col).astype(F32)
    wid = heads * n
    seg_w = min(wid, 256)
    assert wid % seg_w == 0 and seg_w % n == 0
    seg = (lax.broadcasted_iota(jnp.int32, (seg_w, seg_w), 0) // n
           == lax.broadcasted_iota(jnp.int32, (seg_w, seg_w), 1) // n).astype(BF16)

    def split_dot(m, x):
        hi = x.astype(BF16)
        lo = (x - hi.astype(F32)).astype(BF16)
        return _dot(m, hi) + _dot(m, lo)

    def seg_sum(x):
        hi = x.astype(BF16)
        lo = (x - hi.astype(F32)).astype(BF16)
        return jnp.concatenate([_dot(hi[:, j:j + seg_w], seg) + _dot(lo[:, j:j + seg_w], seg)
                                for j in range(0, wid, seg_w)], axis=1)

    heads_of = lambda x: [x[:, hd * n:(hd + 1) * n] for hd in hs]

    r, k, v, a = (ref[0].astype(F32) for ref in (r_ref, k_ref, v_ref, a_ref))
    lw = lw_ref[0]
    kk = k * kk_ref[...]
    kk = kk / jnp.maximum(jnp.sqrt(seg_sum(kk * kk)), 1e-12)
    k2 = k * (1.0 + (a - 1.0) * ka_ref[...])
    beta = kk * a
    c = split_dot(tri, lw)
    c_last = c[ln - 1:ln, :]
    e_neg = jnp.exp(-c)
    e_rem = jnp.exp(c_last - c)
    e_last = heads_of(jnp.exp(c_last))
    a_t = heads_of(-kk * jnp.exp(c - lw))
    r_t = heads_of(r * jnp.exp(c))
    b_t = heads_of(beta * e_neg)
    k_t = heads_of(k2 * e_neg)
    b_rem = heads_of(beta * e_rem)
    k_rem = heads_of(k2 * e_rem)
    vs = heads_of(v)
    bonus = seg_sum(r * k2 * rk_ref[...]) * v

    ar = [jnp.concatenate([a_t[h], r_t[h]], axis=0).astype(BF16) for h in hs]
    bk = [jnp.concatenate([b_t[h], k_t[h]], axis=0).astype(BF16) for h in hs]
    bk_rem = [jnp.concatenate([b_rem[h], k_rem[h]], axis=0).astype(BF16) for h in hs]
    vb = [vs[h].astype(BF16) for h in hs]
    state = [state_ref[h] for h in hs]
    p = [_dot_nt(ar[h], bk[h]) for h in hs]
    q0 = [_dot_nt(ar[h], state[h].astype(BF16)) for h in hs]
    row2 = lax.broadcasted_iota(jnp.int32, (2 * ln, 2 * ln), 0)
    col2 = lax.broadcasted_iota(jnp.int32, (2 * ln, 2 * ln), 1)
    keep = col2 % ln < row2 % ln + row2 // ln
    pm = [jnp.where(keep, p[h], 0.0) for h in hs]
    m_ab = [pm[h][:ln, :ln] for h in hs]
    m_ak = [pm[h][:ln, ln:].astype(BF16) for h in hs]
    m_r = [pm[h][ln:].astype(BF16) for h in hs]
    rhs = [q0[h][:ln] + _dot(m_ak[h], vb[h]) for h in hs]
    inv = [eye + m_ab[h] for h in hs]
    pw = [m_ab[h].astype(BF16) for h in hs]
    for _ in range(int(math.log2(ln)) - 1):
        pw = [_dot(pw[h], pw[h]).astype(BF16) for h in hs]
        inv = [inv[h] + _dot(inv[h].astype(BF16), pw[h]) for h in hs]
    u = [_dot(inv[h].astype(BF16), rhs[h].astype(BF16)) for h in hs]
    uv = [jnp.concatenate([u[h], vs[h]], axis=0) for h in hs]
    y = [q0[h][ln:] + _dot(m_r[h], uv[h].astype(BF16)) for h in hs]
    for h in hs:
        state_ref[h] = state[h] * e_last[h] + _dot(uv[h].T.astype(BF16), bk_rem[h])
    yn = []
    for h in hs:
        mu = jnp.mean(y[h], axis=-1, keepdims=True)
        yc = y[h] - mu
        var = jnp.mean(yc * yc, axis=-1, keepdims=True)
        yn.append(yc * lax.rsqrt(var + RWKV_GN_EPS))
    yn = jnp.concatenate(yn, axis=-1) * lng_ref[...] + lnb_ref[...]
    o_ref[0] = ((yn + bonus) * g_ref[0]).astype(o_ref.dtype)


def rwkv_scan(r, k, v, a, lw, g, k_k, k_a, r_k, ln_g, ln_b):
    b, t, d = r.shape
    n = RWKV_HEAD_DIM
    hps = min(RWKV_HEADS_PER_STEP, d // n)
    w = hps * n
    ln = _pick(t, RWKV_CHUNK)
    tok = pl.BlockSpec((1, ln, w), lambda bi, hi, ci: (bi, ci, hi))
    vec = pl.BlockSpec((1, w), lambda bi, hi, ci: (0, hi))
    return pl.pallas_call(
        functools.partial(_rwkv_scan_body, chunk=ln, heads=hps, n=n),
        grid=(b, d // w, t // ln),
        in_specs=[tok] * 6 + [vec] * 5,
        out_specs=tok,
        out_shape=jax.ShapeDtypeStruct((b, t, d), BF16),
        scratch_shapes=[pltpu.VMEM((hps, n, n), F32)],
        compiler_params=_cparams("parallel", "parallel", "arbitrary"),
        name="rwkv_scan",
    )(r, k, v, a, lw, g, k_k.reshape(1, d), k_a.reshape(1, d), r_k.reshape(1, d), ln_g.reshape(1, d),
      ln_b.reshape(1, d))


def _tanh_epi(parts, extras):
    return jnp.tanh(parts[0])


def _sigmoid_epi(parts, extras):
    return jax.nn.sigmoid(parts[0])


def _sigmoid_bias_epi(parts, extras):
    return jax.nn.sigmoid(extras[0] + parts[0])


def _logdecay_epi(parts, extras):
    return -jnp.exp(-jax.nn.softplus(-(extras[0] + parts[0])) - 0.5)


def rwkv7_time_mix(x, g_pre, scale, shift, mu, w_rkv, w0, w_la, w_lb, a0, a_la, a_lb, g_la, g_lb, k_k, k_a, r_k,
                   ln_g, ln_b, w_out):
    b, t, d = x.shape
    xs = rwkv_pre(x, g_pre, scale, shift, mu).reshape(6 * b * t, d)
    tn = _pick(d, 1024)
    w_rkv2d = w_rkv.reshape(3 * d, d)
    r = mm(xs, [(w_rkv2d, 0, 0)], a_part=(0, 6), out_dtype=BF16, name="rwkv_r")
    k = mm(xs, [(w_rkv2d, 0, 1)], a_part=(1, 6), out_dtype=BF16, name="rwkv_k")
    v = mm(xs, [(w_rkv2d, 0, 2)], a_part=(2, 6), out_dtype=BF16, name="rwkv_v")
    lw = mm(mm(xs, w_la, a_part=(3, 6), epi=_tanh_epi, out_dtype=BF16, name="rwkv_w_la"), w_lb, epi=_logdecay_epi,
            extras=[_row_extra(w0, tn)], name="rwkv_w_lb")
    a = mm(mm(xs, a_la, a_part=(4, 6), out_dtype=BF16, name="rwkv_a_la"), a_lb, epi=_sigmoid_bias_epi,
           extras=[_row_extra(a0, tn)], out_dtype=BF16, name="rwkv_a_lb")
    g = mm(mm(xs, g_la, a_part=(5, 6), epi=_sigmoid_epi, out_dtype=BF16, name="rwkv_g_la"), g_lb, out_dtype=BF16,
           name="rwkv_g_lb")
    sh = lambda z: z.reshape(b, t, d)
    z = rwkv_scan(sh(r), sh(k), sh(v), sh(a), sh(lw), sh(g), k_k, k_a, r_k, ln_g, ln_b)
    return z.reshape(b * t, d), [(w_out, 0, 0)], None


def _nsa_compress_body(kv_ref, pe_ref, w1_ref, w2_ref, o_ref, buf_ref, *, nchunk, dh):
    st = NSA_CMP_STRIDE
    hid = w1_ref.shape[1]
    buf_ref[...] = kv_ref[0].astype(F32)
    first = jnp.zeros((nchunk, hid), F32)
    second = jnp.zeros((nchunk, hid), F32)
    for j in range(st):
        xj = buf_ref[pl.ds(j, nchunk, stride=st), :]
        first = first + _dot((xj + pe_ref[j:j + 1, :]).astype(BF16), w1_ref[j * dh:(j + 1) * dh, :])
        second = second + _dot((xj + pe_ref[st + j:st + j + 1, :]).astype(BF16),
                               w1_ref[(st + j) * dh:(st + j + 1) * dh, :])
    hidden = jax.nn.gelu(first + pltpu.roll(second, nchunk - 1, axis=0))
    out = _dot(hidden.astype(BF16), w2_ref[...])
    row = lax.broadcasted_iota(jnp.int32, out.shape, 0)
    o_ref[0, 0] = jnp.where(row < nchunk - 1, out, 0.0)


def nsa_compress(proj, col0, pe, w1, w2):
    b, t, _ = proj.shape
    g, dh = NSA_KV_GROUPS, NSA_HEAD_DIM
    nchunk = t // NSA_CMP_STRIDE
    hid = w1.shape[1]
    return pl.pallas_call(
        functools.partial(_nsa_compress_body, nchunk=nchunk, dh=dh),
        grid=(b, g),
        in_specs=[
            pl.BlockSpec((1, t, dh), lambda bi, gi: (bi, 0, col0 // dh + gi)),
            pl.BlockSpec((NSA_CMP_BLOCK, dh), lambda bi, gi: (0, 0)),
            pl.BlockSpec((NSA_CMP_BLOCK * dh, hid), lambda bi, gi: (0, 0)),
            pl.BlockSpec((hid, dh), lambda bi, gi: (0, 0)),
        ],
        out_specs=pl.BlockSpec((1, 1, nchunk, dh), lambda bi, gi: (bi, gi, 0, 0)),
        out_shape=jax.ShapeDtypeStruct((b, g, nchunk, dh), F32),
        scratch_shapes=[pltpu.VMEM((t, dh), F32)],
        compiler_params=_cparams("parallel", "parallel"),
        name="nsa_compress",
    )(proj, pe, w1.astype(BF16), w2.astype(BF16))


def _stack_heads(q, r, dh):
    return jnp.concatenate([q[:, i * dh:(i + 1) * dh] for i in range(r)], axis=0)


def _unstack_heads(o, r, tq):
    return jnp.concatenate([o[i * tq:(i + 1) * tq] for i in range(r)], axis=-1)


def _gate_rows(gates, branch, r, tq):
    return jnp.concatenate([gates[:, branch * r + i:branch * r + i + 1] for i in range(r)], axis=0)


def _nsa_cmp_body(q_ref, kc_ref, vc_ref, gates_ref, selmap_t_ref, o_ref, sel_ref, *, tq, r, dh, n_sel):
    q0 = pl.program_id(2) * tq
    qs = (_stack_heads(q_ref[0].astype(F32), r, dh) * (dh ** -0.5)).astype(BF16)
    kc, vc = kc_ref[0, 0].astype(BF16), vc_ref[0, 0].astype(BF16)
    ncp = kc.shape[0]
    s = _dot_nt(qs, kc)
    rows = lax.broadcasted_iota(jnp.int32, (r * tq, ncp), 0)
    cmp_end = lax.broadcasted_iota(jnp.int32, (r * tq, ncp), 1) * NSA_CMP_STRIDE + (NSA_CMP_BLOCK - 1)
    t_pos = q0 + rows % tq
    vis = cmp_end <= t_pos
    s = jnp.where(vis, s, NEG_INF)
    e = jnp.exp(s - jnp.max(s, axis=-1, keepdims=True))
    p = e / jnp.sum(e, axis=-1, keepdims=True) * vis.astype(F32)
    o = _dot(p.astype(BF16), vc)
    o = o * _gate_rows(gates_ref[0, 0], 0, r, tq)
    o_ref[0] = _unstack_heads(o, r, tq)
    p_sum = p[0:tq]
    for i in range(1, r):
        p_sum = p_sum + p[i * tq:(i + 1) * tq]
    hi = p_sum.astype(BF16)
    lo = (p_sum - hi.astype(F32)).astype(BF16)
    selmap_t = selmap_t_ref[...]
    imp = _dot_nt(selmap_t, hi) + _dot_nt(selmap_t, lo)
    ns = imp.shape[0]
    blk = lax.broadcasted_iota(jnp.int32, (ns, tq), 0)
    tq_pos = q0 + lax.broadcasted_iota(jnp.int32, (ns, tq), 1)
    cur = tq_pos // NSA_SEL_BLOCK
    valid = blk * NSA_SEL_BLOCK <= tq_pos
    forced = (blk == 0) | (blk == cur) | (blk == cur - 1)
    score = jnp.where(valid, jnp.where(forced, POS_BIG, imp), NEG_INF)
    rank = jnp.zeros((ns, tq), F32)
    for m in range(ns):
        sm = score[m:m + 1, :]
        beats = (sm > score) | ((sm == score) & (blk > m))
        rank = rank + beats.astype(F32)
    chosen = ((rank < n_sel) & valid).astype(F32)
    sel_ref[0, 0] = (chosen.T - 1.0).astype(sel_ref.dtype)


def _flash_update(s, v_ones, m_ref, acc_ref):
    lanes = m_ref.shape[1]
    assert s.shape[1] % lanes == 0 and acc_ref.shape[1] == 2 * lanes
    m_old = m_ref[...]
    m_new = jnp.maximum(m_old, jnp.max(s, axis=-1, keepdims=True))
    alpha = jnp.exp2(m_old - m_new)
    p = jnp.exp2((s - jnp.concatenate([m_new] * (s.shape[1] // lanes), axis=1)).astype(BF16))
    acc_ref[...] = jnp.concatenate([alpha, alpha], axis=1) * acc_ref[...] + _dot(p, v_ones)
    m_ref[...] = m_new


def _nsa_sel_body(qi_tbl, ki_tbl, diag_tbl, q_ref, k_ref, v_ref, gates_ref, oin_ref, sel_ref, o_ref, qa_ref, m_ref,
                  acc_ref, *, tq, tk, r, dh):
    step = pl.program_id(2)
    qi, ki, diag = qi_tbl[step], ki_tbl[step], diag_tbl[step]
    ns = sel_ref.shape[-1]
    assert ns <= dh

    @pl.when(ki == 0)
    def _():
        m_ref[...] = jnp.full_like(m_ref, NEG_INF)
        acc_ref[...] = jnp.zeros_like(acc_ref)
        qs = _stack_heads(q_ref[0].astype(F32), r, dh) * (dh ** -0.5 * LOG2_E)
        pieces = [qs, jnp.concatenate([sel_ref[0, 0].astype(F32)] * r, axis=0)]
        if ns < dh:
            pieces.append(jnp.zeros((r * tq, dh - ns), F32))
        qa_ref[...] = jnp.concatenate(pieces, axis=1).astype(BF16)

    blk_of_key = (ki * tk + lax.broadcasted_iota(jnp.int32, (tk, dh), 0)) // NSA_SEL_BLOCK
    own_block = jnp.where(blk_of_key == lax.broadcasted_iota(jnp.int32, (tk, dh), 1), POS_BIG, 0.0).astype(BF16)
    ka = jnp.concatenate([k_ref[0].astype(BF16), own_block], axis=1)
    s = _dot_nt(qa_ref[...], ka)
    v_ones = jnp.concatenate([v_ref[0].astype(BF16), jnp.ones((tk, dh), BF16)], axis=1)

    @pl.when(diag == 0)
    def _():
        _flash_update(s, v_ones, m_ref, acc_ref)

    @pl.when(diag == 1)
    def _():
        t_pos = qi * tq + lax.broadcasted_iota(jnp.int32, (r * tq, tk), 0) % tq
        k_pos = ki * tk + lax.broadcasted_iota(jnp.int32, (r * tq, tk), 1)
        _flash_update(jnp.where(k_pos <= t_pos, s, NEG_INF), v_ones, m_ref, acc_ref)
        o = acc_ref[:, :dh] / acc_ref[:, dh:] * _gate_rows(gates_ref[0, 0], 1, r, tq)
        o_ref[0] = (oin_ref[0] + _unstack_heads(o, r, tq)).astype(o_ref.dtype)


def _nsa_win_body(q_ref, *rest, tq, r, dh, nwin):
    k_refs, v_refs = rest[:nwin], rest[nwin:2 * nwin]
    gates_ref, oin_ref, o_ref = rest[2 * nwin:]
    qi = pl.program_id(2)
    qs = (_stack_heads(q_ref[0].astype(F32), r, dh) * (dh ** -0.5 * LOG2_E)).astype(BF16)
    t_loc = lax.broadcasted_iota(jnp.int32, (r * tq, tq), 0) % tq
    k_loc = lax.broadcasted_iota(jnp.int32, (r * tq, tq), 1)
    tiles = []
    for j in range(nwin):
        s = _dot_nt(qs, k_refs[j][0].astype(BF16))
        if j == nwin - 1:
            s = jnp.where(k_loc <= t_loc, s, NEG_INF)
        else:
            if j == 0:
                s = jnp.where(k_loc > t_loc, s, NEG_INF)
            s = jnp.where(qi - (nwin - 1) + j >= 0, s, NEG_INF)
        tiles.append(s)
    s = jnp.concatenate(tiles, axis=1)
    p = jnp.exp2((s - jnp.max(s, axis=-1, keepdims=True)).astype(BF16))
    v = jnp.concatenate([v_refs[j][0] for j in range(nwin)], axis=0).astype(BF16)
    on = _dot(p, jnp.concatenate([v, jnp.ones_like(v)], axis=1))
    o = on[:, :dh] / on[:, dh:] * _gate_rows(gates_ref[0, 0], 2, r, tq)
    o_ref[0] = (oin_ref[0] + _unstack_heads(o, r, tq)).astype(o_ref.dtype)


def nsa_mix(h, w_in, pe_k, pe_v, ck_w1, ck_w2, cv_w1, cv_w2, w_out, tq=256):
    b, t, d = h.shape
    nh, g, dh = NSA_HEADS, NSA_KV_GROUPS, NSA_HEAD_DIM
    r = nh // g
    hd, kd = nh * dh, g * dh
    n_main = hd + 6 * kd
    h2 = h.reshape(b * t, d)
    proj = mm(h2, w_in[:, :n_main], out_dtype=BF16, name="nsa_in").reshape(b, t, n_main)
    n_gate = w_in.shape[1] - n_main
    w_gate = jnp.pad(w_in[:, n_main:], ((0, 0), (0, 128 - n_gate)))
    gates = mm(h2, w_gate, epi=_sigmoid_epi, name="nsa_gates")[:, :n_gate]
    gates = jnp.transpose(gates.reshape(b, t, 3, g, r), (0, 3, 1, 2, 4)).reshape(b, g, t, 3 * r)
    k_cmp = nsa_compress(proj, hd, pe_k, ck_w1, ck_w2)
    v_cmp = nsa_compress(proj, hd + kd, pe_v, cv_w1, cv_w2)
    ncp = k_cmp.shape[2]
    ns = t // NSA_SEL_BLOCK
    n_sel = min(NSA_SEL_TOP, ns)
    cs = jnp.arange(ncp)[None, :] * NSA_CMP_STRIDE
    ss = jnp.arange(ns)[:, None] * NSA_SEL_BLOCK
    sel_map_t = (jnp.maximum(jnp.minimum(cs + NSA_CMP_BLOCK, ss + NSA_SEL_BLOCK) - jnp.maximum(cs, ss), 0)
                 .astype(F32) / NSA_CMP_BLOCK).astype(BF16)
    tq = _pick(t, tq)
    assert NSA_WINDOW % tq == 0
    nq = t // tq
    wq = r * dh
    col = lambda base: base // dh
    q_spec3 = pl.BlockSpec((1, tq, wq), lambda bi, gi, qi: (bi, qi, gi))
    gate_spec3 = pl.BlockSpec((1, 1, tq, 3 * r), lambda bi, gi, qi: (bi, gi, qi, 0))
    o_cmp, sel = pl.pallas_call(
        functools.partial(_nsa_cmp_body, tq=tq, r=r, dh=dh, n_sel=n_sel),
        grid=(b, g, nq),
        in_specs=[
            q_spec3,
            pl.BlockSpec((1, 1, ncp, dh), lambda bi, gi, qi: (bi, gi, 0, 0)),
            pl.BlockSpec((1, 1, ncp, dh), lambda bi, gi, qi: (bi, gi, 0, 0)),
            gate_spec3,
            pl.BlockSpec((ns, ncp), lambda bi, gi, qi: (0, 0)),
        ],
        out_specs=[q_spec3, pl.BlockSpec((1, 1, tq, ns), lambda bi, gi, qi: (bi, gi, qi, 0))],
        out_shape=[jax.ShapeDtypeStruct((b, t, hd), F32), jax.ShapeDtypeStruct((b, g, t, ns), BF16)],
        compiler_params=_cparams("parallel", "parallel", "parallel"),
        name="nsa_cmp",
    )(proj, k_cmp, v_cmp, gates, sel_map_t)

    tk = _pick(t, NSA_SEL_KEY_TILE)
    tqs = _pick(t, NSA_SEL_QUERY_TILE)
    assert tk % tqs == 0
    pairs = [(qi, ki) for qi in range(t // tqs) for ki in range((qi * tqs + tqs - 1) // tk + 1)]
    qi_tbl = jnp.asarray([pq for pq, _ in pairs], jnp.int32)
    ki_tbl = jnp.asarray([pk for _, pk in pairs], jnp.int32)
    diag_tbl = jnp.asarray([int(pk == (pq * tqs + tqs - 1) // tk) for pq, pk in pairs], jnp.int32)
    q_spec_p = pl.BlockSpec((1, tqs, wq), lambda bi, gi, p, qt, kt, dt: (bi, qt[p], gi))

    def kv_spec_p(base):
        return pl.BlockSpec((1, tk, dh), lambda bi, gi, p, qt, kt, dt: (bi, kt[p], col(base) + gi))

    o_sel = pl.pallas_call(
        functools.partial(_nsa_sel_body, tq=tqs, tk=tk, r=r, dh=dh),
        grid_spec=pltpu.PrefetchScalarGridSpec(
            num_scalar_prefetch=3,
            grid=(b, g, len(pairs)),
            in_specs=[q_spec_p, kv_spec_p(hd + 2 * kd), kv_spec_p(hd + 3 * kd),
                      pl.BlockSpec((1, 1, tqs, 3 * r), lambda bi, gi, p, qt, kt, dt: (bi, gi, qt[p], 0)),
                      q_spec_p,
                      pl.BlockSpec((1, 1, tqs, ns), lambda bi, gi, p, qt, kt, dt: (bi, gi, qt[p], 0))],
            out_specs=q_spec_p,
            scratch_shapes=[pltpu.VMEM((r * tqs, 2 * dh), BF16), pltpu.VMEM((r * tqs, dh), F32),
                            pltpu.VMEM((r * tqs, 2 * dh), F32)],
        ),
        out_shape=jax.ShapeDtypeStruct((b, t, hd), F32),
        compiler_params=_cparams("parallel", "parallel", "arbitrary"),
        name="nsa_sel",
    )(qi_tbl, ki_tbl, diag_tbl, proj, proj, proj, gates, o_cmp, sel)

    nwin = NSA_WINDOW // tq + 1

    def kv_spec_w(base, j):
        return pl.BlockSpec((1, tq, dh),
                            lambda bi, gi, qi: (bi, jnp.maximum(qi - (nwin - 1) + j, 0), col(base) + gi))

    o_all = pl.pallas_call(
        functools.partial(_nsa_win_body, tq=tq, r=r, dh=dh, nwin=nwin),
        grid=(b, g, nq),
        in_specs=([q_spec3] + [kv_spec_w(hd + 4 * kd, j) for j in range(nwin)]
                  + [kv_spec_w(hd + 5 * kd, j) for j in range(nwin)] + [gate_spec3, q_spec3]),
        out_specs=q_spec3,
        out_shape=jax.ShapeDtypeStruct((b, t, hd), BF16),
        compiler_params=_cparams("parallel", "parallel", "parallel"),
        name="nsa_win",
    )(proj, *([proj] * (2 * nwin)), gates, o_sel)
    return o_all.reshape(b * t, hd), [(w_out, 0, 0)], None


def _silu(a):
    return a * jax.nn.sigmoid(a)


def _ada_epi(parts, extras):
    return parts[0] + extras[0]


def ada_modulation(c, ada_w, ada_b):
    depth, d, n6 = ada_w.shape
    b = c.shape[0]
    rows = ((b + 7) // 8) * 8
    cond = jnp.pad(c, ((0, rows - b), (0, 0)))
    w2d = ada_w.reshape(depth * d, n6)
    mods = []
    for i in range(depth):
        mods.append(mm(cond, [(w2d, 0, i)], a_act=_silu, epi=_ada_epi, extras=[_row_extra(ada_b[i], _pick(n6, 1024))],
                       precision=HIGHEST, name="ada_mod")[:b])
    return jnp.stack(mods)


def kernel(x, c, ada_w, ada_b, norm_g, mlp_w1, mlp_w2, rwkv_mu, rwkv_w_rkv, rwkv_w0, rwkv_w_la, rwkv_w_lb, rwkv_a0, rwkv_a_la, rwkv_a_lb, rwkv_g_la, rwkv_g_lb, rwkv_k_k, rwkv_k_a, rwkv_r_k, rwkv_ln_g, rwkv_ln_b, rwkv_w_out, ret_w_in, ret_gn_g, ret_gn_b, ret_w_out, conv_pw1_w, conv_pw1_b, conv_dw_w, conv_dw_b, conv_ln_g, conv_ln_b, conv_pw2_w, conv_pw2_b, nsa_w_in, nsa_pe_k, nsa_pe_v, nsa_ck_w1, nsa_ck_w2, nsa_cv_w1, nsa_cv_w2, nsa_w_out):
    b, t, d = x.shape
    depth = ada_w.shape[0]
    mod = ada_modulation(c, ada_w, ada_b).reshape(depth, b, 6, d)
    h = None
    for i in range(depth):
        sh_t, sc_t, gt_t, sh_c, sc_c, gt_c = (mod[i, :, j] for j in range(6))
        kind = i % 4
        if kind == 0:
            act, ws, bias = rwkv7_time_mix(x, norm_g[i, 0], sc_t, sh_t, rwkv_mu, rwkv_w_rkv, rwkv_w0, rwkv_w_la,
                                           rwkv_w_lb, rwkv_a0, rwkv_a_la, rwkv_a_lb, rwkv_g_la, rwkv_g_lb, rwkv_k_k,
                                           rwkv_k_a, rwkv_r_k.reshape(-1), rwkv_ln_g, rwkv_ln_b, rwkv_w_out)
        else:
            if h is None:
                (h,) = resid_norm(x, pre=(norm_g[i, 0], sc_t, sh_t))
            if kind == 1:
                act, ws, bias = retention_mix(h, ret_w_in, ret_gn_g, ret_gn_b, ret_w_out)
            elif kind == 2:
                act, ws, bias = conformer_conv_mix(h, conv_pw1_w, conv_pw1_b, conv_dw_w, conv_dw_b, conv_ln_g,
                                                   conv_ln_b, conv_pw2_w, conv_pw2_b)
            else:
                act, ws, bias = nsa_mix(h, nsa_w_in, nsa_pe_k, nsa_pe_v, nsa_ck_w1, nsa_ck_w2, nsa_cv_w1, nsa_cv_w2,
                                        nsa_w_out)
        y = project(act, ws, bias, "mixer_out").reshape(b, t, d)
        act, x = mlp_up_resid(x, y, (norm_g[i, 1], gt_t), (norm_g[i, 2], sc_c, sh_c), mlp_w1, i)
        nxt = i + 1
        if nxt < depth and nxt % 4 != 0:
            x, h = mlp_down_resid(act, mlp_w2, i, x, (norm_g[i, 3], gt_c),
                                  (norm_g[nxt, 0], mod[nxt, :, 1], mod[nxt, :, 0]))
        else:
            (x,) = mlp_down_resid(act, mlp_w2, i, x, (norm_g[i, 3], gt_c))
            h = None
    return x
```

```python
import functools
import math

import jax
import jax.numpy as jnp
from jax import lax
from jax.experimental import pallas as pl
from jax.experimental.pallas import tpu as pltpu

F32 = jnp.float32
BF16 = jnp.bfloat16
HIGHEST = lax.Precision.HIGHEST

NORM_EPS = 1e-6
NEG_INF = -1e30
POS_BIG = 1e30
LOG2_E = math.log2(math.e)

RWKV_HEAD_DIM = 64
RWKV_GN_EPS = 64e-5
RWKV_CHUNK = 64
RWKV_HEADS_PER_STEP = 32

RET_HEADS = 8
RET_HEADS_PER_STEP = 4
RET_CHUNK = 256
RET_GN_EPS = 1e-5
ROPE_BASE = 10000.0

CONV_WIDTH = 31
CONV_HALO = 32
CONV_LN_EPS = 1e-5

NSA_HEADS = 16
NSA_KV_GROUPS = 4
NSA_HEAD_DIM = 128
NSA_CMP_BLOCK = 32
NSA_CMP_STRIDE = 16
NSA_SEL_BLOCK = 64
NSA_SEL_TOP = 16
NSA_WINDOW = 512
NSA_SEL_KEY_TILE = 512
NSA_SEL_QUERY_TILE = 512
NSA_WIN_GROUPS_PER_STEP = 4

VMEM_LIMIT_BYTES = 56 * 1024 * 1024


def _cparams(*sem):
    return pltpu.CompilerParams(dimension_semantics=sem, vmem_limit_bytes=VMEM_LIMIT_BYTES)


def _dot(a, b, precision=None):
    return jnp.dot(a, b, preferred_element_type=F32, precision=precision)


def _dot_nt(a, b, precision=None):
    return lax.dot_general(a, b, (((1,), (1,)), ((), ())), preferred_element_type=F32, precision=precision)


def _pick(n, pref):
    if n <= pref:
        return n
    t = pref
    while n % t:
        t //= 2
    return t


def _mm_body(*refs, n_w, n_ex, n_out, nk, epi, precision, has_a_add, a_act):
    a_ref = refs[0]
    pos = 1
    a_add_ref = None
    if has_a_add:
        a_add_ref = refs[pos]
        pos += 1
    w_refs = refs[pos:pos + n_w]
    pos += n_w
    ex_refs = refs[pos:pos + n_ex]
    pos += n_ex
    o_refs = refs[pos:pos + n_out]
    acc_refs = refs[pos + n_out:]

    def finish(parts):
        res = epi(parts, [e[...] for e in ex_refs])
        if n_out == 1 and not isinstance(res, (tuple, list)):
            res = (res,)
        for o_ref, val in zip(o_refs, res, strict=True):
            o_ref[...] = val.astype(o_ref.dtype)

    a = a_ref[...]
    if a_add_ref is not None:
        a = a.astype(F32) + a_add_ref[...]
    if a_act is not None:
        a = a_act(a)
    if precision is None:
        a = a.astype(BF16)
    parts = []
    for w_ref in w_refs:
        w = w_ref[...]
        if precision is None:
            w = w.astype(BF16)
        parts.append(_dot(a, w, precision))

    if nk == 1:
        finish(parts)
    else:
        k = pl.program_id(2)

        @pl.when(k == 0)
        def _():
            for acc, p in zip(acc_refs, parts):
                acc[...] = p

        @pl.when(k > 0)
        def _():
            for acc, p in zip(acc_refs, parts):
                acc[...] += p

        @pl.when(k == nk - 1)
        def _():
            finish([acc[...] for acc in acc_refs])


def _first(parts, extras):
    return parts[0]


def mm(a, ws, *, n_out=None, extras=(), epi=_first, out_dtype=F32, tm=2048, tn=1024, tk=2048, precision=None,
       a_add=None, a_act=None, a_part=(0, 1), cast_in_kernel=False, name="mm"):
    m, kdim = a.shape
    a_s, a_parts = a_part
    assert m % a_parts == 0
    m //= a_parts
    if not isinstance(ws, (list, tuple)):
        ws = [(ws, 0, 0)]
    ws = [tuple(w) + (0,) * (3 - len(w)) for w in ws]
    if n_out is None:
        n_out = ws[0][0].shape[1]
    tm = _pick(m, tm)
    tn = _pick(n_out, tn)
    tk = _pick(kdim, tk)
    nk = kdim // tk
    assert m % tm == 0 and n_out % tn == 0 and kdim % tk == 0
    in_specs = [pl.BlockSpec((tm, tk), lambda i, j, k, o=a_s * (m // tm): (i + o, k))]
    args = [a]
    if a_add is not None:
        in_specs.append(pl.BlockSpec((1, tk), lambda i, j, k: (0, k)))
        args.append(a_add)
    for w, off, koff in ws:
        in_specs.append(pl.BlockSpec((tk, tn), lambda i, j, k, off=off, ko=koff * nk: (k + ko, j + off)))
        args.append(w.astype(BF16) if precision is None and not cast_in_kernel else w)
    for arr, bshape, imap in extras:
        in_specs.append(pl.BlockSpec(bshape, imap))
        args.append(arr)
    scratch = [pltpu.VMEM((tm, tn), F32) for _ in ws] if nk > 1 else []
    multi = isinstance(out_dtype, (tuple, list))
    out_dtypes = tuple(out_dtype) if multi else (out_dtype,)
    body = functools.partial(_mm_body, n_w=len(ws), n_ex=len(extras), n_out=len(out_dtypes), nk=nk, epi=epi,
                             precision=precision, has_a_add=a_add is not None, a_act=a_act)
    outs = pl.pallas_call(
        body,
        grid=(m // tm, n_out // tn, nk),
        in_specs=in_specs,
        out_specs=[pl.BlockSpec((tm, tn), lambda i, j, k: (i, j)) for _ in out_dtypes],
        out_shape=[jax.ShapeDtypeStruct((m, n_out), dt) for dt in out_dtypes],
        scratch_shapes=scratch,
        compiler_params=_cparams("parallel", "parallel", "arbitrary"),
        name=name,
    )(*args)
    return tuple(outs) if multi else outs[0]


def _row_extra(vec, tn):
    return (vec.reshape(1, -1), (1, tn), lambda i, j, k: (0, j))


def _rms(x, g):
    return x * lax.rsqrt(jnp.mean(x * x, axis=-1, keepdims=True) + NORM_EPS) * g


def _resid_norm_body(*refs, has_y, has_h):
    pos = 0
    x_ref = refs[pos]; pos += 1
    if has_y:
        y_ref, gpost_ref, gate_ref = refs[pos:pos + 3]; pos += 3
    if has_h:
        gpre_ref, scale_ref, shift_ref = refs[pos:pos + 3]; pos += 3
    outs = refs[pos:]
    x = x_ref[0]
    o = 0
    if has_y:
        y = y_ref[0].astype(F32)
        x = x + (1.0 + gate_ref[0]) * _rms(y, gpost_ref[...])
        outs[o][0] = x
        o += 1
    if has_h:
        h = _rms(x, gpre_ref[...]) * (1.0 + scale_ref[0]) + shift_ref[0]
        outs[o][0] = h.astype(outs[o].dtype)


def resid_norm(x, y=None, post=None, pre=None, tt=512):
    b, t, d = x.shape
    tt = _pick(t, tt)
    row = pl.BlockSpec((1, tt, d), lambda bi, ti: (bi, ti, 0))
    vec = pl.BlockSpec((1, d), lambda bi, ti: (0, 0))
    bvec = pl.BlockSpec((1, 1, d), lambda bi, ti: (bi, 0, 0))
    args, in_specs, out_shapes, out_specs = [x], [row], [], []
    if y is not None:
        args += [y, post[0].reshape(1, d), post[1].reshape(b, 1, d)]
        in_specs += [row, vec, bvec]
        out_shapes.append(jax.ShapeDtypeStruct((b, t, d), F32))
        out_specs.append(row)
    if pre is not None:
        args += [pre[0].reshape(1, d), pre[1].reshape(b, 1, d), pre[2].reshape(b, 1, d)]
        in_specs += [vec, bvec, bvec]
        out_shapes.append(jax.ShapeDtypeStruct((b, t, d), BF16))
        out_specs.append(row)
    res = pl.pallas_call(
        functools.partial(_resid_norm_body, has_y=y is not None, has_h=pre is not None),
        grid=(b, t // tt),
        in_specs=in_specs,
        out_specs=out_specs,
        out_shape=out_shapes,
        compiler_params=_cparams("parallel", "parallel"),
        name="resid_norm",
    )(*args)
    return tuple(res)


def _add_bias(parts, extras):
    return parts[0] + extras[0]


def project(act, ws, bias, name):
    if bias is None:
        return mm(act, ws, out_dtype=BF16, name=name)
    tn = _pick(ws[0][0].shape[1], 1024)
    return mm(act, ws, epi=_add_bias, extras=[_row_extra(bias, tn)], tn=tn, out_dtype=BF16, name=name)


def _mlp_up_body(x_ref, y_ref, gpost_ref, gate_ref, gpre_ref, scale_ref, shift_ref, w_ref, a_ref, xn_ref, h0_ref,
                 h1_ref, *, n_tiles, rows):
    i, j = pl.program_id(0), pl.program_id(1)
    bufs = (h0_ref, h1_ref)

    tm = a_ref.shape[0]
    pieces = 8
    nr, mr = rows // pieces, tm // pieces

    def norm_rows(dst_ref, s):
        sl = pl.ds(s * nr, nr)
        xn = x_ref[sl, :] + (1.0 + gate_ref[0]) * _rms(y_ref[sl, :].astype(F32), gpost_ref[...])
        xn_ref[sl, :] = xn
        h = _rms(xn, gpre_ref[...]) * (1.0 + scale_ref[0]) + shift_ref[0]
        dst_ref[pl.ds(pl.multiple_of(j * rows + s * nr, nr), nr), :] = h.astype(dst_ref.dtype)

    def up_project(src_ref, w, s):
        sl = pl.ds(s * mr, mr)
        r = jnp.maximum(_dot(src_ref[sl, :], w), 0.0)
        a_ref[sl, :] = (r * r).astype(a_ref.dtype)

    @pl.when(i == 0)
    def _():
        for s in range(pieces):
            norm_rows(bufs[0], s)

    for parity in (0, 1):
        @pl.when((i > 0) & (i < n_tiles) & (i % 2 == parity))
        def _():
            w = w_ref[...].astype(BF16)
            for s in range(pieces):
                up_project(bufs[1 - parity], w, s)
                norm_rows(bufs[parity], s)

    @pl.when(i == n_tiles)
    def _():
        w = w_ref[...].astype(BF16)
        for s in range(pieces):
            up_project(bufs[(n_tiles - 1) % 2], w, s)


def mlp_up_resid(x, y, post, pre, w1, layer, tm=2048, tn=512):
    b, t, d = x.shape
    nl, _, f = w1.shape
    m = b * t
    tm = _pick(t, tm)
    tn = _pick(f, tn)
    n_tiles, nj = m // tm, f // tn
    assert tm % nj == 0
    rows = tm // nj
    assert rows % 16 == 0
    tiles_per_batch = t // tm
    slice_idx = lambda i, j: jnp.where(i < n_tiles, i * nj + j, n_tiles * nj - 1)
    rows_spec = pl.BlockSpec((rows, d), lambda i, j: (slice_idx(i, j), 0))
    vec = pl.BlockSpec((1, d), lambda i, j: (0, 0))
    bvec = pl.BlockSpec((1, 1, d), lambda i, j: (jnp.minimum(i, n_tiles - 1) // tiles_per_batch, 0, 0))
    a, x_new = pl.pallas_call(
        functools.partial(_mlp_up_body, n_tiles=n_tiles, rows=rows),
        grid=(n_tiles + 1, nj),
        in_specs=[rows_spec, rows_spec, vec, bvec, vec, bvec, bvec,
                  pl.BlockSpec((d, tn), lambda i, j: (layer, j))],
        out_specs=[pl.BlockSpec((tm, tn), lambda i, j: (jnp.maximum(i - 1, 0), jnp.where(i > 0, j, 0))),
                   rows_spec],
        out_shape=[jax.ShapeDtypeStruct((m, f), BF16), jax.ShapeDtypeStruct((m, d), F32)],
        scratch_shapes=[pltpu.VMEM((tm, d), BF16), pltpu.VMEM((tm, d), BF16)],
        compiler_params=_cparams("arbitrary", "arbitrary"),
        name="mlp_up_resid",
    )(x.reshape(m, d), y.reshape(m, d), post[0].reshape(1, d), post[1].reshape(b, 1, d), pre[0].reshape(1, d),
      pre[1].reshape(b, 1, d), pre[2].reshape(b, 1, d), w1.reshape(nl * d, f))
    return a, x_new.reshape(b, t, d)


def _mlp_down_body(*refs, n_tiles, rows, has_pre):
    a_ref, w_ref, x_ref, gpost_ref, gate_ref = refs[:5]
    if has_pre:
        gpre_ref, scale_ref, shift_ref, xn_ref, h_ref, acc0_ref, acc1_ref = refs[5:]
    else:
        xn_ref, acc0_ref, acc1_ref = refs[5:]
    i, k = pl.program_id(0), pl.program_id(1)
    accs = (acc0_ref, acc1_ref)
    tm = a_ref.shape[0]
    pieces = 4
    mr = tm // pieces
    nr = rows // pieces

    @pl.when((i == 0) & (k == 0))
    def _():
        acc0_ref[...] = jnp.zeros_like(acc0_ref)
        acc1_ref[...] = jnp.zeros_like(acc1_ref)

    def accumulate(dst_ref, w, s):
        sl = pl.ds(s * mr, mr)
        part = _dot(a_ref[sl, :], w)
        dst_ref[sl, :] = jnp.where(k == 0, part, dst_ref[sl, :] + part)

    def finish_rows(src_ref, s):
        sl = pl.ds(s * nr, nr)
        y = src_ref[pl.ds(pl.multiple_of(k * rows + s * nr, nr), nr), :]
        xn = x_ref[sl, :] + (1.0 + gate_ref[0]) * _rms(y, gpost_ref[...])
        xn_ref[sl, :] = xn
        if has_pre:
            h_ref[sl, :] = (_rms(xn, gpre_ref[...]) * (1.0 + scale_ref[0]) + shift_ref[0]).astype(h_ref.dtype)

    @pl.when(i == 0)
    def _():
        w = w_ref[...]
        for s in range(pieces):
            accumulate(accs[0], w, s)

    for parity in (0, 1):
        @pl.when((i > 0) & (i < n_tiles) & (i % 2 == parity))
        def _():
            w = w_ref[...]
            for s in range(pieces):
                accumulate(accs[parity], w, s)
                finish_rows(accs[1 - parity], s)

    @pl.when(i == n_tiles)
    def _():
        for s in range(pieces):
            finish_rows(accs[(n_tiles - 1) % 2], s)


def mlp_down_resid(a, w2, layer, x, post, pre=None, tm=1024, tk=1024):
    b, t, d = x.shape
    nl, f, _ = w2.shape
    m = b * t
    tm = _pick(t, tm)
    tk = _pick(f, tk)
    n_tiles, nk = m // tm, f // tk
    assert tm % nk == 0
    rows = tm // nk
    assert rows % (8 * 16) == 0
    tiles_per_batch = t // tm
    slice_idx = lambda i, k: jnp.where(i > 0, (i - 1) * nk + k, 0)
    rows_spec = pl.BlockSpec((rows, d), lambda i, k: (slice_idx(i, k), 0))
    vec = pl.BlockSpec((1, d), lambda i, k: (0, 0))
    bvec = pl.BlockSpec((1, 1, d), lambda i, k: (jnp.maximum(i - 1, 0) // tiles_per_batch, 0, 0))
    in_specs = [pl.BlockSpec((tm, tk), lambda i, k: (jnp.minimum(i, n_tiles - 1), jnp.where(i < n_tiles, k, nk - 1))),
                pl.BlockSpec((tk, d), lambda i, k: (layer * nk + jnp.where(i < n_tiles, k, nk - 1), 0)),
                rows_spec, vec, bvec]
    args = [a, w2.reshape(nl * f, d).astype(BF16), x.reshape(m, d), post[0].reshape(1, d), post[1].reshape(b, 1, d)]
    out_specs, out_shapes = [rows_spec], [jax.ShapeDtypeStruct((m, d), F32)]
    if pre is not None:
        in_specs += [vec, bvec, bvec]
        args += [pre[0].reshape(1, d), pre[1].reshape(b, 1, d), pre[2].reshape(b, 1, d)]
        out_specs.append(rows_spec)
        out_shapes.append(jax.ShapeDtypeStruct((m, d), BF16))
    outs = pl.pallas_call(
        functools.partial(_mlp_down_body, n_tiles=n_tiles, rows=rows, has_pre=pre is not None),
        grid=(n_tiles + 1, nk),
        in_specs=in_specs,
        out_specs=out_specs,
        out_shape=out_shapes,
        scratch_shapes=[pltpu.VMEM((tm, d), F32), pltpu.VMEM((tm, d), F32)],
        compiler_params=_cparams("arbitrary", "arbitrary"),
        name="mlp_down_resid",
    )(*args)
    return tuple(o.reshape(b, t, d) for o in outs)


def _glu(parts, extras):
    return (parts[0] + extras[0]) * jax.nn.sigmoid(parts[1] + extras[1])


def _conv_ln_body(cur_ref, prev_ref, w_ref, b_ref, g_ref, beta_ref, o_ref, buf_ref, acc_ref, *, tt, d):
    ti = pl.program_id(1)
    halo = prev_ref[0].astype(F32)
    buf_ref[0:CONV_HALO, :] = jnp.where(ti == 0, jnp.zeros_like(halo), halo)
    buf_ref[CONV_HALO:, :] = cur_ref[0].astype(F32)
    lane_chunk = min(d, 256)
    row_chunk = min(tt, 64)
    base = CONV_HALO - (CONV_WIDTH - 1)
    sub = 8
    for c0 in range(0, d, lane_chunk):
        for r0 in range(0, tt, row_chunk):
            acc = None
            for s in range(sub):
                taps = [j for j in range(CONV_WIDTH) if (base + j) % sub == s]
                if not taps:
                    continue
                rows = row_chunk + (sub if s else 0)
                part = jnp.zeros((rows, lane_chunk), F32)
                for j in taps:
                    off = r0 + base + j - s
                    part = part + buf_ref[off:off + rows, c0:c0 + lane_chunk] * w_ref[j:j + 1, c0:c0 + lane_chunk]
                part = part[s:s + row_chunk]
                acc = part if acc is None else acc + part
            acc_ref[r0:r0 + row_chunk, c0:c0 + lane_chunk] = acc
    y = acc_ref[...] + b_ref[...]
    mu = jnp.mean(y, axis=-1, keepdims=True)
    yc = y - mu
    var = jnp.mean(yc * yc, axis=-1, keepdims=True)
    z = yc * lax.rsqrt(var + CONV_LN_EPS) * g_ref[...] + beta_ref[...]
    o_ref[0] = (z * jax.nn.sigmoid(z)).astype(o_ref.dtype)


def conv_ln_silu(u, dw_w, dw_b, ln_g, ln_b, tt=128):
    b, t, d = u.shape
    tt = _pick(t, tt)
    hb = tt // CONV_HALO
    vec = pl.BlockSpec((1, d), lambda bi, ti: (0, 0))
    return pl.pallas_call(
        functools.partial(_conv_ln_body, tt=tt, d=d),
        grid=(b, t // tt),
        in_specs=[
            pl.BlockSpec((1, tt, d), lambda bi, ti: (bi, ti, 0)),
            pl.BlockSpec((1, CONV_HALO, d), lambda bi, ti: (bi, jnp.maximum(ti * hb - 1, 0), 0)),
            pl.BlockSpec((CONV_WIDTH, d), lambda bi, ti: (0, 0)),
            vec, vec, vec,
        ],
        out_specs=pl.BlockSpec((1, tt, d), lambda bi, ti: (bi, ti, 0)),
        out_shape=jax.ShapeDtypeStruct((b, t, d), BF16),
        scratch_shapes=[pltpu.VMEM((tt + CONV_HALO, d), F32), pltpu.VMEM((tt, d), F32)],
        compiler_params=_cparams("parallel", "parallel"),
        name="conv_ln_silu",
    )(u, u, dw_w, dw_b.reshape(1, d), ln_g.reshape(1, d), ln_b.reshape(1, d))


def conformer_conv_mix(h, pw1_w, pw1_b, dw_w, dw_b, ln_g, ln_b, pw2_w, pw2_b):
    b, t, d = h.shape
    tn = _pick(d, 1024)
    b1 = pw1_b.reshape(1, -1)
    u = mm(h.reshape(b * t, d), [(pw1_w, 0), (pw1_w, d // tn)], n_out=d, tm=1024, tn=tn, epi=_glu,
           extras=[(b1, (1, tn), lambda i, j, k: (0, j)),
                   (b1, (1, tn), lambda i, j, k, o=d // tn: (0, j + o))], out_dtype=BF16, name="conf_pw1_glu")
    z = conv_ln_silu(u.reshape(b, t, d), dw_w, dw_b, ln_g, ln_b)
    return z.reshape(b * t, d), [(pw2_w, 0, 0)], pw2_b


def _rope(x, cos, sin):
    half = x.shape[-1] // 2
    x1, x2 = x[:, :half], x[:, half:]
    return jnp.concatenate([x1 * cos - x2 * sin, x2 * cos + x1 * sin], axis=-1)


def _retention_body(q_ref, k_ref, v_ref, gate_ref, cos_ref, sin_ref, inner_ref, qdec_ref, kdec_ref, cdec_ref,
                    gng_ref, gnb_ref, o_ref, state_ref, *, dk, heads):
    @pl.when(pl.program_id(2) == 0)
    def _():
        state_ref[...] = jnp.zeros_like(state_ref)

    cos, sin = cos_ref[...], sin_ref[...]
    dv = v_ref.shape[2] // heads
    for hh in range(heads):
        ks, vs = slice(hh * dk, (hh + 1) * dk), slice(hh * dv, (hh + 1) * dv)
        q = _rope(q_ref[0, :, ks].astype(F32), cos, sin)
        k = _rope(k_ref[0, :, ks].astype(F32), cos, sin) * (dk ** -0.5)
        v = v_ref[0, :, vs].astype(BF16)
        qb = q.astype(BF16)
        s = _dot_nt(qb, k.astype(BF16)) * inner_ref[hh]
        state = state_ref[hh]
        o = _dot(s.astype(BF16), v) + _dot(qb, state.astype(BF16)) * qdec_ref[hh]
        kd_t = (k * kdec_ref[hh]).T.astype(BF16)
        state_ref[hh] = state * cdec_ref[hh] + _dot(kd_t, v)
        mu = jnp.mean(o, axis=-1, keepdims=True)
        oc = o - mu
        var = jnp.mean(oc * oc, axis=-1, keepdims=True)
        on = oc * lax.rsqrt(var + RET_GN_EPS) * gng_ref[:, vs] + gnb_ref[:, vs]
        gate = gate_ref[0, :, vs].astype(F32)
        o_ref[0, :, vs] = (gate * jax.nn.sigmoid(gate) * on).astype(o_ref.dtype)


def retention_mix(h, w_in, gn_g, gn_b, w_out):
    b, t, d = h.shape
    nh, c = RET_HEADS, _pick(t, RET_CHUNK)
    dk = d // nh
    dv = 2 * dk
    proj = mm(h.reshape(b * t, d), w_in, out_dtype=BF16, tm=2048, tn=512, cast_in_kernel=True,
              name="ret_in").reshape(b, t, 6 * d)
    pos = jnp.arange(t, dtype=F32)
    inv_freq = ROPE_BASE ** (-jnp.arange(0, dk, 2, dtype=F32) / dk)
    ang = pos[:, None] * inv_freq[None, :]
    cos, sin = jnp.cos(ang), jnp.sin(ang)
    log_gamma = jnp.log(1.0 - 2.0 ** (-5.0 - jnp.arange(nh, dtype=F32)))
    idx = jnp.arange(c, dtype=F32)
    diff = idx[:, None] - idx[None, :]
    inner = jnp.where(diff >= 0, jnp.exp(jnp.maximum(diff, 0.0)[None] * log_gamma[:, None, None]), 0.0)
    q_dec = jnp.exp((idx + 1.0)[None] * log_gamma[:, None])[:, :, None]
    k_dec = jnp.exp((c - 1.0 - idx)[None] * log_gamma[:, None])[:, :, None]
    c_dec = jnp.exp(c * log_gamma)[:, None, None]
    hp = RET_HEADS_PER_STEP
    assert nh % hp == 0
    ng = nh // hp
    o = pl.pallas_call(
        functools.partial(_retention_body, dk=dk, heads=hp),
        grid=(b, ng, t // c),
        in_specs=[
            pl.BlockSpec((1, c, hp * dk), lambda bi, hi, ci: (bi, ci, hi)),
            pl.BlockSpec((1, c, hp * dk), lambda bi, hi, ci: (bi, ci, ng + hi)),
            pl.BlockSpec((1, c, hp * dv), lambda bi, hi, ci: (bi, ci, ng + hi)),
            pl.BlockSpec((1, c, hp * dv), lambda bi, hi, ci: (bi, ci, 2 * ng + hi)),
            pl.BlockSpec((c, dk // 2), lambda bi, hi, ci: (ci, 0)),
            pl.BlockSpec((c, dk // 2), lambda bi, hi, ci: (ci, 0)),
            pl.BlockSpec((hp, c, c), lambda bi, hi, ci: (hi, 0, 0)),
            pl.BlockSpec((hp, c, 1), lambda bi, hi, ci: (hi, 0, 0)),
            pl.BlockSpec((hp, c, 1), lambda bi, hi, ci: (hi, 0, 0)),
            pl.BlockSpec((hp, 1, 1), lambda bi, hi, ci: (hi, 0, 0)),
            pl.BlockSpec((1, hp * dv), lambda bi, hi, ci: (0, hi)),
            pl.BlockSpec((1, hp * dv), lambda bi, hi, ci: (0, hi)),
        ],
        out_specs=pl.BlockSpec((1, c, hp * dv), lambda bi, hi, ci: (bi, ci, hi)),
        out_shape=jax.ShapeDtypeStruct((b, t, nh * dv), BF16),
        scratch_shapes=[pltpu.VMEM((hp, dk, dv), F32)],
        compiler_params=_cparams("parallel", "parallel", "arbitrary"),
        name="retention",
    )(proj, proj, proj, proj, cos, sin, inner, q_dec, k_dec, c_dec, gn_g.reshape(1, -1), gn_b.reshape(1, -1))
    return o.reshape(b * t, nh * dv), [(w_out, 0, 0)], None


def _rwkv_pre_body(x_ref, prev_ref, g_ref, scale_ref, shift_ref, mu_ref, o_ref):
    ti = pl.program_id(1)
    g, scale, shift = g_ref[...], scale_ref[0], shift_ref[0]
    h = _rms(x_ref[0], g) * (1.0 + scale) + shift
    hp = _rms(prev_ref[0], g) * (1.0 + scale) + shift
    last = jnp.where(ti == 0, 0.0, hp[7:8, :])
    row = lax.broadcasted_iota(jnp.int32, h.shape, 0)
    shifted = jnp.where(row == 0, last, pltpu.roll(h, 1, axis=0))
    xx = shifted - h
    for s in range(6):
        o_ref[s, 0] = (h + xx * mu_ref[s:s + 1, :]).astype(o_ref.dtype)


def rwkv_pre(x, g_pre, scale, shift, mu, tt=256):
    b, t, d = x.shape
    tt = _pick(t, tt)
    bvec = pl.BlockSpec((1, 1, d), lambda bi, ti: (bi, 0, 0))
    return pl.pallas_call(
        _rwkv_pre_body,
        grid=(b, t // tt),
        in_specs=[
            pl.BlockSpec((1, tt, d), lambda bi, ti: (bi, ti, 0)),
            pl.BlockSpec((1, 8, d), lambda bi, ti: (bi, jnp.maximum(ti * (tt // 8) - 1, 0), 0)),
            pl.BlockSpec((1, d), lambda bi, ti: (0, 0)),
            bvec, bvec,
            pl.BlockSpec((6, d), lambda bi, ti: (0, 0)),
        ],
        out_specs=pl.BlockSpec((6, 1, tt, d), lambda bi, ti: (0, bi, ti, 0)),
        out_shape=jax.ShapeDtypeStruct((6, b, t, d), BF16),
        compiler_params=_cparams("parallel", "parallel"),
        name="rwkv_pre",
    )(x, x, g_pre.reshape(1, d), scale.reshape(b, 1, d), shift.reshape(b, 1, d), mu)


def _rwkv_scan_body(r_ref, k_ref, v_ref, a_ref, lw_ref, g_ref, kk_ref, ka_ref, rk_ref, lng_ref, lnb_ref, o_ref,
                    state_ref, *, chunk, heads, n):
    @pl.when(pl.program_id(2) == 0)
    def _():
        state_ref[...] = jnp.zeros_like(state_ref)

    ln = chunk
    hs = range(heads)
    row = lax.broadcasted_iota(jnp.int32, (ln, ln), 0)
    col = lax.broadcasted_iota(jnp.int32, (ln, ln), 1)
    tri = (col <= row).astype(BF16)
    eye = (row == col).astype(F32)
    wid = heads * n
    seg_w = min(wid, 256)
    assert wid % seg_w == 0 and seg_w % n == 0
    seg = (lax.broadcasted_iota(jnp.int32, (seg_w, seg_w), 0) // n
           == lax.broadcasted_iota(jnp.int32, (seg_w, seg_w), 1) // n).astype(BF16)

    def split_dot(m, x):
        hi = x.astype(BF16)
        lo = (x - hi.astype(F32)).astype(BF16)
        return _dot(m, hi) + _dot(m, lo)

    def seg_sum(x):
        hi = x.astype(BF16)
        lo = (x - hi.astype(F32)).astype(BF16)
        return jnp.concatenate([_dot(hi[:, j:j + seg_w], seg) + _dot(lo[:, j:j + seg_w], seg)
                                for j in range(0, wid, seg_w)], axis=1)

    heads_of = lambda x: [x[:, hd * n:(hd + 1) * n] for hd in hs]

    r, k, v, a = (ref[0].astype(F32) for ref in (r_ref, k_ref, v_ref, a_ref))
    lw = lw_ref[0]
    kk = k * kk_ref[...]
    kk = kk / jnp.maximum(jnp.sqrt(seg_sum(kk * kk)), 1e-12)
    k2 = k * (1.0 + (a - 1.0) * ka_ref[...])
    beta = kk * a
    c = split_dot(tri, lw)
    c_last = c[ln - 1:ln, :]
    e_neg = jnp.exp(-c)
    e_rem = jnp.exp(c_last - c)
    e_last = heads_of(jnp.exp(c_last))
    a_t = heads_of(-kk * jnp.exp(c - lw))
    r_t = heads_of(r * jnp.exp(c))
    b_t = heads_of(beta * e_neg)
    k_t = heads_of(k2 * e_neg)
    b_rem = heads_of(beta * e_rem)
    k_rem = heads_of(k2 * e_rem)
    vs = heads_of(v)
    bonus = seg_sum(r * k2 * rk_ref[...]) * v

    ar = [jnp.concatenate([a_t[h], r_t[h]], axis=0).astype(BF16) for h in hs]
    bk = [jnp.concatenate([b_t[h], k_t[h]], axis=0).astype(BF16) for h in hs]
    bk_rem = [jnp.concatenate([b_rem[h], k_rem[h]], axis=0).astype(BF16) for h in hs]
    vb = [vs[h].astype(BF16) for h in hs]
    state = [state_ref[h] for h in hs]
    p = [_dot_nt(ar[h], bk[h]) for h in hs]
    q0 = [_dot_nt(ar[h], state[h].astype(BF16)) for h in hs]
    row2 = lax.broadcasted_iota(jnp.int32, (2 * ln, 2 * ln), 0)
    col2 = lax.broadcasted_iota(jnp.int32, (2 * ln, 2 * ln), 1)
    keep = col2 % ln < row2 % ln + row2 // ln
    pm = [jnp.where(keep, p[h], 0.0) for h in hs]
    m_ab = [pm[h][:ln, :ln] for h in hs]
    m_ak = [pm[h][:ln, ln:].astype(BF16) for h in hs]
    m_r = [pm[h][ln:].astype(BF16) for h in hs]
    rhs = [q0[h][:ln] + _dot(m_ak[h], vb[h]) for h in hs]
    inv = [eye + m_ab[h] for h in hs]
    pw = [m_ab[h].astype(BF16) for h in hs]
    for _ in range(int(math.log2(ln)) - 1):
        pw = [_dot(pw[h], pw[h]).astype(BF16) for h in hs]
        inv = [inv[h] + _dot(inv[h].astype(BF16), pw[h]) for h in hs]
    u = [_dot(inv[h].astype(BF16), rhs[h].astype(BF16)) for h in hs]
    uv = [jnp.concatenate([u[h], vs[h]], axis=0) for h in hs]
    y = [q0[h][ln:] + _dot(m_r[h], uv[h].astype(BF16)) for h in hs]
    for h in hs:
        state_ref[h] = state[h] * e_last[h] + _dot(uv[h].T.astype(BF16), bk_rem[h])
    yn = []
    for h in hs:
        mu = jnp.mean(y[h], axis=-1, keepdims=True)
        yc = y[h] - mu
        var = jnp.mean(yc * yc, axis=-1, keepdims=True)
        yn.append(yc * lax.rsqrt(var + RWKV_GN_EPS))
    yn = jnp.concatenate(yn, axis=-1) * lng_ref[...] + lnb_ref[...]
    o_ref[0] = ((yn + bonus) * g_ref[0]).astype(o_ref.dtype)


def rwkv_scan(r, k, v, a, lw, g, k_k, k_a, r_k, ln_g, ln_b):
    b, t, d = r.shape
    n = RWKV_HEAD_DIM
    hps = min(RWKV_HEADS_PER_STEP, d // n)
    w = hps * n
    ln = _pick(t, RWKV_CHUNK)
    tok = pl.BlockSpec((1, ln, w), lambda bi, hi, ci: (bi, ci, hi))
    vec = pl.BlockSpec((1, w), lambda bi, hi, ci: (0, hi))
    return pl.pallas_call(
        functools.partial(_rwkv_scan_body, chunk=ln, heads=hps, n=n),
        grid=(b, d // w, t // ln),
        in_specs=[tok] * 6 + [vec] * 5,
        out_specs=tok,
        out_shape=jax.ShapeDtypeStruct((b, t, d), BF16),
        scratch_shapes=[pltpu.VMEM((hps, n, n), F32)],
        compiler_params=_cparams("parallel", "parallel", "arbitrary"),
        name="rwkv_scan",
    )(r, k, v, a, lw, g, k_k.reshape(1, d), k_a.reshape(1, d), r_k.reshape(1, d), ln_g.reshape(1, d),
      ln_b.reshape(1, d))


def _tanh_epi(parts, extras):
    return jnp.tanh(parts[0])


def _sigmoid_epi(parts, extras):
    return jax.nn.sigmoid(parts[0])


def _sigmoid_bias_epi(parts, extras):
    return jax.nn.sigmoid(extras[0] + parts[0])


def _logdecay_epi(parts, extras):
    return -jnp.exp(-jax.nn.softplus(-(extras[0] + parts[0])) - 0.5)


def rwkv7_time_mix(x, g_pre, scale, shift, mu, w_rkv, w0, w_la, w_lb, a0, a_la, a_lb, g_la, g_lb, k_k, k_a, r_k,
                   ln_g, ln_b, w_out):
    b, t, d = x.shape
    xs = rwkv_pre(x, g_pre, scale, shift, mu).reshape(6 * b * t, d)
    tn = _pick(d, 1024)
    w_rkv2d = w_rkv.reshape(3 * d, d)
    r = mm(xs, [(w_rkv2d, 0, 0)], a_part=(0, 6), out_dtype=BF16, name="rwkv_r")
    k = mm(xs, [(w_rkv2d, 0, 1)], a_part=(1, 6), out_dtype=BF16, name="rwkv_k")
    v = mm(xs, [(w_rkv2d, 0, 2)], a_part=(2, 6), out_dtype=BF16, name="rwkv_v")
    lw = mm(mm(xs, w_la, a_part=(3, 6), epi=_tanh_epi, out_dtype=BF16, name="rwkv_w_la"), w_lb, epi=_logdecay_epi,
            extras=[_row_extra(w0, tn)], name="rwkv_w_lb")
    a = mm(mm(xs, a_la, a_part=(4, 6), out_dtype=BF16, name="rwkv_a_la"), a_lb, epi=_sigmoid_bias_epi,
           extras=[_row_extra(a0, tn)], out_dtype=BF16, name="rwkv_a_lb")
    g = mm(mm(xs, g_la, a_part=(5, 6), epi=_sigmoid_epi, out_dtype=BF16, name="rwkv_g_la"), g_lb, out_dtype=BF16,
           name="rwkv_g_lb")
    sh = lambda z: z.reshape(b, t, d)
    z = rwkv_scan(sh(r), sh(k), sh(v), sh(a), sh(lw), sh(g), k_k, k_a, r_k, ln_g, ln_b)
    return z.reshape(b * t, d), [(w_out, 0, 0)], None


def _nsa_compress_body(kv_ref, pe_ref, w1_ref, w2_ref, o_ref, buf_ref, *, nchunk, dh):
    st = NSA_CMP_STRIDE
    hid = w1_ref.shape[1]
    buf_ref[...] = kv_ref[0].astype(F32)
    first = jnp.zeros((nchunk, hid), F32)
    second = jnp.zeros((nchunk, hid), F32)
    for j in range(st):
        xj = buf_ref[pl.ds(j, nchunk, stride=st), :]
        first = first + _dot((xj + pe_ref[j:j + 1, :]).astype(BF16), w1_ref[j * dh:(j + 1) * dh, :])
        second = second + _dot((xj + pe_ref[st + j:st + j + 1, :]).astype(BF16),
                               w1_ref[(st + j) * dh:(st + j + 1) * dh, :])
    hidden = jax.nn.gelu(first + pltpu.roll(second, nchunk - 1, axis=0))
    out = _dot(hidden.astype(BF16), w2_ref[...])
    row = lax.broadcasted_iota(jnp.int32, out.shape, 0)
    o_ref[0, 0] = jnp.where(row < nchunk - 1, out, 0.0)


def nsa_compress(proj, col0, pe, w1, w2):
    b, t, _ = proj.shape
    g, dh = NSA_KV_GROUPS, NSA_HEAD_DIM
    nchunk = t // NSA_CMP_STRIDE
    hid = w1.shape[1]
    return pl.pallas_call(
        functools.partial(_nsa_compress_body, nchunk=nchunk, dh=dh),
        grid=(b, g),
        in_specs=[
            pl.BlockSpec((1, t, dh), lambda bi, gi: (bi, 0, col0 // dh + gi)),
            pl.BlockSpec((NSA_CMP_BLOCK, dh), lambda bi, gi: (0, 0)),
            pl.BlockSpec((NSA_CMP_BLOCK * dh, hid), lambda bi, gi: (0, 0)),
            pl.BlockSpec((hid, dh), lambda bi, gi: (0, 0)),
        ],
        out_specs=pl.BlockSpec((1, 1, nchunk, dh), lambda bi, gi: (bi, gi, 0, 0)),
        out_shape=jax.ShapeDtypeStruct((b, g, nchunk, dh), F32),
        scratch_shapes=[pltpu.VMEM((t, dh), F32)],
        compiler_params=_cparams("parallel", "parallel"),
        name="nsa_compress",
    )(proj, pe, w1.astype(BF16), w2.astype(BF16))


def _stack_heads(q, r, dh):
    return jnp.concatenate([q[:, i * dh:(i + 1) * dh] for i in range(r)], axis=0)


def _unstack_heads(o, r, tq):
    return jnp.concatenate([o[i * tq:(i + 1) * tq] for i in range(r)], axis=-1)


def _gate_rows(gates, branch, r, tq):
    return jnp.concatenate([gates[:, branch * r + i:branch * r + i + 1] for i in range(r)], axis=0)


def _nsa_cmp_body(q_ref, kc_ref, vc_ref, gates_ref, selmap_t_ref, o_ref, sel_ref, *, tq, r, dh, n_sel):
    q0 = pl.program_id(2) * tq
    qs = (_stack_heads(q_ref[0].astype(F32), r, dh) * (dh ** -0.5)).astype(BF16)
    kc, vc = kc_ref[0, 0].astype(BF16), vc_ref[0, 0].astype(BF16)
    ncp = kc.shape[0]
    s = _dot_nt(qs, kc)
    rows = lax.broadcasted_iota(jnp.int32, (r * tq, ncp), 0)
    cmp_end = lax.broadcasted_iota(jnp.int32, (r * tq, ncp), 1) * NSA_CMP_STRIDE + (NSA_CMP_BLOCK - 1)
    t_pos = q0 + rows % tq
    vis = cmp_end <= t_pos
    s = jnp.where(vis, s, NEG_INF)
    e = jnp.exp(s - jnp.max(s, axis=-1, keepdims=True))
    p = e / jnp.sum(e, axis=-1, keepdims=True) * vis.astype(F32)
    o = _dot(p.astype(BF16), vc)
    o = o * _gate_rows(gates_ref[0, 0], 0, r, tq)
    o_ref[0] = _unstack_heads(o, r, tq)
    p_sum = p[0:tq]
    for i in range(1, r):
        p_sum = p_sum + p[i * tq:(i + 1) * tq]
    hi = p_sum.astype(BF16)
    lo = (p_sum - hi.astype(F32)).astype(BF16)
    selmap_t = selmap_t_ref[...]
    imp = _dot_nt(selmap_t, hi) + _dot_nt(selmap_t, lo)
    ns = imp.shape[0]
    blk = lax.broadcasted_iota(jnp.int32, (ns, tq), 0)
    tq_pos = q0 + lax.broadcasted_iota(jnp.int32, (ns, tq), 1)
    cur = tq_pos // NSA_SEL_BLOCK
    valid = blk * NSA_SEL_BLOCK <= tq_pos
    forced = (blk == 0) | (blk == cur) | (blk == cur - 1)
    score = jnp.where(valid, jnp.where(forced, POS_BIG, imp), NEG_INF)
    rank = jnp.zeros((ns, tq), F32)
    for m in range(ns):
        sm = score[m:m + 1, :]
        beats = (sm > score) | ((sm == score) & (blk > m))
        rank = rank + beats.astype(F32)
    chosen = ((rank < n_sel) & valid).astype(F32)
    sel_ref[0, 0] = (chosen.T - 1.0).astype(sel_ref.dtype)


def _flash_update(s, v_ones, m_ref, acc_ref):
    lanes = m_ref.shape[1]
    assert s.shape[1] % lanes == 0 and acc_ref.shape[1] == 2 * lanes
    m_old = m_ref[...]
    m_new = jnp.maximum(m_old, jnp.max(s, axis=-1, keepdims=True))
    alpha = jnp.exp2(m_old - m_new)
    p = jnp.exp2((s - jnp.concatenate([m_new] * (s.shape[1] // lanes), axis=1)).astype(BF16))
    acc_ref[...] = jnp.concatenate([alpha, alpha], axis=1) * acc_ref[...] + _dot(p, v_ones)
    m_ref[...] = m_new


def _nsa_sel_body(qi_tbl, ki_tbl, diag_tbl, q_ref, k_ref, v_ref, gates_ref, oin_ref, sel_ref, o_ref, qa_ref, m_ref,
                  acc_ref, *, tq, tk, r, dh):
    step = pl.program_id(2)
    qi, ki, diag = qi_tbl[step], ki_tbl[step], diag_tbl[step]
    ns = sel_ref.shape[-1]
    assert ns <= dh

    @pl.when(ki == 0)
    def _():
        m_ref[...] = jnp.full_like(m_ref, NEG_INF)
        acc_ref[...] = jnp.zeros_like(acc_ref)
        qs = _stack_heads(q_ref[0].astype(F32), r, dh) * (dh ** -0.5 * LOG2_E)
        pieces = [qs, jnp.concatenate([sel_ref[0, 0].astype(F32)] * r, axis=0)]
        if ns < dh:
            pieces.append(jnp.zeros((r * tq, dh - ns), F32))
        qa_ref[...] = jnp.concatenate(pieces, axis=1).astype(BF16)

    blk_of_key = (ki * tk + lax.broadcasted_iota(jnp.int32, (tk, dh), 0)) // NSA_SEL_BLOCK
    own_block = jnp.where(blk_of_key == lax.broadcasted_iota(jnp.int32, (tk, dh), 1), POS_BIG, 0.0).astype(BF16)
    ka = jnp.concatenate([k_ref[0].astype(BF16), own_block], axis=1)
    s = _dot_nt(qa_ref[...], ka)
    v_ones = jnp.concatenate([v_ref[0].astype(BF16), jnp.ones((tk, dh), BF16)], axis=1)

    @pl.when(diag == 0)
    def _():
        _flash_update(s, v_ones, m_ref, acc_ref)

    @pl.when(diag == 1)
    def _():
        t_pos = qi * tq + lax.broadcasted_iota(jnp.int32, (r * tq, tk), 0) % tq
        k_pos = ki * tk + lax.broadcasted_iota(jnp.int32, (r * tq, tk), 1)
        _flash_update(jnp.where(k_pos <= t_pos, s, NEG_INF), v_ones, m_ref, acc_ref)
        o = acc_ref[:, :dh] / acc_ref[:, dh:] * _gate_rows(gates_ref[0, 0], 1, r, tq)
        o_ref[0] = (oin_ref[0] + _unstack_heads(o, r, tq)).astype(o_ref.dtype)


def _nsa_win_body(q_ref, *rest, tq, r, dh, nwin, groups):
    k_refs, v_refs = rest[:nwin], rest[nwin:2 * nwin]
    gates_ref, oin_ref, o_ref = rest[2 * nwin:]
    qi = pl.program_id(2)
    wq = r * dh
    t_loc = lax.broadcasted_iota(jnp.int32, (r * tq, tq), 0) % tq
    k_loc = lax.broadcasted_iota(jnp.int32, (r * tq, tq), 1)
    for gg in range(groups):
        qcols, kcols = slice(gg * wq, (gg + 1) * wq), slice(gg * dh, (gg + 1) * dh)
        qs = (_stack_heads(q_ref[0, :, qcols].astype(F32), r, dh) * (dh ** -0.5 * LOG2_E)).astype(BF16)
        tiles = []
        for j in range(nwin):
            s = _dot_nt(qs, k_refs[j][0, :, kcols].astype(BF16))
            if j == nwin - 1:
                s = jnp.where(k_loc <= t_loc, s, NEG_INF)
            else:
                if j == 0:
                    s = jnp.where(k_loc > t_loc, s, NEG_INF)
                s = jnp.where(qi - (nwin - 1) + j >= 0, s, NEG_INF)
            tiles.append(s)
        s = jnp.concatenate(tiles, axis=1)
        p = jnp.exp2((s - jnp.max(s, axis=-1, keepdims=True)).astype(BF16))
        v = jnp.concatenate([v_refs[j][0, :, kcols] for j in range(nwin)], axis=0).astype(BF16)
        on = _dot(p, jnp.concatenate([v, jnp.ones_like(v)], axis=1))
        o = on[:, :dh] / on[:, dh:] * _gate_rows(gates_ref[0, gg], 2, r, tq)
        o_ref[0, :, qcols] = (oin_ref[0, :, qcols] + _unstack_heads(o, r, tq)).astype(o_ref.dtype)


def nsa_mix(h, w_in, pe_k, pe_v, ck_w1, ck_w2, cv_w1, cv_w2, w_out, tq=256):
    b, t, d = h.shape
    nh, g, dh = NSA_HEADS, NSA_KV_GROUPS, NSA_HEAD_DIM
    r = nh // g
    hd, kd = nh * dh, g * dh
    n_main = hd + 6 * kd
    h2 = h.reshape(b * t, d)
    proj = mm(h2, w_in[:, :n_main], out_dtype=BF16, name="nsa_in").reshape(b, t, n_main)
    n_gate = w_in.shape[1] - n_main
    w_gate = jnp.pad(w_in[:, n_main:], ((0, 0), (0, 128 - n_gate)))
    gates = mm(h2, w_gate, epi=_sigmoid_epi, name="nsa_gates")[:, :n_gate]
    gates = jnp.transpose(gates.reshape(b, t, 3, g, r), (0, 3, 1, 2, 4)).reshape(b, g, t, 3 * r)
    k_cmp = nsa_compress(proj, hd, pe_k, ck_w1, ck_w2)
    v_cmp = nsa_compress(proj, hd + kd, pe_v, cv_w1, cv_w2)
    ncp = k_cmp.shape[2]
    ns = t // NSA_SEL_BLOCK
    n_sel = min(NSA_SEL_TOP, ns)
    cs = jnp.arange(ncp)[None, :] * NSA_CMP_STRIDE
    ss = jnp.arange(ns)[:, None] * NSA_SEL_BLOCK
    sel_map_t = (jnp.maximum(jnp.minimum(cs + NSA_CMP_BLOCK, ss + NSA_SEL_BLOCK) - jnp.maximum(cs, ss), 0)
                 .astype(F32) / NSA_CMP_BLOCK).astype(BF16)
    tq = _pick(t, tq)
    assert NSA_WINDOW % tq == 0
    nq = t // tq
    wq = r * dh
    col = lambda base: base // dh
    q_spec3 = pl.BlockSpec((1, tq, wq), lambda bi, gi, qi: (bi, qi, gi))
    gate_spec3 = pl.BlockSpec((1, 1, tq, 3 * r), lambda bi, gi, qi: (bi, gi, qi, 0))
    o_cmp, sel = pl.pallas_call(
        functools.partial(_nsa_cmp_body, tq=tq, r=r, dh=dh, n_sel=n_sel),
        grid=(b, g, nq),
        in_specs=[
            q_spec3,
            pl.BlockSpec((1, 1, ncp, dh), lambda bi, gi, qi: (bi, gi, 0, 0)),
            pl.BlockSpec((1, 1, ncp, dh), lambda bi, gi, qi: (bi, gi, 0, 0)),
            gate_spec3,
            pl.BlockSpec((ns, ncp), lambda bi, gi, qi: (0, 0)),
        ],
        out_specs=[q_spec3, pl.BlockSpec((1, 1, tq, ns), lambda bi, gi, qi: (bi, gi, qi, 0))],
        out_shape=[jax.ShapeDtypeStruct((b, t, hd), F32), jax.ShapeDtypeStruct((b, g, t, ns), BF16)],
        compiler_params=_cparams("parallel", "parallel", "parallel"),
        name="nsa_cmp",
    )(proj, k_cmp, v_cmp, gates, sel_map_t)

    tk = _pick(t, NSA_SEL_KEY_TILE)
    tqs = _pick(t, NSA_SEL_QUERY_TILE)
    assert tk % tqs == 0
    pairs = [(qi, ki) for qi in range(t // tqs) for ki in range((qi * tqs + tqs - 1) // tk + 1)]
    qi_tbl = jnp.asarray([pq for pq, _ in pairs], jnp.int32)
    ki_tbl = jnp.asarray([pk for _, pk in pairs], jnp.int32)
    diag_tbl = jnp.asarray([int(pk == (pq * tqs + tqs - 1) // tk) for pq, pk in pairs], jnp.int32)
    q_spec_p = pl.BlockSpec((1, tqs, wq), lambda bi, gi, p, qt, kt, dt: (bi, qt[p], gi))

    def kv_spec_p(base):
        return pl.BlockSpec((1, tk, dh), lambda bi, gi, p, qt, kt, dt: (bi, kt[p], col(base) + gi))

    o_sel = pl.pallas_call(
        functools.partial(_nsa_sel_body, tq=tqs, tk=tk, r=r, dh=dh),
        grid_spec=pltpu.PrefetchScalarGridSpec(
            num_scalar_prefetch=3,
            grid=(b, g, len(pairs)),
            in_specs=[q_spec_p, kv_spec_p(hd + 2 * kd), kv_spec_p(hd + 3 * kd),
                      pl.BlockSpec((1, 1, tqs, 3 * r), lambda bi, gi, p, qt, kt, dt: (bi, gi, qt[p], 0)),
                      q_spec_p,
                      pl.BlockSpec((1, 1, tqs, ns), lambda bi, gi, p, qt, kt, dt: (bi, gi, qt[p], 0))],
            out_specs=q_spec_p,
            scratch_shapes=[pltpu.VMEM((r * tqs, 2 * dh), BF16), pltpu.VMEM((r * tqs, dh), F32),
                            pltpu.VMEM((r * tqs, 2 * dh), F32)],
        ),
        out_shape=jax.ShapeDtypeStruct((b, t, hd), F32),
        compiler_params=_cparams("parallel", "parallel", "arbitrary"),
        name="nsa_sel",
    )(qi_tbl, ki_tbl, diag_tbl, proj, proj, proj, gates, o_cmp, sel)

    nwin = NSA_WINDOW // tq + 1
    gw = NSA_WIN_GROUPS_PER_STEP
    assert g % gw == 0 and kd % (gw * dh) == 0
    q_spec_w = pl.BlockSpec((1, tq, gw * wq), lambda bi, gi, qi: (bi, qi, gi))

    def kv_spec_w(base, j):
        return pl.BlockSpec((1, tq, gw * dh),
                            lambda bi, gi, qi: (bi, jnp.maximum(qi - (nwin - 1) + j, 0), base // (gw * dh) + gi))

    o_all = pl.pallas_call(
        functools.partial(_nsa_win_body, tq=tq, r=r, dh=dh, nwin=nwin, groups=gw),
        grid=(b, g // gw, nq),
        in_specs=([q_spec_w] + [kv_spec_w(hd + 4 * kd, j) for j in range(nwin)]
                  + [kv_spec_w(hd + 5 * kd, j) for j in range(nwin)]
                  + [pl.BlockSpec((1, gw, tq, 3 * r), lambda bi, gi, qi: (bi, gi, qi, 0)), q_spec_w]),
        out_specs=q_spec_w,
        out_shape=jax.ShapeDtypeStruct((b, t, hd), BF16),
        compiler_params=_cparams("parallel", "parallel", "parallel"),
        name="nsa_win",
    )(proj, *([proj] * (2 * nwin)), gates, o_sel)
    return o_all.reshape(b * t, hd), [(w_out, 0, 0)], None


def _silu(a):
    return a * jax.nn.sigmoid(a)


def _ada_epi(parts, extras):
    return parts[0] + extras[0]


def ada_modulation(c, ada_w, ada_b):
    depth, d, n6 = ada_w.shape
    b = c.shape[0]
    rows = ((b + 7) // 8) * 8
    cond = jnp.pad(c, ((0, rows - b), (0, 0)))
    w2d = ada_w.reshape(depth * d, n6)
    mods = []
    for i in range(depth):
        mods.append(mm(cond, [(w2d, 0, i)], a_act=_silu, epi=_ada_epi, extras=[_row_extra(ada_b[i], _pick(n6, 1024))],
                       precision=HIGHEST, name="ada_mod")[:b])
    return jnp.stack(mods)


def kernel(x, c, ada_w, ada_b, norm_g, mlp_w1, mlp_w2, rwkv_mu, rwkv_w_rkv, rwkv_w0, rwkv_w_la, rwkv_w_lb, rwkv_a0, rwkv_a_la, rwkv_a_lb, rwkv_g_la, rwkv_g_lb, rwkv_k_k, rwkv_k_a, rwkv_r_k, rwkv_ln_g, rwkv_ln_b, rwkv_w_out, ret_w_in, ret_gn_g, ret_gn_b, ret_w_out, conv_pw1_w, conv_pw1_b, conv_dw_w, conv_dw_b, conv_ln_g, conv_ln_b, conv_pw2_w, conv_pw2_b, nsa_w_in, nsa_pe_k, nsa_pe_v, nsa_ck_w1, nsa_ck_w2, nsa_cv_w1, nsa_cv_w2, nsa_w_out):
    b, t, d = x.shape
    depth = ada_w.shape[0]
    mod = ada_modulation(c, ada_w, ada_b).reshape(depth, b, 6, d)
    h = None
    for i in range(depth):
        sh_t, sc_t, gt_t, sh_c, sc_c, gt_c = (mod[i, :, j] for j in range(6))
        kind = i % 4
        if kind == 0:
            act, ws, bias = rwkv7_time_mix(x, norm_g[i, 0], sc_t, sh_t, rwkv_mu, rwkv_w_rkv, rwkv_w0, rwkv_w_la,
                                           rwkv_w_lb, rwkv_a0, rwkv_a_la, rwkv_a_lb, rwkv_g_la, rwkv_g_lb, rwkv_k_k,
                                           rwkv_k_a, rwkv_r_k.reshape(-1), rwkv_ln_g, rwkv_ln_b, rwkv_w_out)
        else:
            if h is None:
                (h,) = resid_norm(x, pre=(norm_g[i, 0], sc_t, sh_t))
            if kind == 1:
                act, ws, bias = retention_mix(h, ret_w_in, ret_gn_g, ret_gn_b, ret_w_out)
            elif kind == 2:
                act, ws, bias = conformer_conv_mix(h, conv_pw1_w, conv_pw1_b, conv_dw_w, conv_dw_b, conv_ln_g,
                                                   conv_ln_b, conv_pw2_w, conv_pw2_b)
            else:
                act, ws, bias = nsa_mix(h, nsa_w_in, nsa_pe_k, nsa_pe_v, nsa_ck_w1, nsa_ck_w2, nsa_cv_w1, nsa_cv_w2,
                                        nsa_w_out)
        y = project(act, ws, bias, "mixer_out").reshape(b, t, d)
        act, x = mlp_up_resid(x, y, (norm_g[i, 1], gt_t), (norm_g[i, 2], sc_c, sh_c), mlp_w1, i)
        nxt = i + 1
        if nxt < depth and nxt % 4 != 0:
            x, h = mlp_down_resid(act, mlp_w2, i, x, (norm_g[i, 3], gt_c),
                                  (norm_g[nxt, 0], mod[nxt, :, 1], mod[nxt, :, 0]))
        else:
            (x,) = mlp_down_resid(act, mlp_w2, i, x, (norm_g[i, 3], gt_c))
            h = None
    return x
```

```python
import functools
import math

import jax
import jax.numpy as jnp
from jax import lax
from jax.experimental import pallas as pl
from jax.experimental.pallas import tpu as pltpu

F32 = jnp.float32
BF16 = jnp.bfloat16
HIGHEST = lax.Precision.HIGHEST

NORM_EPS = 1e-6
NEG_INF = -1e30
POS_BIG = 1e30
LOG2_E = math.log2(math.e)

RWKV_HEAD_DIM = 64
RWKV_GN_EPS = 64e-5
RWKV_CHUNK = 64
RWKV_HEADS_PER_STEP = 32

RET_HEADS = 8
RET_HEADS_PER_STEP = 4
RET_CHUNK = 256
RET_GN_EPS = 1e-5
ROPE_BASE = 10000.0

CONV_WIDTH = 31
CONV_HALO = 32
CONV_LN_EPS = 1e-5

NSA_HEADS = 16
NSA_KV_GROUPS = 4
NSA_HEAD_DIM = 128
NSA_CMP_BLOCK = 32
NSA_CMP_STRIDE = 16
NSA_SEL_BLOCK = 64
NSA_SEL_TOP = 16
NSA_WINDOW = 512
NSA_SEL_KEY_TILE = 512
NSA_SEL_QUERY_TILE = 512
NSA_WIN_GROUPS_PER_STEP = 4
NSA_CMP_GROUPS_PER_STEP = 4

VMEM_LIMIT_BYTES = 56 * 1024 * 1024


def _cparams(*sem):
    return pltpu.CompilerParams(dimension_semantics=sem, vmem_limit_bytes=VMEM_LIMIT_BYTES)


def _dot(a, b, precision=None):
    return jnp.dot(a, b, preferred_element_type=F32, precision=precision)


def _dot_nt(a, b, precision=None):
    return lax.dot_general(a, b, (((1,), (1,)), ((), ())), preferred_element_type=F32, precision=precision)


def _pick(n, pref):
    if n <= pref:
        return n
    t = pref
    while n % t:
        t //= 2
    return t


def _mm_body(*refs, n_w, n_ex, n_out, nk, epi, precision, has_a_add, a_act):
    a_ref = refs[0]
    pos = 1
    a_add_ref = None
    if has_a_add:
        a_add_ref = refs[pos]
        pos += 1
    w_refs = refs[pos:pos + n_w]
    pos += n_w
    ex_refs = refs[pos:pos + n_ex]
    pos += n_ex
    o_refs = refs[pos:pos + n_out]
    acc_refs = refs[pos + n_out:]

    def finish(parts):
        res = epi(parts, [e[...] for e in ex_refs])
        if n_out == 1 and not isinstance(res, (tuple, list)):
            res = (res,)
        for o_ref, val in zip(o_refs, res, strict=True):
            o_ref[...] = val.astype(o_ref.dtype)

    a = a_ref[...]
    if a_add_ref is not None:
        a = a.astype(F32) + a_add_ref[...]
    if a_act is not None:
        a = a_act(a)
    if precision is None:
        a = a.astype(BF16)
    parts = []
    for w_ref in w_refs:
        w = w_ref[...]
        if precision is None:
            w = w.astype(BF16)
        parts.append(_dot(a, w, precision))

    if nk == 1:
        finish(parts)
    else:
        k = pl.program_id(2)

        @pl.when(k == 0)
        def _():
            for acc, p in zip(acc_refs, parts):
                acc[...] = p

        @pl.when(k > 0)
        def _():
            for acc, p in zip(acc_refs, parts):
                acc[...] += p

        @pl.when(k == nk - 1)
        def _():
            finish([acc[...] for acc in acc_refs])


def _first(parts, extras):
    return parts[0]


def mm(a, ws, *, n_out=None, extras=(), epi=_first, out_dtype=F32, tm=2048, tn=1024, tk=2048, precision=None,
       a_add=None, a_act=None, a_part=(0, 1), cast_in_kernel=False, name="mm"):
    m, kdim = a.shape
    a_s, a_parts = a_part
    assert m % a_parts == 0
    m //= a_parts
    if not isinstance(ws, (list, tuple)):
        ws = [(ws, 0, 0)]
    ws = [tuple(w) + (0,) * (3 - len(w)) for w in ws]
    if n_out is None:
        n_out = ws[0][0].shape[1]
    tm = _pick(m, tm)
    tn = _pick(n_out, tn)
    tk = _pick(kdim, tk)
    nk = kdim // tk
    assert m % tm == 0 and n_out % tn == 0 and kdim % tk == 0
    in_specs = [pl.BlockSpec((tm, tk), lambda i, j, k, o=a_s * (m // tm): (i + o, k))]
    args = [a]
    if a_add is not None:
        in_specs.append(pl.BlockSpec((1, tk), lambda i, j, k: (0, k)))
        args.append(a_add)
    for w, off, koff in ws:
        in_specs.append(pl.BlockSpec((tk, tn), lambda i, j, k, off=off, ko=koff * nk: (k + ko, j + off)))
        args.append(w.astype(BF16) if precision is None and not cast_in_kernel else w)
    for arr, bshape, imap in extras:
        in_specs.append(pl.BlockSpec(bshape, imap))
        args.append(arr)
    scratch = [pltpu.VMEM((tm, tn), F32) for _ in ws] if nk > 1 else []
    multi = isinstance(out_dtype, (tuple, list))
    out_dtypes = tuple(out_dtype) if multi else (out_dtype,)
    body = functools.partial(_mm_body, n_w=len(ws), n_ex=len(extras), n_out=len(out_dtypes), nk=nk, epi=epi,
                             precision=precision, has_a_add=a_add is not None, a_act=a_act)
    outs = pl.pallas_call(
        body,
        grid=(m // tm, n_out // tn, nk),
        in_specs=in_specs,
        out_specs=[pl.BlockSpec((tm, tn), lambda i, j, k: (i, j)) for _ in out_dtypes],
        out_shape=[jax.ShapeDtypeStruct((m, n_out), dt) for dt in out_dtypes],
        scratch_shapes=scratch,
        compiler_params=_cparams("parallel", "parallel", "arbitrary"),
        name=name,
    )(*args)
    return tuple(outs) if multi else outs[0]


def _row_extra(vec, tn):
    return (vec.reshape(1, -1), (1, tn), lambda i, j, k: (0, j))


def _rms(x, g):
    return x * lax.rsqrt(jnp.mean(x * x, axis=-1, keepdims=True) + NORM_EPS) * g


def _resid_norm_body(*refs, has_y, has_h):
    pos = 0
    x_ref = refs[pos]; pos += 1
    if has_y:
        y_ref, gpost_ref, gate_ref = refs[pos:pos + 3]; pos += 3
    if has_h:
        gpre_ref, scale_ref, shift_ref = refs[pos:pos + 3]; pos += 3
    outs = refs[pos:]
    x = x_ref[0]
    o = 0
    if has_y:
        y = y_ref[0].astype(F32)
        x = x + (1.0 + gate_ref[0]) * _rms(y, gpost_ref[...])
        outs[o][0] = x
        o += 1
    if has_h:
        h = _rms(x, gpre_ref[...]) * (1.0 + scale_ref[0]) + shift_ref[0]
        outs[o][0] = h.astype(outs[o].dtype)


def resid_norm(x, y=None, post=None, pre=None, tt=512):
    b, t, d = x.shape
    tt = _pick(t, tt)
    row = pl.BlockSpec((1, tt, d), lambda bi, ti: (bi, ti, 0))
    vec = pl.BlockSpec((1, d), lambda bi, ti: (0, 0))
    bvec = pl.BlockSpec((1, 1, d), lambda bi, ti: (bi, 0, 0))
    args, in_specs, out_shapes, out_specs = [x], [row], [], []
    if y is not None:
        args += [y, post[0].reshape(1, d), post[1].reshape(b, 1, d)]
        in_specs += [row, vec, bvec]
        out_shapes.append(jax.ShapeDtypeStruct((b, t, d), F32))
        out_specs.append(row)
    if pre is not None:
        args += [pre[0].reshape(1, d), pre[1].reshape(b, 1, d), pre[2].reshape(b, 1, d)]
        in_specs += [vec, bvec, bvec]
        out_shapes.append(jax.ShapeDtypeStruct((b, t, d), BF16))
        out_specs.append(row)
    res = pl.pallas_call(
        functools.partial(_resid_norm_body, has_y=y is not None, has_h=pre is not None),
        grid=(b, t // tt),
        in_specs=in_specs,
        out_specs=out_specs,
        out_shape=out_shapes,
        compiler_params=_cparams("parallel", "parallel"),
        name="resid_norm",
    )(*args)
    return tuple(res)


def _add_bias(parts, extras):
    return parts[0] + extras[0]


def project(act, ws, bias, name):
    if bias is None:
        return mm(act, ws, out_dtype=BF16, name=name)
    tn = _pick(ws[0][0].shape[1], 1024)
    return mm(act, ws, epi=_add_bias, extras=[_row_extra(bias, tn)], tn=tn, out_dtype=BF16, name=name)


def _mlp_up_body(x_ref, y_ref, gpost_ref, gate_ref, gpre_ref, scale_ref, shift_ref, w_ref, a_ref, xn_ref, h0_ref,
                 h1_ref, *, n_tiles, rows):
    i, j = pl.program_id(0), pl.program_id(1)
    bufs = (h0_ref, h1_ref)

    tm = a_ref.shape[0]
    pieces = 8
    nr, mr = rows // pieces, tm // pieces

    def norm_rows(dst_ref, s):
        sl = pl.ds(s * nr, nr)
        xn = x_ref[sl, :] + (1.0 + gate_ref[0]) * _rms(y_ref[sl, :].astype(F32), gpost_ref[...])
        xn_ref[sl, :] = xn
        h = _rms(xn, gpre_ref[...]) * (1.0 + scale_ref[0]) + shift_ref[0]
        dst_ref[pl.ds(pl.multiple_of(j * rows + s * nr, nr), nr), :] = h.astype(dst_ref.dtype)

    def up_project(src_ref, w, s):
        sl = pl.ds(s * mr, mr)
        r = jnp.maximum(_dot(src_ref[sl, :], w), 0.0)
        a_ref[sl, :] = (r * r).astype(a_ref.dtype)

    @pl.when(i == 0)
    def _():
        for s in range(pieces):
            norm_rows(bufs[0], s)

    for parity in (0, 1):
        @pl.when((i > 0) & (i < n_tiles) & (i % 2 == parity))
        def _():
            w = w_ref[...].astype(BF16)
            for s in range(pieces):
                up_project(bufs[1 - parity], w, s)
                norm_rows(bufs[parity], s)

    @pl.when(i == n_tiles)
    def _():
        w = w_ref[...].astype(BF16)
        for s in range(pieces):
            up_project(bufs[(n_tiles - 1) % 2], w, s)


def mlp_up_resid(x, y, post, pre, w1, layer, tm=2048, tn=512):
    b, t, d = x.shape
    nl, _, f = w1.shape
    m = b * t
    tm = _pick(t, tm)
    tn = _pick(f, tn)
    n_tiles, nj = m // tm, f // tn
    assert tm % nj == 0
    rows = tm // nj
    assert rows % 16 == 0
    tiles_per_batch = t // tm
    slice_idx = lambda i, j: jnp.where(i < n_tiles, i * nj + j, n_tiles * nj - 1)
    rows_spec = pl.BlockSpec((rows, d), lambda i, j: (slice_idx(i, j), 0))
    vec = pl.BlockSpec((1, d), lambda i, j: (0, 0))
    bvec = pl.BlockSpec((1, 1, d), lambda i, j: (jnp.minimum(i, n_tiles - 1) // tiles_per_batch, 0, 0))
    a, x_new = pl.pallas_call(
        functools.partial(_mlp_up_body, n_tiles=n_tiles, rows=rows),
        grid=(n_tiles + 1, nj),
        in_specs=[rows_spec, rows_spec, vec, bvec, vec, bvec, bvec,
                  pl.BlockSpec((d, tn), lambda i, j: (layer, j))],
        out_specs=[pl.BlockSpec((tm, tn), lambda i, j: (jnp.maximum(i - 1, 0), jnp.where(i > 0, j, 0))),
                   rows_spec],
        out_shape=[jax.ShapeDtypeStruct((m, f), BF16), jax.ShapeDtypeStruct((m, d), F32)],
        scratch_shapes=[pltpu.VMEM((tm, d), BF16), pltpu.VMEM((tm, d), BF16)],
        compiler_params=_cparams("arbitrary", "arbitrary"),
        name="mlp_up_resid",
    )(x.reshape(m, d), y.reshape(m, d), post[0].reshape(1, d), post[1].reshape(b, 1, d), pre[0].reshape(1, d),
      pre[1].reshape(b, 1, d), pre[2].reshape(b, 1, d), w1.reshape(nl * d, f))
    return a, x_new.reshape(b, t, d)


def _mlp_down_body(*refs, n_tiles, rows, has_pre):
    a_ref, w_ref, x_ref, gpost_ref, gate_ref = refs[:5]
    if has_pre:
        gpre_ref, scale_ref, shift_ref, xn_ref, h_ref, acc0_ref, acc1_ref = refs[5:]
    else:
        xn_ref, acc0_ref, acc1_ref = refs[5:]
    i, k = pl.program_id(0), pl.program_id(1)
    accs = (acc0_ref, acc1_ref)
    tm = a_ref.shape[0]
    pieces = 4
    mr = tm // pieces
    nr = rows // pieces

    @pl.when((i == 0) & (k == 0))
    def _():
        acc0_ref[...] = jnp.zeros_like(acc0_ref)
        acc1_ref[...] = jnp.zeros_like(acc1_ref)

    def accumulate(dst_ref, w, s):
        sl = pl.ds(s * mr, mr)
        part = _dot(a_ref[sl, :], w)
        dst_ref[sl, :] = jnp.where(k == 0, part, dst_ref[sl, :] + part)

    def finish_rows(src_ref, s):
        sl = pl.ds(s * nr, nr)
        y = src_ref[pl.ds(pl.multiple_of(k * rows + s * nr, nr), nr), :]
        xn = x_ref[sl, :] + (1.0 + gate_ref[0]) * _rms(y, gpost_ref[...])
        xn_ref[sl, :] = xn
        if has_pre:
            h_ref[sl, :] = (_rms(xn, gpre_ref[...]) * (1.0 + scale_ref[0]) + shift_ref[0]).astype(h_ref.dtype)

    @pl.when(i == 0)
    def _():
        w = w_ref[...]
        for s in range(pieces):
            accumulate(accs[0], w, s)

    for parity in (0, 1):
        @pl.when((i > 0) & (i < n_tiles) & (i % 2 == parity))
        def _():
            w = w_ref[...]
            for s in range(pieces):
                accumulate(accs[parity], w, s)
                finish_rows(accs[1 - parity], s)

    @pl.when(i == n_tiles)
    def _():
        for s in range(pieces):
            finish_rows(accs[(n_tiles - 1) % 2], s)


def mlp_down_resid(a, w2, layer, x, post, pre=None, tm=1024, tk=1024):
    b, t, d = x.shape
    nl, f, _ = w2.shape
    m = b * t
    tm = _pick(t, tm)
    tk = _pick(f, tk)
    n_tiles, nk = m // tm, f // tk
    assert tm % nk == 0
    rows = tm // nk
    assert rows % (8 * 16) == 0
    tiles_per_batch = t // tm
    slice_idx = lambda i, k: jnp.where(i > 0, (i - 1) * nk + k, 0)
    rows_spec = pl.BlockSpec((rows, d), lambda i, k: (slice_idx(i, k), 0))
    vec = pl.BlockSpec((1, d), lambda i, k: (0, 0))
    bvec = pl.BlockSpec((1, 1, d), lambda i, k: (jnp.maximum(i - 1, 0) // tiles_per_batch, 0, 0))
    in_specs = [pl.BlockSpec((tm, tk), lambda i, k: (jnp.minimum(i, n_tiles - 1), jnp.where(i < n_tiles, k, nk - 1))),
                pl.BlockSpec((tk, d), lambda i, k: (layer * nk + jnp.where(i < n_tiles, k, nk - 1), 0)),
                rows_spec, vec, bvec]
    args = [a, w2.reshape(nl * f, d).astype(BF16), x.reshape(m, d), post[0].reshape(1, d), post[1].reshape(b, 1, d)]
    out_specs, out_shapes = [rows_spec], [jax.ShapeDtypeStruct((m, d), F32)]
    if pre is not None:
        in_specs += [vec, bvec, bvec]
        args += [pre[0].reshape(1, d), pre[1].reshape(b, 1, d), pre[2].reshape(b, 1, d)]
        out_specs.append(rows_spec)
        out_shapes.append(jax.ShapeDtypeStruct((m, d), BF16))
    outs = pl.pallas_call(
        functools.partial(_mlp_down_body, n_tiles=n_tiles, rows=rows, has_pre=pre is not None),
        grid=(n_tiles + 1, nk),
        in_specs=in_specs,
        out_specs=out_specs,
        out_shape=out_shapes,
        scratch_shapes=[pltpu.VMEM((tm, d), F32), pltpu.VMEM((tm, d), F32)],
        compiler_params=_cparams("arbitrary", "arbitrary"),
        name="mlp_down_resid",
    )(*args)
    return tuple(o.reshape(b, t, d) for o in outs)


def _glu(parts, extras):
    return (parts[0] + extras[0]) * jax.nn.sigmoid(parts[1] + extras[1])


def _conv_ln_body(cur_ref, prev_ref, w_ref, b_ref, g_ref, beta_ref, o_ref, buf_ref, acc_ref, *, tt, d):
    ti = pl.program_id(1)
    halo = prev_ref[0].astype(F32)
    buf_ref[0:CONV_HALO, :] = jnp.where(ti == 0, jnp.zeros_like(halo), halo)
    buf_ref[CONV_HALO:, :] = cur_ref[0].astype(F32)
    lane_chunk = min(d, 256)
    row_chunk = min(tt, 64)
    base = CONV_HALO - (CONV_WIDTH - 1)
    sub = 8
    for c0 in range(0, d, lane_chunk):
        for r0 in range(0, tt, row_chunk):
            acc = None
            for s in range(sub):
                taps = [j for j in range(CONV_WIDTH) if (base + j) % sub == s]
                if not taps:
                    continue
                rows = row_chunk + (sub if s else 0)
                part = jnp.zeros((rows, lane_chunk), F32)
                for j in taps:
                    off = r0 + base + j - s
                    part = part + buf_ref[off:off + rows, c0:c0 + lane_chunk] * w_ref[j:j + 1, c0:c0 + lane_chunk]
                part = part[s:s + row_chunk]
                acc = part if acc is None else acc + part
            acc_ref[r0:r0 + row_chunk, c0:c0 + lane_chunk] = acc
    y = acc_ref[...] + b_ref[...]
    mu = jnp.mean(y, axis=-1, keepdims=True)
    yc = y - mu
    var = jnp.mean(yc * yc, axis=-1, keepdims=True)
    z = yc * lax.rsqrt(var + CONV_LN_EPS) * g_ref[...] + beta_ref[...]
    o_ref[0] = (z * jax.nn.sigmoid(z)).astype(o_ref.dtype)


def conv_ln_silu(u, dw_w, dw_b, ln_g, ln_b, tt=128):
    b, t, d = u.shape
    tt = _pick(t, tt)
    hb = tt // CONV_HALO
    vec = pl.BlockSpec((1, d), lambda bi, ti: (0, 0))
    return pl.pallas_call(
        functools.partial(_conv_ln_body, tt=tt, d=d),
        grid=(b, t // tt),
        in_specs=[
            pl.BlockSpec((1, tt, d), lambda bi, ti: (bi, ti, 0)),
            pl.BlockSpec((1, CONV_HALO, d), lambda bi, ti: (bi, jnp.maximum(ti * hb - 1, 0), 0)),
            pl.BlockSpec((CONV_WIDTH, d), lambda bi, ti: (0, 0)),
            vec, vec, vec,
        ],
        out_specs=pl.BlockSpec((1, tt, d), lambda bi, ti: (bi, ti, 0)),
        out_shape=jax.ShapeDtypeStruct((b, t, d), BF16),
        scratch_shapes=[pltpu.VMEM((tt + CONV_HALO, d), F32), pltpu.VMEM((tt, d), F32)],
        compiler_params=_cparams("parallel", "parallel"),
        name="conv_ln_silu",
    )(u, u, dw_w, dw_b.reshape(1, d), ln_g.reshape(1, d), ln_b.reshape(1, d))


def conformer_conv_mix(h, pw1_w, pw1_b, dw_w, dw_b, ln_g, ln_b, pw2_w, pw2_b):
    b, t, d = h.shape
    tn = _pick(d, 1024)
    b1 = pw1_b.reshape(1, -1)
    u = mm(h.reshape(b * t, d), [(pw1_w, 0), (pw1_w, d // tn)], n_out=d, tm=1024, tn=tn, epi=_glu,
           extras=[(b1, (1, tn), lambda i, j, k: (0, j)),
                   (b1, (1, tn), lambda i, j, k, o=d // tn: (0, j + o))], out_dtype=BF16, name="conf_pw1_glu")
    z = conv_ln_silu(u.reshape(b, t, d), dw_w, dw_b, ln_g, ln_b)
    return z.reshape(b * t, d), [(pw2_w, 0, 0)], pw2_b


def _rope(x, cos, sin):
    half = x.shape[-1] // 2
    x1, x2 = x[:, :half], x[:, half:]
    return jnp.concatenate([x1 * cos - x2 * sin, x2 * cos + x1 * sin], axis=-1)


def _retention_body(q_ref, k_ref, v_ref, gate_ref, cos_ref, sin_ref, inner_ref, qdec_ref, kdec_ref, cdec_ref,
                    gng_ref, gnb_ref, o_ref, state_ref, *, dk, heads):
    @pl.when(pl.program_id(2) == 0)
    def _():
        state_ref[...] = jnp.zeros_like(state_ref)

    cos, sin = cos_ref[...], sin_ref[...]
    dv = v_ref.shape[2] // heads
    for hh in range(heads):
        ks, vs = slice(hh * dk, (hh + 1) * dk), slice(hh * dv, (hh + 1) * dv)
        q = _rope(q_ref[0, :, ks].astype(F32), cos, sin)
        k = _rope(k_ref[0, :, ks].astype(F32), cos, sin) * (dk ** -0.5)
        v = v_ref[0, :, vs].astype(BF16)
        qb = q.astype(BF16)
        s = _dot_nt(qb, k.astype(BF16)) * inner_ref[hh]
        state = state_ref[hh]
        o = _dot(s.astype(BF16), v) + _dot(qb, state.astype(BF16)) * qdec_ref[hh]
        kd_t = (k * kdec_ref[hh]).T.astype(BF16)
        state_ref[hh] = state * cdec_ref[hh] + _dot(kd_t, v)
        mu = jnp.mean(o, axis=-1, keepdims=True)
        oc = o - mu
        var = jnp.mean(oc * oc, axis=-1, keepdims=True)
        on = oc * lax.rsqrt(var + RET_GN_EPS) * gng_ref[:, vs] + gnb_ref[:, vs]
        gate = gate_ref[0, :, vs].astype(F32)
        o_ref[0, :, vs] = (gate * jax.nn.sigmoid(gate) * on).astype(o_ref.dtype)


def retention_mix(h, w_in, gn_g, gn_b, w_out):
    b, t, d = h.shape
    nh, c = RET_HEADS, _pick(t, RET_CHUNK)
    dk = d // nh
    dv = 2 * dk
    proj = mm(h.reshape(b * t, d), w_in, out_dtype=BF16, tm=2048, tn=512, cast_in_kernel=True,
              name="ret_in").reshape(b, t, 6 * d)
    pos = jnp.arange(t, dtype=F32)
    inv_freq = ROPE_BASE ** (-jnp.arange(0, dk, 2, dtype=F32) / dk)
    ang = pos[:, None] * inv_freq[None, :]
    cos, sin = jnp.cos(ang), jnp.sin(ang)
    log_gamma = jnp.log(1.0 - 2.0 ** (-5.0 - jnp.arange(nh, dtype=F32)))
    idx = jnp.arange(c, dtype=F32)
    diff = idx[:, None] - idx[None, :]
    inner = jnp.where(diff >= 0, jnp.exp(jnp.maximum(diff, 0.0)[None] * log_gamma[:, None, None]), 0.0)
    q_dec = jnp.exp((idx + 1.0)[None] * log_gamma[:, None])[:, :, None]
    k_dec = jnp.exp((c - 1.0 - idx)[None] * log_gamma[:, None])[:, :, None]
    c_dec = jnp.exp(c * log_gamma)[:, None, None]
    hp = RET_HEADS_PER_STEP
    assert nh % hp == 0
    ng = nh // hp
    o = pl.pallas_call(
        functools.partial(_retention_body, dk=dk, heads=hp),
        grid=(b, ng, t // c),
        in_specs=[
            pl.BlockSpec((1, c, hp * dk), lambda bi, hi, ci: (bi, ci, hi)),
            pl.BlockSpec((1, c, hp * dk), lambda bi, hi, ci: (bi, ci, ng + hi)),
            pl.BlockSpec((1, c, hp * dv), lambda bi, hi, ci: (bi, ci, ng + hi)),
            pl.BlockSpec((1, c, hp * dv), lambda bi, hi, ci: (bi, ci, 2 * ng + hi)),
            pl.BlockSpec((c, dk // 2), lambda bi, hi, ci: (ci, 0)),
            pl.BlockSpec((c, dk // 2), lambda bi, hi, ci: (ci, 0)),
            pl.BlockSpec((hp, c, c), lambda bi, hi, ci: (hi, 0, 0)),
            pl.BlockSpec((hp, c, 1), lambda bi, hi, ci: (hi, 0, 0)),
            pl.BlockSpec((hp, c, 1), lambda bi, hi, ci: (hi, 0, 0)),
            pl.BlockSpec((hp, 1, 1), lambda bi, hi, ci: (hi, 0, 0)),
            pl.BlockSpec((1, hp * dv), lambda bi, hi, ci: (0, hi)),
            pl.BlockSpec((1, hp * dv), lambda bi, hi, ci: (0, hi)),
        ],
        out_specs=pl.BlockSpec((1, c, hp * dv), lambda bi, hi, ci: (bi, ci, hi)),
        out_shape=jax.ShapeDtypeStruct((b, t, nh * dv), BF16),
        scratch_shapes=[pltpu.VMEM((hp, dk, dv), F32)],
        compiler_params=_cparams("parallel", "parallel", "arbitrary"),
        name="retention",
    )(proj, proj, proj, proj, cos, sin, inner, q_dec, k_dec, c_dec, gn_g.reshape(1, -1), gn_b.reshape(1, -1))
    return o.reshape(b * t, nh * dv), [(w_out, 0, 0)], None


def _rwkv_pre_body(x_ref, prev_ref, g_ref, scale_ref, shift_ref, mu_ref, o_ref):
    ti = pl.program_id(1)
    g, scale, shift = g_ref[...], scale_ref[0], shift_ref[0]
    h = _rms(x_ref[0], g) * (1.0 + scale) + shift
    hp = _rms(prev_ref[0], g) * (1.0 + scale) + shift
    last = jnp.where(ti == 0, 0.0, hp[7:8, :])
    row = lax.broadcasted_iota(jnp.int32, h.shape, 0)
    shifted = jnp.where(row == 0, last, pltpu.roll(h, 1, axis=0))
    xx = shifted - h
    for s in range(6):
        o_ref[s, 0] = (h + xx * mu_ref[s:s + 1, :]).astype(o_ref.dtype)


def rwkv_pre(x, g_pre, scale, shift, mu, tt=256):
    b, t, d = x.shape
    tt = _pick(t, tt)
    bvec = pl.BlockSpec((1, 1, d), lambda bi, ti: (bi, 0, 0))
    return pl.pallas_call(
        _rwkv_pre_body,
        grid=(b, t // tt),
        in_specs=[
            pl.BlockSpec((1, tt, d), lambda bi, ti: (bi, ti, 0)),
            pl.BlockSpec((1, 8, d), lambda bi, ti: (bi, jnp.maximum(ti * (tt // 8) - 1, 0), 0)),
            pl.BlockSpec((1, d), lambda bi, ti: (0, 0)),
            bvec, bvec,
            pl.BlockSpec((6, d), lambda bi, ti: (0, 0)),
        ],
        out_specs=pl.BlockSpec((6, 1, tt, d), lambda bi, ti: (0, bi, ti, 0)),
        out_shape=jax.ShapeDtypeStruct((6, b, t, d), BF16),
        compiler_params=_cparams("parallel", "parallel"),
        name="rwkv_pre",
    )(x, x, g_pre.reshape(1, d), scale.reshape(b, 1, d), shift.reshape(b, 1, d), mu)


def _rwkv_scan_body(r_ref, k_ref, v_ref, a_ref, lw_ref, g_ref, kk_ref, ka_ref, rk_ref, lng_ref, lnb_ref, o_ref,
                    state_ref, *, chunk, heads, n):
    @pl.when(pl.program_id(2) == 0)
    def _():
        state_ref[...] = jnp.zeros_like(state_ref)

    ln = chunk
    hs = range(heads)
    row = lax.broadcasted_iota(jnp.int32, (ln, ln), 0)
    col = lax.broadcasted_iota(jnp.int32, (ln, ln), 1)
    tri = (col <= row).astype(BF16)
    eye = (row == col).astype(F32)
    wid = heads * n
    seg_w = min(wid, 256)
    assert wid % seg_w == 0 and seg_w % n == 0
    seg = (lax.broadcasted_iota(jnp.int32, (seg_w, seg_w), 0) // n
           == lax.broadcasted_iota(jnp.int32, (seg_w, seg_w), 1) // n).astype(BF16)

    def split_dot(m, x):
        hi = x.astype(BF16)
        lo = (x - hi.astype(F32)).astype(BF16)
        return _dot(m, hi) + _dot(m, lo)

    def seg_sum(x):
        hi = x.astype(BF16)
        lo = (x - hi.astype(F32)).astype(BF16)
        return jnp.concatenate([_dot(hi[:, j:j + seg_w], seg) + _dot(lo[:, j:j + seg_w], seg)
                                for j in range(0, wid, seg_w)], axis=1)

    heads_of = lambda x: [x[:, hd * n:(hd + 1) * n] for hd in hs]

    r, k, v, a = (ref[0].astype(F32) for ref in (r_ref, k_ref, v_ref, a_ref))
    lw = lw_ref[0]
    kk = k * kk_ref[...]
    kk = kk / jnp.maximum(jnp.sqrt(seg_sum(kk * kk)), 1e-12)
    k2 = k * (1.0 + (a - 1.0) * ka_ref[...])
    beta = kk * a
    c = split_dot(tri, lw)
    c_last = c[ln - 1:ln, :]
    e_neg = jnp.exp(-c)
    e_rem = jnp.exp(c_last - c)
    e_last = heads_of(jnp.exp(c_last))
    a_t = heads_of(-kk * jnp.exp(c - lw))
    r_t = heads_of(r * jnp.exp(c))
    b_t = heads_of(beta * e_neg)
    k_t = heads_of(k2 * e_neg)
    b_rem = heads_of(beta * e_rem)
    k_rem = heads_of(k2 * e_rem)
    vs = heads_of(v)
    bonus = seg_sum(r * k2 * rk_ref[...]) * v

    ar = [jnp.concatenate([a_t[h], r_t[h]], axis=0).astype(BF16) for h in hs]
    bk = [jnp.concatenate([b_t[h], k_t[h]], axis=0).astype(BF16) for h in hs]
    bk_rem = [jnp.concatenate([b_rem[h], k_rem[h]], axis=0).astype(BF16) for h in hs]
    vb = [vs[h].astype(BF16) for h in hs]
    state = [state_ref[h] for h in hs]
    p = [_dot_nt(ar[h], bk[h]) for h in hs]
    q0 = [_dot_nt(ar[h], state[h].astype(BF16)) for h in hs]
    row2 = lax.broadcasted_iota(jnp.int32, (2 * ln, 2 * ln), 0)
    col2 = lax.broadcasted_iota(jnp.int32, (2 * ln, 2 * ln), 1)
    keep = col2 % ln < row2 % ln + row2 // ln
    pm = [jnp.where(keep, p[h], 0.0) for h in hs]
    m_ab = [pm[h][:ln, :ln] for h in hs]
    m_ak = [pm[h][:ln, ln:].astype(BF16) for h in hs]
    m_r = [pm[h][ln:].astype(BF16) for h in hs]
    rhs = [q0[h][:ln] + _dot(m_ak[h], vb[h]) for h in hs]
    inv = [eye + m_ab[h] for h in hs]
    pw = [m_ab[h].astype(BF16) for h in hs]
    for _ in range(int(math.log2(ln)) - 1):
        pw = [_dot(pw[h], pw[h]).astype(BF16) for h in hs]
        inv = [inv[h] + _dot(inv[h].astype(BF16), pw[h]) for h in hs]
    u = [_dot(inv[h].astype(BF16), rhs[h].astype(BF16)) for h in hs]
    uv = [jnp.concatenate([u[h], vs[h]], axis=0) for h in hs]
    y = [q0[h][ln:] + _dot(m_r[h], uv[h].astype(BF16)) for h in hs]
    for h in hs:
        state_ref[h] = state[h] * e_last[h] + _dot(uv[h].T.astype(BF16), bk_rem[h])
    yn = []
    for h in hs:
        mu = jnp.mean(y[h], axis=-1, keepdims=True)
        yc = y[h] - mu
        var = jnp.mean(yc * yc, axis=-1, keepdims=True)
        yn.append(yc * lax.rsqrt(var + RWKV_GN_EPS))
    yn = jnp.concatenate(yn, axis=-1) * lng_ref[...] + lnb_ref[...]
    o_ref[0] = ((yn + bonus) * g_ref[0]).astype(o_ref.dtype)


def rwkv_scan(r, k, v, a, lw, g, k_k, k_a, r_k, ln_g, ln_b):
    b, t, d = r.shape
    n = RWKV_HEAD_DIM
    hps = min(RWKV_HEADS_PER_STEP, d // n)
    w = hps * n
    ln = _pick(t, RWKV_CHUNK)
    tok = pl.BlockSpec((1, ln, w), lambda bi, hi, ci: (bi, ci, hi))
    vec = pl.BlockSpec((1, w), lambda bi, hi, ci: (0, hi))
    return pl.pallas_call(
        functools.partial(_rwkv_scan_body, chunk=ln, heads=hps, n=n),
        grid=(b, d // w, t // ln),
        in_specs=[tok] * 6 + [vec] * 5,
        out_specs=tok,
        out_shape=jax.ShapeDtypeStruct((b, t, d), BF16),
        scratch_shapes=[pltpu.VMEM((hps, n, n), F32)],
        compiler_params=_cparams("parallel", "parallel", "arbitrary"),
        name="rwkv_scan",
    )(r, k, v, a, lw, g, k_k.reshape(1, d), k_a.reshape(1, d), r_k.reshape(1, d), ln_g.reshape(1, d),
      ln_b.reshape(1, d))


def _tanh_epi(parts, extras):
    return jnp.tanh(parts[0])


def _sigmoid_epi(parts, extras):
    return jax.nn.sigmoid(parts[0])


def _sigmoid_bias_epi(parts, extras):
    return jax.nn.sigmoid(extras[0] + parts[0])


def _logdecay_epi(parts, extras):
    return -jnp.exp(-jax.nn.softplus(-(extras[0] + parts[0])) - 0.5)


def rwkv7_time_mix(x, g_pre, scale, shift, mu, w_rkv, w0, w_la, w_lb, a0, a_la, a_lb, g_la, g_lb, k_k, k_a, r_k,
                   ln_g, ln_b, w_out):
    b, t, d = x.shape
    xs = rwkv_pre(x, g_pre, scale, shift, mu).reshape(6 * b * t, d)
    tn = _pick(d, 1024)
    w_rkv2d = w_rkv.reshape(3 * d, d)
    r = mm(xs, [(w_rkv2d, 0, 0)], a_part=(0, 6), out_dtype=BF16, name="rwkv_r")
    k = mm(xs, [(w_rkv2d, 0, 1)], a_part=(1, 6), out_dtype=BF16, name="rwkv_k")
    v = mm(xs, [(w_rkv2d, 0, 2)], a_part=(2, 6), out_dtype=BF16, name="rwkv_v")
    lw = mm(mm(xs, w_la, a_part=(3, 6), epi=_tanh_epi, out_dtype=BF16, name="rwkv_w_la"), w_lb, epi=_logdecay_epi,
            extras=[_row_extra(w0, tn)], name="rwkv_w_lb")
    a = mm(mm(xs, a_la, a_part=(4, 6), out_dtype=BF16, name="rwkv_a_la"), a_lb, epi=_sigmoid_bias_epi,
           extras=[_row_extra(a0, tn)], out_dtype=BF16, name="rwkv_a_lb")
    g = mm(mm(xs, g_la, a_part=(5, 6), epi=_sigmoid_epi, out_dtype=BF16, name="rwkv_g_la"), g_lb, out_dtype=BF16,
           name="rwkv_g_lb")
    sh = lambda z: z.reshape(b, t, d)
    z = rwkv_scan(sh(r), sh(k), sh(v), sh(a), sh(lw), sh(g), k_k, k_a, r_k, ln_g, ln_b)
    return z.reshape(b * t, d), [(w_out, 0, 0)], None


def _nsa_compress_body(kv_ref, pe_ref, w1_ref, w2_ref, o_ref, buf_ref, *, nchunk, dh):
    st = NSA_CMP_STRIDE
    hid = w1_ref.shape[1]
    buf_ref[...] = kv_ref[0].astype(F32)
    first = jnp.zeros((nchunk, hid), F32)
    second = jnp.zeros((nchunk, hid), F32)
    for j in range(st):
        xj = buf_ref[pl.ds(j, nchunk, stride=st), :]
        first = first + _dot((xj + pe_ref[j:j + 1, :]).astype(BF16), w1_ref[j * dh:(j + 1) * dh, :])
        second = second + _dot((xj + pe_ref[st + j:st + j + 1, :]).astype(BF16),
                               w1_ref[(st + j) * dh:(st + j + 1) * dh, :])
    hidden = jax.nn.gelu(first + pltpu.roll(second, nchunk - 1, axis=0))
    out = _dot(hidden.astype(BF16), w2_ref[...])
    row = lax.broadcasted_iota(jnp.int32, out.shape, 0)
    o_ref[0, 0] = jnp.where(row < nchunk - 1, out, 0.0)


def nsa_compress(proj, col0, pe, w1, w2):
    b, t, _ = proj.shape
    g, dh = NSA_KV_GROUPS, NSA_HEAD_DIM
    nchunk = t // NSA_CMP_STRIDE
    hid = w1.shape[1]
    return pl.pallas_call(
        functools.partial(_nsa_compress_body, nchunk=nchunk, dh=dh),
        grid=(b, g),
        in_specs=[
            pl.BlockSpec((1, t, dh), lambda bi, gi: (bi, 0, col0 // dh + gi)),
            pl.BlockSpec((NSA_CMP_BLOCK, dh), lambda bi, gi: (0, 0)),
            pl.BlockSpec((NSA_CMP_BLOCK * dh, hid), lambda bi, gi: (0, 0)),
            pl.BlockSpec((hid, dh), lambda bi, gi: (0, 0)),
        ],
        out_specs=pl.BlockSpec((1, 1, nchunk, dh), lambda bi, gi: (bi, gi, 0, 0)),
        out_shape=jax.ShapeDtypeStruct((b, g, nchunk, dh), F32),
        scratch_shapes=[pltpu.VMEM((t, dh), F32)],
        compiler_params=_cparams("parallel", "parallel"),
        name="nsa_compress",
    )(proj, pe, w1.astype(BF16), w2.astype(BF16))


def _stack_heads(q, r, dh):
    return jnp.concatenate([q[:, i * dh:(i + 1) * dh] for i in range(r)], axis=0)


def _unstack_heads(o, r, tq):
    return jnp.concatenate([o[i * tq:(i + 1) * tq] for i in range(r)], axis=-1)


def _gate_rows(gates, branch, r, tq):
    return jnp.concatenate([gates[:, branch * r + i:branch * r + i + 1] for i in range(r)], axis=0)


def _nsa_cmp_body(q_ref, kc_ref, vc_ref, gates_ref, selmap_t_ref, o_ref, sel_ref, *, tq, r, dh, n_sel, groups):
    q0 = pl.program_id(2) * tq
    wq = r * dh
    ncp = kc_ref.shape[2]
    rows = lax.broadcasted_iota(jnp.int32, (r * tq, ncp), 0)
    cmp_end = lax.broadcasted_iota(jnp.int32, (r * tq, ncp), 1) * NSA_CMP_STRIDE + (NSA_CMP_BLOCK - 1)
    vis = cmp_end <= q0 + rows % tq
    selmap_t = selmap_t_ref[...]
    imps = []
    for gg in range(groups):
        qcols = slice(gg * wq, (gg + 1) * wq)
        qs = (_stack_heads(q_ref[0, :, qcols].astype(F32), r, dh) * (dh ** -0.5)).astype(BF16)
        kc, vc = kc_ref[0, gg].astype(BF16), vc_ref[0, gg].astype(BF16)
        s = jnp.where(vis, _dot_nt(qs, kc), NEG_INF)
        e = jnp.exp(s - jnp.max(s, axis=-1, keepdims=True))
        p = e / jnp.sum(e, axis=-1, keepdims=True) * vis.astype(F32)
        o = _dot(p.astype(BF16), vc) * _gate_rows(gates_ref[0, gg], 0, r, tq)
        o_ref[0, :, qcols] = _unstack_heads(o, r, tq)
        p_sum = p[0:tq]
        for i in range(1, r):
            p_sum = p_sum + p[i * tq:(i + 1) * tq]
        hi = p_sum.astype(BF16)
        lo = (p_sum - hi.astype(F32)).astype(BF16)
        imps.append(_dot_nt(selmap_t, hi) + _dot_nt(selmap_t, lo))
    imp = jnp.concatenate(imps, axis=1)
    ns, width = imp.shape
    blk = lax.broadcasted_iota(jnp.int32, (ns, width), 0)
    tq_pos = q0 + lax.broadcasted_iota(jnp.int32, (ns, width), 1) % tq
    cur = tq_pos // NSA_SEL_BLOCK
    valid = blk * NSA_SEL_BLOCK <= tq_pos
    forced = (blk == 0) | (blk == cur) | (blk == cur - 1)
    score = jnp.where(valid, jnp.where(forced, POS_BIG, imp), NEG_INF)
    rank = jnp.zeros((ns, width), F32)
    for m in range(ns):
        sm = score[m:m + 1, :]
        beats = (sm > score) | ((sm == score) & (blk > m))
        rank = rank + beats.astype(F32)
    chosen = ((rank < n_sel) & valid).astype(F32)
    for gg in range(groups):
        sel_ref[0, gg] = (chosen[:, gg * tq:(gg + 1) * tq].T - 1.0).astype(sel_ref.dtype)


def _flash_update(s, v_ones, m_ref, acc_ref):
    lanes = m_ref.shape[1]
    assert s.shape[1] % lanes == 0 and acc_ref.shape[1] == 2 * lanes
    m_old = m_ref[...]
    m_new = jnp.maximum(m_old, jnp.max(s, axis=-1, keepdims=True))
    alpha = jnp.exp2(m_old - m_new)
    p = jnp.exp2((s - jnp.concatenate([m_new] * (s.shape[1] // lanes), axis=1)).astype(BF16))
    acc_ref[...] = jnp.concatenate([alpha, alpha], axis=1) * acc_ref[...] + _dot(p, v_ones)
    m_ref[...] = m_new


def _nsa_sel_body(qi_tbl, ki_tbl, diag_tbl, q_ref, k_ref, v_ref, gates_ref, oin_ref, sel_ref, o_ref, qa_ref, m_ref,
                  acc_ref, *, tq, tk, r, dh):
    step = pl.program_id(2)
    qi, ki, diag = qi_tbl[step], ki_tbl[step], diag_tbl[step]
    ns = sel_ref.shape[-1]
    assert ns <= dh

    @pl.when(ki == 0)
    def _():
        m_ref[...] = jnp.full_like(m_ref, NEG_INF)
        acc_ref[...] = jnp.zeros_like(acc_ref)
        qs = _stack_heads(q_ref[0].astype(F32), r, dh) * (dh ** -0.5 * LOG2_E)
        pieces = [qs, jnp.concatenate([sel_ref[0, 0].astype(F32)] * r, axis=0)]
        if ns < dh:
            pieces.append(jnp.zeros((r * tq, dh - ns), F32))
        qa_ref[...] = jnp.concatenate(pieces, axis=1).astype(BF16)

    blk_of_key = (ki * tk + lax.broadcasted_iota(jnp.int32, (tk, dh), 0)) // NSA_SEL_BLOCK
    own_block = jnp.where(blk_of_key == lax.broadcasted_iota(jnp.int32, (tk, dh), 1), POS_BIG, 0.0).astype(BF16)
    ka = jnp.concatenate([k_ref[0].astype(BF16), own_block], axis=1)
    s = _dot_nt(qa_ref[...], ka)
    v_ones = jnp.concatenate([v_ref[0].astype(BF16), jnp.ones((tk, dh), BF16)], axis=1)

    @pl.when(diag == 0)
    def _():
        _flash_update(s, v_ones, m_ref, acc_ref)

    @pl.when(diag == 1)
    def _():
        t_pos = qi * tq + lax.broadcasted_iota(jnp.int32, (r * tq, tk), 0) % tq
        k_pos = ki * tk + lax.broadcasted_iota(jnp.int32, (r * tq, tk), 1)
        _flash_update(jnp.where(k_pos <= t_pos, s, NEG_INF), v_ones, m_ref, acc_ref)
        o = acc_ref[:, :dh] / acc_ref[:, dh:] * _gate_rows(gates_ref[0, 0], 1, r, tq)
        o_ref[0] = (oin_ref[0] + _unstack_heads(o, r, tq)).astype(o_ref.dtype)


def _nsa_win_body(q_ref, *rest, tq, r, dh, nwin, groups):
    k_refs, v_refs = rest[:nwin], rest[nwin:2 * nwin]
    gates_ref, oin_ref, o_ref = rest[2 * nwin:]
    qi = pl.program_id(2)
    wq = r * dh
    t_loc = lax.broadcasted_iota(jnp.int32, (r * tq, tq), 0) % tq
    k_loc = lax.broadcasted_iota(jnp.int32, (r * tq, tq), 1)
    for gg in range(groups):
        qcols, kcols = slice(gg * wq, (gg + 1) * wq), slice(gg * dh, (gg + 1) * dh)
        qs = (_stack_heads(q_ref[0, :, qcols].astype(F32), r, dh) * (dh ** -0.5 * LOG2_E)).astype(BF16)
        tiles = []
        for j in range(nwin):
            s = _dot_nt(qs, k_refs[j][0, :, kcols].astype(BF16))
            if j == nwin - 1:
                s = jnp.where(k_loc <= t_loc, s, NEG_INF)
            else:
                if j == 0:
                    s = jnp.where(k_loc > t_loc, s, NEG_INF)
                s = jnp.where(qi - (nwin - 1) + j >= 0, s, NEG_INF)
            tiles.append(s)
        s = jnp.concatenate(tiles, axis=1)
        p = jnp.exp2((s - jnp.max(s, axis=-1, keepdims=True)).astype(BF16))
        v = jnp.concatenate([v_refs[j][0, :, kcols] for j in range(nwin)], axis=0).astype(BF16)
        on = _dot(p, jnp.concatenate([v, jnp.ones_like(v)], axis=1))
        o = on[:, :dh] / on[:, dh:] * _gate_rows(gates_ref[0, gg], 2, r, tq)
        o_ref[0, :, qcols] = (oin_ref[0, :, qcols] + _unstack_heads(o, r, tq)).astype(o_ref.dtype)


def nsa_mix(h, w_in, pe_k, pe_v, ck_w1, ck_w2, cv_w1, cv_w2, w_out, tq=256):
    b, t, d = h.shape
    nh, g, dh = NSA_HEADS, NSA_KV_GROUPS, NSA_HEAD_DIM
    r = nh // g
    hd, kd = nh * dh, g * dh
    n_main = hd + 6 * kd
    h2 = h.reshape(b * t, d)
    proj = mm(h2, w_in[:, :n_main], out_dtype=BF16, name="nsa_in").reshape(b, t, n_main)
    n_gate = w_in.shape[1] - n_main
    w_gate = jnp.pad(w_in[:, n_main:], ((0, 0), (0, 128 - n_gate)))
    gates = mm(h2, w_gate, epi=_sigmoid_epi, name="nsa_gates")[:, :n_gate]
    gates = jnp.transpose(gates.reshape(b, t, 3, g, r), (0, 3, 1, 2, 4)).reshape(b, g, t, 3 * r)
    k_cmp = nsa_compress(proj, hd, pe_k, ck_w1, ck_w2)
    v_cmp = nsa_compress(proj, hd + kd, pe_v, cv_w1, cv_w2)
    ncp = k_cmp.shape[2]
    ns = t // NSA_SEL_BLOCK
    n_sel = min(NSA_SEL_TOP, ns)
    cs = jnp.arange(ncp)[None, :] * NSA_CMP_STRIDE
    ss = jnp.arange(ns)[:, None] * NSA_SEL_BLOCK
    sel_map_t = (jnp.maximum(jnp.minimum(cs + NSA_CMP_BLOCK, ss + NSA_SEL_BLOCK) - jnp.maximum(cs, ss), 0)
                 .astype(F32) / NSA_CMP_BLOCK).astype(BF16)
    tq = _pick(t, tq)
    assert NSA_WINDOW % tq == 0
    nq = t // tq
    wq = r * dh
    col = lambda base: base // dh
    q_spec3 = pl.BlockSpec((1, tq, wq), lambda bi, gi, qi: (bi, qi, gi))
    gate_spec3 = pl.BlockSpec((1, 1, tq, 3 * r), lambda bi, gi, qi: (bi, gi, qi, 0))
    gc = NSA_CMP_GROUPS_PER_STEP
    assert g % gc == 0
    q_spec_c = pl.BlockSpec((1, tq, gc * wq), lambda bi, gi, qi: (bi, qi, gi))
    o_cmp, sel = pl.pallas_call(
        functools.partial(_nsa_cmp_body, tq=tq, r=r, dh=dh, n_sel=n_sel, groups=gc),
        grid=(b, g // gc, nq),
        in_specs=[
            q_spec_c,
            pl.BlockSpec((1, gc, ncp, dh), lambda bi, gi, qi: (bi, gi, 0, 0)),
            pl.BlockSpec((1, gc, ncp, dh), lambda bi, gi, qi: (bi, gi, 0, 0)),
            pl.BlockSpec((1, gc, tq, 3 * r), lambda bi, gi, qi: (bi, gi, qi, 0)),
            pl.BlockSpec((ns, ncp), lambda bi, gi, qi: (0, 0)),
        ],
        out_specs=[q_spec_c, pl.BlockSpec((1, gc, tq, ns), lambda bi, gi, qi: (bi, gi, qi, 0))],
        out_shape=[jax.ShapeDtypeStruct((b, t, hd), F32), jax.ShapeDtypeStruct((b, g, t, ns), BF16)],
        compiler_params=_cparams("parallel", "parallel", "parallel"),
        name="nsa_cmp",
    )(proj, k_cmp, v_cmp, gates, sel_map_t)

    tk = _pick(t, NSA_SEL_KEY_TILE)
    tqs = _pick(t, NSA_SEL_QUERY_TILE)
    assert tk % tqs == 0
    pairs = [(qi, ki) for qi in range(t // tqs) for ki in range((qi * tqs + tqs - 1) // tk + 1)]
    qi_tbl = jnp.asarray([pq for pq, _ in pairs], jnp.int32)
    ki_tbl = jnp.asarray([pk for _, pk in pairs], jnp.int32)
    diag_tbl = jnp.asarray([int(pk == (pq * tqs + tqs - 1) // tk) for pq, pk in pairs], jnp.int32)
    q_spec_p = pl.BlockSpec((1, tqs, wq), lambda bi, gi, p, qt, kt, dt: (bi, qt[p], gi))

    def kv_spec_p(base):
        return pl.BlockSpec((1, tk, dh), lambda bi, gi, p, qt, kt, dt: (bi, kt[p], col(base) + gi))

    o_sel = pl.pallas_call(
        functools.partial(_nsa_sel_body, tq=tqs, tk=tk, r=r, dh=dh),
        grid_spec=pltpu.PrefetchScalarGridSpec(
            num_scalar_prefetch=3,
            grid=(b, g, len(pairs)),
            in_specs=[q_spec_p, kv_spec_p(hd + 2 * kd), kv_spec_p(hd + 3 * kd),
                      pl.BlockSpec((1, 1, tqs, 3 * r), lambda bi, gi, p, qt, kt, dt: (bi, gi, qt[p], 0)),
                      q_spec_p,
                      pl.BlockSpec((1, 1, tqs, ns), lambda bi, gi, p, qt, kt, dt: (bi, gi, qt[p], 0))],
            out_specs=q_spec_p,
            scratch_shapes=[pltpu.VMEM((r * tqs, 2 * dh), BF16), pltpu.VMEM((r * tqs, dh), F32),
                            pltpu.VMEM((r * tqs, 2 * dh), F32)],
        ),
        out_shape=jax.ShapeDtypeStruct((b, t, hd), F32),
        compiler_params=_cparams("parallel", "parallel", "arbitrary"),
        name="nsa_sel",
    )(qi_tbl, ki_tbl, diag_tbl, proj, proj, proj, gates, o_cmp, sel)

    nwin = NSA_WINDOW // tq + 1
    gw = NSA_WIN_GROUPS_PER_STEP
    assert g % gw == 0 and kd % (gw * dh) == 0
    q_spec_w = pl.BlockSpec((1, tq, gw * wq), lambda bi, gi, qi: (bi, qi, gi))

    def kv_spec_w(base, j):
        return pl.BlockSpec((1, tq, gw * dh),
                            lambda bi, gi, qi: (bi, jnp.maximum(qi - (nwin - 1) + j, 0), base // (gw * dh) + gi))

    o_all = pl.pallas_call(
        functools.partial(_nsa_win_body, tq=tq, r=r, dh=dh, nwin=nwin, groups=gw),
        grid=(b, g // gw, nq),
        in_specs=([q_spec_w] + [kv_spec_w(hd + 4 * kd, j) for j in range(nwin)]
                  + [kv_spec_w(hd + 5 * kd, j) for j in range(nwin)]
                  + [pl.BlockSpec((1, gw, tq, 3 * r), lambda bi, gi, qi: (bi, gi, qi, 0)), q_spec_w]),
        out_specs=q_spec_w,
        out_shape=jax.ShapeDtypeStruct((b, t, hd), BF16),
        compiler_params=_cparams("parallel", "parallel", "parallel"),
        name="nsa_win",
    )(proj, *([proj] * (2 * nwin)), gates, o_sel)
    return o_all.reshape(b * t, hd), [(w_out, 0, 0)], None


def _silu(a):
    return a * jax.nn.sigmoid(a)


def _ada_epi(parts, extras):
    return parts[0] + extras[0]


def ada_modulation(c, ada_w, ada_b):
    depth, d, n6 = ada_w.shape
    b = c.shape[0]
    rows = ((b + 7) // 8) * 8
    cond = jnp.pad(c, ((0, rows - b), (0, 0)))
    w2d = ada_w.reshape(depth * d, n6)
    mods = []
    for i in range(depth):
        mods.append(mm(cond, [(w2d, 0, i)], a_act=_silu, epi=_ada_epi, extras=[_row_extra(ada_b[i], _pick(n6, 1024))],
                       precision=HIGHEST, name="ada_mod")[:b])
    return jnp.stack(mods)


def kernel(x, c, ada_w, ada_b, norm_g, mlp_w1, mlp_w2, rwkv_mu, rwkv_w_rkv, rwkv_w0, rwkv_w_la, rwkv_w_lb, rwkv_a0, rwkv_a_la, rwkv_a_lb, rwkv_g_la, rwkv_g_lb, rwkv_k_k, rwkv_k_a, rwkv_r_k, rwkv_ln_g, rwkv_ln_b, rwkv_w_out, ret_w_in, ret_gn_g, ret_gn_b, ret_w_out, conv_pw1_w, conv_pw1_b, conv_dw_w, conv_dw_b, conv_ln_g, conv_ln_b, conv_pw2_w, conv_pw2_b, nsa_w_in, nsa_pe_k, nsa_pe_v, nsa_ck_w1, nsa_ck_w2, nsa_cv_w1, nsa_cv_w2, nsa_w_out):
    b, t, d = x.shape
    depth = ada_w.shape[0]
    mod = ada_modulation(c, ada_w, ada_b).reshape(depth, b, 6, d)
    h = None
    for i in range(depth):
        sh_t, sc_t, gt_t, sh_c, sc_c, gt_c = (mod[i, :, j] for j in range(6))
        kind = i % 4
        if kind == 0:
            act, ws, bias = rwkv7_time_mix(x, norm_g[i, 0], sc_t, sh_t, rwkv_mu, rwkv_w_rkv, rwkv_w0, rwkv_w_la,
                                           rwkv_w_lb, rwkv_a0, rwkv_a_la, rwkv_a_lb, rwkv_g_la, rwkv_g_lb, rwkv_k_k,
                                           rwkv_k_a, rwkv_r_k.reshape(-1), rwkv_ln_g, rwkv_ln_b, rwkv_w_out)
        else:
            if h is None:
                (h,) = resid_norm(x, pre=(norm_g[i, 0], sc_t, sh_t))
            if kind == 1:
                act, ws, bias = retention_mix(h, ret_w_in, ret_gn_g, ret_gn_b, ret_w_out)
            elif kind == 2:
                act, ws, bias = conformer_conv_mix(h, conv_pw1_w, conv_pw1_b, conv_dw_w, conv_dw_b, conv_ln_g,
                                                   conv_ln_b, conv_pw2_w, conv_pw2_b)
            else:
                act, ws, bias = nsa_mix(h, nsa_w_in, nsa_pe_k, nsa_pe_v, nsa_ck_w1, nsa_ck_w2, nsa_cv_w1, nsa_cv_w2,
                                        nsa_w_out)
        y = project(act, ws, bias, "mixer_out").reshape(b, t, d)
        act, x = mlp_up_resid(x, y, (norm_g[i, 1], gt_t), (norm_g[i, 2], sc_c, sh_c), mlp_w1, i)
        nxt = i + 1
        if nxt < depth and nxt % 4 != 0:
            x, h = mlp_down_resid(act, mlp_w2, i, x, (norm_g[i, 3], gt_c),
                                  (norm_g[nxt, 0], mod[nxt, :, 1], mod[nxt, :, 0]))
        else:
            (x,) = mlp_down_resid(act, mlp_w2, i, x, (norm_g[i, 3], gt_c))
            h = None
    return x
```

```python
import functools
import math

import jax
import jax.numpy as jnp
from jax import lax
from jax.experimental import pallas as pl
from jax.experimental.pallas import tpu as pltpu

F32 = jnp.float32
BF16 = jnp.bfloat16
HIGHEST = lax.Precision.HIGHEST

NORM_EPS = 1e-6
NEG_INF = -1e30
POS_BIG = 1e30
LOG2_E = math.log2(math.e)

RWKV_HEAD_DIM = 64
RWKV_GN_EPS = 64e-5
RWKV_CHUNK = 64
RWKV_HEADS_PER_STEP = 32

RET_HEADS = 8
RET_HEADS_PER_STEP = 4
RET_CHUNK = 256
RET_GN_EPS = 1e-5
ROPE_BASE = 10000.0

CONV_WIDTH = 31
CONV_HALO = 32
CONV_LN_EPS = 1e-5

NSA_HEADS = 16
NSA_KV_GROUPS = 4
NSA_HEAD_DIM = 128
NSA_CMP_BLOCK = 32
NSA_CMP_STRIDE = 16
NSA_SEL_BLOCK = 64
NSA_SEL_TOP = 16
NSA_WINDOW = 512
NSA_SEL_KEY_TILE = 512
NSA_SEL_QUERY_TILE = 512
NSA_WIN_GROUPS_PER_STEP = 4
NSA_CMP_GROUPS_PER_STEP = 4

VMEM_LIMIT_BYTES = 56 * 1024 * 1024


def _cparams(*sem):
    return pltpu.CompilerParams(dimension_semantics=sem, vmem_limit_bytes=VMEM_LIMIT_BYTES)


def _dot(a, b, precision=None):
    return jnp.dot(a, b, preferred_element_type=F32, precision=precision)


def _dot_nt(a, b, precision=None):
    return lax.dot_general(a, b, (((1,), (1,)), ((), ())), preferred_element_type=F32, precision=precision)


def _pick(n, pref):
    if n <= pref:
        return n
    t = pref
    while n % t:
        t //= 2
    return t


def _mm_body(*refs, n_w, n_ex, n_out, nk, epi, precision, has_a_add, a_act):
    a_ref = refs[0]
    pos = 1
    a_add_ref = None
    if has_a_add:
        a_add_ref = refs[pos]
        pos += 1
    w_refs = refs[pos:pos + n_w]
    pos += n_w
    ex_refs = refs[pos:pos + n_ex]
    pos += n_ex
    o_refs = refs[pos:pos + n_out]
    acc_refs = refs[pos + n_out:]

    def finish(parts):
        res = epi(parts, [e[...] for e in ex_refs])
        if n_out == 1 and not isinstance(res, (tuple, list)):
            res = (res,)
        for o_ref, val in zip(o_refs, res, strict=True):
            o_ref[...] = val.astype(o_ref.dtype)

    a = a_ref[...]
    if a_add_ref is not None:
        a = a.astype(F32) + a_add_ref[...]
    if a_act is not None:
        a = a_act(a)
    if precision is None:
        a = a.astype(BF16)
    parts = []
    for w_ref in w_refs:
        w = w_ref[...]
        if precision is None:
            w = w.astype(BF16)
        parts.append(_dot(a, w, precision))

    if nk == 1:
        finish(parts)
    else:
        k = pl.program_id(2)

        @pl.when(k == 0)
        def _():
            for acc, p in zip(acc_refs, parts):
                acc[...] = p

        @pl.when(k > 0)
        def _():
            for acc, p in zip(acc_refs, parts):
                acc[...] += p

        @pl.when(k == nk - 1)
        def _():
            finish([acc[...] for acc in acc_refs])


def _first(parts, extras):
    return parts[0]


def mm(a, ws, *, n_out=None, extras=(), epi=_first, out_dtype=F32, tm=2048, tn=1024, tk=2048, precision=None,
       a_add=None, a_act=None, a_part=(0, 1), cast_in_kernel=False, name="mm"):
    m, kdim = a.shape
    a_s, a_parts = a_part
    assert m % a_parts == 0
    m //= a_parts
    if not isinstance(ws, (list, tuple)):
        ws = [(ws, 0, 0)]
    ws = [tuple(w) + (0,) * (3 - len(w)) for w in ws]
    if n_out is None:
        n_out = ws[0][0].shape[1]
    tm = _pick(m, tm)
    tn = _pick(n_out, tn)
    tk = _pick(kdim, tk)
    nk = kdim // tk
    assert m % tm == 0 and n_out % tn == 0 and kdim % tk == 0
    in_specs = [pl.BlockSpec((tm, tk), lambda i, j, k, o=a_s * (m // tm): (i + o, k))]
    args = [a]
    if a_add is not None:
        in_specs.append(pl.BlockSpec((1, tk), lambda i, j, k: (0, k)))
        args.append(a_add)
    for w, off, koff in ws:
        in_specs.append(pl.BlockSpec((tk, tn), lambda i, j, k, off=off, ko=koff * nk: (k + ko, j + off)))
        args.append(w.astype(BF16) if precision is None and not cast_in_kernel else w)
    for arr, bshape, imap in extras:
        in_specs.append(pl.BlockSpec(bshape, imap))
        args.append(arr)
    scratch = [pltpu.VMEM((tm, tn), F32) for _ in ws] if nk > 1 else []
    multi = isinstance(out_dtype, (tuple, list))
    out_dtypes = tuple(out_dtype) if multi else (out_dtype,)
    body = functools.partial(_mm_body, n_w=len(ws), n_ex=len(extras), n_out=len(out_dtypes), nk=nk, epi=epi,
                             precision=precision, has_a_add=a_add is not None, a_act=a_act)
    outs = pl.pallas_call(
        body,
        grid=(m // tm, n_out // tn, nk),
        in_specs=in_specs,
        out_specs=[pl.BlockSpec((tm, tn), lambda i, j, k: (i, j)) for _ in out_dtypes],
        out_shape=[jax.ShapeDtypeStruct((m, n_out), dt) for dt in out_dtypes],
        scratch_shapes=scratch,
        compiler_params=_cparams("parallel", "parallel", "arbitrary"),
        name=name,
    )(*args)
    return tuple(outs) if multi else outs[0]


def _row_extra(vec, tn):
    return (vec.reshape(1, -1), (1, tn), lambda i, j, k: (0, j))


def _rms(x, g):
    return x * lax.rsqrt(jnp.mean(x * x, axis=-1, keepdims=True) + NORM_EPS) * g


def _resid_norm_body(*refs, has_y, has_h):
    pos = 0
    x_ref = refs[pos]; pos += 1
    if has_y:
        y_ref, gpost_ref, gate_ref = refs[pos:pos + 3]; pos += 3
    if has_h:
        gpre_ref, scale_ref, shift_ref = refs[pos:pos + 3]; pos += 3
    outs = refs[pos:]
    x = x_ref[0]
    o = 0
    if has_y:
        y = y_ref[0].astype(F32)
        x = x + (1.0 + gate_ref[0]) * _rms(y, gpost_ref[...])
        outs[o][0] = x
        o += 1
    if has_h:
        h = _rms(x, gpre_ref[...]) * (1.0 + scale_ref[0]) + shift_ref[0]
        outs[o][0] = h.astype(outs[o].dtype)


def resid_norm(x, y=None, post=None, pre=None, tt=512):
    b, t, d = x.shape
    tt = _pick(t, tt)
    row = pl.BlockSpec((1, tt, d), lambda bi, ti: (bi, ti, 0))
    vec = pl.BlockSpec((1, d), lambda bi, ti: (0, 0))
    bvec = pl.BlockSpec((1, 1, d), lambda bi, ti: (bi, 0, 0))
    args, in_specs, out_shapes, out_specs = [x], [row], [], []
    if y is not None:
        args += [y, post[0].reshape(1, d), post[1].reshape(b, 1, d)]
        in_specs += [row, vec, bvec]
        out_shapes.append(jax.ShapeDtypeStruct((b, t, d), F32))
        out_specs.append(row)
    if pre is not None:
        args += [pre[0].reshape(1, d), pre[1].reshape(b, 1, d), pre[2].reshape(b, 1, d)]
        in_specs += [vec, bvec, bvec]
        out_shapes.append(jax.ShapeDtypeStruct((b, t, d), BF16))
        out_specs.append(row)
    res = pl.pallas_call(
        functools.partial(_resid_norm_body, has_y=y is not None, has_h=pre is not None),
        grid=(b, t // tt),
        in_specs=in_specs,
        out_specs=out_specs,
        out_shape=out_shapes,
        compiler_params=_cparams("parallel", "parallel"),
        name="resid_norm",
    )(*args)
    return tuple(res)


def _add_bias(parts, extras):
    return parts[0] + extras[0]


def project(act, ws, bias, name):
    if bias is None:
        return mm(act, ws, out_dtype=BF16, name=name)
    tn = _pick(ws[0][0].shape[1], 1024)
    return mm(act, ws, epi=_add_bias, extras=[_row_extra(bias, tn)], tn=tn, out_dtype=BF16, name=name)


def _mlp_up_body(x_ref, y_ref, gpost_ref, gate_ref, gpre_ref, scale_ref, shift_ref, w_ref, a_ref, xn_ref, h0_ref,
                 h1_ref, *, n_tiles, rows):
    i, j = pl.program_id(0), pl.program_id(1)
    bufs = (h0_ref, h1_ref)

    tm = a_ref.shape[0]
    pieces = 8
    nr, mr = rows // pieces, tm // pieces

    def norm_rows(dst_ref, s):
        sl = pl.ds(s * nr, nr)
        xn = x_ref[sl, :] + (1.0 + gate_ref[0]) * _rms(y_ref[sl, :].astype(F32), gpost_ref[...])
        xn_ref[sl, :] = xn
        h = _rms(xn, gpre_ref[...]) * (1.0 + scale_ref[0]) + shift_ref[0]
        dst_ref[pl.ds(pl.multiple_of(j * rows + s * nr, nr), nr), :] = h.astype(dst_ref.dtype)

    def up_project(src_ref, w, s):
        sl = pl.ds(s * mr, mr)
        r = jnp.maximum(_dot(src_ref[sl, :], w), 0.0)
        a_ref[sl, :] = (r * r).astype(a_ref.dtype)

    @pl.when(i == 0)
    def _():
        for s in range(pieces):
            norm_rows(bufs[0], s)

    for parity in (0, 1):
        @pl.when((i > 0) & (i < n_tiles) & (i % 2 == parity))
        def _():
            w = w_ref[...].astype(BF16)
            for s in range(pieces):
                up_project(bufs[1 - parity], w, s)
                norm_rows(bufs[parity], s)

    @pl.when(i == n_tiles)
    def _():
        w = w_ref[...].astype(BF16)
        for s in range(pieces):
            up_project(bufs[(n_tiles - 1) % 2], w, s)


def mlp_up_resid(x, y, post, pre, w1, layer, tm=2048, tn=512):
    b, t, d = x.shape
    nl, _, f = w1.shape
    m = b * t
    tm = _pick(t, tm)
    tn = _pick(f, tn)
    n_tiles, nj = m // tm, f // tn
    assert tm % nj == 0
    rows = tm // nj
    assert rows % 16 == 0
    tiles_per_batch = t // tm
    slice_idx = lambda i, j: jnp.where(i < n_tiles, i * nj + j, n_tiles * nj - 1)
    rows_spec = pl.BlockSpec((rows, d), lambda i, j: (slice_idx(i, j), 0))
    vec = pl.BlockSpec((1, d), lambda i, j: (0, 0))
    bvec = pl.BlockSpec((1, 1, d), lambda i, j: (jnp.minimum(i, n_tiles - 1) // tiles_per_batch, 0, 0))
    a, x_new = pl.pallas_call(
        functools.partial(_mlp_up_body, n_tiles=n_tiles, rows=rows),
        grid=(n_tiles + 1, nj),
        in_specs=[rows_spec, rows_spec, vec, bvec, vec, bvec, bvec,
                  pl.BlockSpec((d, tn), lambda i, j: (layer, j))],
        out_specs=[pl.BlockSpec((tm, tn), lambda i, j: (jnp.maximum(i - 1, 0), jnp.where(i > 0, j, 0))),
                   rows_spec],
        out_shape=[jax.ShapeDtypeStruct((m, f), BF16), jax.ShapeDtypeStruct((m, d), F32)],
        scratch_shapes=[pltpu.VMEM((tm, d), BF16), pltpu.VMEM((tm, d), BF16)],
        compiler_params=_cparams("arbitrary", "arbitrary"),
        name="mlp_up_resid",
    )(x.reshape(m, d), y.reshape(m, d), post[0].reshape(1, d), post[1].reshape(b, 1, d), pre[0].reshape(1, d),
      pre[1].reshape(b, 1, d), pre[2].reshape(b, 1, d), w1.reshape(nl * d, f))
    return a, x_new.reshape(b, t, d)


def _mlp_down_body(*refs, n_tiles, rows, has_pre):
    a_ref, w_ref, x_ref, gpost_ref, gate_ref = refs[:5]
    if has_pre:
        gpre_ref, scale_ref, shift_ref, xn_ref, h_ref, acc0_ref, acc1_ref = refs[5:]
    else:
        xn_ref, acc0_ref, acc1_ref = refs[5:]
    i, k = pl.program_id(0), pl.program_id(1)
    accs = (acc0_ref, acc1_ref)
    tm = a_ref.shape[0]
    pieces = 4
    mr = tm // pieces
    nr = rows // pieces

    @pl.when((i == 0) & (k == 0))
    def _():
        acc0_ref[...] = jnp.zeros_like(acc0_ref)
        acc1_ref[...] = jnp.zeros_like(acc1_ref)

    def accumulate(dst_ref, w, s):
        sl = pl.ds(s * mr, mr)
        part = _dot(a_ref[sl, :], w)
        dst_ref[sl, :] = jnp.where(k == 0, part, dst_ref[sl, :] + part)

    def finish_rows(src_ref, s):
        sl = pl.ds(s * nr, nr)
        y = src_ref[pl.ds(pl.multiple_of(k * rows + s * nr, nr), nr), :]
        xn = x_ref[sl, :] + (1.0 + gate_ref[0]) * _rms(y, gpost_ref[...])
        xn_ref[sl, :] = xn
        if has_pre:
            h_ref[sl, :] = (_rms(xn, gpre_ref[...]) * (1.0 + scale_ref[0]) + shift_ref[0]).astype(h_ref.dtype)

    @pl.when(i == 0)
    def _():
        w = w_ref[...]
        for s in range(pieces):
            accumulate(accs[0], w, s)

    for parity in (0, 1):
        @pl.when((i > 0) & (i < n_tiles) & (i % 2 == parity))
        def _():
            w = w_ref[...]
            for s in range(pieces):
                accumulate(accs[parity], w, s)
                finish_rows(accs[1 - parity], s)

    @pl.when(i == n_tiles)
    def _():
        for s in range(pieces):
            finish_rows(accs[(n_tiles - 1) % 2], s)


def mlp_down_resid(a, w2, layer, x, post, pre=None, tm=1024, tk=1024):
    b, t, d = x.shape
    nl, f, _ = w2.shape
    m = b * t
    tm = _pick(t, tm)
    tk = _pick(f, tk)
    n_tiles, nk = m // tm, f // tk
    assert tm % nk == 0
    rows = tm // nk
    assert rows % (8 * 16) == 0
    tiles_per_batch = t // tm
    slice_idx = lambda i, k: jnp.where(i > 0, (i - 1) * nk + k, 0)
    rows_spec = pl.BlockSpec((rows, d), lambda i, k: (slice_idx(i, k), 0))
    vec = pl.BlockSpec((1, d), lambda i, k: (0, 0))
    bvec = pl.BlockSpec((1, 1, d), lambda i, k: (jnp.maximum(i - 1, 0) // tiles_per_batch, 0, 0))
    in_specs = [pl.BlockSpec((tm, tk), lambda i, k: (jnp.minimum(i, n_tiles - 1), jnp.where(i < n_tiles, k, nk - 1))),
                pl.BlockSpec((tk, d), lambda i, k: (layer * nk + jnp.where(i < n_tiles, k, nk - 1), 0)),
                rows_spec, vec, bvec]
    args = [a, w2.reshape(nl * f, d).astype(BF16), x.reshape(m, d), post[0].reshape(1, d), post[1].reshape(b, 1, d)]
    out_specs, out_shapes = [rows_spec], [jax.ShapeDtypeStruct((m, d), F32)]
    if pre is not None:
        in_specs += [vec, bvec, bvec]
        args += [pre[0].reshape(1, d), pre[1].reshape(b, 1, d), pre[2].reshape(b, 1, d)]
        out_specs.append(rows_spec)
        out_shapes.append(jax.ShapeDtypeStruct((m, d), BF16))
    outs = pl.pallas_call(
        functools.partial(_mlp_down_body, n_tiles=n_tiles, rows=rows, has_pre=pre is not None),
        grid=(n_tiles + 1, nk),
        in_specs=in_specs,
        out_specs=out_specs,
        out_shape=out_shapes,
        scratch_shapes=[pltpu.VMEM((tm, d), F32), pltpu.VMEM((tm, d), F32)],
        compiler_params=_cparams("arbitrary", "arbitrary"),
        name="mlp_down_resid",
    )(*args)
    return tuple(o.reshape(b, t, d) for o in outs)


def _glu(parts, extras):
    return (parts[0] + extras[0]) * jax.nn.sigmoid(parts[1] + extras[1])


def _conv_ln_body(cur_ref, prev_ref, w_ref, b_ref, g_ref, beta_ref, o_ref, buf_ref, acc_ref, *, tt, d):
    ti = pl.program_id(1)
    halo = prev_ref[0].astype(F32)
    buf_ref[0:CONV_HALO, :] = jnp.where(ti == 0, jnp.zeros_like(halo), halo)
    buf_ref[CONV_HALO:, :] = cur_ref[0].astype(F32)
    lane_chunk = min(d, 256)
    row_chunk = min(tt, 64)
    base = CONV_HALO - (CONV_WIDTH - 1)
    sub = 8
    for c0 in range(0, d, lane_chunk):
        for r0 in range(0, tt, row_chunk):
            acc = None
            for s in range(sub):
                taps = [j for j in range(CONV_WIDTH) if (base + j) % sub == s]
                if not taps:
                    continue
                rows = row_chunk + (sub if s else 0)
                part = jnp.zeros((rows, lane_chunk), F32)
                for j in taps:
                    off = r0 + base + j - s
                    part = part + buf_ref[off:off + rows, c0:c0 + lane_chunk] * w_ref[j:j + 1, c0:c0 + lane_chunk]
                part = part[s:s + row_chunk]
                acc = part if acc is None else acc + part
            acc_ref[r0:r0 + row_chunk, c0:c0 + lane_chunk] = acc
    y = acc_ref[...] + b_ref[...]
    mu = jnp.mean(y, axis=-1, keepdims=True)
    yc = y - mu
    var = jnp.mean(yc * yc, axis=-1, keepdims=True)
    z = yc * lax.rsqrt(var + CONV_LN_EPS) * g_ref[...] + beta_ref[...]
    o_ref[0] = (z * jax.nn.sigmoid(z)).astype(o_ref.dtype)


def conv_ln_silu(u, dw_w, dw_b, ln_g, ln_b, tt=128):
    b, t, d = u.shape
    tt = _pick(t, tt)
    hb = tt // CONV_HALO
    vec = pl.BlockSpec((1, d), lambda bi, ti: (0, 0))
    return pl.pallas_call(
        functools.partial(_conv_ln_body, tt=tt, d=d),
        grid=(b, t // tt),
        in_specs=[
            pl.BlockSpec((1, tt, d), lambda bi, ti: (bi, ti, 0)),
            pl.BlockSpec((1, CONV_HALO, d), lambda bi, ti: (bi, jnp.maximum(ti * hb - 1, 0), 0)),
            pl.BlockSpec((CONV_WIDTH, d), lambda bi, ti: (0, 0)),
            vec, vec, vec,
        ],
        out_specs=pl.BlockSpec((1, tt, d), lambda bi, ti: (bi, ti, 0)),
        out_shape=jax.ShapeDtypeStruct((b, t, d), BF16),
        scratch_shapes=[pltpu.VMEM((tt + CONV_HALO, d), F32), pltpu.VMEM((tt, d), F32)],
        compiler_params=_cparams("parallel", "parallel"),
        name="conv_ln_silu",
    )(u, u, dw_w, dw_b.reshape(1, d), ln_g.reshape(1, d), ln_b.reshape(1, d))


def conformer_conv_mix(h, pw1_w, pw1_b, dw_w, dw_b, ln_g, ln_b, pw2_w, pw2_b):
    b, t, d = h.shape
    tn = _pick(d, 1024)
    b1 = pw1_b.reshape(1, -1)
    u = mm(h.reshape(b * t, d), [(pw1_w, 0), (pw1_w, d // tn)], n_out=d, tm=1024, tn=tn, epi=_glu,
           extras=[(b1, (1, tn), lambda i, j, k: (0, j)),
                   (b1, (1, tn), lambda i, j, k, o=d // tn: (0, j + o))], out_dtype=BF16, name="conf_pw1_glu")
    z = conv_ln_silu(u.reshape(b, t, d), dw_w, dw_b, ln_g, ln_b)
    return z.reshape(b * t, d), [(pw2_w, 0, 0)], pw2_b


def _rope(x, cos, sin):
    half = x.shape[-1] // 2
    x1, x2 = x[:, :half], x[:, half:]
    return jnp.concatenate([x1 * cos - x2 * sin, x2 * cos + x1 * sin], axis=-1)


def _retention_body(q_ref, k_ref, v_ref, gate_ref, cos_ref, sin_ref, inner_ref, qdec_ref, kdec_ref, cdec_ref,
                    gng_ref, gnb_ref, o_ref, state_ref, *, dk, heads):
    @pl.when(pl.program_id(2) == 0)
    def _():
        state_ref[...] = jnp.zeros_like(state_ref)

    cos, sin = cos_ref[...], sin_ref[...]
    dv = v_ref.shape[2] // heads
    for hh in range(heads):
        ks, vs = slice(hh * dk, (hh + 1) * dk), slice(hh * dv, (hh + 1) * dv)
        q = _rope(q_ref[0, :, ks].astype(F32), cos, sin)
        k = _rope(k_ref[0, :, ks].astype(F32), cos, sin) * (dk ** -0.5)
        v = v_ref[0, :, vs].astype(BF16)
        qb = q.astype(BF16)
        s = _dot_nt(qb, k.astype(BF16)) * inner_ref[hh]
        state = state_ref[hh]
        o = _dot(s.astype(BF16), v) + _dot(qb, state.astype(BF16)) * qdec_ref[hh]
        kd_t = (k * kdec_ref[hh]).T.astype(BF16)
        state_ref[hh] = state * cdec_ref[hh] + _dot(kd_t, v)
        mu = jnp.mean(o, axis=-1, keepdims=True)
        oc = o - mu
        var = jnp.mean(oc * oc, axis=-1, keepdims=True)
        on = oc * lax.rsqrt(var + RET_GN_EPS) * gng_ref[:, vs] + gnb_ref[:, vs]
        gate = gate_ref[0, :, vs].astype(F32)
        o_ref[0, :, vs] = (gate * jax.nn.sigmoid(gate) * on).astype(o_ref.dtype)


def retention_mix(h, w_in, gn_g, gn_b, w_out):
    b, t, d = h.shape
    nh, c = RET_HEADS, _pick(t, RET_CHUNK)
    dk = d // nh
    dv = 2 * dk
    proj = mm(h.reshape(b * t, d), w_in, out_dtype=BF16, tm=2048, tn=512, cast_in_kernel=True,
              name="ret_in").reshape(b, t, 6 * d)
    pos = jnp.arange(t, dtype=F32)
    inv_freq = ROPE_BASE ** (-jnp.arange(0, dk, 2, dtype=F32) / dk)
    ang = pos[:, None] * inv_freq[None, :]
    cos, sin = jnp.cos(ang), jnp.sin(ang)
    log_gamma = jnp.log(1.0 - 2.0 ** (-5.0 - jnp.arange(nh, dtype=F32)))
    idx = jnp.arange(c, dtype=F32)
    diff = idx[:, None] - idx[None, :]
    inner = jnp.where(diff >= 0, jnp.exp(jnp.maximum(diff, 0.0)[None] * log_gamma[:, None, None]), 0.0)
    q_dec = jnp.exp((idx + 1.0)[None] * log_gamma[:, None])[:, :, None]
    k_dec = jnp.exp((c - 1.0 - idx)[None] * log_gamma[:, None])[:, :, None]
    c_dec = jnp.exp(c * log_gamma)[:, None, None]
    hp = RET_HEADS_PER_STEP
    assert nh % hp == 0
    ng = nh // hp
    o = pl.pallas_call(
        functools.partial(_retention_body, dk=dk, heads=hp),
        grid=(b, ng, t // c),
        in_specs=[
            pl.BlockSpec((1, c, hp * dk), lambda bi, hi, ci: (bi, ci, hi)),
            pl.BlockSpec((1, c, hp * dk), lambda bi, hi, ci: (bi, ci, ng + hi)),
            pl.BlockSpec((1, c, hp * dv), lambda bi, hi, ci: (bi, ci, ng + hi)),
            pl.BlockSpec((1, c, hp * dv), lambda bi, hi, ci: (bi, ci, 2 * ng + hi)),
            pl.BlockSpec((c, dk // 2), lambda bi, hi, ci: (ci, 0)),
            pl.BlockSpec((c, dk // 2), lambda bi, hi, ci: (ci, 0)),
            pl.BlockSpec((hp, c, c), lambda bi, hi, ci: (hi, 0, 0)),
            pl.BlockSpec((hp, c, 1), lambda bi, hi, ci: (hi, 0, 0)),
            pl.BlockSpec((hp, c, 1), lambda bi, hi, ci: (hi, 0, 0)),
            pl.BlockSpec((hp, 1, 1), lambda bi, hi, ci: (hi, 0, 0)),
            pl.BlockSpec((1, hp * dv), lambda bi, hi, ci: (0, hi)),
            pl.BlockSpec((1, hp * dv), lambda bi, hi, ci: (0, hi)),
        ],
        out_specs=pl.BlockSpec((1, c, hp * dv), lambda bi, hi, ci: (bi, ci, hi)),
        out_shape=jax.ShapeDtypeStruct((b, t, nh * dv), BF16),
        scratch_shapes=[pltpu.VMEM((hp, dk, dv), F32)],
        compiler_params=_cparams("parallel", "parallel", "arbitrary"),
        name="retention",
    )(proj, proj, proj, proj, cos, sin, inner, q_dec, k_dec, c_dec, gn_g.reshape(1, -1), gn_b.reshape(1, -1))
    return o.reshape(b * t, nh * dv), [(w_out, 0, 0)], None


def _rwkv_pre_body(x_ref, prev_ref, g_ref, scale_ref, shift_ref, mu_ref, o_ref):
    ti = pl.program_id(1)
    g, scale, shift = g_ref[...], scale_ref[0], shift_ref[0]
    h = _rms(x_ref[0], g) * (1.0 + scale) + shift
    hp = _rms(prev_ref[0], g) * (1.0 + scale) + shift
    last = jnp.where(ti == 0, 0.0, hp[7:8, :])
    row = lax.broadcasted_iota(jnp.int32, h.shape, 0)
    shifted = jnp.where(row == 0, last, pltpu.roll(h, 1, axis=0))
    xx = shifted - h
    for s in range(6):
        o_ref[s, 0] = (h + xx * mu_ref[s:s + 1, :]).astype(o_ref.dtype)


def rwkv_pre(x, g_pre, scale, shift, mu, tt=256):
    b, t, d = x.shape
    tt = _pick(t, tt)
    bvec = pl.BlockSpec((1, 1, d), lambda bi, ti: (bi, 0, 0))
    return pl.pallas_call(
        _rwkv_pre_body,
        grid=(b, t // tt),
        in_specs=[
            pl.BlockSpec((1, tt, d), lambda bi, ti: (bi, ti, 0)),
            pl.BlockSpec((1, 8, d), lambda bi, ti: (bi, jnp.maximum(ti * (tt // 8) - 1, 0), 0)),
            pl.BlockSpec((1, d), lambda bi, ti: (0, 0)),
            bvec, bvec,
            pl.BlockSpec((6, d), lambda bi, ti: (0, 0)),
        ],
        out_specs=pl.BlockSpec((6, 1, tt, d), lambda bi, ti: (0, bi, ti, 0)),
        out_shape=jax.ShapeDtypeStruct((6, b, t, d), BF16),
        compiler_params=_cparams("parallel", "parallel"),
        name="rwkv_pre",
    )(x, x, g_pre.reshape(1, d), scale.reshape(b, 1, d), shift.reshape(b, 1, d), mu)


def _rwkv_scan_body(r_ref, k_ref, v_ref, a_ref, lw_ref, g_ref, kk_ref, ka_ref, rk_ref, lng_ref, lnb_ref, o_ref,
                    state_ref, *, chunk, heads, n):
    @pl.when(pl.program_id(2) == 0)
    def _():
        state_ref[...] = jnp.zeros_like(state_ref)

    ln = chunk
    hs = range(heads)
    row = lax.broadcasted_iota(jnp.int32, (ln, ln), 0)
    col = lax.broadcasted_iota(jnp.int32, (ln, ln), 1)
    tri = (col <= row).astype(BF16)
    eye = (row == col).astype(F32)
    wid = heads * n
    seg_w = min(wid, 256)
    assert wid % seg_w == 0 and seg_w % n == 0
    seg = (lax.broadcasted_iota(jnp.int32, (seg_w, seg_w), 0) // n
           == lax.broadcasted_iota(jnp.int32, (seg_w, seg_w), 1) // n).astype(BF16)

    def split_dot(m, x):
        hi = x.astype(BF16)
        lo = (x - hi.astype(F32)).astype(BF16)
        return _dot(m, hi) + _dot(m, lo)

    def seg_sum(x):
        hi = x.astype(BF16)
        lo = (x - hi.astype(F32)).astype(BF16)
        return jnp.concatenate([_dot(hi[:, j:j + seg_w], seg) + _dot(lo[:, j:j + seg_w], seg)
                                for j in range(0, wid, seg_w)], axis=1)

    def seg_sum_group(x):
        hi = x.astype(BF16)
        lo = (x - hi.astype(F32)).astype(BF16)
        return _dot(hi, seg) + _dot(lo, seg)

    heads_of = lambda x: [x[:, i * n:(i + 1) * n] for i in range(seg_w // n)]

    e_last, a_t, r_t, b_t, k_t, b_rem, k_rem, vs, bonus = [], [], [], [], [], [], [], [], []
    for j in range(0, wid, seg_w):
        cols = slice(j, j + seg_w)
        r, k, v, a = (ref[0, :, cols].astype(F32) for ref in (r_ref, k_ref, v_ref, a_ref))
        lw = lw_ref[0, :, cols]
        kk = k * kk_ref[:, cols]
        kk = kk / jnp.maximum(jnp.sqrt(seg_sum_group(kk * kk)), 1e-12)
        k2 = k * (1.0 + (a - 1.0) * ka_ref[:, cols])
        beta = kk * a
        c = split_dot(tri, lw)
        c_last = c[ln - 1:ln, :]
        e_neg = jnp.exp(-c)
        e_rem = jnp.exp(c_last - c)
        e_last += heads_of(jnp.exp(c_last))
        a_t += heads_of(-kk * jnp.exp(c - lw))
        r_t += heads_of(r * jnp.exp(c))
        b_t += heads_of(beta * e_neg)
        k_t += heads_of(k2 * e_neg)
        b_rem += heads_of(beta * e_rem)
        k_rem += heads_of(k2 * e_rem)
        vs += heads_of(v)
        bonus.append(seg_sum_group(r * k2 * rk_ref[:, cols]) * v)
    bonus = jnp.concatenate(bonus, axis=1)

    ar = [jnp.concatenate([a_t[h], r_t[h]], axis=0).astype(BF16) for h in hs]
    bk = [jnp.concatenate([b_t[h], k_t[h]], axis=0).astype(BF16) for h in hs]
    bk_rem = [jnp.concatenate([b_rem[h], k_rem[h]], axis=0).astype(BF16) for h in hs]
    vb = [vs[h].astype(BF16) for h in hs]
    state = [state_ref[h] for h in hs]
    p = [_dot_nt(ar[h], bk[h]) for h in hs]
    q0 = [_dot_nt(ar[h], state[h].astype(BF16)) for h in hs]
    row2 = lax.broadcasted_iota(jnp.int32, (2 * ln, 2 * ln), 0)
    col2 = lax.broadcasted_iota(jnp.int32, (2 * ln, 2 * ln), 1)
    keep = col2 % ln < row2 % ln + row2 // ln
    pm = [jnp.where(keep, p[h], 0.0) for h in hs]
    m_ab = [pm[h][:ln, :ln] for h in hs]
    m_ak = [pm[h][:ln, ln:].astype(BF16) for h in hs]
    m_r = [pm[h][ln:].astype(BF16) for h in hs]
    rhs = [q0[h][:ln] + _dot(m_ak[h], vb[h]) for h in hs]
    inv = [eye + m_ab[h] for h in hs]
    pw = [m_ab[h].astype(BF16) for h in hs]
    for _ in range(int(math.log2(ln)) - 1):
        pw = [_dot(pw[h], pw[h]).astype(BF16) for h in hs]
        inv = [inv[h] + _dot(inv[h].astype(BF16), pw[h]) for h in hs]
    u = [_dot(inv[h].astype(BF16), rhs[h].astype(BF16)) for h in hs]
    uv = [jnp.concatenate([u[h], vs[h]], axis=0) for h in hs]
    y = [q0[h][ln:] + _dot(m_r[h], uv[h].astype(BF16)) for h in hs]
    for h in hs:
        state_ref[h] = state[h] * e_last[h] + _dot(uv[h].T.astype(BF16), bk_rem[h])
    yn = []
    for h in hs:
        mu = jnp.mean(y[h], axis=-1, keepdims=True)
        yc = y[h] - mu
        var = jnp.mean(yc * yc, axis=-1, keepdims=True)
        yn.append(yc * lax.rsqrt(var + RWKV_GN_EPS))
    yn = jnp.concatenate(yn, axis=-1) * lng_ref[...] + lnb_ref[...]
    o_ref[0] = ((yn + bonus) * g_ref[0]).astype(o_ref.dtype)


def rwkv_scan(r, k, v, a, lw, g, k_k, k_a, r_k, ln_g, ln_b):
    b, t, d = r.shape
    n = RWKV_HEAD_DIM
    hps = min(RWKV_HEADS_PER_STEP, d // n)
    w = hps * n
    ln = _pick(t, RWKV_CHUNK)
    tok = pl.BlockSpec((1, ln, w), lambda bi, hi, ci: (bi, ci, hi))
    vec = pl.BlockSpec((1, w), lambda bi, hi, ci: (0, hi))
    return pl.pallas_call(
        functools.partial(_rwkv_scan_body, chunk=ln, heads=hps, n=n),
        grid=(b, d // w, t // ln),
        in_specs=[tok] * 6 + [vec] * 5,
        out_specs=tok,
        out_shape=jax.ShapeDtypeStruct((b, t, d), BF16),
        scratch_shapes=[pltpu.VMEM((hps, n, n), F32)],
        compiler_params=_cparams("parallel", "parallel", "arbitrary"),
        name="rwkv_scan",
    )(r, k, v, a, lw, g, k_k.reshape(1, d), k_a.reshape(1, d), r_k.reshape(1, d), ln_g.reshape(1, d),
      ln_b.reshape(1, d))


def _tanh_epi(parts, extras):
    return jnp.tanh(parts[0])


def _sigmoid_epi(parts, extras):
    return jax.nn.sigmoid(parts[0])


def _sigmoid_bias_epi(parts, extras):
    return jax.nn.sigmoid(extras[0] + parts[0])


def _logdecay_epi(parts, extras):
    return -jnp.exp(-jax.nn.softplus(-(extras[0] + parts[0])) - 0.5)


def rwkv7_time_mix(x, g_pre, scale, shift, mu, w_rkv, w0, w_la, w_lb, a0, a_la, a_lb, g_la, g_lb, k_k, k_a, r_k,
                   ln_g, ln_b, w_out):
    b, t, d = x.shape
    xs = rwkv_pre(x, g_pre, scale, shift, mu).reshape(6 * b * t, d)
    tn = _pick(d, 1024)
    w_rkv2d = w_rkv.reshape(3 * d, d)
    r = mm(xs, [(w_rkv2d, 0, 0)], a_part=(0, 6), out_dtype=BF16, name="rwkv_r")
    k = mm(xs, [(w_rkv2d, 0, 1)], a_part=(1, 6), out_dtype=BF16, name="rwkv_k")
    v = mm(xs, [(w_rkv2d, 0, 2)], a_part=(2, 6), out_dtype=BF16, name="rwkv_v")
    lw = mm(mm(xs, w_la, a_part=(3, 6), epi=_tanh_epi, out_dtype=BF16, name="rwkv_w_la"), w_lb, epi=_logdecay_epi,
            extras=[_row_extra(w0, tn)], name="rwkv_w_lb")
    a = mm(mm(xs, a_la, a_part=(4, 6), out_dtype=BF16, name="rwkv_a_la"), a_lb, epi=_sigmoid_bias_epi,
           extras=[_row_extra(a0, tn)], out_dtype=BF16, name="rwkv_a_lb")
    g = mm(mm(xs, g_la, a_part=(5, 6), epi=_sigmoid_epi, out_dtype=BF16, name="rwkv_g_la"), g_lb, out_dtype=BF16,
           name="rwkv_g_lb")
    sh = lambda z: z.reshape(b, t, d)
    z = rwkv_scan(sh(r), sh(k), sh(v), sh(a), sh(lw), sh(g), k_k, k_a, r_k, ln_g, ln_b)
    return z.reshape(b * t, d), [(w_out, 0, 0)], None


def _nsa_compress_body(kv_ref, pe_ref, w1_ref, w2_ref, o_ref, buf_ref, *, nchunk, dh):
    st = NSA_CMP_STRIDE
    hid = w1_ref.shape[1]
    buf_ref[...] = kv_ref[0].astype(F32)
    first = jnp.zeros((nchunk, hid), F32)
    second = jnp.zeros((nchunk, hid), F32)
    for j in range(st):
        xj = buf_ref[pl.ds(j, nchunk, stride=st), :]
        first = first + _dot((xj + pe_ref[j:j + 1, :]).astype(BF16), w1_ref[j * dh:(j + 1) * dh, :])
        second = second + _dot((xj + pe_ref[st + j:st + j + 1, :]).astype(BF16),
                               w1_ref[(st + j) * dh:(st + j + 1) * dh, :])
    hidden = jax.nn.gelu(first + pltpu.roll(second, nchunk - 1, axis=0))
    out = _dot(hidden.astype(BF16), w2_ref[...])
    row = lax.broadcasted_iota(jnp.int32, out.shape, 0)
    o_ref[0, 0] = jnp.where(row < nchunk - 1, out, 0.0)


def nsa_compress(proj, col0, pe, w1, w2):
    b, t, _ = proj.shape
    g, dh = NSA_KV_GROUPS, NSA_HEAD_DIM
    nchunk = t // NSA_CMP_STRIDE
    hid = w1.shape[1]
    return pl.pallas_call(
        functools.partial(_nsa_compress_body, nchunk=nchunk, dh=dh),
        grid=(b, g),
        in_specs=[
            pl.BlockSpec((1, t, dh), lambda bi, gi: (bi, 0, col0 // dh + gi)),
            pl.BlockSpec((NSA_CMP_BLOCK, dh), lambda bi, gi: (0, 0)),
            pl.BlockSpec((NSA_CMP_BLOCK * dh, hid), lambda bi, gi: (0, 0)),
            pl.BlockSpec((hid, dh), lambda bi, gi: (0, 0)),
        ],
        out_specs=pl.BlockSpec((1, 1, nchunk, dh), lambda bi, gi: (bi, gi, 0, 0)),
        out_shape=jax.ShapeDtypeStruct((b, g, nchunk, dh), F32),
        scratch_shapes=[pltpu.VMEM((t, dh), F32)],
        compiler_params=_cparams("parallel", "parallel"),
        name="nsa_compress",
    )(proj, pe, w1.astype(BF16), w2.astype(BF16))


def _stack_heads(q, r, dh):
    return jnp.concatenate([q[:, i * dh:(i + 1) * dh] for i in range(r)], axis=0)


def _unstack_heads(o, r, tq):
    return jnp.concatenate([o[i * tq:(i + 1) * tq] for i in range(r)], axis=-1)


def _gate_rows(gates, branch, r, tq):
    return jnp.concatenate([gates[:, branch * r + i:branch * r + i + 1] for i in range(r)], axis=0)


def _nsa_cmp_body(q_ref, kc_ref, vc_ref, gates_ref, selmap_t_ref, o_ref, sel_ref, *, tq, r, dh, n_sel, groups):
    q0 = pl.program_id(2) * tq
    wq = r * dh
    ncp = kc_ref.shape[2]
    rows = lax.broadcasted_iota(jnp.int32, (r * tq, ncp), 0)
    cmp_end = lax.broadcasted_iota(jnp.int32, (r * tq, ncp), 1) * NSA_CMP_STRIDE + (NSA_CMP_BLOCK - 1)
    vis = cmp_end <= q0 + rows % tq
    selmap_t = selmap_t_ref[...]
    imps = []
    for gg in range(groups):
        qcols = slice(gg * wq, (gg + 1) * wq)
        qs = (_stack_heads(q_ref[0, :, qcols].astype(F32), r, dh) * (dh ** -0.5)).astype(BF16)
        kc, vc = kc_ref[0, gg].astype(BF16), vc_ref[0, gg].astype(BF16)
        s = jnp.where(vis, _dot_nt(qs, kc), NEG_INF)
        e = jnp.exp(s - jnp.max(s, axis=-1, keepdims=True))
        p = e / jnp.sum(e, axis=-1, keepdims=True) * vis.astype(F32)
        o = _dot(p.astype(BF16), vc) * _gate_rows(gates_ref[0, gg], 0, r, tq)
        o_ref[0, :, qcols] = _unstack_heads(o, r, tq)
        p_sum = p[0:tq]
        for i in range(1, r):
            p_sum = p_sum + p[i * tq:(i + 1) * tq]
        hi = p_sum.astype(BF16)
        lo = (p_sum - hi.astype(F32)).astype(BF16)
        imps.append(_dot_nt(selmap_t, hi) + _dot_nt(selmap_t, lo))
    imp = jnp.concatenate(imps, axis=1)
    ns, width = imp.shape
    blk = lax.broadcasted_iota(jnp.int32, (ns, width), 0)
    tq_pos = q0 + lax.broadcasted_iota(jnp.int32, (ns, width), 1) % tq
    cur = tq_pos // NSA_SEL_BLOCK
    valid = blk * NSA_SEL_BLOCK <= tq_pos
    forced = (blk == 0) | (blk == cur) | (blk == cur - 1)
    score = jnp.where(valid, jnp.where(forced, POS_BIG, imp), NEG_INF)
    rank = jnp.zeros((ns, width), F32)
    for m in range(ns):
        sm = score[m:m + 1, :]
        beats = (sm > score) | ((sm == score) & (blk > m))
        rank = rank + beats.astype(F32)
    chosen = ((rank < n_sel) & valid).astype(F32)
    for gg in range(groups):
        sel_ref[0, gg] = (chosen[:, gg * tq:(gg + 1) * tq].T - 1.0).astype(sel_ref.dtype)


def _flash_update(s, v_ones, m_ref, acc_ref):
    lanes = m_ref.shape[1]
    assert s.shape[1] % lanes == 0 and acc_ref.shape[1] == 2 * lanes
    m_old = m_ref[...]
    m_new = jnp.maximum(m_old, jnp.max(s, axis=-1, keepdims=True))
    alpha = jnp.exp2(m_old - m_new)
    p = jnp.exp2((s - jnp.concatenate([m_new] * (s.shape[1] // lanes), axis=1)).astype(BF16))
    acc_ref[...] = jnp.concatenate([alpha, alpha], axis=1) * acc_ref[...] + _dot(p, v_ones)
    m_ref[...] = m_new


def _nsa_sel_body(qi_tbl, ki_tbl, diag_tbl, q_ref, k_ref, v_ref, gates_ref, oin_ref, sel_ref, o_ref, qa_ref, m_ref,
                  acc_ref, *, tq, tk, r, dh):
    step = pl.program_id(2)
    qi, ki, diag = qi_tbl[step], ki_tbl[step], diag_tbl[step]
    ns = sel_ref.shape[-1]
    assert ns <= dh

    @pl.when(ki == 0)
    def _():
        m_ref[...] = jnp.full_like(m_ref, NEG_INF)
        acc_ref[...] = jnp.zeros_like(acc_ref)
        qs = _stack_heads(q_ref[0].astype(F32), r, dh) * (dh ** -0.5 * LOG2_E)
        pieces = [qs, jnp.concatenate([sel_ref[0, 0].astype(F32)] * r, axis=0)]
        if ns < dh:
            pieces.append(jnp.zeros((r * tq, dh - ns), F32))
        qa_ref[...] = jnp.concatenate(pieces, axis=1).astype(BF16)

    blk_of_key = (ki * tk + lax.broadcasted_iota(jnp.int32, (tk, dh), 0)) // NSA_SEL_BLOCK
    own_block = jnp.where(blk_of_key == lax.broadcasted_iota(jnp.int32, (tk, dh), 1), POS_BIG, 0.0).astype(BF16)
    ka = jnp.concatenate([k_ref[0].astype(BF16), own_block], axis=1)
    s = _dot_nt(qa_ref[...], ka)
    v_ones = jnp.concatenate([v_ref[0].astype(BF16), jnp.ones((tk, dh), BF16)], axis=1)

    @pl.when(diag == 0)
    def _():
        _flash_update(s, v_ones, m_ref, acc_ref)

    @pl.when(diag == 1)
    def _():
        t_pos = qi * tq + lax.broadcasted_iota(jnp.int32, (r * tq, tk), 0) % tq
        k_pos = ki * tk + lax.broadcasted_iota(jnp.int32, (r * tq, tk), 1)
        _flash_update(jnp.where(k_pos <= t_pos, s, NEG_INF), v_ones, m_ref, acc_ref)
        o = acc_ref[:, :dh] / acc_ref[:, dh:] * _gate_rows(gates_ref[0, 0], 1, r, tq)
        o_ref[0] = (oin_ref[0] + _unstack_heads(o, r, tq)).astype(o_ref.dtype)


def _nsa_win_body(q_ref, *rest, tq, r, dh, nwin, groups):
    k_refs, v_refs = rest[:nwin], rest[nwin:2 * nwin]
    gates_ref, oin_ref, o_ref = rest[2 * nwin:]
    qi = pl.program_id(2)
    wq = r * dh
    t_loc = lax.broadcasted_iota(jnp.int32, (r * tq, tq), 0) % tq
    k_loc = lax.broadcasted_iota(jnp.int32, (r * tq, tq), 1)
    for gg in range(groups):
        qcols, kcols = slice(gg * wq, (gg + 1) * wq), slice(gg * dh, (gg + 1) * dh)
        qs = (_stack_heads(q_ref[0, :, qcols].astype(F32), r, dh) * (dh ** -0.5 * LOG2_E)).astype(BF16)
        tiles = []
        for j in range(nwin):
            s = _dot_nt(qs, k_refs[j][0, :, kcols].astype(BF16))
            if j == nwin - 1:
                s = jnp.where(k_loc <= t_loc, s, NEG_INF)
            else:
                if j == 0:
                    s = jnp.where(k_loc > t_loc, s, NEG_INF)
                s = jnp.where(qi - (nwin - 1) + j >= 0, s, NEG_INF)
            tiles.append(s)
        s = jnp.concatenate(tiles, axis=1)
        p = jnp.exp2((s - jnp.max(s, axis=-1, keepdims=True)).astype(BF16))
        v = jnp.concatenate([v_refs[j][0, :, kcols] for j in range(nwin)], axis=0).astype(BF16)
        on = _dot(p, jnp.concatenate([v, jnp.ones_like(v)], axis=1))
        o = on[:, :dh] / on[:, dh:] * _gate_rows(gates_ref[0, gg], 2, r, tq)
        o_ref[0, :, qcols] = (oin_ref[0, :, qcols] + _unstack_heads(o, r, tq)).astype(o_ref.dtype)


def nsa_mix(h, w_in, pe_k, pe_v, ck_w1, ck_w2, cv_w1, cv_w2, w_out, tq=256):
    b, t, d = h.shape
    nh, g, dh = NSA_HEADS, NSA_KV_GROUPS, NSA_HEAD_DIM
    r = nh // g
    hd, kd = nh * dh, g * dh
    n_main = hd + 6 * kd
    h2 = h.reshape(b * t, d)
    proj = mm(h2, w_in[:, :n_main], out_dtype=BF16, name="nsa_in").reshape(b, t, n_main)
    n_gate = w_in.shape[1] - n_main
    w_gate = jnp.pad(w_in[:, n_main:], ((0, 0), (0, 128 - n_gate)))
    gates = mm(h2, w_gate, epi=_sigmoid_epi, name="nsa_gates")[:, :n_gate]
    gates = jnp.transpose(gates.reshape(b, t, 3, g, r), (0, 3, 1, 2, 4)).reshape(b, g, t, 3 * r)
    k_cmp = nsa_compress(proj, hd, pe_k, ck_w1, ck_w2)
    v_cmp = nsa_compress(proj, hd + kd, pe_v, cv_w1, cv_w2)
    ncp = k_cmp.shape[2]
    ns = t // NSA_SEL_BLOCK
    n_sel = min(NSA_SEL_TOP, ns)
    cs = jnp.arange(ncp)[None, :] * NSA_CMP_STRIDE
    ss = jnp.arange(ns)[:, None] * NSA_SEL_BLOCK
    sel_map_t = (jnp.maximum(jnp.minimum(cs + NSA_CMP_BLOCK, ss + NSA_SEL_BLOCK) - jnp.maximum(cs, ss), 0)
                 .astype(F32) / NSA_CMP_BLOCK).astype(BF16)
    tq = _pick(t, tq)
    assert NSA_WINDOW % tq == 0
    nq = t // tq
    wq = r * dh
    col = lambda base: base // dh
    q_spec3 = pl.BlockSpec((1, tq, wq), lambda bi, gi, qi: (bi, qi, gi))
    gate_spec3 = pl.BlockSpec((1, 1, tq, 3 * r), lambda bi, gi, qi: (bi, gi, qi, 0))
    gc = NSA_CMP_GROUPS_PER_STEP
    assert g % gc == 0
    q_spec_c = pl.BlockSpec((1, tq, gc * wq), lambda bi, gi, qi: (bi, qi, gi))
    o_cmp, sel = pl.pallas_call(
        functools.partial(_nsa_cmp_body, tq=tq, r=r, dh=dh, n_sel=n_sel, groups=gc),
        grid=(b, g // gc, nq),
        in_specs=[
            q_spec_c,
            pl.BlockSpec((1, gc, ncp, dh), lambda bi, gi, qi: (bi, gi, 0, 0)),
            pl.BlockSpec((1, gc, ncp, dh), lambda bi, gi, qi: (bi, gi, 0, 0)),
            pl.BlockSpec((1, gc, tq, 3 * r), lambda bi, gi, qi: (bi, gi, qi, 0)),
            pl.BlockSpec((ns, ncp), lambda bi, gi, qi: (0, 0)),
        ],
        out_specs=[q_spec_c, pl.BlockSpec((1, gc, tq, ns), lambda bi, gi, qi: (bi, gi, qi, 0))],
        out_shape=[jax.ShapeDtypeStruct((b, t, hd), F32), jax.ShapeDtypeStruct((b, g, t, ns), BF16)],
        compiler_params=_cparams("parallel", "parallel", "parallel"),
        name="nsa_cmp",
    )(proj, k_cmp, v_cmp, gates, sel_map_t)

    tk = _pick(t, NSA_SEL_KEY_TILE)
    tqs = _pick(t, NSA_SEL_QUERY_TILE)
    assert tk % tqs == 0
    pairs = [(qi, ki) for qi in range(t // tqs) for ki in range((qi * tqs + tqs - 1) // tk + 1)]
    qi_tbl = jnp.asarray([pq for pq, _ in pairs], jnp.int32)
    ki_tbl = jnp.asarray([pk for _, pk in pairs], jnp.int32)
    diag_tbl = jnp.asarray([int(pk == (pq * tqs + tqs - 1) // tk) for pq, pk in pairs], jnp.int32)
    q_spec_p = pl.BlockSpec((1, tqs, wq), lambda bi, gi, p, qt, kt, dt: (bi, qt[p], gi))

    def kv_spec_p(base):
        return pl.BlockSpec((1, tk, dh), lambda bi, gi, p, qt, kt, dt: (bi, kt[p], col(base) + gi))

    o_sel = pl.pallas_call(
        functools.partial(_nsa_sel_body, tq=tqs, tk=tk, r=r, dh=dh),
        grid_spec=pltpu.PrefetchScalarGridSpec(
            num_scalar_prefetch=3,
            grid=(b, g, len(pairs)),
            in_specs=[q_spec_p, kv_spec_p(hd + 2 * kd), kv_spec_p(hd + 3 * kd),
                      pl.BlockSpec((1, 1, tqs, 3 * r), lambda bi, gi, p, qt, kt, dt: (bi, gi, qt[p], 0)),
                      q_spec_p,
                      pl.BlockSpec((1, 1, tqs, ns), lambda bi, gi, p, qt, kt, dt: (bi, gi, qt[p], 0))],
            out_specs=q_spec_p,
            scratch_shapes=[pltpu.VMEM((r * tqs, 2 * dh), BF16), pltpu.VMEM((r * tqs, dh), F32),
                            pltpu.VMEM((r * tqs, 2 * dh), F32)],
        ),
        out_shape=jax.ShapeDtypeStruct((b, t, hd), F32),
        compiler_params=_cparams("parallel", "parallel", "arbitrary"),
        name="nsa_sel",
    )(qi_tbl, ki_tbl, diag_tbl, proj, proj, proj, gates, o_cmp, sel)

    nwin = NSA_WINDOW // tq + 1
    gw = NSA_WIN_GROUPS_PER_STEP
    assert g % gw == 0 and kd % (gw * dh) == 0
    q_spec_w = pl.BlockSpec((1, tq, gw * wq), lambda bi, gi, qi: (bi, qi, gi))

    def kv_spec_w(base, j):
        return pl.BlockSpec((1, tq, gw * dh),
                            lambda bi, gi, qi: (bi, jnp.maximum(qi - (nwin - 1) + j, 0), base // (gw * dh) + gi))

    o_all = pl.pallas_call(
        functools.partial(_nsa_win_body, tq=tq, r=r, dh=dh, nwin=nwin, groups=gw),
        grid=(b, g // gw, nq),
        in_specs=([q_spec_w] + [kv_spec_w(hd + 4 * kd, j) for j in range(nwin)]
                  + [kv_spec_w(hd + 5 * kd, j) for j in range(nwin)]
                  + [pl.BlockSpec((1, gw, tq, 3 * r), lambda bi, gi, qi: (bi, gi, qi, 0)), q_spec_w]),
        out_specs=q_spec_w,
        out_shape=jax.ShapeDtypeStruct((b, t, hd), BF16),
        compiler_params=_cparams("parallel", "parallel", "parallel"),
        name="nsa_win",
    )(proj, *([proj] * (2 * nwin)), gates, o_sel)
    return o_all.reshape(b * t, hd), [(w_out, 0, 0)], None


def _silu(a):
    return a * jax.nn.sigmoid(a)


def _ada_epi(parts, extras):
    return parts[0] + extras[0]


def ada_modulation(c, ada_w, ada_b):
    depth, d, n6 = ada_w.shape
    b = c.shape[0]
    rows = ((b + 7) // 8) * 8
    cond = jnp.pad(c, ((0, rows - b), (0, 0)))
    w2d = ada_w.reshape(depth * d, n6)
    mods = []
    for i in range(depth):
        mods.append(mm(cond, [(w2d, 0, i)], a_act=_silu, epi=_ada_epi, extras=[_row_extra(ada_b[i], _pick(n6, 1024))],
                       precision=HIGHEST, name="ada_mod")[:b])
    return jnp.stack(mods)


def kernel(x, c, ada_w, ada_b, norm_g, mlp_w1, mlp_w2, rwkv_mu, rwkv_w_rkv, rwkv_w0, rwkv_w_la, rwkv_w_lb, rwkv_a0, rwkv_a_la, rwkv_a_lb, rwkv_g_la, rwkv_g_lb, rwkv_k_k, rwkv_k_a, rwkv_r_k, rwkv_ln_g, rwkv_ln_b, rwkv_w_out, ret_w_in, ret_gn_g, ret_gn_b, ret_w_out, conv_pw1_w, conv_pw1_b, conv_dw_w, conv_dw_b, conv_ln_g, conv_ln_b, conv_pw2_w, conv_pw2_b, nsa_w_in, nsa_pe_k, nsa_pe_v, nsa_ck_w1, nsa_ck_w2, nsa_cv_w1, nsa_cv_w2, nsa_w_out):
    b, t, d = x.shape
    depth = ada_w.shape[0]
    mod = ada_modulation(c, ada_w, ada_b).reshape(depth, b, 6, d)
    h = None
    for i in range(depth):
        sh_t, sc_t, gt_t, sh_c, sc_c, gt_c = (mod[i, :, j] for j in range(6))
        kind = i % 4
        if kind == 0:
            act, ws, bias = rwkv7_time_mix(x, norm_g[i, 0], sc_t, sh_t, rwkv_mu, rwkv_w_rkv, rwkv_w0, rwkv_w_la,
                                           rwkv_w_lb, rwkv_a0, rwkv_a_la, rwkv_a_lb, rwkv_g_la, rwkv_g_lb, rwkv_k_k,
                                           rwkv_k_a, rwkv_r_k.reshape(-1), rwkv_ln_g, rwkv_ln_b, rwkv_w_out)
        else:
            if h is None:
                (h,) = resid_norm(x, pre=(norm_g[i, 0], sc_t, sh_t))
            if kind == 1:
                act, ws, bias = retention_mix(h, ret_w_in, ret_gn_g, ret_gn_b, ret_w_out)
            elif kind == 2:
                act, ws, bias = conformer_conv_mix(h, conv_pw1_w, conv_pw1_b, conv_dw_w, conv_dw_b, conv_ln_g,
                                                   conv_ln_b, conv_pw2_w, conv_pw2_b)
            else:
                act, ws, bias = nsa_mix(h, nsa_w_in, nsa_pe_k, nsa_pe_v, nsa_ck_w1, nsa_ck_w2, nsa_cv_w1, nsa_cv_w2,
                                        nsa_w_out)
        y = project(act, ws, bias, "mixer_out").reshape(b, t, d)
        act, x = mlp_up_resid(x, y, (norm_g[i, 1], gt_t), (norm_g[i, 2], sc_c, sh_c), mlp_w1, i)
        nxt = i + 1
        if nxt < depth and nxt % 4 != 0:
            x, h = mlp_down_resid(act, mlp_w2, i, x, (norm_g[i, 3], gt_c),
                                  (norm_g[nxt, 0], mod[nxt, :, 1], mod[nxt, :, 0]))
        else:
            (x,) = mlp_down_resid(act, mlp_w2, i, x, (norm_g[i, 3], gt_c))
            h = None
    return x
```
